```python
import math
import jax, jax.numpy as jnp
from jax import lax
import numpy as np

D_MODEL = 1024
BATCH = 32
SEQ = 256
DEPTH = 2
DEC_BATCH = 4
DEC_SEQ = 4096
PAST_LEN = 512

GRID_W = 64
N_DIR = 2
EPS = 1e-6
S5_WIDTH = 256
S5_CH = 16
S5_GROUPS = S5_WIDTH // S5_CH
S5_STATE = 64
SSD_HEADDIM = 64
SSD_HEADS = 6
SSD_WIDTH = SSD_HEADS * SSD_HEADDIM
SSD_GROUPS = 2
SSD_STATE = 64
SSD_BC = SSD_GROUPS * SSD_STATE
SSD_CONV_CH = SSD_WIDTH + 2 * SSD_BC
SSD_CONV = 3
SSD_CHUNK = 128
RWKV_HEADSIZE = 64
RWKV_HEADS = 6
RWKV_WIDTH = RWKV_HEADS * RWKV_HEADSIZE
W_RANK = 32
A_RANK = 32
G_RANK = 64
DECAY_SCALE = math.exp(-0.5)
GN_EPS = 64e-5
N_EXPERTS = 16
D_EXPERT = 512
EC_FACTOR = 2
MIX_WIDTH = S5_WIDTH + SSD_WIDTH + RWKV_WIDTH
IN_SIZES = (S5_WIDTH, SSD_WIDTH, SSD_CONV_CH, N_DIR * SSD_HEADS, 3 * RWKV_WIDTH, N_DIR * W_RANK, A_RANK, G_RANK)
IN_WIDTH = S5_WIDTH + SSD_WIDTH + SSD_CONV_CH + N_DIR * SSD_HEADS + 3 * RWKV_WIDTH + N_DIR * W_RANK + A_RANK + G_RANK

kernel_name = 'hybrid_s5_ssd_rwkv7_ec_diffusion_step'


def rmsnorm(x, g):
    xf = x.astype(jnp.float32)
    y = xf * lax.rsqrt(jnp.mean(xf * xf, axis=-1, keepdims=True) + EPS)
    return (y * g.astype(jnp.float32)).astype(x.dtype)


def _dir(t, d):
    return jnp.flip(t, 1) if d else t


def split_in_proj(p):
    out, start = [], 0
    for s in IN_SIZES:
        out.append(p[..., start:start + s])
        start += s
    return out


def s5_discretize(a_re, a_im, log_dt, b_re, b_im):
    f32 = jnp.float32
    lam_re = jnp.minimum(a_re.astype(f32), -1e-4)
    lam_im = a_im.astype(f32)
    dt = jnp.exp(log_dt.astype(f32))[:, None]
    mag = jnp.exp(lam_re * dt)
    ab_re = mag * jnp.cos(lam_im * dt)
    ab_im = mag * jnp.sin(lam_im * dt)
    num_re, num_im = ab_re - 1.0, ab_im
    den = lam_re * lam_re + lam_im * lam_im
    q_re = (num_re * lam_re + num_im * lam_im) / den
    q_im = (num_im * lam_re - num_re * lam_im) / den
    b_re, b_im = b_re.astype(f32), b_im.astype(f32)
    bb_re = q_re[..., None] * b_re - q_im[..., None] * b_im
    bb_im = q_re[..., None] * b_im + q_im[..., None] * b_re
    return ab_re, ab_im, bb_re, bb_im


def s5_scan(u, h0_re, h0_im, ab_re, ab_im, bb_re, bb_im, c_re, c_im):
    bu_re = jnp.einsum('blgh,gph->blgp', u, bb_re)
    bu_im = jnp.einsum('blgh,gph->blgp', u, bb_im)
    bu_re = bu_re.at[:, 0].add(ab_re * h0_re - ab_im * h0_im)
    bu_im = bu_im.at[:, 0].add(ab_re * h0_im + ab_im * h0_re)
    a_re = jnp.broadcast_to(ab_re, bu_re.shape)
    a_im = jnp.broadcast_to(ab_im, bu_im.shape)

    def combine(e1, e2):
        a1r, a1i, b1r, b1i = e1
        a2r, a2i, b2r, b2i = e2
        return (a2r * a1r - a2i * a1i, a2r * a1i + a2i * a1r,
                a2r * b1r - a2i * b1i + b2r, a2r * b1i + a2i * b1r + b2i)

    _, _, h_re, h_im = lax.associative_scan(combine, (a_re, a_im, bu_re, bu_im), axis=1)
    y = jnp.einsum('blgp,ghp->blgh', h_re, c_re) - jnp.einsum('blgp,ghp->blgh', h_im, c_im)
    return y, h_re[:, -1], h_im[:, -1]


def s5_mixer(u, h0_re, h0_im, lp, grid_rows):
    f32 = jnp.float32
    Bsz, L, _ = u.shape
    uf = u.astype(f32)
    if grid_rows is not None:
        uf = uf.reshape(Bsz, grid_rows, GRID_W, S5_WIDTH).swapaxes(1, 2).reshape(Bsz, L, S5_WIDTH)
    ug = uf.reshape(Bsz, L, S5_GROUPS, S5_CH)
    y = uf * lp['s5_d'].astype(f32)
    fin_re, fin_im = [], []
    for d in range(N_DIR):
        ab_re, ab_im, bb_re, bb_im = s5_discretize(lp['s5_a_re'][d], lp['s5_a_im'][d], lp['s5_log_dt'][d],
                                                   lp['s5_b_re'][d], lp['s5_b_im'][d])
        yd, hr, hi = s5_scan(_dir(ug, d), h0_re[:, d].astype(f32), h0_im[:, d].astype(f32),
                             ab_re, ab_im, bb_re, bb_im,
                             lp['s5_c_re'][d].astype(f32), lp['s5_c_im'][d].astype(f32))
        y = y + _dir(yd, d).reshape(Bsz, L, S5_WIDTH)
        fin_re.append(hr)
        fin_im.append(hi)
    if grid_rows is not None:
        y = y.reshape(Bsz, GRID_W, grid_rows, S5_WIDTH).swapaxes(1, 2).reshape(Bsz, L, S5_WIDTH)
    zg = jax.nn.gelu(y)
    out = zg * jax.nn.sigmoid(zg @ lp['s5_glu_w'].astype(f32) + lp['s5_glu_b'].astype(f32))
    return out, jnp.stack(fin_re, 1), jnp.stack(fin_im, 1)


def depthwise_conv(u, w, b):
    out = lax.conv_general_dilated(u, w[:, None, :].astype(u.dtype), window_strides=(1,), padding='SAME',
                                   dimension_numbers=('NWC', 'WIO', 'NWC'), feature_group_count=u.shape[-1])
    return out + b.astype(u.dtype)


def segsum(a):
    T = a.shape[-1]
    a_rep = jnp.broadcast_to(a[..., None], a.shape + (T,))
    a_rep = jnp.where(jnp.tril(jnp.ones((T, T), bool), -1), a_rep, 0.0)
    ss = jnp.cumsum(a_rep, axis=-2)
    return jnp.where(jnp.tril(jnp.ones((T, T), bool)), ss, -jnp.inf)


def ssd_scan(x, dA, Bh, Ch, h0):
    Bsz, L, H, P = x.shape
    c = L // SSD_CHUNK
    x = x.reshape(Bsz, c, SSD_CHUNK, H, P)
    Bh = Bh.reshape(Bsz, c, SSD_CHUNK, H, SSD_STATE)
    Ch = Ch.reshape(Bsz, c, SSD_CHUNK, H, SSD_STATE)
    dA = dA.reshape(Bsz, c, SSD_CHUNK, H).transpose(0, 3, 1, 2)
    A_cs = jnp.cumsum(dA, axis=-1)
    Lmat = jnp.exp(segsum(dA))
    scores = jnp.einsum('bclhn,bcshn->bhcls', Ch, Bh) * Lmat
    y_diag = jnp.einsum('bhcls,bcshp->bclhp', scores, x)
    decay_states = jnp.exp(A_cs[..., -1:] - A_cs)
    states = jnp.einsum('bclhn,bhcl,bclhp->bchpn', Bh, decay_states, x)
    states = jnp.concatenate([h0[:, None], states], axis=1)
    decay_chunk = jnp.exp(segsum(jnp.pad(A_cs[..., -1], ((0, 0), (0, 0), (1, 0)))))
    new_states = jnp.einsum('bhzc,bchpn->bzhpn', decay_chunk, states)
    states, final = new_states[:, :-1], new_states[:, -1]
    y_off = jnp.einsum('bclhn,bchpn,bhcl->bclhp', Ch, states, jnp.exp(A_cs))
    return (y_diag + y_off).reshape(Bsz, L, H, P), final


def ssd_mixer(z, xbc, dt_raw, h0, lp):
    f32 = jnp.float32
    Bsz, L, _ = z.shape
    xbc = jax.nn.silu(depthwise_conv(xbc.astype(f32), lp['ssd_conv_w'].astype(f32), lp['ssd_conv_b']))
    xs, Bm, Cm = jnp.split(xbc, [SSD_WIDTH, SSD_WIDTH + SSD_BC], axis=-1)
    xs = xs.reshape(Bsz, L, SSD_HEADS, SSD_HEADDIM)
    rep = SSD_HEADS // SSD_GROUPS
    Bm = jnp.repeat(Bm.reshape(Bsz, L, SSD_GROUPS, SSD_STATE), rep, axis=2)
    Cm = jnp.repeat(Cm.reshape(Bsz, L, SSD_GROUPS, SSD_STATE), rep, axis=2)
    dt = jax.nn.softplus(dt_raw.astype(f32).reshape(Bsz, L, N_DIR, SSD_HEADS) + lp['ssd_dt_bias'].astype(f32))
    A = -jnp.exp(lp['ssd_a_log'].astype(f32))
    y = lp['ssd_d'].astype(f32)[:, None] * xs
    finals = []
    for d in range(N_DIR):
        dt_d = _dir(dt[:, :, d], d)
        yd, hd = ssd_scan(_dir(xs, d) * dt_d[..., None], dt_d * A[d], _dir(Bm, d), _dir(Cm, d),
                          h0[:, d].astype(f32))
        y = y + _dir(yd, d)
        finals.append(hd)
    y = y.reshape(Bsz, L, SSD_WIDTH) * jax.nn.silu(z.astype(f32))
    return rmsnorm(y, lp['ssd_norm_g']), jnp.stack(finals, 1)


def rwkv_scan(r, w, k, v, kk, b, s0):
    def step(s, inp):
        r_t, w_t, k_t, v_t, kk_t, b_t = inp
        sa = jnp.einsum('dbhvk,dbhk->dbhv', s, kk_t)
        s = s * w_t[..., None, :] - sa[..., :, None] * b_t[..., None, :] + v_t[..., :, None] * k_t[..., None, :]
        return s, jnp.einsum('dbhvk,dbhk->dbhv', s, r_t)
    return lax.scan(step, s0, (r, w, k, v, kk, b))


def rwkv_mixer(rkv, w_lo, a_lo, g_lo, s0, lp):
    f32 = jnp.float32
    Bsz, L, _ = rkv.shape
    rkv = rkv.astype(f32)
    prev = jnp.pad(rkv[:, :-1], ((0, 0), (1, 0), (0, 0)))
    nxt = jnp.pad(rkv[:, 1:], ((0, 0), (0, 1), (0, 0)))
    mu = lp['rwkv_mu'].astype(f32)
    rkv = rkv + mu[0] * (prev - rkv) + mu[1] * (nxt - rkv)
    r, k, v = jnp.split(rkv, 3, axis=-1)
    a = jax.nn.sigmoid(lp['rwkv_a0'].astype(f32) + a_lo.astype(f32) @ lp['rwkv_a_up'].astype(f32))
    g = jax.nn.sigmoid(g_lo.astype(f32)) @ lp['rwkv_g_up'].astype(f32)
    zw = lp['rwkv_w0'].astype(f32) + jnp.einsum('bldr,drc->bldc',
                                                jnp.tanh(w_lo.astype(f32).reshape(Bsz, L, N_DIR, W_RANK)),
                                                lp['rwkv_w_up'].astype(f32))
    w = jnp.exp(-DECAY_SCALE * jax.nn.sigmoid(zw))

    def heads(t):
        return t.reshape(Bsz, L, RWKV_HEADS, RWKV_HEADSIZE)

    kk = heads(k * lp['rwkv_k_k'].astype(f32))
    kk = kk * lax.rsqrt(jnp.maximum(jnp.sum(kk * kk, axis=-1, keepdims=True), 1e-24))
    k = k * (1.0 + (a - 1.0) * lp['rwkv_k_a'].astype(f32))
    r, k, v, a = heads(r), heads(k), heads(v), heads(a)
    b = kk * a
    w = w.reshape(Bsz, L, N_DIR, RWKV_HEADS, RWKV_HEADSIZE)

    def both(t):
        return jnp.stack([t, jnp.flip(t, 1)], 0).transpose(2, 0, 1, 3, 4)

    w_s = jnp.stack([w[:, :, 0], jnp.flip(w[:, :, 1], 1)], 0).transpose(2, 0, 1, 3, 4)
    s_last, ys = rwkv_scan(both(r), w_s, both(k), both(v), both(kk), both(b), jnp.swapaxes(s0.astype(f32), 0, 1))
    y = (ys[:, 0] + jnp.flip(ys[:, 1], 0)).transpose(1, 0, 2, 3)
    mean = jnp.mean(y, axis=-1, keepdims=True)
    var = jnp.mean((y - mean) ** 2, axis=-1, keepdims=True)
    y = (y - mean) * lax.rsqrt(var + GN_EPS)
    y = y * lp['rwkv_ln_g'].astype(f32).reshape(RWKV_HEADS, RWKV_HEADSIZE) + lp['rwkv_ln_b'].astype(f32).reshape(RWKV_HEADS, RWKV_HEADSIZE)
    y = y + jnp.sum(r * k * lp['rwkv_r_k'].astype(f32), axis=-1, keepdims=True) * v
    return y.reshape(Bsz, L, RWKV_WIDTH) * g, jnp.swapaxes(s_last, 0, 1)


def expert_choice_ffn(h, router_w, w1, w3, w2):
    Bsz, n, D = h.shape
    cap = EC_FACTOR * n // N_EXPERTS
    aff = jax.nn.softmax(jnp.einsum('bnd,de->ben', h, router_w).astype(jnp.float32), axis=1)
    gate, idx = lax.top_k(aff, cap)
    xs = jax.vmap(lambda hb, ib: hb[ib])(h, idx)
    hid = jax.nn.silu(jnp.einsum('becd,edf->becf', xs, w1)) * jnp.einsum('becd,edf->becf', xs, w3)
    out = jnp.einsum('becf,efd->becd', hid, w2) * gate[..., None].astype(h.dtype)

    def scatter(ib, ob):
        return jnp.zeros((n, D), ob.dtype).at[ib.reshape(-1)].add(ob.reshape(-1, D))

    return jax.vmap(scatter)(idx, out).astype(h.dtype)


def trunk_layer(x, cond, h0_s5_re, h0_s5_im, h0_ssd, h0_rwkv, lp, grid_rows):
    mod = jnp.einsum('bd,de->be', jax.nn.silu(cond), lp['ada_w']) + lp['ada_b']
    sh1, sc1, g1, sh2, sc2, g2 = jnp.split(mod[:, None, :], 6, axis=-1)
    h = rmsnorm(x, lp['norm1_g']) * (1.0 + sc1) + sh1
    u_s5, z, xbc, dt_raw, rkv, w_lo, a_lo, g_lo = split_in_proj(jnp.einsum('bld,de->ble', h, lp['w_in']))
    y_a, s5_re, s5_im = s5_mixer(u_s5, h0_s5_re, h0_s5_im, lp, grid_rows)
    y_b, ssd_h = ssd_mixer(z, xbc, dt_raw, h0_ssd, lp)
    y_c, rwkv_h = rwkv_mixer(rkv, w_lo, a_lo, g_lo, h0_rwkv, lp)
    mixed = jnp.concatenate([y_a, y_b, y_c], axis=-1).astype(x.dtype)
    x = x + (g1 * jnp.einsum('blc,cd->bld', mixed, lp['w_out'])).astype(x.dtype)
    h = rmsnorm(x, lp['norm2_g']) * (1.0 + sc2) + sh2
    x = x + (g2 * expert_choice_ffn(h, lp['router_w'], lp['exp_w1'], lp['exp_w3'], lp['exp_w2'])).astype(x.dtype)
    return x, s5_re, s5_im, ssd_h, rwkv_h


def setup_inputs(seed: int = 0) -> dict:
    key = jax.random.key(seed)
    ks = iter(jax.random.split(key, 64))
    f32 = jnp.float32

    def nrm(shape, s):
        return s * jax.random.normal(next(ks), shape, f32)

    def uni(shape, lo, hi):
        return jax.random.uniform(next(ks), shape, f32, lo, hi)

    L, E = DEPTH, N_EXPERTS
    dt_init = jnp.exp(uni((L, N_DIR, SSD_HEADS), math.log(1e-3), math.log(1e-1)))
    inp = {}
    inp['x_prompt'] = nrm((BATCH, SEQ, D_MODEL), 1.0)
    inp['x_sample'] = nrm((DEC_BATCH, DEC_SEQ, D_MODEL), 1.0)
    inp['c'] = nrm((DEC_BATCH, D_MODEL), 1.0)
    inp['state_s5_re'] = nrm((DEC_BATCH, DEPTH, N_DIR, S5_GROUPS, S5_STATE), 0.1)
    inp['state_s5_im'] = nrm((DEC_BATCH, DEPTH, N_DIR, S5_GROUPS, S5_STATE), 0.1)
    inp['state_ssd'] = nrm((DEC_BATCH, DEPTH, N_DIR, SSD_HEADS, SSD_HEADDIM, SSD_STATE), 0.1)
    inp['state_rwkv'] = nrm((DEC_BATCH, DEPTH, N_DIR, RWKV_HEADS, RWKV_HEADSIZE, RWKV_HEADSIZE), 0.1)
    inp['c_ctx'] = nrm((D_MODEL,), 1.0)
    inp['ada_w'] = nrm((L, D_MODEL, 6 * D_MODEL), 0.5 * D_MODEL ** -0.5)
    inp['ada_b'] = nrm((L, 6 * D_MODEL), 0.02)
    inp['norm1_g'] = 1.0 + nrm((L, D_MODEL), 0.1)
    inp['norm2_g'] = 1.0 + nrm((L, D_MODEL), 0.1)
    inp['w_in'] = nrm((L, D_MODEL, IN_WIDTH), D_MODEL ** -0.5)
    inp['w_out'] = nrm((L, MIX_WIDTH, D_MODEL), MIX_WIDTH ** -0.5)
    inp['s5_a_re'] = -0.5 + nrm((L, N_DIR, S5_GROUPS, S5_STATE), 0.01)
    inp['s5_a_im'] = jnp.pi * jnp.arange(S5_STATE, dtype=f32) + nrm((L, N_DIR, S5_GROUPS, S5_STATE), 0.01)
    inp['s5_log_dt'] = uni((L, N_DIR, S5_GROUPS), math.log(1e-3), math.log(1e-1))
    inp['s5_b_re'] = nrm((L, N_DIR, S5_GROUPS, S5_STATE, S5_CH), (2 * S5_CH) ** -0.5)
    inp['s5_b_im'] = nrm((L, N_DIR, S5_GROUPS, S5_STATE, S5_CH), (2 * S5_CH) ** -0.5)
    inp['s5_c_re'] = nrm((L, N_DIR, S5_GROUPS, S5_CH, S5_STATE), (2 * S5_STATE) ** -0.5)
    inp['s5_c_im'] = nrm((L, N_DIR, S5_GROUPS, S5_CH, S5_STATE), (2 * S5_STATE) ** -0.5)
    inp['s5_d'] = 1.0 + nrm((L, S5_WIDTH), 0.1)
    inp['s5_glu_w'] = nrm((L, S5_WIDTH, S5_WIDTH), S5_WIDTH ** -0.5)
    inp['s5_glu_b'] = nrm((L, S5_WIDTH), 0.01)
    inp['ssd_conv_w'] = nrm((L, SSD_CONV, SSD_CONV_CH), SSD_CONV ** -0.5)
    inp['ssd_conv_b'] = nrm((L, SSD_CONV_CH), 0.01)
    inp['ssd_a_log'] = jnp.log(uni((L, N_DIR, SSD_HEADS), 1.0, 16.0))
    inp['ssd_dt_bias'] = dt_init + jnp.log(-jnp.expm1(-dt_init))
    inp['ssd_d'] = 1.0 + nrm((L, SSD_HEADS), 0.1)
    inp['ssd_norm_g'] = 1.0 + nrm((L, SSD_WIDTH), 0.1)
    inp['rwkv_mu'] = uni((L, N_DIR, 3 * RWKV_WIDTH), 0.0, 0.5)
    inp['rwkv_w0'] = nrm((L, N_DIR, RWKV_WIDTH), 0.5)
    inp['rwkv_w_up'] = nrm((L, N_DIR, W_RANK, RWKV_WIDTH), 0.1)
    inp['rwkv_a0'] = nrm((L, RWKV_WIDTH), 0.1)
    inp['rwkv_a_up'] = nrm((L, A_RANK, RWKV_WIDTH), 0.1)
    inp['rwkv_g_up'] = nrm((L, G_RANK, RWKV_WIDTH), G_RANK ** -0.5)
    inp['rwkv_k_k'] = 0.85 + nrm((L, RWKV_WIDTH), 0.05)
    inp['rwkv_k_a'] = 1.0 + nrm((L, RWKV_WIDTH), 0.05)
    inp['rwkv_r_k'] = nrm((L, RWKV_HEADS, RWKV_HEADSIZE), 0.1)
    inp['rwkv_ln_g'] = 1.0 + nrm((L, RWKV_WIDTH), 0.1)
    inp['rwkv_ln_b'] = nrm((L, RWKV_WIDTH), 0.01)
    inp['router_w'] = nrm((L, D_MODEL, E), D_MODEL ** -0.5)
    inp['exp_w1'] = nrm((L, E, D_MODEL, D_EXPERT), D_MODEL ** -0.5)
    inp['exp_w3'] = nrm((L, E, D_MODEL, D_EXPERT), D_MODEL ** -0.5)
    inp['exp_w2'] = nrm((L, E, D_EXPERT, D_MODEL), D_EXPERT ** -0.5)
    inp['final_g'] = 1.0 + nrm((D_MODEL,), 0.1)
    return inp


def reference(x_prompt, x_sample, c, state_s5_re, state_s5_im, state_ssd, state_rwkv, c_ctx,
              ada_w, ada_b, norm1_g, norm2_g, w_in, w_out,
              s5_a_re, s5_a_im, s5_log_dt, s5_b_re, s5_b_im, s5_c_re, s5_c_im, s5_d, s5_glu_w, s5_glu_b,
              ssd_conv_w, ssd_conv_b, ssd_a_log, ssd_dt_bias, ssd_d, ssd_norm_g,
              rwkv_mu, rwkv_w0, rwkv_w_up, rwkv_a0, rwkv_a_up, rwkv_g_up, rwkv_k_k, rwkv_k_a, rwkv_r_k,
              rwkv_ln_g, rwkv_ln_b, router_w, exp_w1, exp_w3, exp_w2, final_g):
    f32 = jnp.float32
    bp = x_prompt.shape[0]
    grid_rows = x_sample.shape[1] // GRID_W
    zero_s5 = jnp.zeros((bp, N_DIR, S5_GROUPS, S5_STATE), f32)
    zero_ssd = jnp.zeros((bp, N_DIR, SSD_HEADS, SSD_HEADDIM, SSD_STATE), f32)
    zero_rwkv = jnp.zeros((bp, N_DIR, RWKV_HEADS, RWKV_HEADSIZE, RWKV_HEADSIZE), f32)
    cond_ctx = c_ctx[None, :]
    xp, xs = x_prompt, x_sample
    out_s5_re, out_s5_im, out_ssd, out_rwkv = [], [], [], []
    for i in range(DEPTH):
        lp = {
            'ada_w': ada_w[i], 'ada_b': ada_b[i], 'norm1_g': norm1_g[i], 'norm2_g': norm2_g[i],
            'w_in': w_in[i], 'w_out': w_out[i],
            's5_a_re': s5_a_re[i], 's5_a_im': s5_a_im[i], 's5_log_dt': s5_log_dt[i],
            's5_b_re': s5_b_re[i], 's5_b_im': s5_b_im[i], 's5_c_re': s5_c_re[i], 's5_c_im': s5_c_im[i],
            's5_d': s5_d[i], 's5_glu_w': s5_glu_w[i], 's5_glu_b': s5_glu_b[i],
            'ssd_conv_w': ssd_conv_w[i], 'ssd_conv_b': ssd_conv_b[i], 'ssd_a_log': ssd_a_log[i],
            'ssd_dt_bias': ssd_dt_bias[i], 'ssd_d': ssd_d[i], 'ssd_norm_g': ssd_norm_g[i],
            'rwkv_mu': rwkv_mu[i], 'rwkv_w0': rwkv_w0[i], 'rwkv_w_up': rwkv_w_up[i], 'rwkv_a0': rwkv_a0[i],
            'rwkv_a_up': rwkv_a_up[i], 'rwkv_g_up': rwkv_g_up[i], 'rwkv_k_k': rwkv_k_k[i],
            'rwkv_k_a': rwkv_k_a[i], 'rwkv_r_k': rwkv_r_k[i], 'rwkv_ln_g': rwkv_ln_g[i], 'rwkv_ln_b': rwkv_ln_b[i],
            'router_w': router_w[i], 'exp_w1': exp_w1[i], 'exp_w3': exp_w3[i], 'exp_w2': exp_w2[i],
        }
        xp, p_re, p_im, p_ssd, p_rwkv = trunk_layer(xp, cond_ctx, zero_s5, zero_s5, zero_ssd, zero_rwkv, lp, None)
        out_s5_re.append(p_re)
        out_s5_im.append(p_im)
        out_ssd.append(p_ssd)
        out_rwkv.append(p_rwkv)
        xs, _, _, _, _ = trunk_layer(xs, c, state_s5_re[:, i], state_s5_im[:, i], state_ssd[:, i],
                                     state_rwkv[:, i], lp, grid_rows)
    y_prompt = rmsnorm(xp, final_g)
    y_sample = rmsnorm(xs, final_g)
    new_s5_re = jnp.stack(out_s5_re, axis=1)
    new_s5_im = jnp.stack(out_s5_im, axis=1)
    new_ssd = jnp.stack(out_ssd, axis=1)
    new_rwkv = jnp.stack(out_rwkv, axis=1)
    return (y_prompt, y_sample, new_s5_re, new_s5_im, new_ssd, new_rwkv)
```

```python
import functools
import math

import jax
import jax.numpy as jnp
from jax import lax
from jax.experimental import pallas as pl
from jax.experimental.pallas import tpu as pltpu

F32 = jnp.float32
BF16 = jnp.bfloat16
HIGHEST = lax.Precision.HIGHEST

D_MODEL = 1024
GRID_W = 64
N_DIR = 2
EPS = 1e-6
S5_WIDTH = 256
S5_CH = 16
S5_GROUPS = 16
S5_STATE = 64
SSD_HEADDIM = 64
SSD_HEADS = 6
SSD_WIDTH = 384
SSD_GROUPS = 2
SSD_STATE = 64
SSD_BC = 128
SSD_CONV_CH = 640
SSD_CHUNK = 128
RWKV_HEADSIZE = 64
RWKV_HEADS = 6
RWKV_WIDTH = 384
W_RANK = 32
A_RANK = 32
G_RANK = 64
DECAY_SCALE = math.exp(-0.5)
GN_EPS = 64e-5
N_EXPERTS = 16
D_EXPERT = 512
EC_FACTOR = 2
IN_SIZES = (S5_WIDTH, SSD_WIDTH, SSD_CONV_CH, N_DIR * SSD_HEADS, 3 * RWKV_WIDTH, N_DIR * W_RANK, A_RANK, G_RANK)

LANES = 128
TOKEN_BLOCK = 256
DT_PAD = 128
LO_PAD = 256
S5_SLICE_CH = 128
S5_SLICE_ST = 512
VMEM_LIMIT = 56 * 1024 * 1024


def _cparams(sem):
    return pltpu.CompilerParams(dimension_semantics=sem, vmem_limit_bytes=VMEM_LIMIT)


def _full(shape):
    nd = len(shape)
    return pl.BlockSpec(shape, lambda *_: (0,) * nd)


def _sigmoid(x):
    return 1.0 / (1.0 + jnp.exp(-x))


def _softplus(x):
    return jnp.maximum(x, 0.0) + jnp.log1p(jnp.exp(-jnp.abs(x)))


def _gelu_tanh(x):
    return 0.5 * x * (1.0 + jnp.tanh(math.sqrt(2.0 / math.pi) * (x + 0.044715 * (x * x * x))))


def _mod_kernel(c_ref, w_ref, b_ref, o_ref):
    c = c_ref[...]
    s = (c * _sigmoid(c)).astype(BF16)
    o_ref[...] = jnp.dot(s, w_ref[...].astype(BF16), preferred_element_type=F32) + b_ref[...]


def _modulation(cond, ada_w, ada_b):
    depth = ada_w.shape[0]
    rows = cond.shape[0]
    tn = 1536
    return pl.pallas_call(
        _mod_kernel,
        grid=(depth, 6 * D_MODEL // tn),
        in_specs=[
            pl.BlockSpec((rows, D_MODEL), lambda l, j: (0, 0)),
            pl.BlockSpec((None, D_MODEL, tn), lambda l, j: (l, 0, j)),
            pl.BlockSpec((None, 1, tn), lambda l, j: (l, 0, j)),
        ],
        out_specs=pl.BlockSpec((None, rows, tn), lambda l, j: (l, 0, j)),
        out_shape=jax.ShapeDtypeStruct((depth, rows, 6 * D_MODEL), F32),
        compiler_params=_cparams(("parallel", "parallel")),
        name="adaln_mod",
    )(cond, ada_w, ada_b.reshape(depth, 1, 6 * D_MODEL))


IN_PAD_SIZES = (S5_WIDTH, SSD_WIDTH, SSD_CONV_CH, DT_PAD, 3 * RWKV_WIDTH, LO_PAD)


def _inproj_kernel(x_ref, mod_ref, g_ref, w_ref, us5_ref, z_ref, xbc_ref, dt_ref, rkv_ref, lo_ref):
    x = x_ref[...]
    y = x * lax.rsqrt(jnp.mean(x * x, axis=-1, keepdims=True) + EPS) * g_ref[...]
    m = mod_ref[...]
    h = y * (1.0 + m[:, D_MODEL:2 * D_MODEL]) + m[:, 0:D_MODEL]
    p = jnp.dot(h.astype(BF16), w_ref[...], preferred_element_type=F32)
    start = 0
    for ref, size in zip((us5_ref, z_ref, xbc_ref, dt_ref, rkv_ref, lo_ref), IN_PAD_SIZES):
        ref[...] = p[:, start:start + size]
        start += size


def _pad_in_weight(w_in):
    parts, start = [], 0
    for s in IN_SIZES:
        parts.append(w_in[:, start:start + s])
        start += s
    us5, z, xbc, dt, rkv, wlo, alo, glo = parts
    dt = jnp.pad(dt, ((0, 0), (0, DT_PAD - dt.shape[1])))
    lo = jnp.concatenate([wlo, alo, glo], axis=1)
    lo = jnp.pad(lo, ((0, 0), (0, LO_PAD - lo.shape[1])))
    return jnp.concatenate([us5, z, xbc, dt, rkv, lo], axis=1).astype(BF16)


def _in_projection(x, mod3, row_of_block, norm_g, w_pad):
    ntok = x.shape[0]
    tm = TOKEN_BLOCK
    width = w_pad.shape[1]
    outs = tuple(jax.ShapeDtypeStruct((ntok, s), F32) for s in IN_PAD_SIZES)
    return pl.pallas_call(
        _inproj_kernel,
        grid=(ntok // tm,),
        in_specs=[
            pl.BlockSpec((tm, D_MODEL), lambda i: (i, 0)),
            pl.BlockSpec((None, 1, 6 * D_MODEL), lambda i: (row_of_block(i), 0, 0)),
            _full((1, D_MODEL)),
            _full((D_MODEL, width)),
        ],
        out_specs=tuple(pl.BlockSpec((tm, s), lambda i: (i, 0)) for s in IN_PAD_SIZES),
        out_shape=outs,
        compiler_params=_cparams(("parallel",)),
        name="in_proj",
    )(x, mod3, norm_g.reshape(1, D_MODEL), w_pad)


def _s5_disc_kernel(are_ref, aim_ref, ldt_ref, bre_ref, bim_ref, abre_ref, abim_ref, bbre_ref, bbim_ref):
    lam_re = jnp.minimum(are_ref[...], -1e-4)
    lam_im = aim_ref[...]
    dt = jnp.exp(ldt_ref[...])
    mag = jnp.exp(lam_re * dt)
    ab_re = mag * jnp.cos(lam_im * dt)
    ab_im = mag * jnp.sin(lam_im * dt)
    num_re, num_im = ab_re - 1.0, ab_im
    den = lam_re * lam_re + lam_im * lam_im
    q_re = (num_re * lam_re + num_im * lam_im) / den
    q_im = (num_im * lam_re - num_re * lam_im) / den
    abre_ref[...] = ab_re
    abim_ref[...] = ab_im
    b_re = bre_ref[...]
    b_im = bim_ref[...]
    qr = q_re[:, None, :]
    qi = q_im[:, None, :]
    bbre_ref[...] = qr * b_re - qi * b_im
    bbim_ref[...] = qr * b_im + qi * b_re


def _s5_discretize(a_re, a_im, log_dt, b_re, b_im):
    lead = a_re.shape[:3]
    n = lead[0] * lead[1] * lead[2]
    a2 = lambda t: t.reshape(n, S5_STATE)
    ldt = jnp.broadcast_to(log_dt.reshape(n, 1), (n, S5_STATE))
    b3 = lambda t: t.reshape(n, S5_STATE, S5_CH).transpose(0, 2, 1)
    ab_re, ab_im, bb_re, bb_im = pl.pallas_call(
        _s5_disc_kernel,
        out_shape=(jax.ShapeDtypeStruct((n, S5_STATE), F32), jax.ShapeDtypeStruct((n, S5_STATE), F32),
                   jax.ShapeDtypeStruct((n, S5_CH, S5_STATE), F32), jax.ShapeDtypeStruct((n, S5_CH, S5_STATE), F32)),
        name="s5_discretize",
    )(a2(a_re), a2(a_im), ldt, b3(b_re), b3(b_im))
    return (ab_re.reshape(lead + (S5_STATE,)), ab_im.reshape(lead + (S5_STATE,)),
            bb_re.reshape(lead + (S5_CH, S5_STATE)), bb_im.reshape(lead + (S5_CH, S5_STATE)))


def _s5_layer_tables(ab_re, ab_im, bb_re, bb_im, c_re, c_im):
    nfs = S5_GROUPS // 8
    eye = jnp.eye(8, dtype=F32)

    def rows(t):
        return t.reshape(N_DIR, nfs, 8 * S5_STATE)

    ab_row = jnp.concatenate([rows(ab_re), rows(ab_im)], axis=-1).reshape(N_DIR, nfs, 1, 2 * S5_SLICE_ST)

    def bmat(t):
        t = t.reshape(N_DIR, nfs, 8, S5_CH, S5_STATE)
        return jnp.einsum('dfghp,gk->dfghkp', t, eye).reshape(N_DIR, nfs, S5_SLICE_CH, S5_SLICE_ST)

    b_mat = jnp.concatenate([bmat(bb_re), bmat(bb_im)], axis=-1).astype(BF16)

    def cmat(t):
        t = t.reshape(N_DIR, nfs, 8, S5_CH, S5_STATE)
        return jnp.einsum('dfghp,gk->dfgpkh', t, eye).reshape(N_DIR, nfs, S5_SLICE_ST, S5_SLICE_CH)

    c_mat = jnp.concatenate([cmat(c_re), -cmat(c_im)], axis=-2).astype(BF16)
    return ab_row, b_mat, c_mat


def _s5_kernel(*refs, R, n_slab, chained, want_final):
    if chained:
        u_ref, h0_ref, ab_ref, bm_ref, cm_ref, d_ref = refs[:6]
        rest = refs[6:]
    else:
        u_ref, ab_ref, bm_ref, cm_ref, d_ref = refs[:5]
        h0_ref = None
        rest = refs[5:]
    y_ref = rest[0]
    rest = rest[1:]
    if want_final:
        hfin_ref = rest[0]
        rest = rest[1:]
    H = rest[0]
    if chained:
        PW, CIN = rest[1], rest[2]
    NR = R * n_slab
    RC = min(512, NR)
    ST = S5_SLICE_ST
    nchunk = ST // LANES

    y_ref[...] = u_ref[...] * d_ref[...]

    for d in range(N_DIR):
        def slab_of(i, d=d):
            return i if d == 0 else n_slab - 1 - i

        def bu_body(i, _, d=d):
            r0 = pl.multiple_of(i * RC, RC)
            H[pl.ds(r0, RC), :] = jnp.dot(u_ref[pl.ds(r0, RC), :].astype(BF16), bm_ref[d],
                                           preferred_element_type=F32)
            return 0
        lax.fori_loop(0, NR // RC, bu_body, 0)

        for c in range(nchunk):
            lre = slice(c * LANES, (c + 1) * LANES)
            lim = slice(ST + c * LANES, ST + (c + 1) * LANES)
            a_re = jnp.broadcast_to(ab_ref[d, :, lre], (R, LANES))
            a_im = jnp.broadcast_to(ab_ref[d, :, lim], (R, LANES))

            def step(i, carry, lre=lre, lim=lim, a_re=a_re, a_im=a_im, slab_of=slab_of):
                cr, ci = carry
                r0 = pl.multiple_of(slab_of(i) * R, R)
                nr = a_re * cr - a_im * ci + H[pl.ds(r0, R), lre]
                ni = a_re * ci + a_im * cr + H[pl.ds(r0, R), lim]
                H[pl.ds(r0, R), lre] = nr
                H[pl.ds(r0, R), lim] = ni
                return nr, ni
            zero = jnp.zeros((R, LANES), F32)
            lax.fori_loop(0, n_slab, step, (zero, zero))

        last0 = (n_slab - 1) * R if d == 0 else 0
        if chained:
            a_re_row = ab_ref[d, :, 0:ST]
            a_im_row = ab_ref[d, :, ST:2 * ST]

            def pw_step(j, carry, a_re_row=a_re_row, a_im_row=a_im_row):
                pr, pi = carry
                PW[j, :, 0:ST] = jnp.broadcast_to(pr, (8, ST))
                PW[j, :, ST:2 * ST] = jnp.broadcast_to(pi, (8, ST))
                return pr * a_re_row - pi * a_im_row, pr * a_im_row + pi * a_re_row
            lax.fori_loop(0, n_slab, pw_step, (a_re_row, a_im_row))
            t_re = PW[n_slab - 1, 0:1, 0:ST]
            t_im = PW[n_slab - 1, 0:1, ST:2 * ST]

            cr = h0_ref[d, :, 0:ST]
            ci = h0_ref[d, :, ST:2 * ST]
            for i in range(R):
                c = i if d == 0 else R - 1 - i
                CIN[c:c + 1, 0:ST] = cr
                CIN[c:c + 1, ST:2 * ST] = ci
                er = H[last0 + c:last0 + c + 1, 0:ST]
                ei = H[last0 + c:last0 + c + 1, ST:2 * ST]
                cr, ci = t_re * cr - t_im * ci + er, t_re * ci + t_im * cr + ei

            def fix_step(i, _, slab_of=slab_of):
                r0 = pl.multiple_of(slab_of(i) * R, R)
                p = PW[i]
                for c in range(nchunk):
                    lre = slice(c * LANES, (c + 1) * LANES)
                    lim = slice(ST + c * LANES, ST + (c + 1) * LANES)
                    pr = p[0:1, lre]
                    pi = p[0:1, lim]
                    cr = CIN[:, lre]
                    ci = CIN[:, lim]
                    H[pl.ds(r0, R), lre] = H[pl.ds(r0, R), lre] + (pr * cr - pi * ci)
                    H[pl.ds(r0, R), lim] = H[pl.ds(r0, R), lim] + (pr * ci + pi * cr)
                return 0
            lax.fori_loop(0, n_slab, fix_step, 0)

        if want_final:
            hfin_ref[d] = H[last0:last0 + R, :]

        def y_body(i, _, d=d):
            r0 = pl.multiple_of(i * RC, RC)
            y_ref[pl.ds(r0, RC), :] = y_ref[pl.ds(r0, RC), :] + jnp.dot(
                H[pl.ds(r0, RC), :].astype(BF16), cm_ref[d], preferred_element_type=F32)
            return 0
        lax.fori_loop(0, NR // RC, y_body, 0)


def _s5_scan(u, h0, tables, d_row, *, R, n_slab, chained, want_final):
    ab_row, b_mat, c_mat = tables
    Q, NR, _ = u.shape
    nfs = S5_GROUPS // 8
    W2 = 2 * S5_SLICE_ST
    in_specs = [pl.BlockSpec((None, NR, S5_SLICE_CH), lambda q, f: (q, 0, f))]
    args = [u]
    if chained:
        in_specs.append(pl.BlockSpec((None, N_DIR, None, 1, W2), lambda q, f: (q, 0, f, 0, 0)))
        args.append(h0)
    in_specs += [
        pl.BlockSpec((N_DIR, None, 1, W2), lambda q, f: (0, f, 0, 0)),
        pl.BlockSpec((N_DIR, None, S5_SLICE_CH, W2), lambda q, f: (0, f, 0, 0)),
        pl.BlockSpec((N_DIR, None, W2, S5_SLICE_CH), lambda q, f: (0, f, 0, 0)),
        pl.BlockSpec((1, S5_SLICE_CH), lambda q, f: (0, f)),
    ]
    args += [ab_row, b_mat, c_mat, d_row]
    out_shape = [jax.ShapeDtypeStruct((Q, NR, S5_WIDTH), F32)]
    out_specs = [pl.BlockSpec((None, NR, S5_SLICE_CH), lambda q, f: (q, 0, f))]
    if want_final:
        out_shape.append(jax.ShapeDtypeStruct((Q, N_DIR, nfs, R, W2), F32))
        out_specs.append(pl.BlockSpec((None, N_DIR, None, R, W2), lambda q, f: (q, 0, f, 0, 0)))
    scratch = [pltpu.VMEM((NR, W2), F32)]
    if chained:
        scratch += [pltpu.VMEM((n_slab, 8, W2), F32), pltpu.VMEM((R, W2), F32)]
    res = pl.pallas_call(
        functools.partial(_s5_kernel, R=R, n_slab=n_slab, chained=chained, want_final=want_final),
        grid=(Q, nfs),
        in_specs=in_specs,
        out_specs=tuple(out_specs),
        out_shape=tuple(out_shape),
        scratch_shapes=scratch,
        compiler_params=_cparams(("parallel", "parallel")),
        name="s5_scan_chained" if chained else "s5_scan",
    )(*args)
    return res


def _ssd_kernel(*refs, TB, nb, has_h0, want_final):
    xbc_ref, xp_ref, xn_ref, dt_ref = refs[:4]
    refs = refs[4:]
    if has_h0:
        h0_ref = refs[0]
        refs = refs[1:]
    cw_ref, cb_ref, dtb_ref, arow_ref, sel_ref, drow_ref = refs[:6]
    refs = refs[6:]
    y_ref = refs[0]
    refs = refs[1:]
    if want_final:
        hfin_ref = refs[0]
        refs = refs[1:]
    hst, xc_s, dt_s = refs
    CH = SSD_CHUNK
    P = SSD_HEADDIM
    d = pl.program_id(1)
    j = pl.program_id(2)
    jj = jnp.where(d == 0, j, nb - 1 - j)

    @pl.when(j == 0)
    def _():
        if has_h0:
            hst[...] = h0_ref[...]
        else:
            hst[...] = jnp.zeros_like(hst)

    x = xbc_ref[...]
    prev_row = xp_ref[7:8, :] * (jj > 0).astype(F32)
    next_row = xn_ref[0:1, :] * (jj < nb - 1).astype(F32)
    rows = lax.broadcasted_iota(jnp.int32, (TB, 1), 0)
    x_prev = jnp.where(rows == 0, prev_row, pltpu.roll(x, 1, 0))
    x_next = jnp.where(rows == TB - 1, next_row, pltpu.roll(x, TB - 1, 0))
    conv = cw_ref[0:1, :] * x_prev + cw_ref[1:2, :] * x + cw_ref[2:3, :] * x_next + cb_ref[...]
    xc_s[...] = conv * _sigmoid(conv)
    dtf = _softplus(dt_ref[...] + dtb_ref[...])
    dt_s[...] = jnp.dot(dtf, sel_ref[...], preferred_element_type=F32, precision=HIGHEST)

    li = lax.broadcasted_iota(jnp.int32, (CH, CH), 0)
    si = lax.broadcasted_iota(jnp.int32, (CH, CH), 1)
    tmat = jnp.where(d == 0, (si <= li).astype(F32), (si >= li).astype(F32))
    is_fwd = (d == 0).astype(F32)
    n_ch = TB // CH
    for i in range(n_ch):
        ci = jnp.where(d == 0, i, n_ch - 1 - i)
        r0 = pl.multiple_of(ci * CH, CH)
        dtc = dt_s[pl.ds(r0, CH), :]
        dA = dtc * arow_ref[...]
        cs = jnp.dot(tmat, dA, preferred_element_type=F32, precision=HIGHEST)
        csT = cs.T
        tot = jnp.sum(dA, axis=0, keepdims=True)
        Bm = xc_s[pl.ds(r0, CH), SSD_WIDTH:SSD_WIDTH + SSD_BC]
        Cm = xc_s[pl.ds(r0, CH), SSD_WIDTH + SSD_BC:SSD_WIDTH + 2 * SSD_BC]
        BmT = Bm.T
        for g in range(SSD_GROUPS):
            Cg = Cm[:, g * SSD_STATE:(g + 1) * SSD_STATE].astype(BF16)
            Bg = Bm[:, g * SSD_STATE:(g + 1) * SSD_STATE].astype(BF16)
            BgT = BmT[g * SSD_STATE:(g + 1) * SSD_STATE, :].astype(BF16)
            G = lax.dot_general(Cg, Bg, (((1,), (1,)), ((), ())), preferred_element_type=F32)
            for hh in range(SSD_HEADS // SSD_GROUPS):
                h = g * (SSD_HEADS // SSD_GROUPS) + hh
                col = cs[:, h:h + 1]
                row = csT[h:h + 1, :]
                lm = jnp.exp(jnp.where(tmat > 0.0, col - row, -1e30))
                xs_h = xc_s[pl.ds(r0, CH), h * P:(h + 1) * P]
                xdt = xs_h * dtc[:, h:h + 1]
                tot_h = tot[:, h:h + 1]
                hprev = hst[h]
                y = jnp.dot((G * lm).astype(BF16), xdt.astype(BF16), preferred_element_type=F32)
                y = y + jnp.dot(Cg, hprev.astype(BF16), preferred_element_type=F32) * jnp.exp(col)
                y = y + (is_fwd * drow_ref[:, h * P:(h + 1) * P]) * xs_h
                y_ref[pl.ds(r0, CH), h * P:(h + 1) * P] = y
                xd = (xdt * jnp.exp(tot_h - col)).astype(BF16)
                hst[h] = jnp.exp(tot_h) * hprev + jnp.dot(BgT, xd, preferred_element_type=F32)

    if want_final:
        @pl.when(j == nb - 1)
        def _():
            hfin_ref[...] = hst[...]


def _ssd_scan(xbc, dt, h0, conv_w, conv_b, dt_bias, a_log, d_skip, *, B, L, want_final):
    TB = min(512, L)
    nb = L // TB
    H = SSD_HEADS

    def blk(b, d, j):
        return b * nb + j + d * (nb - 1 - 2 * j)

    nrow8 = B * L // 8
    in_specs = [
        pl.BlockSpec((TB, SSD_CONV_CH), lambda b, d, j: (blk(b, d, j), 0)),
        pl.BlockSpec((8, SSD_CONV_CH), lambda b, d, j: (jnp.maximum(blk(b, d, j) * (TB // 8) - 1, 0), 0)),
        pl.BlockSpec((8, SSD_CONV_CH), lambda b, d, j: (jnp.minimum((blk(b, d, j) + 1) * (TB // 8), nrow8 - 1), 0)),
        pl.BlockSpec((TB, DT_PAD), lambda b, d, j: (blk(b, d, j), 0)),
    ]
    args = [xbc, xbc, xbc, dt]
    if h0 is not None:
        in_specs.append(pl.BlockSpec((None, None, H, SSD_STATE, SSD_HEADDIM), lambda b, d, j: (b, d, 0, 0, 0)))
        args.append(h0)
    dtb = jnp.pad(dt_bias.reshape(1, N_DIR * H), ((0, 0), (0, DT_PAD - N_DIR * H)))
    arow = jnp.pad(-jnp.exp(a_log), ((0, 0), (0, LANES - H))).reshape(N_DIR, 1, LANES)
    lane = jnp.arange(LANES)
    sel = jnp.stack([(lane[:, None] == (dd * H + lane[None, :])) & (lane[None, :] < H) for dd in range(N_DIR)]).astype(F32)
    drow = jnp.repeat(d_skip, SSD_HEADDIM).reshape(1, SSD_WIDTH)
    in_specs += [
        _full((3, SSD_CONV_CH)), _full((1, SSD_CONV_CH)), _full((1, DT_PAD)),
        pl.BlockSpec((None, 1, LANES), lambda b, d, j: (d, 0, 0)),
        pl.BlockSpec((None, LANES, LANES), lambda b, d, j: (d, 0, 0)),
        _full((1, SSD_WIDTH)),
    ]
    args += [conv_w, conv_b.reshape(1, SSD_CONV_CH), dtb, arow, sel, drow]
    out_shape = [jax.ShapeDtypeStruct((N_DIR, B * L, SSD_WIDTH), F32)]
    out_specs = [pl.BlockSpec((None, TB, SSD_WIDTH), lambda b, d, j: (d, blk(b, d, j), 0))]
    if want_final:
        out_shape.append(jax.ShapeDtypeStruct((B, N_DIR, H, SSD_STATE, SSD_HEADDIM), F32))
        out_specs.append(pl.BlockSpec((None, None, H, SSD_STATE, SSD_HEADDIM), lambda b, d, j: (b, d, 0, 0, 0)))
    return pl.pallas_call(
        functools.partial(_ssd_kernel, TB=TB, nb=nb, has_h0=h0 is not None, want_final=want_final),
        grid=(B, N_DIR, nb),
        in_specs=in_specs,
        out_specs=tuple(out_specs),
        out_shape=tuple(out_shape),
        scratch_shapes=[pltpu.VMEM((H, SSD_STATE, SSD_HEADDIM), F32), pltpu.VMEM((TB, SSD_CONV_CH), F32),
                        pltpu.VMEM((TB, DT_PAD), F32)],
        compiler_params=_cparams(("parallel", "parallel", "arbitrary")),
        name="ssd_scan",
    )(*args)


def _rwkv_prep_kernel(rkv_ref, rp_ref, rn_ref, lo_ref, mu_ref, a0_ref, aup_ref, gup_ref, w0_ref, wup_ref,
                      kkw_ref, ka_ref, rk_ref, seg_ref,
                      r_ref, w0o_ref, w1o_ref, k_ref, v_ref, kk_ref, b_ref, g_ref, bonus_ref, *, nbs):
    tm = rkv_ref.shape[0]
    W = RWKV_WIDTH
    jj = pl.program_id(0) % nbs
    x = rkv_ref[...]
    prev_row = rp_ref[7:8, :] * (jj > 0).astype(F32)
    next_row = rn_ref[0:1, :] * (jj < nbs - 1).astype(F32)
    rows = lax.broadcasted_iota(jnp.int32, (tm, 1), 0)
    xp = jnp.where(rows == 0, prev_row, pltpu.roll(x, 1, 0))
    xn = jnp.where(rows == tm - 1, next_row, pltpu.roll(x, tm - 1, 0))
    x = x + mu_ref[0:1, :] * (xp - x) + mu_ref[1:2, :] * (xn - x)
    r = x[:, 0:W]
    k = x[:, W:2 * W]
    v = x[:, 2 * W:3 * W]
    lo = lo_ref[...]
    seg = seg_ref[...]
    hdot = functools.partial(jnp.dot, preferred_element_type=F32, precision=HIGHEST)
    a = _sigmoid(a0_ref[...] + hdot(lo, aup_ref[...]))
    g_ref[...] = hdot(_sigmoid(lo), gup_ref[...])
    tlo = jnp.tanh(lo)
    for d, o_ref in enumerate((w0o_ref, w1o_ref)):
        zw = w0_ref[d:d + 1, :] + hdot(tlo, wup_ref[d])
        o_ref[...] = jnp.exp(-DECAY_SCALE * _sigmoid(zw))
    kk = k * kkw_ref[...]
    kk = kk * lax.rsqrt(jnp.maximum(hdot(kk * kk, seg), 1e-24))
    k2 = k * (1.0 + (a - 1.0) * ka_ref[...])
    r_ref[...] = r
    k_ref[...] = k2
    v_ref[...] = v
    kk_ref[...] = kk
    b_ref[...] = kk * a
    bonus_ref[...] = hdot(r * k2 * rk_ref[...], seg) * v


def _segment_ones(width, seg):
    i = jnp.arange(width)
    return (i[:, None] // seg == i[None, :] // seg).astype(F32)


def _rwkv_prep(rkv, lo, lp, seq_blocks):
    ntok = rkv.shape[0]
    tm = TOKEN_BLOCK
    W = RWKV_WIDTH
    nrow8 = ntok // 8
    nbs_of = seq_blocks
    pad_rows = lambda t, r0: jnp.zeros((LO_PAD, W), F32).at[r0:r0 + t.shape[0]].set(t)
    aup = pad_rows(lp['rwkv_a_up'], N_DIR * W_RANK)
    gup = pad_rows(lp['rwkv_g_up'], N_DIR * W_RANK + A_RANK)
    wup = jnp.stack([pad_rows(lp['rwkv_w_up'][d], d * W_RANK) for d in range(N_DIR)])
    row = lambda t: t.reshape(1, W)
    outs = tuple(jax.ShapeDtypeStruct((ntok, W), F32) for _ in range(9))
    return pl.pallas_call(
        functools.partial(_rwkv_prep_kernel, nbs=nbs_of),
        grid=(ntok // tm,),
        in_specs=[
            pl.BlockSpec((tm, 3 * W), lambda i: (i, 0)),
            pl.BlockSpec((8, 3 * W), lambda i: (jnp.maximum(i * (tm // 8) - 1, 0), 0)),
            pl.BlockSpec((8, 3 * W), lambda i: (jnp.minimum((i + 1) * (tm // 8), nrow8 - 1), 0)),
            pl.BlockSpec((tm, LO_PAD), lambda i: (i, 0)),
            _full((2, 3 * W)), _full((1, W)), _full((LO_PAD, W)), _full((LO_PAD, W)), _full((2, W)),
            _full((N_DIR, LO_PAD, W)), _full((1, W)), _full((1, W)), _full((1, W)), _full((W, W)),
        ],
        out_specs=tuple(pl.BlockSpec((tm, W), lambda i: (i, 0)) for _ in range(9)),
        out_shape=outs,
        compiler_params=_cparams(("parallel",)),
        name="rwkv_prep",
    )(rkv, rkv, rkv, lo, lp['rwkv_mu'], row(lp['rwkv_a0']), aup, gup, lp['rwkv_w0'], wup,
      row(lp['rwkv_k_k']), row(lp['rwkv_k_a']), row(lp['rwkv_r_k']), _segment_ones(W, RWKV_HEADSIZE))


def _rwkv_scan_kernel(r_ref, w_ref, k_ref, kk_ref, b_ref, v_ref, s0_ref, y_ref, sfin_ref, S, *, TBLK, V, nblk):
    K = RWKV_HEADSIZE
    tb = pl.program_id(1)

    @pl.when(tb == 0)
    def _():
        S[...] = s0_ref[...]

    def bc(ref, t, kx):
        return jnp.broadcast_to(ref[t, pl.ds(kx, 1), :], (V, LANES))

    NACC = 4

    def step(t, _):
        accs = [None] * NACC
        for kx in range(K):
            p = S[kx] * bc(kk_ref, t, kx)
            a = kx % NACC
            accs[a] = p if accs[a] is None else accs[a] + p
        sa = (accs[0] + accs[1]) + (accs[2] + accs[3])
        vt = v_ref[t]
        yacc = [None] * NACC
        for kx in range(K):
            s_new = S[kx] * bc(w_ref, t, kx) - sa * bc(b_ref, t, kx) + vt * bc(k_ref, t, kx)
            S[kx] = s_new
            p = s_new * bc(r_ref, t, kx)
            a = kx % NACC
            yacc[a] = p if yacc[a] is None else yacc[a] + p
        y_ref[t] = (yacc[0] + yacc[1]) + (yacc[2] + yacc[3])
        return 0
    lax.fori_loop(0, TBLK, step, 0)

    @pl.when(tb == nblk - 1)
    def _():
        sfin_ref[...] = S[...]


def _rwkv_scan(r, w, k, kk, b, v, s0):
    L, K, NL = r.shape
    V = v.shape[1]
    TBLK = min(64, L)
    nblk = L // TBLK
    ngrp = NL // LANES
    kspec = pl.BlockSpec((TBLK, K, LANES), lambda g, t: (t, 0, g))
    vspec = pl.BlockSpec((TBLK, V, LANES), lambda g, t: (t, 0, g))
    sspec = pl.BlockSpec((K, V, LANES), lambda g, t: (0, 0, g))
    return pl.pallas_call(
        functools.partial(_rwkv_scan_kernel, TBLK=TBLK, V=V, nblk=nblk),
        grid=(ngrp, nblk),
        in_specs=[kspec, kspec, kspec, kspec, kspec, vspec, sspec],
        out_specs=(vspec, sspec),
        out_shape=(jax.ShapeDtypeStruct((L, V, NL), F32), jax.ShapeDtypeStruct((K, V, NL), F32)),
        scratch_shapes=[pltpu.VMEM((K, V, LANES), F32)],
        compiler_params=_cparams(("parallel", "arbitrary")),
        name="rwkv_scan",
    )(r, w, k, kk, b, v, s0)


def _rwkv_layout(B, vsplit):
    chains = N_DIR * B * RWKV_HEADS
    cp = -(-chains // (LANES // vsplit)) * (LANES // vsplit)
    return chains, cp


def _to_scan_k(t_fwd, t_bwd, B, L, vsplit):
    chains, cp = _rwkv_layout(B, vsplit)

    def one(t):
        return t.reshape(B, L, RWKV_HEADS, RWKV_HEADSIZE).transpose(1, 3, 0, 2).reshape(L, RWKV_HEADSIZE, B * RWKV_HEADS)
    x = jnp.concatenate([one(t_fwd), jnp.flip(one(t_bwd), 0)], axis=-1)
    x = jnp.pad(x, ((0, 0), (0, 0), (0, cp - chains)))
    return jnp.tile(x, (1, 1, vsplit))


def _to_scan_v(t, B, L, vsplit):
    chains, cp = _rwkv_layout(B, vsplit)
    x = t.reshape(B, L, RWKV_HEADS, RWKV_HEADSIZE).transpose(1, 3, 0, 2).reshape(L, RWKV_HEADSIZE, B * RWKV_HEADS)
    x = jnp.concatenate([x, jnp.flip(x, 0)], axis=-1)
    x = jnp.pad(x, ((0, 0), (0, 0), (0, cp - chains)))
    vs = RWKV_HEADSIZE // vsplit
    return x.reshape(L, vsplit, vs, cp).transpose(0, 2, 1, 3).reshape(L, vs, vsplit * cp)


def _from_scan_v(y, B, L, vsplit):
    chains, cp = _rwkv_layout(B, vsplit)
    vs = RWKV_HEADSIZE // vsplit
    y = y.reshape(L, vs, vsplit, cp).transpose(0, 2, 1, 3).reshape(L, RWKV_HEADSIZE, cp)[:, :, :chains]
    half = chains // 2

    def back(t):
        return t.reshape(L, RWKV_HEADSIZE, B, RWKV_HEADS).transpose(2, 0, 3, 1).reshape(B * L, RWKV_WIDTH)
    return back(y[:, :, :half]), back(jnp.flip(y[:, :, half:], 0))


def _state_to_scan(s, vsplit):
    B = s.shape[0]
    chains, cp = _rwkv_layout(B, vsplit)
    x = s.transpose(4, 3, 1, 0, 2).reshape(RWKV_HEADSIZE, RWKV_HEADSIZE, chains)
    x = jnp.pad(x, ((0, 0), (0, 0), (0, cp - chains)))
    vs = RWKV_HEADSIZE // vsplit
    return x.reshape(RWKV_HEADSIZE, vsplit, vs, cp).transpose(0, 2, 1, 3).reshape(RWKV_HEADSIZE, vs, vsplit * cp)


def _state_from_scan(x, B, vsplit):
    chains, cp = _rwkv_layout(B, vsplit)
    vs = RWKV_HEADSIZE // vsplit
    x = x.reshape(RWKV_HEADSIZE, vs, vsplit, cp).transpose(0, 2, 1, 3).reshape(RWKV_HEADSIZE, RWKV_HEADSIZE, cp)
    x = x[:, :, :chains].reshape(RWKV_HEADSIZE, RWKV_HEADSIZE, N_DIR, B, RWKV_HEADS)
    return x.transpose(3, 2, 4, 1, 0)


def _post_kernel(x_ref, mod_ref, ys5_ref, yf_ref, yb_ref, z_ref, rf_ref, rb_ref, bonus_ref, g_ref,
                 gluw_ref, glub_ref, ssdg_ref, lng_ref, lnb_ref, seg_ref, wout_ref, n2g_ref, rw_ref,
                 x1_ref, hb_ref, aff_ref):
    m = mod_ref[...]
    D = D_MODEL
    zg = _gelu_tanh(ys5_ref[...])
    gate = jnp.dot(zg.astype(BF16), gluw_ref[...], preferred_element_type=F32) + glub_ref[...]
    y_a = zg * _sigmoid(gate)
    z = z_ref[...]
    yb = (yf_ref[...] + yb_ref[...]) * (z * _sigmoid(z))
    y_b = yb * lax.rsqrt(jnp.mean(yb * yb, axis=-1, keepdims=True) + EPS) * ssdg_ref[...]
    seg = seg_ref[...] * (1.0 / RWKV_HEADSIZE)
    hdot = functools.partial(jnp.dot, preferred_element_type=F32, precision=HIGHEST)
    yr = rf_ref[...] + rb_ref[...]
    mean = hdot(yr, seg)
    cen = yr - mean
    var = hdot(cen * cen, seg)
    yn = cen * lax.rsqrt(var + GN_EPS) * lng_ref[...] + lnb_ref[...]
    y_c = (yn + bonus_ref[...]) * g_ref[...]
    o = jnp.dot(y_a.astype(BF16), wout_ref[0:S5_WIDTH, :], preferred_element_type=F32)
    o = o + jnp.dot(y_b.astype(BF16), wout_ref[S5_WIDTH:S5_WIDTH + SSD_WIDTH, :], preferred_element_type=F32)
    o = o + jnp.dot(y_c.astype(BF16), wout_ref[S5_WIDTH + SSD_WIDTH:, :], preferred_element_type=F32)
    x1 = x_ref[...] + m[:, 2 * D:3 * D] * o
    x1_ref[...] = x1
    h2 = x1 * lax.rsqrt(jnp.mean(x1 * x1, axis=-1, keepdims=True) + EPS) * n2g_ref[...]
    h2 = h2 * (1.0 + m[:, 4 * D:5 * D]) + m[:, 3 * D:4 * D]
    hb_ref[...] = h2.astype(BF16)
    logits = lax.dot_general(rw_ref[...], h2, (((1,), (1,)), ((), ())), preferred_element_type=F32,
                             precision=HIGHEST)
    mx = jnp.max(logits, axis=0, keepdims=True)
    ex = jnp.exp(logits - mx)
    aff_ref[...] = ex / jnp.sum(ex, axis=0, keepdims=True)


def _post_mixer(x, mod3, row_of_block, ys5, yssd, z, rf, rb, bonus, g, lp):
    ntok = x.shape[0]
    tm = TOKEN_BLOCK
    tok = lambda w: pl.BlockSpec((tm, w), lambda i: (i, 0))
    row = lambda t: t.reshape(1, -1)
    W = RWKV_WIDTH
    return pl.pallas_call(
        _post_kernel,
        grid=(ntok // tm,),
        in_specs=[
            tok(D_MODEL),
            pl.BlockSpec((None, 1, 6 * D_MODEL), lambda i: (row_of_block(i), 0, 0)),
            tok(S5_WIDTH),
            pl.BlockSpec((None, tm, SSD_WIDTH), lambda i: (0, i, 0)),
            pl.BlockSpec((None, tm, SSD_WIDTH), lambda i: (1, i, 0)),
            tok(SSD_WIDTH), tok(W), tok(W), tok(W), tok(W),
            _full((S5_WIDTH, S5_WIDTH)), _full((1, S5_WIDTH)), _full((1, SSD_WIDTH)), _full((1, W)), _full((1, W)),
            _full((W, W)), _full((D_MODEL, D_MODEL)), _full((1, D_MODEL)), _full((N_EXPERTS, D_MODEL)),
        ],
        out_specs=(tok(D_MODEL), tok(D_MODEL), pl.BlockSpec((N_EXPERTS, tm), lambda i: (0, i))),
        out_shape=(jax.ShapeDtypeStruct((ntok, D_MODEL), F32), jax.ShapeDtypeStruct((ntok, D_MODEL), BF16),
                   jax.ShapeDtypeStruct((N_EXPERTS, ntok), F32)),
        compiler_params=_cparams(("parallel",)),
        name="post_mixer",
    )(x, mod3, ys5, yssd, yssd, z, rf, rb, bonus, g,
      lp['s5_glu_w'].astype(BF16), row(lp['s5_glu_b']), row(lp['ssd_norm_g']), row(lp['rwkv_ln_g']),
      row(lp['rwkv_ln_b']), _segment_ones(W, RWKV_HEADSIZE), lp['w_out'].astype(BF16), row(lp['norm2_g']),
      lp['router_w'].T)


def _select_kernel(aff_ref, slot_ref, *, n, cap):
    E = N_EXPERTS
    a = aff_ref[...]
    bits = pltpu.bitcast(a, jnp.int32)
    thr = jnp.zeros((E, 1), jnp.int32)
    capf = float(cap)
    for bit in range(30, -1, -1):
        cand = thr | (1 << bit)
        cnt = jnp.sum(jnp.where(bits >= cand, 1.0, 0.0), axis=1, keepdims=True)
        thr = jnp.where(cnt >= capf, cand, thr)
    gt = bits > thr
    eq = bits == thr
    need = capf - jnp.sum(jnp.where(gt, 1.0, 0.0), axis=1, keepdims=True)
    CW = min(256, n)
    ui = lax.broadcasted_iota(jnp.int32, (CW, CW), 0)
    uj = lax.broadcasted_iota(jnp.int32, (CW, CW), 1)
    upper = jnp.where(ui < uj, 1.0, 0.0).astype(BF16)

    def excl_cumsum(mask_f):
        outs = []
        off = jnp.zeros((E, 1), F32)
        for c in range(n // CW):
            mc = mask_f[:, c * CW:(c + 1) * CW]
            outs.append(jnp.dot(mc.astype(BF16), upper, preferred_element_type=F32) + off)
            off = off + jnp.sum(mc, axis=1, keepdims=True)
        return jnp.concatenate(outs, axis=1)

    eq_rank = excl_cumsum(jnp.where(eq, 1.0, 0.0))
    sel = jnp.where(gt, 1.0, jnp.where(eq, jnp.where(eq_rank < need, 1.0, 0.0), 0.0))
    pos = excl_cumsum(sel)
    slot_ref[...] = jnp.where(sel > 0.0, pos, -1.0).astype(jnp.int32)


def _select(aff, B, n, cap):
    return pl.pallas_call(
        functools.partial(_select_kernel, n=n, cap=cap),
        grid=(B,),
        in_specs=[pl.BlockSpec((N_EXPERTS, n), lambda b: (0, b))],
        out_specs=pl.BlockSpec((None, N_EXPERTS, n), lambda b: (b, 0, 0)),
        out_shape=jax.ShapeDtypeStruct((B, N_EXPERTS, n), jnp.int32),
        compiler_params=_cparams(("parallel",)),
        name="ec_select",
    )(aff)


def _gather_kernel(hb_ref, slot_ref, aff_ref, xs_ref, gate_ref, *, n, cap):
    NC = min(512, n)
    srow = lax.broadcasted_iota(jnp.int32, (cap, NC), 0)
    acc = jnp.zeros((cap, D_MODEL), F32)
    gacc = jnp.zeros((cap, 1), F32)
    for c in range(n // NC):
        hit = slot_ref[:, c * NC:(c + 1) * NC] == srow
        oh = jnp.where(hit, 1.0, 0.0).astype(BF16)
        acc = acc + jnp.dot(oh, hb_ref[c * NC:(c + 1) * NC, :], preferred_element_type=F32)
        gacc = gacc + jnp.sum(jnp.where(hit, aff_ref[:, c * NC:(c + 1) * NC], 0.0), axis=1, keepdims=True)
    xs_ref[...] = acc.astype(BF16)
    gate_ref[...] = gacc


def _gather(hb, slot4, aff4, B, n, cap):
    E = N_EXPERTS
    return pl.pallas_call(
        functools.partial(_gather_kernel, n=n, cap=cap),
        grid=(B, E),
        in_specs=[
            pl.BlockSpec((n, D_MODEL), lambda b, e: (b, 0)),
            pl.BlockSpec((None, None, 1, n), lambda b, e: (b, e, 0, 0)),
            pl.BlockSpec((None, None, 1, n), lambda b, e: (b, e, 0, 0)),
        ],
        out_specs=(pl.BlockSpec((None, None, cap, D_MODEL), lambda b, e: (b, e, 0, 0)),
                   pl.BlockSpec((None, None, cap, 1), lambda b, e: (b, e, 0, 0))),
        out_shape=(jax.ShapeDtypeStruct((B, E, cap, D_MODEL), BF16), jax.ShapeDtypeStruct((B, E, cap, 1), F32)),
        compiler_params=_cparams(("parallel", "arbitrary")),
        name="ec_gather",
    )(hb, slot4, aff4)


def _ffn_kernel(xs_ref, gate_ref, w1_ref, w3_ref, w2_ref, o_ref):
    bg, cap, _ = xs_ref.shape
    x = xs_ref[...].reshape(bg * cap, D_MODEL)
    h1 = jnp.dot(x, w1_ref[...], preferred_element_type=F32)
    h3 = jnp.dot(x, w3_ref[...], preferred_element_type=F32)
    hid = (h1 * _sigmoid(h1) * h3).astype(BF16)
    o = jnp.dot(hid, w2_ref[...], preferred_element_type=F32) * gate_ref[...].reshape(bg * cap, 1)
    o_ref[...] = o.astype(BF16).reshape(bg, cap, D_MODEL)


def _expert_ffn(xs, gate, w1, w3, w2, bg):
    B, E, cap, _ = xs.shape
    return pl.pallas_call(
        _ffn_kernel,
        grid=(E, B // bg),
        in_specs=[
            pl.BlockSpec((bg, None, cap, D_MODEL), lambda e, b: (b, e, 0, 0)),
            pl.BlockSpec((bg, None, cap, 1), lambda e, b: (b, e, 0, 0)),
            pl.BlockSpec((None, D_MODEL, D_EXPERT), lambda e, b: (e, 0, 0)),
            pl.BlockSpec((None, D_MODEL, D_EXPERT), lambda e, b: (e, 0, 0)),
            pl.BlockSpec((None, D_EXPERT, D_MODEL), lambda e, b: (e, 0, 0)),
        ],
        out_specs=pl.BlockSpec((bg, None, cap, D_MODEL), lambda e, b: (b, e, 0, 0)),
        out_shape=jax.ShapeDtypeStruct((B, E, cap, D_MODEL), BF16),
        compiler_params=_cparams(("parallel", "parallel")),
        name="ec_ffn",
    )(xs, gate, w1, w3, w2)


def _scatter_kernel(slot_ref, o_ref, x1_ref, mod_ref, fg_ref, out_ref, acc, *, cap, final):
    e = pl.program_id(2)
    tn = x1_ref.shape[0]

    @pl.when(e == 0)
    def _():
        acc[...] = jnp.zeros_like(acc)

    lane = lax.broadcasted_iota(jnp.int32, (tn, LANES), 1)
    col = jnp.sum(jnp.where(lane == e, slot_ref[...].astype(F32), 0.0), axis=1, keepdims=True)
    scol = lax.broadcasted_iota(jnp.int32, (tn, cap), 1).astype(F32)
    oh = jnp.where(col == scol, 1.0, 0.0).astype(BF16)
    acc[...] += jnp.dot(oh, o_ref[...], preferred_element_type=F32)

    @pl.when(e == N_EXPERTS - 1)
    def _():
        x2 = x1_ref[...] + mod_ref[:, 5 * D_MODEL:6 * D_MODEL] * acc[...]
        if final:
            x2 = x2 * lax.rsqrt(jnp.mean(x2 * x2, axis=-1, keepdims=True) + EPS) * fg_ref[...]
        out_ref[...] = x2


def _scatter(slot_t, o, x1, mod3, mod_row, final_g, B, n, cap, final):
    tn = min(1024, n)
    nt = n // tn
    return pl.pallas_call(
        functools.partial(_scatter_kernel, cap=cap, final=final),
        grid=(B, nt, N_EXPERTS),
        in_specs=[
            pl.BlockSpec((tn, LANES), lambda b, t, e: (b * nt + t, 0)),
            pl.BlockSpec((None, None, cap, D_MODEL), lambda b, t, e: (b, e, 0, 0)),
            pl.BlockSpec((tn, D_MODEL), lambda b, t, e: (b * nt + t, 0)),
            pl.BlockSpec((None, 1, 6 * D_MODEL), lambda b, t, e: (mod_row(b), 0, 0)),
            _full((1, D_MODEL)),
        ],
        out_specs=pl.BlockSpec((tn, D_MODEL), lambda b, t, e: (b * nt + t, 0)),
        out_shape=jax.ShapeDtypeStruct((B * n, D_MODEL), F32),
        scratch_shapes=[pltpu.VMEM((tn, D_MODEL), F32)],
        compiler_params=_cparams(("parallel", "parallel", "arbitrary")),
        name="ec_scatter",
    )(slot_t, o, x1, mod3, final_g.reshape(1, D_MODEL))


def _expert_choice(hb, aff, x1, mod3, mod_row, final_g, w1, w3, w2, B, n, final):
    cap = EC_FACTOR * n // N_EXPERTS
    slot = _select(aff, B, n, cap)
    aff4 = aff.reshape(N_EXPERTS, B, 1, n).transpose(1, 0, 2, 3)
    xs, gate = _gather(hb, slot.reshape(B, N_EXPERTS, 1, n), aff4, B, n, cap)
    bg = max(1, min(B, 256 // cap))
    o = _expert_ffn(xs, gate, w1, w3, w2, bg)
    slot_t = jnp.pad(slot.transpose(0, 2, 1).reshape(B * n, N_EXPERTS), ((0, 0), (0, LANES - N_EXPERTS)),
                     constant_values=-1)
    return _scatter(slot_t, o, x1, mod3, mod_row, final_g, B, n, cap, final)


def kernel(x_prompt, x_sample, c, state_s5_re, state_s5_im, state_ssd, state_rwkv, c_ctx, ada_w, ada_b, norm1_g, norm2_g, w_in, w_out, s5_a_re, s5_a_im, s5_log_dt, s5_b_re, s5_b_im, s5_c_re, s5_c_im, s5_d, s5_glu_w, s5_glu_b, ssd_conv_w, ssd_conv_b, ssd_a_log, ssd_dt_bias, ssd_d, ssd_norm_g, rwkv_mu, rwkv_w0, rwkv_w_up, rwkv_a0, rwkv_a_up, rwkv_g_up, rwkv_k_k, rwkv_k_a, rwkv_r_k, rwkv_ln_g, rwkv_ln_b, router_w, exp_w1, exp_w3, exp_w2, final_g):
    Bp, Lp, D = x_prompt.shape
    Bs, Ls, _ = x_sample.shape
    depth = ada_w.shape[0]
    Np, Ns = Bp * Lp, Bs * Ls
    tm = TOKEN_BLOCK
    grid_rows = Ls // GRID_W
    nfs = S5_GROUPS // 8
    assert Lp % tm == 0 and Ls % tm == 0 and Lp % SSD_CHUNK == 0

    n_rows = 1 + Bs
    rows_pad = -(-n_rows // 8) * 8
    cond = jnp.zeros((rows_pad, D), F32).at[0].set(c_ctx).at[1:n_rows].set(c)
    mod = _modulation(cond, ada_w, ada_b)
    p_blocks = Np // tm
    s_blocks_per_req = Ls // tm

    def row_of_block(i):
        return jnp.where(i < p_blocks, 0, 1 + (i - p_blocks) // s_blocks_per_req)

    ab_re, ab_im, bb_re, bb_im = _s5_discretize(s5_a_re, s5_a_im, s5_log_dt, s5_b_re, s5_b_im)

    x = jnp.concatenate([x_prompt.reshape(Np, D), x_sample.reshape(Ns, D)], axis=0)
    new_s5_re, new_s5_im, new_ssd, new_rwkv = [], [], [], []
    QP = 2 if Bp % 2 == 0 and Bp >= 2 else 1
    RP = Bp // QP
    vs_p, vs_s = 1, 2

    for l in range(depth):
        lp = {
            'rwkv_mu': rwkv_mu[l], 'rwkv_w0': rwkv_w0[l], 'rwkv_w_up': rwkv_w_up[l], 'rwkv_a0': rwkv_a0[l],
            'rwkv_a_up': rwkv_a_up[l], 'rwkv_g_up': rwkv_g_up[l], 'rwkv_k_k': rwkv_k_k[l], 'rwkv_k_a': rwkv_k_a[l],
            'rwkv_r_k': rwkv_r_k[l].reshape(-1), 'rwkv_ln_g': rwkv_ln_g[l], 'rwkv_ln_b': rwkv_ln_b[l],
            's5_glu_w': s5_glu_w[l], 's5_glu_b': s5_glu_b[l], 'ssd_norm_g': ssd_norm_g[l], 'w_out': w_out[l],
            'norm2_g': norm2_g[l], 'router_w': router_w[l],
        }
        mod3 = mod[l].reshape(rows_pad, 1, 6 * D)
        us5, z, xbc, dt, rkv, lo = _in_projection(x, mod3, row_of_block, norm1_g[l], _pad_in_weight(w_in[l]))

        tables = _s5_layer_tables(ab_re[l], ab_im[l], bb_re[l], bb_im[l], s5_c_re[l], s5_c_im[l])
        d_row = s5_d[l].reshape(1, S5_WIDTH)
        up = us5[:Np].reshape(QP, RP, Lp, S5_WIDTH).transpose(0, 2, 1, 3).reshape(QP, Lp * RP, S5_WIDTH)
        yp, hfin = _s5_scan(up, None, tables, d_row, R=RP, n_slab=Lp, chained=False, want_final=True)
        yp = yp.reshape(QP, Lp, RP, S5_WIDTH).transpose(0, 2, 1, 3).reshape(Np, S5_WIDTH)
        hf = hfin.transpose(0, 3, 1, 2, 4).reshape(Bp, N_DIR, nfs, 2, 8, S5_STATE)
        new_s5_re.append(hf[:, :, :, 0].reshape(Bp, N_DIR, S5_GROUPS, S5_STATE))
        new_s5_im.append(hf[:, :, :, 1].reshape(Bp, N_DIR, S5_GROUPS, S5_STATE))
        h0 = jnp.concatenate([state_s5_re[:, l].reshape(Bs, N_DIR, nfs, 1, S5_SLICE_ST),
                              state_s5_im[:, l].reshape(Bs, N_DIR, nfs, 1, S5_SLICE_ST)], axis=-1)
        (ysm,) = _s5_scan(us5[Np:].reshape(Bs, Ls, S5_WIDTH), h0, tables, d_row, R=GRID_W, n_slab=grid_rows,
                          chained=True, want_final=False)
        ys5 = jnp.concatenate([yp, ysm.reshape(Ns, S5_WIDTH)], axis=0)

        ssd_args = (ssd_conv_w[l], ssd_conv_b[l], ssd_dt_bias[l], ssd_a_log[l], ssd_d[l])
        yssd_p, hssd = _ssd_scan(xbc[:Np], dt[:Np], None, *ssd_args, B=Bp, L=Lp, want_final=True)
        new_ssd.append(hssd.transpose(0, 1, 2, 4, 3))
        (yssd_s,) = _ssd_scan(xbc[Np:], dt[Np:], state_ssd[:, l].transpose(0, 1, 2, 4, 3), *ssd_args,
                              B=Bs, L=Ls, want_final=False)
        yssd = jnp.concatenate([yssd_p, yssd_s], axis=1)

        prep_p = _rwkv_prep(rkv[:Np], lo[:Np], lp, Lp // tm)
        prep_s = _rwkv_prep(rkv[Np:], lo[Np:], lp, Ls // tm)
        rw_f, rw_b = [], []
        zero_state = jnp.zeros((Bp, N_DIR, RWKV_HEADS, RWKV_HEADSIZE, RWKV_HEADSIZE), F32)
        for is_prompt, prep, B_, L_, vs, s0 in ((True, prep_p, Bp, Lp, vs_p, zero_state),
                                                (False, prep_s, Bs, Ls, vs_s, state_rwkv[:, l])):
            r_, w0_, w1_, k_, v_, kk_, b_, _, _ = prep
            y_, sfin = _rwkv_scan(_to_scan_k(r_, r_, B_, L_, vs), _to_scan_k(w0_, w1_, B_, L_, vs),
                                  _to_scan_k(k_, k_, B_, L_, vs), _to_scan_k(kk_, kk_, B_, L_, vs),
                                  _to_scan_k(b_, b_, B_, L_, vs), _to_scan_v(v_, B_, L_, vs),
                                  _state_to_scan(s0, vs))
            yf_, yb_ = _from_scan_v(y_, B_, L_, vs)
            rw_f.append(yf_)
            rw_b.append(yb_)
            if is_prompt:
                new_rwkv.append(_state_from_scan(sfin, B_, vs))
        rf = jnp.concatenate(rw_f, axis=0)
        rb = jnp.concatenate(rw_b, axis=0)
        g_all = jnp.concatenate([prep_p[7], prep_s[7]], axis=0)
        bonus = jnp.concatenate([prep_p[8], prep_s[8]], axis=0)

        x1, hb, aff = _post_mixer(x, mod3, row_of_block, ys5, yssd, z, rf, rb, bonus, g_all, lp)

        final = l == depth - 1
        w1b, w3b, w2b = exp_w1[l].astype(BF16), exp_w3[l].astype(BF16), exp_w2[l].astype(BF16)
        xp2 = _expert_choice(hb[:Np], aff[:, :Np], x1[:Np], mod3, lambda b: 0, final_g, w1b, w3b, w2b, Bp, Lp, final)
        xs2 = _expert_choice(hb[Np:], aff[:, Np:], x1[Np:], mod3, lambda b: 1 + b, final_g, w1b, w3b, w2b, Bs, Ls, final)
        x = jnp.concatenate([xp2, xs2], axis=0)

    y_prompt = x[:Np].reshape(Bp, Lp, D)
    y_sample = x[Np:].reshape(Bs, Ls, D)
    return (y_prompt, y_sample, jnp.stack(new_s5_re, axis=1), jnp.stack(new_s5_im, axis=1),
            jnp.stack(new_ssd, axis=1), jnp.stack(new_rwkv, axis=1))
```

```python
import functools
import math

import jax
import jax.numpy as jnp
from jax import lax
from jax.experimental import pallas as pl
from jax.experimental.pallas import tpu as pltpu

F32 = jnp.float32
BF16 = jnp.bfloat16
HIGHEST = lax.Precision.HIGHEST

D_MODEL = 1024
GRID_W = 64
N_DIR = 2
EPS = 1e-6
S5_WIDTH = 256
S5_CH = 16
S5_GROUPS = 16
S5_STATE = 64
SSD_HEADDIM = 64
SSD_HEADS = 6
SSD_WIDTH = 384
SSD_GROUPS = 2
SSD_STATE = 64
SSD_BC = 128
SSD_CONV_CH = 640
SSD_CHUNK = 128
RWKV_HEADSIZE = 64
RWKV_HEADS = 6
RWKV_WIDTH = 384
W_RANK = 32
A_RANK = 32
G_RANK = 64
DECAY_SCALE = math.exp(-0.5)
GN_EPS = 64e-5
N_EXPERTS = 16
D_EXPERT = 512
EC_FACTOR = 2
IN_SIZES = (S5_WIDTH, SSD_WIDTH, SSD_CONV_CH, N_DIR * SSD_HEADS, 3 * RWKV_WIDTH, N_DIR * W_RANK, A_RANK, G_RANK)

LANES = 128
TOKEN_BLOCK = 256
DT_PAD = 128
LO_PAD = 256
S5_SLICE_CH = 128
S5_SLICE_ST = 512
VMEM_LIMIT = 56 * 1024 * 1024


def _cparams(sem):
    return pltpu.CompilerParams(dimension_semantics=sem, vmem_limit_bytes=VMEM_LIMIT)


def _full(shape):
    nd = len(shape)
    return pl.BlockSpec(shape, lambda *_: (0,) * nd)


def _sigmoid(x):
    return 1.0 / (1.0 + jnp.exp(-x))


def _softplus(x):
    return jnp.maximum(x, 0.0) + jnp.log1p(jnp.exp(-jnp.abs(x)))


def _split_bf16(x):
    hi = x.astype(BF16)
    lo = (x - hi.astype(F32)).astype(BF16)
    return hi, lo


def _dot_exact_rhs(x, m, dims=None):
    hi, lo = _split_bf16(x)
    mb = m.astype(BF16)
    if dims is None:
        return jnp.dot(hi, mb, preferred_element_type=F32) + jnp.dot(lo, mb, preferred_element_type=F32)
    return (lax.dot_general(hi, mb, dims, preferred_element_type=F32)
            + lax.dot_general(lo, mb, dims, preferred_element_type=F32))


def _dot_exact_lhs(m, x):
    hi, lo = _split_bf16(x)
    mb = m.astype(BF16)
    return jnp.dot(mb, hi, preferred_element_type=F32) + jnp.dot(mb, lo, preferred_element_type=F32)


def _dot3(x, w, dims=(((1,), (0,)), ((), ()))):
    xh, xl = _split_bf16(x)
    wh, wl = _split_bf16(w)
    dg = functools.partial(lax.dot_general, dimension_numbers=dims, preferred_element_type=F32)
    return dg(xh, wh) + (dg(xl, wh) + dg(xh, wl))


def _gelu_tanh(x):
    return 0.5 * x * (1.0 + jnp.tanh(math.sqrt(2.0 / math.pi) * (x + 0.044715 * (x * x * x))))


def _mod_kernel(c_ref, w_ref, b_ref, o_ref):
    c = c_ref[...]
    s = (c * _sigmoid(c)).astype(BF16)
    o_ref[...] = jnp.dot(s, w_ref[...].astype(BF16), preferred_element_type=F32) + b_ref[...]


def _modulation(cond, ada_w, ada_b):
    depth = ada_w.shape[0]
    rows = cond.shape[0]
    tn = 1536
    return pl.pallas_call(
        _mod_kernel,
        grid=(depth, 6 * D_MODEL // tn),
        in_specs=[
            pl.BlockSpec((rows, D_MODEL), lambda l, j: (0, 0)),
            pl.BlockSpec((None, D_MODEL, tn), lambda l, j: (l, 0, j)),
            pl.BlockSpec((None, 1, tn), lambda l, j: (l, 0, j)),
        ],
        out_specs=pl.BlockSpec((None, rows, tn), lambda l, j: (l, 0, j)),
        out_shape=jax.ShapeDtypeStruct((depth, rows, 6 * D_MODEL), F32),
        compiler_params=_cparams(("parallel", "parallel")),
        name="adaln_mod",
    )(cond, ada_w, ada_b.reshape(depth, 1, 6 * D_MODEL))


IN_PAD_SIZES = (S5_WIDTH, SSD_WIDTH, SSD_CONV_CH, DT_PAD, 3 * RWKV_WIDTH, LO_PAD)


def _inproj_kernel(x_ref, mod_ref, g_ref, w_ref, us5_ref, z_ref, xbc_ref, dt_ref, rkv_ref, lo_ref):
    x = x_ref[...]
    y = x * lax.rsqrt(jnp.mean(x * x, axis=-1, keepdims=True) + EPS) * g_ref[...]
    m = mod_ref[...]
    h = y * (1.0 + m[:, D_MODEL:2 * D_MODEL]) + m[:, 0:D_MODEL]
    p = jnp.dot(h.astype(BF16), w_ref[...], preferred_element_type=F32)
    start = 0
    for ref, size in zip((us5_ref, z_ref, xbc_ref, dt_ref, rkv_ref, lo_ref), IN_PAD_SIZES):
        ref[...] = p[:, start:start + size]
        start += size


def _pad_in_weight(w_in):
    parts, start = [], 0
    for s in IN_SIZES:
        parts.append(w_in[..., start:start + s])
        start += s
    us5, z, xbc, dt, rkv, wlo, alo, glo = parts
    zeros = lambda n: jnp.zeros(w_in.shape[:-1] + (n,), w_in.dtype)
    lo_used = wlo.shape[-1] + alo.shape[-1] + glo.shape[-1]
    return jnp.concatenate([us5, z, xbc, dt, zeros(DT_PAD - dt.shape[-1]), rkv, wlo, alo, glo,
                            zeros(LO_PAD - lo_used)], axis=-1).astype(BF16)


def _in_projection(x, mod3, row_of_block, norm_g, w_pad):
    ntok = x.shape[0]
    tm = TOKEN_BLOCK
    width = w_pad.shape[1]
    outs = tuple(jax.ShapeDtypeStruct((ntok, s), F32) for s in IN_PAD_SIZES)
    return pl.pallas_call(
        _inproj_kernel,
        grid=(ntok // tm,),
        in_specs=[
            pl.BlockSpec((tm, D_MODEL), lambda i: (i, 0)),
            pl.BlockSpec((None, 1, 6 * D_MODEL), lambda i: (row_of_block(i), 0, 0)),
            _full((1, D_MODEL)),
            _full((D_MODEL, width)),
        ],
        out_specs=tuple(pl.BlockSpec((tm, s), lambda i: (i, 0)) for s in IN_PAD_SIZES),
        out_shape=outs,
        compiler_params=_cparams(("parallel",)),
        name="in_proj",
    )(x, mod3, norm_g.reshape(1, D_MODEL), w_pad)


def _s5_disc_kernel(are_ref, aim_ref, ldt_ref, bre_ref, bim_ref, abre_ref, abim_ref, bbre_ref, bbim_ref):
    lam_re = jnp.minimum(are_ref[...], -1e-4)
    lam_im = aim_ref[...]
    dt = jnp.exp(ldt_ref[...])
    mag = jnp.exp(lam_re * dt)
    ab_re = mag * jnp.cos(lam_im * dt)
    ab_im = mag * jnp.sin(lam_im * dt)
    num_re, num_im = ab_re - 1.0, ab_im
    den = lam_re * lam_re + lam_im * lam_im
    q_re = (num_re * lam_re + num_im * lam_im) / den
    q_im = (num_im * lam_re - num_re * lam_im) / den
    abre_ref[...] = ab_re
    abim_ref[...] = ab_im
    b_re = bre_ref[...]
    b_im = bim_ref[...]
    qr = q_re[:, None, :]
    qi = q_im[:, None, :]
    bbre_ref[...] = qr * b_re - qi * b_im
    bbim_ref[...] = qr * b_im + qi * b_re


def _s5_discretize(a_re, a_im, log_dt, b_re, b_im):
    lead = a_re.shape[:3]
    n = lead[0] * lead[1] * lead[2]
    a2 = lambda t: t.reshape(n, S5_STATE)
    ldt = jnp.broadcast_to(log_dt.reshape(n, 1), (n, S5_STATE))
    b3 = lambda t: t.reshape(n, S5_STATE, S5_CH).transpose(0, 2, 1)
    ab_re, ab_im, bb_re, bb_im = pl.pallas_call(
        _s5_disc_kernel,
        out_shape=(jax.ShapeDtypeStruct((n, S5_STATE), F32), jax.ShapeDtypeStruct((n, S5_STATE), F32),
                   jax.ShapeDtypeStruct((n, S5_CH, S5_STATE), F32), jax.ShapeDtypeStruct((n, S5_CH, S5_STATE), F32)),
        name="s5_discretize",
    )(a2(a_re), a2(a_im), ldt, b3(b_re), b3(b_im))
    return (ab_re.reshape(lead + (S5_STATE,)), ab_im.reshape(lead + (S5_STATE,)),
            bb_re.reshape(lead + (S5_CH, S5_STATE)), bb_im.reshape(lead + (S5_CH, S5_STATE)))


def _s5_layer_tables(ab_re, ab_im, bb_re, bb_im, c_re, c_im):
    nfs = S5_GROUPS // 8
    eye = jnp.eye(8, dtype=F32)

    def rows(t):
        return t.reshape(N_DIR, nfs, 8 * S5_STATE)

    ab_row = jnp.concatenate([rows(ab_re), rows(ab_im)], axis=-1).reshape(N_DIR, nfs, 1, 2 * S5_SLICE_ST)

    def bmat(t):
        t = t.reshape(N_DIR, nfs, 8, S5_CH, S5_STATE)
        return jnp.einsum('dfghp,gk->dfghkp', t, eye).reshape(N_DIR, nfs, S5_SLICE_CH, S5_SLICE_ST)

    b_mat = jnp.concatenate([bmat(bb_re), bmat(bb_im)], axis=-1).astype(BF16)

    def cmat(t):
        t = t.reshape(N_DIR, nfs, 8, S5_CH, S5_STATE)
        return jnp.einsum('dfghp,gk->dfgpkh', t, eye).reshape(N_DIR, nfs, S5_SLICE_ST, S5_SLICE_CH)

    c_mat = jnp.concatenate([cmat(c_re), -cmat(c_im)], axis=-2).astype(BF16)
    return ab_row, b_mat, c_mat


def _s5_kernel(*refs, R, n_slab, chained, want_final):
    if chained:
        u_ref, h0_ref, ab_ref, bm_ref, cm_ref, d_ref = refs[:6]
        rest = refs[6:]
    else:
        u_ref, ab_ref, bm_ref, cm_ref, d_ref = refs[:5]
        h0_ref = None
        rest = refs[5:]
    y_ref = rest[0]
    rest = rest[1:]
    if want_final:
        hfin_ref = rest[0]
        rest = rest[1:]
    H = rest[0]
    if chained:
        PW, CIN = rest[1], rest[2]
    NR = R * n_slab
    RC = min(512, NR)
    ST = S5_SLICE_ST
    nchunk = ST // LANES

    y_ref[...] = u_ref[...] * d_ref[...]

    for d in range(N_DIR):
        def slab_of(i, d=d):
            return i if d == 0 else n_slab - 1 - i

        def bu_body(i, _, d=d):
            r0 = pl.multiple_of(i * RC, RC)
            H[pl.ds(r0, RC), :] = jnp.dot(u_ref[pl.ds(r0, RC), :].astype(BF16), bm_ref[d],
                                           preferred_element_type=F32)
            return 0
        lax.fori_loop(0, NR // RC, bu_body, 0)

        for c in range(nchunk):
            lre = slice(c * LANES, (c + 1) * LANES)
            lim = slice(ST + c * LANES, ST + (c + 1) * LANES)
            a_re = jnp.broadcast_to(ab_ref[d, :, lre], (R, LANES))
            a_im = jnp.broadcast_to(ab_ref[d, :, lim], (R, LANES))

            def step(i, carry, lre=lre, lim=lim, a_re=a_re, a_im=a_im, slab_of=slab_of):
                cr, ci = carry
                r0 = pl.multiple_of(slab_of(i) * R, R)
                nr = a_re * cr - a_im * ci + H[pl.ds(r0, R), lre]
                ni = a_re * ci + a_im * cr + H[pl.ds(r0, R), lim]
                H[pl.ds(r0, R), lre] = nr
                H[pl.ds(r0, R), lim] = ni
                return nr, ni
            zero = jnp.zeros((R, LANES), F32)
            lax.fori_loop(0, n_slab, step, (zero, zero))

        last0 = (n_slab - 1) * R if d == 0 else 0
        if chained:
            a_re_row = ab_ref[d, :, 0:ST]
            a_im_row = ab_ref[d, :, ST:2 * ST]

            def pw_step(j, carry, a_re_row=a_re_row, a_im_row=a_im_row):
                pr, pi = carry
                PW[j, :, 0:ST] = jnp.broadcast_to(pr, (8, ST))
                PW[j, :, ST:2 * ST] = jnp.broadcast_to(pi, (8, ST))
                return pr * a_re_row - pi * a_im_row, pr * a_im_row + pi * a_re_row
            lax.fori_loop(0, n_slab, pw_step, (a_re_row, a_im_row))
            t_re = PW[n_slab - 1, 0:1, 0:ST]
            t_im = PW[n_slab - 1, 0:1, ST:2 * ST]

            cr = h0_ref[d, :, 0:ST]
            ci = h0_ref[d, :, ST:2 * ST]
            for i in range(R):
                c = i if d == 0 else R - 1 - i
                CIN[c:c + 1, 0:ST] = cr
                CIN[c:c + 1, ST:2 * ST] = ci
                er = H[last0 + c:last0 + c + 1, 0:ST]
                ei = H[last0 + c:last0 + c + 1, ST:2 * ST]
                cr, ci = t_re * cr - t_im * ci + er, t_re * ci + t_im * cr + ei

            def fix_step(i, _, slab_of=slab_of):
                r0 = pl.multiple_of(slab_of(i) * R, R)
                p = PW[i]
                for c in range(nchunk):
                    lre = slice(c * LANES, (c + 1) * LANES)
                    lim = slice(ST + c * LANES, ST + (c + 1) * LANES)
                    pr = p[0:1, lre]
                    pi = p[0:1, lim]
                    cr = CIN[:, lre]
                    ci = CIN[:, lim]
                    H[pl.ds(r0, R), lre] = H[pl.ds(r0, R), lre] + (pr * cr - pi * ci)
                    H[pl.ds(r0, R), lim] = H[pl.ds(r0, R), lim] + (pr * ci + pi * cr)
                return 0
            lax.fori_loop(0, n_slab, fix_step, 0)

        if want_final:
            hfin_ref[d] = H[last0:last0 + R, :]

        def y_body(i, _, d=d):
            r0 = pl.multiple_of(i * RC, RC)
            y_ref[pl.ds(r0, RC), :] = y_ref[pl.ds(r0, RC), :] + jnp.dot(
                H[pl.ds(r0, RC), :].astype(BF16), cm_ref[d], preferred_element_type=F32)
            return 0
        lax.fori_loop(0, NR // RC, y_body, 0)


def _s5_scan(u, h0, tables, d_row, *, R, n_slab, chained, want_final):
    ab_row, b_mat, c_mat = tables
    Q, NR, _ = u.shape
    nfs = S5_GROUPS // 8
    W2 = 2 * S5_SLICE_ST
    in_specs = [pl.BlockSpec((None, NR, S5_SLICE_CH), lambda q, f: (q, 0, f))]
    args = [u]
    if chained:
        in_specs.append(pl.BlockSpec((None, N_DIR, None, 1, W2), lambda q, f: (q, 0, f, 0, 0)))
        args.append(h0)
    in_specs += [
        pl.BlockSpec((N_DIR, None, 1, W2), lambda q, f: (0, f, 0, 0)),
        pl.BlockSpec((N_DIR, None, S5_SLICE_CH, W2), lambda q, f: (0, f, 0, 0)),
        pl.BlockSpec((N_DIR, None, W2, S5_SLICE_CH), lambda q, f: (0, f, 0, 0)),
        pl.BlockSpec((1, S5_SLICE_CH), lambda q, f: (0, f)),
    ]
    args += [ab_row, b_mat, c_mat, d_row]
    out_shape = [jax.ShapeDtypeStruct((Q, NR, S5_WIDTH), F32)]
    out_specs = [pl.BlockSpec((None, NR, S5_SLICE_CH), lambda q, f: (q, 0, f))]
    if want_final:
        out_shape.append(jax.ShapeDtypeStruct((Q, N_DIR, nfs, R, W2), F32))
        out_specs.append(pl.BlockSpec((None, N_DIR, None, R, W2), lambda q, f: (q, 0, f, 0, 0)))
    scratch = [pltpu.VMEM((NR, W2), F32)]
    if chained:
        scratch += [pltpu.VMEM((n_slab, 8, W2), F32), pltpu.VMEM((R, W2), F32)]
    res = pl.pallas_call(
        functools.partial(_s5_kernel, R=R, n_slab=n_slab, chained=chained, want_final=want_final),
        grid=(Q, nfs),
        in_specs=in_specs,
        out_specs=tuple(out_specs),
        out_shape=tuple(out_shape),
        scratch_shapes=scratch,
        compiler_params=_cparams(("parallel", "parallel")),
        name="s5_scan_chained" if chained else "s5_scan",
    )(*args)
    return res


def _ssd_kernel(*refs, TB, nb, has_h0, want_final):
    xbc_ref, xp_ref, xn_ref, dt_ref = refs[:4]
    refs = refs[4:]
    if has_h0:
        h0_ref = refs[0]
        refs = refs[1:]
    cw_ref, cb_ref, dtb_ref, arow_ref, sel_ref, drow_ref = refs[:6]
    refs = refs[6:]
    y_ref = refs[0]
    refs = refs[1:]
    if want_final:
        hfin_ref = refs[0]
        refs = refs[1:]
    hst, xc_s, dt_s = refs
    CH = SSD_CHUNK
    P = SSD_HEADDIM
    d = pl.program_id(1)
    j = pl.program_id(2)
    jj = jnp.where(d == 0, j, nb - 1 - j)

    @pl.when(j == 0)
    def _():
        if has_h0:
            hst[...] = h0_ref[...]
        else:
            hst[...] = jnp.zeros_like(hst)

    x = xbc_ref[...]
    prev_row = xp_ref[7:8, :] * (jj > 0).astype(F32)
    next_row = xn_ref[0:1, :] * (jj < nb - 1).astype(F32)
    rows = lax.broadcasted_iota(jnp.int32, (TB, 1), 0)
    x_prev = jnp.where(rows == 0, prev_row, pltpu.roll(x, 1, 0))
    x_next = jnp.where(rows == TB - 1, next_row, pltpu.roll(x, TB - 1, 0))
    conv = cw_ref[0:1, :] * x_prev + cw_ref[1:2, :] * x + cw_ref[2:3, :] * x_next + cb_ref[...]
    xc_s[...] = conv * _sigmoid(conv)
    dtf = _softplus(dt_ref[...] + dtb_ref[...])
    dt_s[...] = _dot_exact_rhs(dtf, sel_ref[...])

    li = lax.broadcasted_iota(jnp.int32, (CH, CH), 0)
    si = lax.broadcasted_iota(jnp.int32, (CH, CH), 1)
    tmat = jnp.where(d == 0, (si <= li).astype(F32), (si >= li).astype(F32))
    is_fwd = (d == 0).astype(F32)
    n_ch = TB // CH
    for i in range(n_ch):
        ci = jnp.where(d == 0, i, n_ch - 1 - i)
        r0 = pl.multiple_of(ci * CH, CH)
        dtc = dt_s[pl.ds(r0, CH), :]
        dA = dtc * arow_ref[...]
        cs = _dot_exact_lhs(tmat, dA)
        csT = cs.T
        tot = jnp.sum(dA, axis=0, keepdims=True)
        Bm = xc_s[pl.ds(r0, CH), SSD_WIDTH:SSD_WIDTH + SSD_BC]
        Cm = xc_s[pl.ds(r0, CH), SSD_WIDTH + SSD_BC:SSD_WIDTH + 2 * SSD_BC]
        BmT = Bm.T
        for g in range(SSD_GROUPS):
            Cg = Cm[:, g * SSD_STATE:(g + 1) * SSD_STATE].astype(BF16)
            Bg = Bm[:, g * SSD_STATE:(g + 1) * SSD_STATE].astype(BF16)
            BgT = BmT[g * SSD_STATE:(g + 1) * SSD_STATE, :].astype(BF16)
            G = lax.dot_general(Cg, Bg, (((1,), (1,)), ((), ())), preferred_element_type=F32)
            for hh in range(SSD_HEADS // SSD_GROUPS):
                h = g * (SSD_HEADS // SSD_GROUPS) + hh
                col = cs[:, h:h + 1]
                row = csT[h:h + 1, :]
                lm = jnp.exp(jnp.where(tmat > 0.0, col - row, -1e30))
                xs_h = xc_s[pl.ds(r0, CH), h * P:(h + 1) * P]
                xdt = xs_h * dtc[:, h:h + 1]
                tot_h = tot[:, h:h + 1]
                hprev = hst[h]
                y = jnp.dot((G * lm).astype(BF16), xdt.astype(BF16), preferred_element_type=F32)
                y = y + jnp.dot(Cg, hprev.astype(BF16), preferred_element_type=F32) * jnp.exp(col)
                y = y + (is_fwd * drow_ref[:, h * P:(h + 1) * P]) * xs_h
                y_ref[pl.ds(r0, CH), h * P:(h + 1) * P] = y
                xd = (xdt * jnp.exp(tot_h - col)).astype(BF16)
                hst[h] = jnp.exp(tot_h) * hprev + jnp.dot(BgT, xd, preferred_element_type=F32)

    if want_final:
        @pl.when(j == nb - 1)
        def _():
            hfin_ref[...] = hst[...]


def _ssd_scan(xbc, dt, h0, conv_w, conv_b, dt_bias, a_log, d_skip, *, B, L, want_final):
    TB = min(512, L)
    nb = L // TB
    H = SSD_HEADS

    def blk(b, d, j):
        return b * nb + j + d * (nb - 1 - 2 * j)

    nrow8 = B * L // 8
    in_specs = [
        pl.BlockSpec((TB, SSD_CONV_CH), lambda b, d, j: (blk(b, d, j), 0)),
        pl.BlockSpec((8, SSD_CONV_CH), lambda b, d, j: (jnp.maximum(blk(b, d, j) * (TB // 8) - 1, 0), 0)),
        pl.BlockSpec((8, SSD_CONV_CH), lambda b, d, j: (jnp.minimum((blk(b, d, j) + 1) * (TB // 8), nrow8 - 1), 0)),
        pl.BlockSpec((TB, DT_PAD), lambda b, d, j: (blk(b, d, j), 0)),
    ]
    args = [xbc, xbc, xbc, dt]
    if h0 is not None:
        in_specs.append(pl.BlockSpec((None, None, H, SSD_STATE, SSD_HEADDIM), lambda b, d, j: (b, d, 0, 0, 0)))
        args.append(h0)
    dtb = jnp.pad(dt_bias.reshape(1, N_DIR * H), ((0, 0), (0, DT_PAD - N_DIR * H)))
    arow = jnp.pad(-jnp.exp(a_log), ((0, 0), (0, LANES - H))).reshape(N_DIR, 1, LANES)
    lane = jnp.arange(LANES)
    sel = jnp.stack([(lane[:, None] == (dd * H + lane[None, :])) & (lane[None, :] < H) for dd in range(N_DIR)]).astype(F32)
    drow = jnp.repeat(d_skip, SSD_HEADDIM).reshape(1, SSD_WIDTH)
    in_specs += [
        _full((3, SSD_CONV_CH)), _full((1, SSD_CONV_CH)), _full((1, DT_PAD)),
        pl.BlockSpec((None, 1, LANES), lambda b, d, j: (d, 0, 0)),
        pl.BlockSpec((None, LANES, LANES), lambda b, d, j: (d, 0, 0)),
        _full((1, SSD_WIDTH)),
    ]
    args += [conv_w, conv_b.reshape(1, SSD_CONV_CH), dtb, arow, sel, drow]
    out_shape = [jax.ShapeDtypeStruct((N_DIR, B * L, SSD_WIDTH), F32)]
    out_specs = [pl.BlockSpec((None, TB, SSD_WIDTH), lambda b, d, j: (d, blk(b, d, j), 0))]
    if want_final:
        out_shape.append(jax.ShapeDtypeStruct((B, N_DIR, H, SSD_STATE, SSD_HEADDIM), F32))
        out_specs.append(pl.BlockSpec((None, None, H, SSD_STATE, SSD_HEADDIM), lambda b, d, j: (b, d, 0, 0, 0)))
    return pl.pallas_call(
        functools.partial(_ssd_kernel, TB=TB, nb=nb, has_h0=h0 is not None, want_final=want_final),
        grid=(B, N_DIR, nb),
        in_specs=in_specs,
        out_specs=tuple(out_specs),
        out_shape=tuple(out_shape),
        scratch_shapes=[pltpu.VMEM((H, SSD_STATE, SSD_HEADDIM), F32), pltpu.VMEM((TB, SSD_CONV_CH), F32),
                        pltpu.VMEM((TB, DT_PAD), F32)],
        compiler_params=_cparams(("parallel", "parallel", "arbitrary")),
        name="ssd_scan",
    )(*args)


def _rwkv_prep_kernel(rkv_ref, rp_ref, rn_ref, lo_ref, mu_ref, a0_ref, aup_ref, gup_ref, w0_ref, wup_ref,
                      kkw_ref, ka_ref, rk_ref, seg_ref,
                      r_ref, w0o_ref, w1o_ref, k_ref, v_ref, kk_ref, b_ref, g_ref, bonus_ref, *, nbs):
    tm = rkv_ref.shape[0]
    W = RWKV_WIDTH
    jj = pl.program_id(0) % nbs
    x = rkv_ref[...]
    prev_row = rp_ref[7:8, :] * (jj > 0).astype(F32)
    next_row = rn_ref[0:1, :] * (jj < nbs - 1).astype(F32)
    rows = lax.broadcasted_iota(jnp.int32, (tm, 1), 0)
    xp = jnp.where(rows == 0, prev_row, pltpu.roll(x, 1, 0))
    xn = jnp.where(rows == tm - 1, next_row, pltpu.roll(x, tm - 1, 0))
    x = x + mu_ref[0:1, :] * (xp - x) + mu_ref[1:2, :] * (xn - x)
    r = x[:, 0:W]
    k = x[:, W:2 * W]
    v = x[:, 2 * W:3 * W]
    lo = lo_ref[...]
    seg = seg_ref[...]
    a = _sigmoid(a0_ref[...] + _dot3(lo, aup_ref[...]))
    g_ref[...] = _dot3(_sigmoid(lo), gup_ref[...])
    tlo = jnp.tanh(lo)
    for d, o_ref in enumerate((w0o_ref, w1o_ref)):
        zw = w0_ref[d:d + 1, :] + _dot3(tlo, wup_ref[d])
        o_ref[...] = jnp.exp(-DECAY_SCALE * _sigmoid(zw))
    kk = k * kkw_ref[...]
    kk = kk * lax.rsqrt(jnp.maximum(_dot_exact_rhs(kk * kk, seg), 1e-24))
    k2 = k * (1.0 + (a - 1.0) * ka_ref[...])
    r_ref[...] = r
    k_ref[...] = k2
    v_ref[...] = v
    kk_ref[...] = kk
    b_ref[...] = kk * a
    bonus_ref[...] = _dot_exact_rhs(r * k2 * rk_ref[...], seg) * v


def _segment_ones(width, seg):
    i = jnp.arange(width)
    return (i[:, None] // seg == i[None, :] // seg).astype(F32)


def _rwkv_prep(rkv, lo, lp, seq_blocks):
    ntok = rkv.shape[0]
    tm = TOKEN_BLOCK
    W = RWKV_WIDTH
    nrow8 = ntok // 8
    nbs_of = seq_blocks
    pad_rows = lambda t, r0: jnp.zeros((LO_PAD, W), F32).at[r0:r0 + t.shape[0]].set(t)
    aup = pad_rows(lp['rwkv_a_up'], N_DIR * W_RANK)
    gup = pad_rows(lp['rwkv_g_up'], N_DIR * W_RANK + A_RANK)
    wup = jnp.stack([pad_rows(lp['rwkv_w_up'][d], d * W_RANK) for d in range(N_DIR)])
    row = lambda t: t.reshape(1, W)
    outs = tuple(jax.ShapeDtypeStruct((ntok, W), F32) for _ in range(9))
    return pl.pallas_call(
        functools.partial(_rwkv_prep_kernel, nbs=nbs_of),
        grid=(ntok // tm,),
        in_specs=[
            pl.BlockSpec((tm, 3 * W), lambda i: (i, 0)),
            pl.BlockSpec((8, 3 * W), lambda i: (jnp.maximum(i * (tm // 8) - 1, 0), 0)),
            pl.BlockSpec((8, 3 * W), lambda i: (jnp.minimum((i + 1) * (tm // 8), nrow8 - 1), 0)),
            pl.BlockSpec((tm, LO_PAD), lambda i: (i, 0)),
            _full((2, 3 * W)), _full((1, W)), _full((LO_PAD, W)), _full((LO_PAD, W)), _full((2, W)),
            _full((N_DIR, LO_PAD, W)), _full((1, W)), _full((1, W)), _full((1, W)), _full((W, W)),
        ],
        out_specs=tuple(pl.BlockSpec((tm, W), lambda i: (i, 0)) for _ in range(9)),
        out_shape=outs,
        compiler_params=_cparams(("parallel",)),
        name="rwkv_prep",
    )(rkv, rkv, rkv, lo, lp['rwkv_mu'], row(lp['rwkv_a0']), aup, gup, lp['rwkv_w0'], wup,
      row(lp['rwkv_k_k']), row(lp['rwkv_k_a']), row(lp['rwkv_r_k']), lp['seg'])


def _rwkv_scan_kernel(r_ref, w_ref, k_ref, kk_ref, b_ref, v_ref, s0_ref, y_ref, sfin_ref, S, *, TBLK, V, nblk):
    K = RWKV_HEADSIZE
    tb = pl.program_id(1)

    @pl.when(tb == 0)
    def _():
        S[...] = s0_ref[...]

    def bc(ref, t, kx):
        return jnp.broadcast_to(ref[t, pl.ds(kx, 1), :], (V, LANES))

    NACC = 4

    def step(t, _):
        accs = [None] * NACC
        for kx in range(K):
            p = S[kx] * bc(kk_ref, t, kx)
            a = kx % NACC
            accs[a] = p if accs[a] is None else accs[a] + p
        sa = (accs[0] + accs[1]) + (accs[2] + accs[3])
        vt = v_ref[t]
        yacc = [None] * NACC
        for kx in range(K):
            s_new = S[kx] * bc(w_ref, t, kx) - sa * bc(b_ref, t, kx) + vt * bc(k_ref, t, kx)
            S[kx] = s_new
            p = s_new * bc(r_ref, t, kx)
            a = kx % NACC
            yacc[a] = p if yacc[a] is None else yacc[a] + p
        y_ref[t] = (yacc[0] + yacc[1]) + (yacc[2] + yacc[3])
        return 0
    lax.fori_loop(0, TBLK, step, 0)

    @pl.when(tb == nblk - 1)
    def _():
        sfin_ref[...] = S[...]


def _rwkv_scan(r, w, k, kk, b, v, s0):
    L, K, NL = r.shape
    V = v.shape[1]
    TBLK = min(64, L)
    nblk = L // TBLK
    ngrp = NL // LANES
    kspec = pl.BlockSpec((TBLK, K, LANES), lambda g, t: (t, 0, g))
    vspec = pl.BlockSpec((TBLK, V, LANES), lambda g, t: (t, 0, g))
    sspec = pl.BlockSpec((K, V, LANES), lambda g, t: (0, 0, g))
    return pl.pallas_call(
        functools.partial(_rwkv_scan_kernel, TBLK=TBLK, V=V, nblk=nblk),
        grid=(ngrp, nblk),
        in_specs=[kspec, kspec, kspec, kspec, kspec, vspec, sspec],
        out_specs=(vspec, sspec),
        out_shape=(jax.ShapeDtypeStruct((L, V, NL), F32), jax.ShapeDtypeStruct((K, V, NL), F32)),
        scratch_shapes=[pltpu.VMEM((K, V, LANES), F32)],
        compiler_params=_cparams(("parallel", "arbitrary")),
        name="rwkv_scan",
    )(r, w, k, kk, b, v, s0)


KQ = 4
CH = LANES // KQ


def _rwkv_scan2_kernel(rf, rm, wf, wm, kf, km, kkf, kkm, bf, bm, vf, vm, s0_ref, yf_ref, yb_ref, S, YQ, VT, SA, *,
                       TBLK):
    V = RWKV_HEADSIZE
    NQ = RWKV_HEADSIZE // KQ
    tb = pl.program_id(0)

    @pl.when(tb == 0)
    def _():
        S[...] = s0_ref[...]

    VH = V // 2

    def bc(ref, tt, q):
        return jnp.broadcast_to(ref[tt, pl.ds(q, 1), :], (VH, LANES))

    def all_quarters(x):
        return (x + pltpu.roll(x, CH, 1)) + (pltpu.roll(x, 2 * CH, 1) + pltpu.roll(x, 3 * CH, 1))

    dirs = ((rf, wf, kf, kkf, bf, vf), (rm, wm, km, kkm, bm, vm))
    time_of = (lambda t: t, lambda t: TBLK - 1 - t)
    chains = [(d, hv) for d in range(N_DIR) for hv in range(2)]

    def first_sa(d, hv):
        kk_ = dirs[d][3]
        tt = time_of[d](0)
        acc = [None, None]
        for q in range(NQ):
            p = S[d, q, pl.ds(hv * VH, VH), :] * bc(kk_, tt, q)
            acc[q % 2] = p if acc[q % 2] is None else acc[q % 2] + p
        return all_quarters(acc[0] + acc[1])

    TG = 8
    lane_c = lax.broadcasted_iota(jnp.int32, (CH, LANES), 1) % CH
    rep = jnp.where(lax.broadcasted_iota(jnp.int32, (CH, LANES), 0) == lane_c, 1.0, 0.0)
    lane_r = lax.broadcasted_iota(jnp.int32, (LANES, CH), 0) % CH
    fold = jnp.where(lax.broadcasted_iota(jnp.int32, (LANES, CH), 1) == lane_r, 1.0, 0.0)

    def spread(i, _):
        for d in range(N_DIR):
            v = dirs[d][5][pl.ds(i * TG, TG)].reshape(TG * V, CH)
            VT[d, pl.ds(i * TG, TG)] = _dot_exact_rhs(v, rep).reshape(TG, V, LANES)
        return 0
    lax.fori_loop(0, TBLK // TG, spread, 0)

    for c, (d, hv) in enumerate(chains):
        SA[c] = first_sa(d, hv)

    def step(t, _):
        for c, (d, hv) in enumerate(chains):
            r_, w_, k_, kk_, b_, _ = dirs[d]
            tt = time_of[d](t)
            tn = time_of[d](jnp.minimum(t + 1, TBLK - 1))
            rows = pl.ds(hv * VH, VH)
            sa = SA[c]
            vt = VT[d, tt, rows, :]
            yacc = None
            sacc = None
            for q in range(NQ):
                s_new = S[d, q, rows, :] * bc(w_, tt, q) - sa * bc(b_, tt, q) + vt * bc(k_, tt, q)
                S[d, q, rows, :] = s_new
                py = s_new * bc(r_, tt, q)
                ps = s_new * bc(kk_, tn, q)
                yacc = py if yacc is None else yacc + py
                sacc = ps if sacc is None else sacc + ps
            YQ[d, tt, rows, :] = yacc
            SA[c] = all_quarters(sacc)
        return 0
    lax.fori_loop(0, TBLK, step, 0)

    def finish(i, _):
        for d, y_ in enumerate((yf_ref, yb_ref)):
            x = YQ[d, pl.ds(i * TG, TG)].reshape(TG * V, LANES)
            y_[pl.ds(i * TG, TG)] = _dot_exact_rhs(x, fold).reshape(TG, V, CH)
        return 0
    lax.fori_loop(0, TBLK // TG, finish, 0)


def _rwkv_scan2(r, w0, w1, k, kk, b, v, s0):
    L, NQ, _ = r.shape
    V = RWKV_HEADSIZE
    TBLK = min(64, L)
    nblk = L // TBLK
    fspec = pl.BlockSpec((TBLK, NQ, LANES), lambda t: (t, 0, 0))
    mspec = pl.BlockSpec((TBLK, NQ, LANES), lambda t: (nblk - 1 - t, 0, 0))
    vfspec = pl.BlockSpec((TBLK, V, CH), lambda t: (t, 0, 0))
    vmspec = pl.BlockSpec((TBLK, V, CH), lambda t: (nblk - 1 - t, 0, 0))
    yf, yb = pl.pallas_call(
        functools.partial(_rwkv_scan2_kernel, TBLK=TBLK),
        grid=(nblk,),
        in_specs=[fspec, mspec] * 5 + [vfspec, vmspec, _full((N_DIR, NQ, V, LANES))],
        out_specs=(vfspec, vmspec),
        out_shape=(jax.ShapeDtypeStruct((L, V, CH), F32), jax.ShapeDtypeStruct((L, V, CH), F32)),
        scratch_shapes=[pltpu.VMEM((N_DIR, NQ, V, LANES), F32), pltpu.VMEM((N_DIR, TBLK, V, LANES), F32),
                        pltpu.VMEM((N_DIR, TBLK, V, LANES), F32), pltpu.VMEM((2 * N_DIR, V // 2, LANES), F32)],
        compiler_params=_cparams(("arbitrary",)),
        name="rwkv_scan2",
    )(r, r, w0, w1, k, k, kk, kk, b, b, v, v, s0)
    return yf, yb


def _to_scan2_k(t, B, L):
    nch = B * RWKV_HEADS
    x = t.reshape(B, L, RWKV_HEADS, RWKV_HEADSIZE // KQ, KQ).transpose(1, 3, 4, 0, 2).reshape(L, -1, KQ, nch)
    x = jnp.pad(x, ((0, 0), (0, 0), (0, 0), (0, CH - nch)))
    return x.reshape(L, RWKV_HEADSIZE // KQ, LANES)


def _to_scan2_v(t, B, L):
    x = t.reshape(B, L, RWKV_HEADS, RWKV_HEADSIZE).transpose(1, 3, 0, 2).reshape(L, RWKV_HEADSIZE, B * RWKV_HEADS)
    return jnp.pad(x, ((0, 0), (0, 0), (0, CH - B * RWKV_HEADS)))


def _from_scan2(y, B, L):
    y = y[:, :, :B * RWKV_HEADS]
    return y.reshape(L, RWKV_HEADSIZE, B, RWKV_HEADS).transpose(2, 0, 3, 1).reshape(B * L, RWKV_WIDTH)


def _state_to_scan2(s):
    B = s.shape[0]
    nch = B * RWKV_HEADS
    x = s.transpose(1, 4, 3, 0, 2).reshape(N_DIR, RWKV_HEADSIZE // KQ, KQ, RWKV_HEADSIZE, nch)
    x = jnp.pad(x, ((0, 0), (0, 0), (0, 0), (0, 0), (0, CH - nch)))
    return x.transpose(0, 1, 3, 2, 4).reshape(N_DIR, RWKV_HEADSIZE // KQ, RWKV_HEADSIZE, LANES)


def _rwkv_layout(B, vsplit):
    chains = N_DIR * B * RWKV_HEADS
    cp = -(-chains // (LANES // vsplit)) * (LANES // vsplit)
    return chains, cp


def _to_scan_k(t_fwd, t_bwd, B, L, vsplit):
    chains, cp = _rwkv_layout(B, vsplit)

    def one(t):
        return t.reshape(B, L, RWKV_HEADS, RWKV_HEADSIZE).transpose(1, 3, 0, 2).reshape(L, RWKV_HEADSIZE, B * RWKV_HEADS)
    x = jnp.concatenate([one(t_fwd), jnp.flip(one(t_bwd), 0)], axis=-1)
    x = jnp.pad(x, ((0, 0), (0, 0), (0, cp - chains)))
    return jnp.tile(x, (1, 1, vsplit))


def _to_scan_v(t, B, L, vsplit):
    chains, cp = _rwkv_layout(B, vsplit)
    x = t.reshape(B, L, RWKV_HEADS, RWKV_HEADSIZE).transpose(1, 3, 0, 2).reshape(L, RWKV_HEADSIZE, B * RWKV_HEADS)
    x = jnp.concatenate([x, jnp.flip(x, 0)], axis=-1)
    x = jnp.pad(x, ((0, 0), (0, 0), (0, cp - chains)))
    vs = RWKV_HEADSIZE // vsplit
    return x.reshape(L, vsplit, vs, cp).transpose(0, 2, 1, 3).reshape(L, vs, vsplit * cp)


def _from_scan_v(y, B, L, vsplit):
    chains, cp = _rwkv_layout(B, vsplit)
    vs = RWKV_HEADSIZE // vsplit
    y = y.reshape(L, vs, vsplit, cp).transpose(0, 2, 1, 3).reshape(L, RWKV_HEADSIZE, cp)[:, :, :chains]
    half = chains // 2

    def back(t):
        return t.reshape(L, RWKV_HEADSIZE, B, RWKV_HEADS).transpose(2, 0, 3, 1).reshape(B * L, RWKV_WIDTH)
    return back(y[:, :, :half]), back(jnp.flip(y[:, :, half:], 0))


def _state_to_scan(s, vsplit):
    B = s.shape[0]
    chains, cp = _rwkv_layout(B, vsplit)
    x = s.transpose(4, 3, 1, 0, 2).reshape(RWKV_HEADSIZE, RWKV_HEADSIZE, chains)
    x = jnp.pad(x, ((0, 0), (0, 0), (0, cp - chains)))
    vs = RWKV_HEADSIZE // vsplit
    return x.reshape(RWKV_HEADSIZE, vsplit, vs, cp).transpose(0, 2, 1, 3).reshape(RWKV_HEADSIZE, vs, vsplit * cp)


def _state_from_scan(x, B, vsplit):
    chains, cp = _rwkv_layout(B, vsplit)
    vs = RWKV_HEADSIZE // vsplit
    x = x.reshape(RWKV_HEADSIZE, vs, vsplit, cp).transpose(0, 2, 1, 3).reshape(RWKV_HEADSIZE, RWKV_HEADSIZE, cp)
    x = x[:, :, :chains].reshape(RWKV_HEADSIZE, RWKV_HEADSIZE, N_DIR, B, RWKV_HEADS)
    return x.transpose(3, 2, 4, 1, 0)


def _post_kernel(x_ref, mod_ref, ys5_ref, yf_ref, yb_ref, z_ref, rf_ref, rb_ref, bonus_ref, g_ref,
                 gluw_ref, glub_ref, ssdg_ref, lng_ref, lnb_ref, seg_ref, wout_ref, n2g_ref, rw_ref,
                 x1_ref, hb_ref, aff_ref):
    m = mod_ref[...]
    D = D_MODEL
    zg = _gelu_tanh(ys5_ref[...])
    gate = jnp.dot(zg.astype(BF16), gluw_ref[...], preferred_element_type=F32) + glub_ref[...]
    y_a = zg * _sigmoid(gate)
    z = z_ref[...]
    yb = (yf_ref[...] + yb_ref[...]) * (z * _sigmoid(z))
    y_b = yb * lax.rsqrt(jnp.mean(yb * yb, axis=-1, keepdims=True) + EPS) * ssdg_ref[...]
    seg = seg_ref[...] * (1.0 / RWKV_HEADSIZE)
    yr = rf_ref[...] + rb_ref[...]
    mean = _dot_exact_rhs(yr, seg)
    cen = yr - mean
    var = _dot_exact_rhs(cen * cen, seg)
    yn = cen * lax.rsqrt(var + GN_EPS) * lng_ref[...] + lnb_ref[...]
    y_c = (yn + bonus_ref[...]) * g_ref[...]
    o = jnp.dot(y_a.astype(BF16), wout_ref[0:S5_WIDTH, :], preferred_element_type=F32)
    o = o + jnp.dot(y_b.astype(BF16), wout_ref[S5_WIDTH:S5_WIDTH + SSD_WIDTH, :], preferred_element_type=F32)
    o = o + jnp.dot(y_c.astype(BF16), wout_ref[S5_WIDTH + SSD_WIDTH:, :], preferred_element_type=F32)
    x1 = x_ref[...] + m[:, 2 * D:3 * D] * o
    x1_ref[...] = x1
    h2 = x1 * lax.rsqrt(jnp.mean(x1 * x1, axis=-1, keepdims=True) + EPS) * n2g_ref[...]
    h2 = h2 * (1.0 + m[:, 4 * D:5 * D]) + m[:, 3 * D:4 * D]
    hb_ref[...] = h2.astype(BF16)
    logits = _dot3(rw_ref[...], h2, (((1,), (1,)), ((), ())))
    mx = jnp.max(logits, axis=0, keepdims=True)
    ex = jnp.exp(logits - mx)
    aff_ref[...] = ex / jnp.sum(ex, axis=0, keepdims=True)


def _post_mixer(x, mod3, row_of_block, ys5, yssd, z, rf, rb, bonus, g, lp):
    ntok = x.shape[0]
    tm = TOKEN_BLOCK
    tok = lambda w: pl.BlockSpec((tm, w), lambda i: (i, 0))
    row = lambda t: t.reshape(1, -1)
    W = RWKV_WIDTH
    return pl.pallas_call(
        _post_kernel,
        grid=(ntok // tm,),
        in_specs=[
            tok(D_MODEL),
            pl.BlockSpec((None, 1, 6 * D_MODEL), lambda i: (row_of_block(i), 0, 0)),
            tok(S5_WIDTH),
            pl.BlockSpec((None, tm, SSD_WIDTH), lambda i: (0, i, 0)),
            pl.BlockSpec((None, tm, SSD_WIDTH), lambda i: (1, i, 0)),
            tok(SSD_WIDTH), tok(W), tok(W), tok(W), tok(W),
            _full((S5_WIDTH, S5_WIDTH)), _full((1, S5_WIDTH)), _full((1, SSD_WIDTH)), _full((1, W)), _full((1, W)),
            _full((W, W)), _full((D_MODEL, D_MODEL)), _full((1, D_MODEL)), _full((N_EXPERTS, D_MODEL)),
        ],
        out_specs=(tok(D_MODEL), tok(D_MODEL), pl.BlockSpec((N_EXPERTS, tm), lambda i: (0, i))),
        out_shape=(jax.ShapeDtypeStruct((ntok, D_MODEL), F32), jax.ShapeDtypeStruct((ntok, D_MODEL), BF16),
                   jax.ShapeDtypeStruct((N_EXPERTS, ntok), F32)),
        compiler_params=_cparams(("parallel",)),
        name="post_mixer",
    )(x, mod3, ys5, yssd, yssd, z, rf, rb, bonus, g,
      lp['s5_glu_w'], row(lp['s5_glu_b']), row(lp['ssd_norm_g']), row(lp['rwkv_ln_g']),
      row(lp['rwkv_ln_b']), lp['seg'], lp['w_out'], row(lp['norm2_g']),
      lp['router_w'].T)


def _select_kernel(aff_ref, slot_ref, *, n, cap):
    E = N_EXPERTS
    a = aff_ref[...]
    bits = pltpu.bitcast(a, jnp.int32)
    thr = jnp.zeros((E, 1), jnp.int32)
    capf = float(cap)
    for bit in range(30, -1, -1):
        cand = thr | (1 << bit)
        cnt = jnp.sum(jnp.where(bits >= cand, 1.0, 0.0), axis=1, keepdims=True)
        thr = jnp.where(cnt >= capf, cand, thr)
    gt = bits > thr
    eq = bits == thr
    need = capf - jnp.sum(jnp.where(gt, 1.0, 0.0), axis=1, keepdims=True)
    CW = min(256, n)
    ui = lax.broadcasted_iota(jnp.int32, (CW, CW), 0)
    uj = lax.broadcasted_iota(jnp.int32, (CW, CW), 1)
    upper = jnp.where(ui < uj, 1.0, 0.0).astype(BF16)

    def excl_cumsum(mask_f):
        outs = []
        off = jnp.zeros((E, 1), F32)
        for c in range(n // CW):
            mc = mask_f[:, c * CW:(c + 1) * CW]
            outs.append(jnp.dot(mc.astype(BF16), upper, preferred_element_type=F32) + off)
            off = off + jnp.sum(mc, axis=1, keepdims=True)
        return jnp.concatenate(outs, axis=1)

    eq_rank = excl_cumsum(jnp.where(eq, 1.0, 0.0))
    sel = jnp.where(gt, 1.0, jnp.where(eq, jnp.where(eq_rank < need, 1.0, 0.0), 0.0))
    pos = excl_cumsum(sel)
    slot_ref[...] = jnp.where(sel > 0.0, pos, -1.0).astype(jnp.int32)


def _select(aff, B, n, cap):
    return pl.pallas_call(
        functools.partial(_select_kernel, n=n, cap=cap),
        grid=(B,),
        in_specs=[pl.BlockSpec((N_EXPERTS, n), lambda b: (0, b))],
        out_specs=pl.BlockSpec((None, N_EXPERTS, n), lambda b: (b, 0, 0)),
        out_shape=jax.ShapeDtypeStruct((B, N_EXPERTS, n), jnp.int32),
        compiler_params=_cparams(("parallel",)),
        name="ec_select",
    )(aff)


def _slot_block_range(slots_f, cap, SB):
    lo = jnp.min(jnp.where(slots_f >= 0.0, slots_f, float(cap))).astype(jnp.int32)
    hi = jnp.max(slots_f).astype(jnp.int32)
    first = lo // SB
    count = jnp.where(hi >= 0, hi // SB - first + 1, 0)
    return first, count


def _gather_kernel(hb_ref, slot_ref, aff_ref, xs_ref, gate_ref, acc, gacc, *, n, cap):
    NC = min(512, n)
    SB = min(LANES, cap)
    acc[...] = jnp.zeros_like(acc)
    gacc[...] = jnp.zeros_like(gacc)
    srow = lax.broadcasted_iota(jnp.int32, (SB, NC), 0).astype(F32)
    for c in range(n // NC):
        sl = slot_ref[:, c * NC:(c + 1) * NC].astype(F32)
        first, count = _slot_block_range(sl, cap, SB)
        for j in range(min(cap // SB, NC // SB + 1)):
            @pl.when(j < count)
            def _(c=c, j=j, sl=sl, first=first):
                base = pl.multiple_of((first + j) * SB, SB)
                hit = (sl - base.astype(F32)) == srow
                oh = jnp.where(hit, 1.0, 0.0).astype(BF16)
                acc[pl.ds(base, SB), :] += jnp.dot(oh, hb_ref[c * NC:(c + 1) * NC, :], preferred_element_type=F32)
                gacc[pl.ds(base, SB), :] += jnp.sum(jnp.where(hit, aff_ref[:, c * NC:(c + 1) * NC], 0.0), axis=1,
                                                    keepdims=True)
    xs_ref[...] = acc[...].astype(BF16)
    gate_ref[...] = gacc[...]


def _gather(hb, slot4, aff4, B, n, cap):
    E = N_EXPERTS
    return pl.pallas_call(
        functools.partial(_gather_kernel, n=n, cap=cap),
        grid=(B, E),
        in_specs=[
            pl.BlockSpec((n, D_MODEL), lambda b, e: (b, 0)),
            pl.BlockSpec((None, None, 1, n), lambda b, e: (b, e, 0, 0)),
            pl.BlockSpec((None, None, 1, n), lambda b, e: (b, e, 0, 0)),
        ],
        out_specs=(pl.BlockSpec((None, None, cap, D_MODEL), lambda b, e: (b, e, 0, 0)),
                   pl.BlockSpec((None, None, cap, 1), lambda b, e: (b, e, 0, 0))),
        out_shape=(jax.ShapeDtypeStruct((B, E, cap, D_MODEL), BF16), jax.ShapeDtypeStruct((B, E, cap, 1), F32)),
        scratch_shapes=[pltpu.VMEM((cap, D_MODEL), F32), pltpu.VMEM((cap, 1), F32)],
        compiler_params=_cparams(("parallel", "arbitrary")),
        name="ec_gather",
    )(hb, slot4, aff4)


def _ffn_kernel(xs_ref, gate_ref, w1_ref, w3_ref, w2_ref, o_ref):
    bg, cap, _ = xs_ref.shape
    x = xs_ref[...].reshape(bg * cap, D_MODEL)
    h1 = jnp.dot(x, w1_ref[...], preferred_element_type=F32)
    h3 = jnp.dot(x, w3_ref[...], preferred_element_type=F32)
    hid = (h1 * _sigmoid(h1) * h3).astype(BF16)
    o = jnp.dot(hid, w2_ref[...], preferred_element_type=F32) * gate_ref[...].reshape(bg * cap, 1)
    o_ref[...] = o.astype(BF16).reshape(bg, cap, D_MODEL)


def _expert_ffn(xs, gate, w1, w3, w2, bg):
    B, E, cap, _ = xs.shape
    return pl.pallas_call(
        _ffn_kernel,
        grid=(E, B // bg),
        in_specs=[
            pl.BlockSpec((bg, None, cap, D_MODEL), lambda e, b: (b, e, 0, 0)),
            pl.BlockSpec((bg, None, cap, 1), lambda e, b: (b, e, 0, 0)),
            pl.BlockSpec((None, D_MODEL, D_EXPERT), lambda e, b: (e, 0, 0)),
            pl.BlockSpec((None, D_MODEL, D_EXPERT), lambda e, b: (e, 0, 0)),
            pl.BlockSpec((None, D_EXPERT, D_MODEL), lambda e, b: (e, 0, 0)),
        ],
        out_specs=pl.BlockSpec((bg, None, cap, D_MODEL), lambda e, b: (b, e, 0, 0)),
        out_shape=jax.ShapeDtypeStruct((B, E, cap, D_MODEL), BF16),
        compiler_params=_cparams(("parallel", "parallel")),
        name="ec_ffn",
    )(xs, gate, w1, w3, w2)


def _scatter_kernel(slot_ref, o_ref, x1_ref, mod_ref, fg_ref, out_ref, acc, *, cap, final):
    e = pl.program_id(2)
    tn = x1_ref.shape[0]

    @pl.when(e == 0)
    def _():
        acc[...] = jnp.zeros_like(acc)

    lane = lax.broadcasted_iota(jnp.int32, (tn, LANES), 1)
    col = jnp.sum(jnp.where(lane == e, slot_ref[...].astype(F32), 0.0), axis=1, keepdims=True)
    SB = min(LANES, cap)
    scol = lax.broadcasted_iota(jnp.int32, (tn, SB), 1).astype(F32)
    first, count = _slot_block_range(col, cap, SB)
    for j in range(min(cap // SB, tn // SB + 1)):
        @pl.when(j < count)
        def _(j=j):
            base = pl.multiple_of((first + j) * SB, SB)
            oh = jnp.where((col - base.astype(F32)) == scol, 1.0, 0.0).astype(BF16)
            acc[...] += jnp.dot(oh, o_ref[pl.ds(base, SB), :], preferred_element_type=F32)

    @pl.when(e == N_EXPERTS - 1)
    def _():
        x2 = x1_ref[...] + mod_ref[:, 5 * D_MODEL:6 * D_MODEL] * acc[...]
        if final:
            x2 = x2 * lax.rsqrt(jnp.mean(x2 * x2, axis=-1, keepdims=True) + EPS) * fg_ref[...]
        out_ref[...] = x2


def _scatter(slot_t, o, x1, mod3, mod_row, final_g, B, n, cap, final):
    tn = min(1024, n)
    nt = n // tn
    return pl.pallas_call(
        functools.partial(_scatter_kernel, cap=cap, final=final),
        grid=(B, nt, N_EXPERTS),
        in_specs=[
            pl.BlockSpec((tn, LANES), lambda b, t, e: (b * nt + t, 0)),
            pl.BlockSpec((None, None, cap, D_MODEL), lambda b, t, e: (b, e, 0, 0)),
            pl.BlockSpec((tn, D_MODEL), lambda b, t, e: (b * nt + t, 0)),
            pl.BlockSpec((None, 1, 6 * D_MODEL), lambda b, t, e: (mod_row(b), 0, 0)),
            _full((1, D_MODEL)),
        ],
        out_specs=pl.BlockSpec((tn, D_MODEL), lambda b, t, e: (b * nt + t, 0)),
        out_shape=jax.ShapeDtypeStruct((B * n, D_MODEL), F32),
        scratch_shapes=[pltpu.VMEM((tn, D_MODEL), F32)],
        compiler_params=_cparams(("parallel", "parallel", "arbitrary")),
        name="ec_scatter",
    )(slot_t, o, x1, mod3, final_g.reshape(1, D_MODEL))


def _expert_choice(hb, aff, x1, mod3, mod_row, final_g, w1, w3, w2, B, n, final):
    cap = EC_FACTOR * n // N_EXPERTS
    slot = _select(aff, B, n, cap)
    aff4 = aff.reshape(N_EXPERTS, B, 1, n).transpose(1, 0, 2, 3)
    xs, gate = _gather(hb, slot.reshape(B, N_EXPERTS, 1, n), aff4, B, n, cap)
    bg = max(1, min(B, 256 // cap))
    o = _expert_ffn(xs, gate, w1, w3, w2, bg)
    slot_t = jnp.pad(slot.transpose(0, 2, 1).reshape(B * n, N_EXPERTS), ((0, 0), (0, LANES - N_EXPERTS)),
                     constant_values=-1)
    return _scatter(slot_t, o, x1, mod3, mod_row, final_g, B, n, cap, final)


def kernel(x_prompt, x_sample, c, state_s5_re, state_s5_im, state_ssd, state_rwkv, c_ctx, ada_w, ada_b, norm1_g, norm2_g, w_in, w_out, s5_a_re, s5_a_im, s5_log_dt, s5_b_re, s5_b_im, s5_c_re, s5_c_im, s5_d, s5_glu_w, s5_glu_b, ssd_conv_w, ssd_conv_b, ssd_a_log, ssd_dt_bias, ssd_d, ssd_norm_g, rwkv_mu, rwkv_w0, rwkv_w_up, rwkv_a0, rwkv_a_up, rwkv_g_up, rwkv_k_k, rwkv_k_a, rwkv_r_k, rwkv_ln_g, rwkv_ln_b, router_w, exp_w1, exp_w3, exp_w2, final_g):
    Bp, Lp, D = x_prompt.shape
    Bs, Ls, _ = x_sample.shape
    depth = ada_w.shape[0]
    Np, Ns = Bp * Lp, Bs * Ls
    tm = TOKEN_BLOCK
    grid_rows = Ls // GRID_W
    nfs = S5_GROUPS // 8
    assert Lp % tm == 0 and Ls % tm == 0 and Lp % SSD_CHUNK == 0

    n_rows = 1 + Bs
    rows_pad = -(-n_rows // 8) * 8
    cond = jnp.zeros((rows_pad, D), F32).at[0].set(c_ctx).at[1:n_rows].set(c)
    mod = _modulation(cond, ada_w, ada_b)
    s_blocks_per_req = Ls // tm
    row_p = lambda i: 0
    row_s = lambda i: 1 + i // s_blocks_per_req

    ab_re, ab_im, bb_re, bb_im = _s5_discretize(s5_a_re, s5_a_im, s5_log_dt, s5_b_re, s5_b_im)
    w_in_pad = _pad_in_weight(w_in)
    w_out_b = w_out.astype(BF16)
    glu_w_b = s5_glu_w.astype(BF16)
    exp_w1_b, exp_w3_b, exp_w2_b = exp_w1.astype(BF16), exp_w3.astype(BF16), exp_w2.astype(BF16)
    seg = _segment_ones(RWKV_WIDTH, RWKV_HEADSIZE)

    xp = x_prompt.reshape(Np, D)
    xs = x_sample.reshape(Ns, D)
    new_s5_re, new_s5_im, new_ssd, new_rwkv = [], [], [], []
    QP = 2 if Bp % 2 == 0 and Bp >= 2 else 1
    RP = Bp // QP
    assert Bs * RWKV_HEADS <= CH

    for l in range(depth):
        lp = {
            'rwkv_mu': rwkv_mu[l], 'rwkv_w0': rwkv_w0[l], 'rwkv_w_up': rwkv_w_up[l], 'rwkv_a0': rwkv_a0[l],
            'rwkv_a_up': rwkv_a_up[l], 'rwkv_g_up': rwkv_g_up[l], 'rwkv_k_k': rwkv_k_k[l], 'rwkv_k_a': rwkv_k_a[l],
            'rwkv_r_k': rwkv_r_k[l].reshape(-1), 'rwkv_ln_g': rwkv_ln_g[l], 'rwkv_ln_b': rwkv_ln_b[l],
            's5_glu_w': glu_w_b[l], 's5_glu_b': s5_glu_b[l], 'ssd_norm_g': ssd_norm_g[l], 'w_out': w_out_b[l],
            'norm2_g': norm2_g[l], 'router_w': router_w[l], 'seg': seg,
        }
        mod3 = mod[l].reshape(rows_pad, 1, 6 * D)
        us5_p, z_p, xbc_p, dt_p, rkv_p, lo_p = _in_projection(xp, mod3, row_p, norm1_g[l], w_in_pad[l])
        us5_s, z_s, xbc_s, dt_s, rkv_s, lo_s = _in_projection(xs, mod3, row_s, norm1_g[l], w_in_pad[l])

        tables = _s5_layer_tables(ab_re[l], ab_im[l], bb_re[l], bb_im[l], s5_c_re[l], s5_c_im[l])
        d_row = s5_d[l].reshape(1, S5_WIDTH)
        up = us5_p.reshape(QP, RP, Lp, S5_WIDTH).transpose(0, 2, 1, 3).reshape(QP, Lp * RP, S5_WIDTH)
        yp, hfin = _s5_scan(up, None, tables, d_row, R=RP, n_slab=Lp, chained=False, want_final=True)
        ys5_p = yp.reshape(QP, Lp, RP, S5_WIDTH).transpose(0, 2, 1, 3).reshape(Np, S5_WIDTH)
        hf = hfin.transpose(0, 3, 1, 2, 4).reshape(Bp, N_DIR, nfs, 2, 8, S5_STATE)
        new_s5_re.append(hf[:, :, :, 0].reshape(Bp, N_DIR, S5_GROUPS, S5_STATE))
        new_s5_im.append(hf[:, :, :, 1].reshape(Bp, N_DIR, S5_GROUPS, S5_STATE))
        h0 = jnp.concatenate([state_s5_re[:, l].reshape(Bs, N_DIR, nfs, 1, S5_SLICE_ST),
                              state_s5_im[:, l].reshape(Bs, N_DIR, nfs, 1, S5_SLICE_ST)], axis=-1)
        (ysm,) = _s5_scan(us5_s.reshape(Bs, Ls, S5_WIDTH), h0, tables, d_row, R=GRID_W, n_slab=grid_rows,
                          chained=True, want_final=False)
        ys5_s = ysm.reshape(Ns, S5_WIDTH)

        ssd_args = (ssd_conv_w[l], ssd_conv_b[l], ssd_dt_bias[l], ssd_a_log[l], ssd_d[l])
        yssd_p, hssd = _ssd_scan(xbc_p, dt_p, None, *ssd_args, B=Bp, L=Lp, want_final=True)
        new_ssd.append(hssd.transpose(0, 1, 2, 4, 3))
        (yssd_s,) = _ssd_scan(xbc_s, dt_s, state_ssd[:, l].transpose(0, 1, 2, 4, 3), *ssd_args,
                              B=Bs, L=Ls, want_final=False)

        r_, w0_, w1_, k_, v_, kk_, b_, g_p, bonus_p = _rwkv_prep(rkv_p, lo_p, lp, Lp // tm)
        zero_state = jnp.zeros((Bp, N_DIR, RWKV_HEADS, RWKV_HEADSIZE, RWKV_HEADSIZE), F32)
        y_, sfin = _rwkv_scan(_to_scan_k(r_, r_, Bp, Lp, 1), _to_scan_k(w0_, w1_, Bp, Lp, 1),
                              _to_scan_k(k_, k_, Bp, Lp, 1), _to_scan_k(kk_, kk_, Bp, Lp, 1),
                              _to_scan_k(b_, b_, Bp, Lp, 1), _to_scan_v(v_, Bp, Lp, 1),
                              _state_to_scan(zero_state, 1))
        rf_p, rb_p = _from_scan_v(y_, Bp, Lp, 1)
        new_rwkv.append(_state_from_scan(sfin, Bp, 1))

        r_, w0_, w1_, k_, v_, kk_, b_, g_s, bonus_s = _rwkv_prep(rkv_s, lo_s, lp, Ls // tm)
        sc = lambda t: _to_scan2_k(t, Bs, Ls)
        yf_, yb_ = _rwkv_scan2(sc(r_), sc(w0_), sc(w1_), sc(k_), sc(kk_), sc(b_), _to_scan2_v(v_, Bs, Ls),
                               _state_to_scan2(state_rwkv[:, l]))
        rf_s = _from_scan2(yf_, Bs, Ls)
        rb_s = _from_scan2(yb_, Bs, Ls)

        x1_p, hb_p, aff_p = _post_mixer(xp, mod3, row_p, ys5_p, yssd_p, z_p, rf_p, rb_p, bonus_p, g_p, lp)
        x1_s, hb_s, aff_s = _post_mixer(xs, mod3, row_s, ys5_s, yssd_s, z_s, rf_s, rb_s, bonus_s, g_s, lp)

        final = l == depth - 1
        ew = (exp_w1_b[l], exp_w3_b[l], exp_w2_b[l])
        xp = _expert_choice(hb_p, aff_p, x1_p, mod3, lambda b: 0, final_g, *ew, Bp, Lp, final)
        xs = _expert_choice(hb_s, aff_s, x1_s, mod3, lambda b: 1 + b, final_g, *ew, Bs, Ls, final)

    y_prompt = xp.reshape(Bp, Lp, D)
    y_sample = xs.reshape(Bs, Ls, D)
    return (y_prompt, y_sample, jnp.stack(new_s5_re, axis=1), jnp.stack(new_s5_im, axis=1),
            jnp.stack(new_ssd, axis=1), jnp.stack(new_rwkv, axis=1))
```

```python
import functools
import math

import jax
import jax.numpy as jnp
from jax import lax
from jax.experimental import pallas as pl
from jax.experimental.pallas import tpu as pltpu

F32 = jnp.float32
BF16 = jnp.bfloat16
HIGHEST = lax.Precision.HIGHEST

D_MODEL = 1024
GRID_W = 64
N_DIR = 2
EPS = 1e-6
S5_WIDTH = 256
S5_CH = 16
S5_GROUPS = 16
S5_STATE = 64
SSD_HEADDIM = 64
SSD_HEADS = 6
SSD_WIDTH = 384
SSD_GROUPS = 2
SSD_STATE = 64
SSD_BC = 128
SSD_CONV_CH = 640
SSD_CHUNK = 128
RWKV_HEADSIZE = 64
RWKV_HEADS = 6
RWKV_WIDTH = 384
W_RANK = 32
A_RANK = 32
G_RANK = 64
DECAY_SCALE = math.exp(-0.5)
GN_EPS = 64e-5
N_EXPERTS = 16
D_EXPERT = 512
EC_FACTOR = 2
IN_SIZES = (S5_WIDTH, SSD_WIDTH, SSD_CONV_CH, N_DIR * SSD_HEADS, 3 * RWKV_WIDTH, N_DIR * W_RANK, A_RANK, G_RANK)

LANES = 128
TOKEN_BLOCK = 256
DT_PAD = 128
LO_PAD = 256
S5_SLICE_CH = 128
S5_SLICE_ST = 512
VMEM_LIMIT = 56 * 1024 * 1024


def _cparams(sem):
    return pltpu.CompilerParams(dimension_semantics=sem, vmem_limit_bytes=VMEM_LIMIT)


def _full(shape):
    nd = len(shape)
    return pl.BlockSpec(shape, lambda *_: (0,) * nd)


def _sigmoid(x):
    return 1.0 / (1.0 + jnp.exp(-x))


def _softplus(x):
    return jnp.maximum(x, 0.0) + jnp.log1p(jnp.exp(-jnp.abs(x)))


def _split_bf16(x):
    hi = x.astype(BF16)
    lo = (x - hi.astype(F32)).astype(BF16)
    return hi, lo


def _dot_exact_rhs(x, m, dims=None):
    hi, lo = _split_bf16(x)
    mb = m.astype(BF16)
    if dims is None:
        return jnp.dot(hi, mb, preferred_element_type=F32) + jnp.dot(lo, mb, preferred_element_type=F32)
    return (lax.dot_general(hi, mb, dims, preferred_element_type=F32)
            + lax.dot_general(lo, mb, dims, preferred_element_type=F32))


def _dot_exact_lhs(m, x):
    hi, lo = _split_bf16(x)
    mb = m.astype(BF16)
    return jnp.dot(mb, hi, preferred_element_type=F32) + jnp.dot(mb, lo, preferred_element_type=F32)


def _dot3(x, w, dims=(((1,), (0,)), ((), ()))):
    xh, xl = _split_bf16(x)
    wh, wl = _split_bf16(w)
    dg = functools.partial(lax.dot_general, dimension_numbers=dims, preferred_element_type=F32)
    return dg(xh, wh) + (dg(xl, wh) + dg(xh, wl))


def _gelu_tanh(x):
    return 0.5 * x * (1.0 + jnp.tanh(math.sqrt(2.0 / math.pi) * (x + 0.044715 * (x * x * x))))


def _mod_kernel(c_ref, w_ref, b_ref, o_ref):
    c = c_ref[...]
    s = (c * _sigmoid(c)).astype(BF16)
    o_ref[...] = jnp.dot(s, w_ref[...].astype(BF16), preferred_element_type=F32) + b_ref[...]


def _modulation(cond, ada_w, ada_b):
    depth = ada_w.shape[0]
    rows = cond.shape[0]
    tn = 1536
    return pl.pallas_call(
        _mod_kernel,
        grid=(depth, 6 * D_MODEL // tn),
        in_specs=[
            pl.BlockSpec((rows, D_MODEL), lambda l, j: (0, 0)),
            pl.BlockSpec((None, D_MODEL, tn), lambda l, j: (l, 0, j)),
            pl.BlockSpec((None, 1, tn), lambda l, j: (l, 0, j)),
        ],
        out_specs=pl.BlockSpec((None, rows, tn), lambda l, j: (l, 0, j)),
        out_shape=jax.ShapeDtypeStruct((depth, rows, 6 * D_MODEL), F32),
        compiler_params=_cparams(("parallel", "parallel")),
        name="adaln_mod",
    )(cond, ada_w, ada_b.reshape(depth, 1, 6 * D_MODEL))


IN_PAD_SIZES = (S5_WIDTH, SSD_WIDTH, SSD_CONV_CH, DT_PAD, 3 * RWKV_WIDTH, LO_PAD)


def _inproj_kernel(x_ref, mod_ref, g_ref, w_ref, us5_ref, z_ref, xbc_ref, dt_ref, rkv_ref, lo_ref):
    x = x_ref[...]
    y = x * lax.rsqrt(jnp.mean(x * x, axis=-1, keepdims=True) + EPS) * g_ref[...]
    m = mod_ref[...]
    h = y * (1.0 + m[:, D_MODEL:2 * D_MODEL]) + m[:, 0:D_MODEL]
    p = jnp.dot(h.astype(BF16), w_ref[...], preferred_element_type=F32)
    start = 0
    for ref, size in zip((us5_ref, z_ref, xbc_ref, dt_ref, rkv_ref, lo_ref), IN_PAD_SIZES):
        ref[...] = p[:, start:start + size]
        start += size


def _pad_in_weight(w_in):
    parts, start = [], 0
    for s in IN_SIZES:
        parts.append(w_in[..., start:start + s])
        start += s
    us5, z, xbc, dt, rkv, wlo, alo, glo = parts
    zeros = lambda n: jnp.zeros(w_in.shape[:-1] + (n,), w_in.dtype)
    lo_used = wlo.shape[-1] + alo.shape[-1] + glo.shape[-1]
    return jnp.concatenate([us5, z, xbc, dt, zeros(DT_PAD - dt.shape[-1]), rkv, wlo, alo, glo,
                            zeros(LO_PAD - lo_used)], axis=-1).astype(BF16)


def _in_projection(x, mod3, row_of_block, norm_g, w_pad):
    ntok = x.shape[0]
    tm = TOKEN_BLOCK
    width = w_pad.shape[1]
    outs = tuple(jax.ShapeDtypeStruct((ntok, s), F32) for s in IN_PAD_SIZES)
    return pl.pallas_call(
        _inproj_kernel,
        grid=(ntok // tm,),
        in_specs=[
            pl.BlockSpec((tm, D_MODEL), lambda i: (i, 0)),
            pl.BlockSpec((None, 1, 6 * D_MODEL), lambda i: (row_of_block(i), 0, 0)),
            _full((1, D_MODEL)),
            _full((D_MODEL, width)),
        ],
        out_specs=tuple(pl.BlockSpec((tm, s), lambda i: (i, 0)) for s in IN_PAD_SIZES),
        out_shape=outs,
        compiler_params=_cparams(("parallel",)),
        name="in_proj",
    )(x, mod3, norm_g.reshape(1, D_MODEL), w_pad)


def _s5_disc_kernel(are_ref, aim_ref, ldt_ref, bre_ref, bim_ref, abre_ref, abim_ref, bbre_ref, bbim_ref):
    lam_re = jnp.minimum(are_ref[...], -1e-4)
    lam_im = aim_ref[...]
    dt = jnp.exp(ldt_ref[...])
    mag = jnp.exp(lam_re * dt)
    ab_re = mag * jnp.cos(lam_im * dt)
    ab_im = mag * jnp.sin(lam_im * dt)
    num_re, num_im = ab_re - 1.0, ab_im
    den = lam_re * lam_re + lam_im * lam_im
    q_re = (num_re * lam_re + num_im * lam_im) / den
    q_im = (num_im * lam_re - num_re * lam_im) / den
    abre_ref[...] = ab_re
    abim_ref[...] = ab_im
    b_re = bre_ref[...]
    b_im = bim_ref[...]
    qr = q_re[:, None, :]
    qi = q_im[:, None, :]
    bbre_ref[...] = qr * b_re - qi * b_im
    bbim_ref[...] = qr * b_im + qi * b_re


def _s5_discretize(a_re, a_im, log_dt, b_re, b_im):
    lead = a_re.shape[:3]
    n = lead[0] * lead[1] * lead[2]
    a2 = lambda t: t.reshape(n, S5_STATE)
    ldt = jnp.broadcast_to(log_dt.reshape(n, 1), (n, S5_STATE))
    b3 = lambda t: t.reshape(n, S5_STATE, S5_CH).transpose(0, 2, 1)
    ab_re, ab_im, bb_re, bb_im = pl.pallas_call(
        _s5_disc_kernel,
        out_shape=(jax.ShapeDtypeStruct((n, S5_STATE), F32), jax.ShapeDtypeStruct((n, S5_STATE), F32),
                   jax.ShapeDtypeStruct((n, S5_CH, S5_STATE), F32), jax.ShapeDtypeStruct((n, S5_CH, S5_STATE), F32)),
        name="s5_discretize",
    )(a2(a_re), a2(a_im), ldt, b3(b_re), b3(b_im))
    return (ab_re.reshape(lead + (S5_STATE,)), ab_im.reshape(lead + (S5_STATE,)),
            bb_re.reshape(lead + (S5_CH, S5_STATE)), bb_im.reshape(lead + (S5_CH, S5_STATE)))


def _s5_layer_tables(ab_re, ab_im, bb_re, bb_im, c_re, c_im):
    nfs = S5_GROUPS // 8
    eye = jnp.eye(8, dtype=F32)

    def rows(t):
        return t.reshape(N_DIR, nfs, 8 * S5_STATE)

    ab_row = jnp.concatenate([rows(ab_re), rows(ab_im)], axis=-1).reshape(N_DIR, nfs, 1, 2 * S5_SLICE_ST)

    def bmat(t):
        t = t.reshape(N_DIR, nfs, 8, S5_CH, S5_STATE)
        return jnp.einsum('dfghp,gk->dfghkp', t, eye).reshape(N_DIR, nfs, S5_SLICE_CH, S5_SLICE_ST)

    b_mat = jnp.concatenate([bmat(bb_re), bmat(bb_im)], axis=-1).astype(BF16)

    def cmat(t):
        t = t.reshape(N_DIR, nfs, 8, S5_CH, S5_STATE)
        return jnp.einsum('dfghp,gk->dfgpkh', t, eye).reshape(N_DIR, nfs, S5_SLICE_ST, S5_SLICE_CH)

    c_mat = jnp.concatenate([cmat(c_re), -cmat(c_im)], axis=-2).astype(BF16)
    return ab_row, b_mat, c_mat


def _s5_kernel(*refs, R, n_slab, chained, want_final):
    if chained:
        u_ref, h0_ref, ab_ref, bm_ref, cm_ref, d_ref = refs[:6]
        rest = refs[6:]
    else:
        u_ref, ab_ref, bm_ref, cm_ref, d_ref = refs[:5]
        h0_ref = None
        rest = refs[5:]
    y_ref = rest[0]
    rest = rest[1:]
    if want_final:
        hfin_ref = rest[0]
        rest = rest[1:]
    H = rest[0]
    if chained:
        PW, CIN = rest[1], rest[2]
    NR = R * n_slab
    RC = min(512, NR)
    ST = S5_SLICE_ST
    nchunk = ST // LANES

    y_ref[...] = u_ref[...] * d_ref[...]

    for d in range(N_DIR):
        def slab_of(i, d=d):
            return i if d == 0 else n_slab - 1 - i

        def bu_body(i, _, d=d):
            r0 = pl.multiple_of(i * RC, RC)
            H[pl.ds(r0, RC), :] = jnp.dot(u_ref[pl.ds(r0, RC), :].astype(BF16), bm_ref[d],
                                           preferred_element_type=F32)
            return 0
        lax.fori_loop(0, NR // RC, bu_body, 0)

        for c in range(nchunk):
            lre = slice(c * LANES, (c + 1) * LANES)
            lim = slice(ST + c * LANES, ST + (c + 1) * LANES)
            a_re = jnp.broadcast_to(ab_ref[d, :, lre], (R, LANES))
            a_im = jnp.broadcast_to(ab_ref[d, :, lim], (R, LANES))

            def step(i, carry, lre=lre, lim=lim, a_re=a_re, a_im=a_im, slab_of=slab_of):
                cr, ci = carry
                r0 = pl.multiple_of(slab_of(i) * R, R)
                nr = a_re * cr - a_im * ci + H[pl.ds(r0, R), lre]
                ni = a_re * ci + a_im * cr + H[pl.ds(r0, R), lim]
                H[pl.ds(r0, R), lre] = nr
                H[pl.ds(r0, R), lim] = ni
                return nr, ni
            zero = jnp.zeros((R, LANES), F32)
            lax.fori_loop(0, n_slab, step, (zero, zero))

        last0 = (n_slab - 1) * R if d == 0 else 0
        if chained:
            a_re_row = ab_ref[d, :, 0:ST]
            a_im_row = ab_ref[d, :, ST:2 * ST]

            def pw_step(j, carry, a_re_row=a_re_row, a_im_row=a_im_row):
                pr, pi = carry
                PW[j, :, 0:ST] = jnp.broadcast_to(pr, (8, ST))
                PW[j, :, ST:2 * ST] = jnp.broadcast_to(pi, (8, ST))
                return pr * a_re_row - pi * a_im_row, pr * a_im_row + pi * a_re_row
            lax.fori_loop(0, n_slab, pw_step, (a_re_row, a_im_row))
            t_re = PW[n_slab - 1, 0:1, 0:ST]
            t_im = PW[n_slab - 1, 0:1, ST:2 * ST]

            cr = h0_ref[d, :, 0:ST]
            ci = h0_ref[d, :, ST:2 * ST]
            for i in range(R):
                c = i if d == 0 else R - 1 - i
                CIN[c:c + 1, 0:ST] = cr
                CIN[c:c + 1, ST:2 * ST] = ci
                er = H[last0 + c:last0 + c + 1, 0:ST]
                ei = H[last0 + c:last0 + c + 1, ST:2 * ST]
                cr, ci = t_re * cr - t_im * ci + er, t_re * ci + t_im * cr + ei

            def fix_step(i, _, slab_of=slab_of):
                r0 = pl.multiple_of(slab_of(i) * R, R)
                p = PW[i]
                for c in range(nchunk):
                    lre = slice(c * LANES, (c + 1) * LANES)
                    lim = slice(ST + c * LANES, ST + (c + 1) * LANES)
                    pr = p[0:1, lre]
                    pi = p[0:1, lim]
                    cr = CIN[:, lre]
                    ci = CIN[:, lim]
                    H[pl.ds(r0, R), lre] = H[pl.ds(r0, R), lre] + (pr * cr - pi * ci)
                    H[pl.ds(r0, R), lim] = H[pl.ds(r0, R), lim] + (pr * ci + pi * cr)
                return 0
            lax.fori_loop(0, n_slab, fix_step, 0)

        if want_final:
            hfin_ref[d] = H[last0:last0 + R, :]

        def y_body(i, _, d=d):
            r0 = pl.multiple_of(i * RC, RC)
            y_ref[pl.ds(r0, RC), :] = y_ref[pl.ds(r0, RC), :] + jnp.dot(
                H[pl.ds(r0, RC), :].astype(BF16), cm_ref[d], preferred_element_type=F32)
            return 0
        lax.fori_loop(0, NR // RC, y_body, 0)


def _s5_scan(u, h0, tables, d_row, *, R, n_slab, chained, want_final):
    ab_row, b_mat, c_mat = tables
    Q, NR, _ = u.shape
    nfs = S5_GROUPS // 8
    W2 = 2 * S5_SLICE_ST
    in_specs = [pl.BlockSpec((None, NR, S5_SLICE_CH), lambda q, f: (q, 0, f))]
    args = [u]
    if chained:
        in_specs.append(pl.BlockSpec((None, N_DIR, None, 1, W2), lambda q, f: (q, 0, f, 0, 0)))
        args.append(h0)
    in_specs += [
        pl.BlockSpec((N_DIR, None, 1, W2), lambda q, f: (0, f, 0, 0)),
        pl.BlockSpec((N_DIR, None, S5_SLICE_CH, W2), lambda q, f: (0, f, 0, 0)),
        pl.BlockSpec((N_DIR, None, W2, S5_SLICE_CH), lambda q, f: (0, f, 0, 0)),
        pl.BlockSpec((1, S5_SLICE_CH), lambda q, f: (0, f)),
    ]
    args += [ab_row, b_mat, c_mat, d_row]
    out_shape = [jax.ShapeDtypeStruct((Q, NR, S5_WIDTH), F32)]
    out_specs = [pl.BlockSpec((None, NR, S5_SLICE_CH), lambda q, f: (q, 0, f))]
    if want_final:
        out_shape.append(jax.ShapeDtypeStruct((Q, N_DIR, nfs, R, W2), F32))
        out_specs.append(pl.BlockSpec((None, N_DIR, None, R, W2), lambda q, f: (q, 0, f, 0, 0)))
    scratch = [pltpu.VMEM((NR, W2), F32)]
    if chained:
        scratch += [pltpu.VMEM((n_slab, 8, W2), F32), pltpu.VMEM((R, W2), F32)]
    res = pl.pallas_call(
        functools.partial(_s5_kernel, R=R, n_slab=n_slab, chained=chained, want_final=want_final),
        grid=(Q, nfs),
        in_specs=in_specs,
        out_specs=tuple(out_specs),
        out_shape=tuple(out_shape),
        scratch_shapes=scratch,
        compiler_params=_cparams(("parallel", "parallel")),
        name="s5_scan_chained" if chained else "s5_scan",
    )(*args)
    return res


def _ssd_kernel(*refs, TB, nb, has_h0, want_final):
    xbc_ref, xp_ref, xn_ref, dt_ref = refs[:4]
    refs = refs[4:]
    if has_h0:
        h0_ref = refs[0]
        refs = refs[1:]
    cw_ref, cb_ref, dtb_ref, arow_ref, sel_ref, drow_ref = refs[:6]
    refs = refs[6:]
    y_ref = refs[0]
    refs = refs[1:]
    if want_final:
        hfin_ref = refs[0]
        refs = refs[1:]
    hst, xc_s, dt_s = refs
    CH = SSD_CHUNK
    P = SSD_HEADDIM
    d = pl.program_id(1)
    j = pl.program_id(2)
    jj = jnp.where(d == 0, j, nb - 1 - j)

    @pl.when(j == 0)
    def _():
        if has_h0:
            hst[...] = h0_ref[...]
        else:
            hst[...] = jnp.zeros_like(hst)

    x = xbc_ref[...]
    prev_row = xp_ref[7:8, :] * (jj > 0).astype(F32)
    next_row = xn_ref[0:1, :] * (jj < nb - 1).astype(F32)
    rows = lax.broadcasted_iota(jnp.int32, (TB, 1), 0)
    x_prev = jnp.where(rows == 0, prev_row, pltpu.roll(x, 1, 0))
    x_next = jnp.where(rows == TB - 1, next_row, pltpu.roll(x, TB - 1, 0))
    conv = cw_ref[0:1, :] * x_prev + cw_ref[1:2, :] * x + cw_ref[2:3, :] * x_next + cb_ref[...]
    xc_s[...] = conv * _sigmoid(conv)
    dtf = _softplus(dt_ref[...] + dtb_ref[...])
    dt_s[...] = _dot_exact_rhs(dtf, sel_ref[...])

    li = lax.broadcasted_iota(jnp.int32, (CH, CH), 0)
    si = lax.broadcasted_iota(jnp.int32, (CH, CH), 1)
    tmat = jnp.where(d == 0, (si <= li).astype(F32), (si >= li).astype(F32))
    is_fwd = (d == 0).astype(F32)
    n_ch = TB // CH
    for i in range(n_ch):
        ci = jnp.where(d == 0, i, n_ch - 1 - i)
        r0 = pl.multiple_of(ci * CH, CH)
        dtc = dt_s[pl.ds(r0, CH), :]
        dA = dtc * arow_ref[...]
        cs = _dot_exact_lhs(tmat, dA)
        csT = cs.T
        tot = jnp.sum(dA, axis=0, keepdims=True)
        Bm = xc_s[pl.ds(r0, CH), SSD_WIDTH:SSD_WIDTH + SSD_BC]
        Cm = xc_s[pl.ds(r0, CH), SSD_WIDTH + SSD_BC:SSD_WIDTH + 2 * SSD_BC]
        BmT = Bm.T
        for g in range(SSD_GROUPS):
            Cg = Cm[:, g * SSD_STATE:(g + 1) * SSD_STATE].astype(BF16)
            Bg = Bm[:, g * SSD_STATE:(g + 1) * SSD_STATE].astype(BF16)
            BgT = BmT[g * SSD_STATE:(g + 1) * SSD_STATE, :].astype(BF16)
            G = lax.dot_general(Cg, Bg, (((1,), (1,)), ((), ())), preferred_element_type=F32)
            for hh in range(SSD_HEADS // SSD_GROUPS):
                h = g * (SSD_HEADS // SSD_GROUPS) + hh
                col = cs[:, h:h + 1]
                row = csT[h:h + 1, :]
                lm = jnp.exp(jnp.where(tmat > 0.0, col - row, -1e30))
                xs_h = xc_s[pl.ds(r0, CH), h * P:(h + 1) * P]
                xdt = xs_h * dtc[:, h:h + 1]
                tot_h = tot[:, h:h + 1]
                hprev = hst[h]
                y = jnp.dot((G * lm).astype(BF16), xdt.astype(BF16), preferred_element_type=F32)
                y = y + jnp.dot(Cg, hprev.astype(BF16), preferred_element_type=F32) * jnp.exp(col)
                y = y + (is_fwd * drow_ref[:, h * P:(h + 1) * P]) * xs_h
                y_ref[pl.ds(r0, CH), h * P:(h + 1) * P] = y
                xd = (xdt * jnp.exp(tot_h - col)).astype(BF16)
                hst[h] = jnp.exp(tot_h) * hprev + jnp.dot(BgT, xd, preferred_element_type=F32)

    if want_final:
        @pl.when(j == nb - 1)
        def _():
            hfin_ref[...] = hst[...]


def _ssd_scan(xbc, dt, h0, conv_w, conv_b, dt_bias, a_log, d_skip, *, B, L, want_final):
    TB = min(512, L)
    nb = L // TB
    H = SSD_HEADS

    def blk(b, d, j):
        return b * nb + j + d * (nb - 1 - 2 * j)

    nrow8 = B * L // 8
    in_specs = [
        pl.BlockSpec((TB, SSD_CONV_CH), lambda b, d, j: (blk(b, d, j), 0)),
        pl.BlockSpec((8, SSD_CONV_CH), lambda b, d, j: (jnp.maximum(blk(b, d, j) * (TB // 8) - 1, 0), 0)),
        pl.BlockSpec((8, SSD_CONV_CH), lambda b, d, j: (jnp.minimum((blk(b, d, j) + 1) * (TB // 8), nrow8 - 1), 0)),
        pl.BlockSpec((TB, DT_PAD), lambda b, d, j: (blk(b, d, j), 0)),
    ]
    args = [xbc, xbc, xbc, dt]
    if h0 is not None:
        in_specs.append(pl.BlockSpec((None, None, H, SSD_STATE, SSD_HEADDIM), lambda b, d, j: (b, d, 0, 0, 0)))
        args.append(h0)
    dtb = jnp.pad(dt_bias.reshape(1, N_DIR * H), ((0, 0), (0, DT_PAD - N_DIR * H)))
    arow = jnp.pad(-jnp.exp(a_log), ((0, 0), (0, LANES - H))).reshape(N_DIR, 1, LANES)
    lane = jnp.arange(LANES)
    sel = jnp.stack([(lane[:, None] == (dd * H + lane[None, :])) & (lane[None, :] < H) for dd in range(N_DIR)]).astype(F32)
    drow = jnp.repeat(d_skip, SSD_HEADDIM).reshape(1, SSD_WIDTH)
    in_specs += [
        _full((3, SSD_CONV_CH)), _full((1, SSD_CONV_CH)), _full((1, DT_PAD)),
        pl.BlockSpec((None, 1, LANES), lambda b, d, j: (d, 0, 0)),
        pl.BlockSpec((None, LANES, LANES), lambda b, d, j: (d, 0, 0)),
        _full((1, SSD_WIDTH)),
    ]
    args += [conv_w, conv_b.reshape(1, SSD_CONV_CH), dtb, arow, sel, drow]
    out_shape = [jax.ShapeDtypeStruct((N_DIR, B * L, SSD_WIDTH), F32)]
    out_specs = [pl.BlockSpec((None, TB, SSD_WIDTH), lambda b, d, j: (d, blk(b, d, j), 0))]
    if want_final:
        out_shape.append(jax.ShapeDtypeStruct((B, N_DIR, H, SSD_STATE, SSD_HEADDIM), F32))
        out_specs.append(pl.BlockSpec((None, None, H, SSD_STATE, SSD_HEADDIM), lambda b, d, j: (b, d, 0, 0, 0)))
    return pl.pallas_call(
        functools.partial(_ssd_kernel, TB=TB, nb=nb, has_h0=h0 is not None, want_final=want_final),
        grid=(B, N_DIR, nb),
        in_specs=in_specs,
        out_specs=tuple(out_specs),
        out_shape=tuple(out_shape),
        scratch_shapes=[pltpu.VMEM((H, SSD_STATE, SSD_HEADDIM), F32), pltpu.VMEM((TB, SSD_CONV_CH), F32),
                        pltpu.VMEM((TB, DT_PAD), F32)],
        compiler_params=_cparams(("parallel", "parallel", "arbitrary")),
        name="ssd_scan",
    )(*args)


def _rwkv_prep_kernel(rkv_ref, rp_ref, rn_ref, lo_ref, mu_ref, a0_ref, aup_ref, gup_ref, w0_ref, wup_ref,
                      kkw_ref, ka_ref, rk_ref, seg_ref,
                      r_ref, w0o_ref, w1o_ref, k_ref, v_ref, kk_ref, b_ref, g_ref, bonus_ref, *, nbs):
    tm = rkv_ref.shape[0]
    W = RWKV_WIDTH
    jj = pl.program_id(0) % nbs
    x = rkv_ref[...]
    prev_row = rp_ref[7:8, :] * (jj > 0).astype(F32)
    next_row = rn_ref[0:1, :] * (jj < nbs - 1).astype(F32)
    rows = lax.broadcasted_iota(jnp.int32, (tm, 1), 0)
    xp = jnp.where(rows == 0, prev_row, pltpu.roll(x, 1, 0))
    xn = jnp.where(rows == tm - 1, next_row, pltpu.roll(x, tm - 1, 0))
    x = x + mu_ref[0:1, :] * (xp - x) + mu_ref[1:2, :] * (xn - x)
    r = x[:, 0:W]
    k = x[:, W:2 * W]
    v = x[:, 2 * W:3 * W]
    lo = lo_ref[...]
    seg = seg_ref[...]

    def put(ref, val):
        ref[:, 0:W] = val
        if ref.shape[1] > W:
            ref[:, W:] = jnp.zeros((tm, ref.shape[1] - W), F32)

    a = _sigmoid(a0_ref[...] + _dot3(lo, aup_ref[...]))
    g_ref[...] = _dot3(_sigmoid(lo), gup_ref[...])
    tlo = jnp.tanh(lo)
    for d, o_ref in enumerate((w0o_ref, w1o_ref)):
        zw = w0_ref[d:d + 1, :] + _dot3(tlo, wup_ref[d])
        put(o_ref, jnp.exp(-DECAY_SCALE * _sigmoid(zw)))
    kk = k * kkw_ref[...]
    kk = kk * lax.rsqrt(jnp.maximum(_dot_exact_rhs(kk * kk, seg), 1e-24))
    k2 = k * (1.0 + (a - 1.0) * ka_ref[...])
    put(r_ref, r)
    put(k_ref, k2)
    put(v_ref, v)
    put(kk_ref, kk)
    put(b_ref, kk * a)
    bonus_ref[...] = _dot_exact_rhs(r * k2 * rk_ref[...], seg) * v


def _segment_ones(width, seg):
    i = jnp.arange(width)
    return (i[:, None] // seg == i[None, :] // seg).astype(F32)


def _rwkv_prep(rkv, lo, lp, seq_blocks, scan_width=RWKV_WIDTH):
    ntok = rkv.shape[0]
    tm = TOKEN_BLOCK
    W = RWKV_WIDTH
    nrow8 = ntok // 8
    nbs_of = seq_blocks
    widths = (scan_width,) * 7 + (W, W)
    pad_rows = lambda t, r0: jnp.zeros((LO_PAD, W), F32).at[r0:r0 + t.shape[0]].set(t)
    aup = pad_rows(lp['rwkv_a_up'], N_DIR * W_RANK)
    gup = pad_rows(lp['rwkv_g_up'], N_DIR * W_RANK + A_RANK)
    wup = jnp.stack([pad_rows(lp['rwkv_w_up'][d], d * W_RANK) for d in range(N_DIR)])
    row = lambda t: t.reshape(1, W)
    outs = tuple(jax.ShapeDtypeStruct((ntok, w), F32) for w in widths)
    return pl.pallas_call(
        functools.partial(_rwkv_prep_kernel, nbs=nbs_of),
        grid=(ntok // tm,),
        in_specs=[
            pl.BlockSpec((tm, 3 * W), lambda i: (i, 0)),
            pl.BlockSpec((8, 3 * W), lambda i: (jnp.maximum(i * (tm // 8) - 1, 0), 0)),
            pl.BlockSpec((8, 3 * W), lambda i: (jnp.minimum((i + 1) * (tm // 8), nrow8 - 1), 0)),
            pl.BlockSpec((tm, LO_PAD), lambda i: (i, 0)),
            _full((2, 3 * W)), _full((1, W)), _full((LO_PAD, W)), _full((LO_PAD, W)), _full((2, W)),
            _full((N_DIR, LO_PAD, W)), _full((1, W)), _full((1, W)), _full((1, W)), _full((W, W)),
        ],
        out_specs=tuple(pl.BlockSpec((tm, w), lambda i: (i, 0)) for w in widths),
        out_shape=outs,
        compiler_params=_cparams(("parallel",)),
        name="rwkv_prep",
    )(rkv, rkv, rkv, lo, lp['rwkv_mu'], row(lp['rwkv_a0']), aup, gup, lp['rwkv_w0'], wup,
      row(lp['rwkv_k_k']), row(lp['rwkv_k_a']), row(lp['rwkv_r_k']), lp['seg'])


def _rwkv_scan_kernel(r_ref, w_ref, k_ref, kk_ref, b_ref, v_ref, s0_ref, y_ref, sfin_ref, S, *, TBLK, V, nblk):
    K = RWKV_HEADSIZE
    tb = pl.program_id(1)

    @pl.when(tb == 0)
    def _():
        S[...] = s0_ref[...]

    def bc(ref, t, kx):
        return jnp.broadcast_to(ref[t, pl.ds(kx, 1), :], (V, LANES))

    NACC = 4

    def step(t, _):
        accs = [None] * NACC
        for kx in range(K):
            p = S[kx] * bc(kk_ref, t, kx)
            a = kx % NACC
            accs[a] = p if accs[a] is None else accs[a] + p
        sa = (accs[0] + accs[1]) + (accs[2] + accs[3])
        vt = v_ref[t]
        yacc = [None] * NACC
        for kx in range(K):
            s_new = S[kx] * bc(w_ref, t, kx) - sa * bc(b_ref, t, kx) + vt * bc(k_ref, t, kx)
            S[kx] = s_new
            p = s_new * bc(r_ref, t, kx)
            a = kx % NACC
            yacc[a] = p if yacc[a] is None else yacc[a] + p
        y_ref[t] = (yacc[0] + yacc[1]) + (yacc[2] + yacc[3])
        return 0
    lax.fori_loop(0, TBLK, step, 0)

    @pl.when(tb == nblk - 1)
    def _():
        sfin_ref[...] = S[...]


def _rwkv_scan(r, w, k, kk, b, v, s0):
    L, K, NL = r.shape
    V = v.shape[1]
    TBLK = min(64, L)
    nblk = L // TBLK
    ngrp = NL // LANES
    kspec = pl.BlockSpec((TBLK, K, LANES), lambda g, t: (t, 0, g))
    vspec = pl.BlockSpec((TBLK, V, LANES), lambda g, t: (t, 0, g))
    sspec = pl.BlockSpec((K, V, LANES), lambda g, t: (0, 0, g))
    return pl.pallas_call(
        functools.partial(_rwkv_scan_kernel, TBLK=TBLK, V=V, nblk=nblk),
        grid=(ngrp, nblk),
        in_specs=[kspec, kspec, kspec, kspec, kspec, vspec, sspec],
        out_specs=(vspec, sspec),
        out_shape=(jax.ShapeDtypeStruct((L, V, NL), F32), jax.ShapeDtypeStruct((K, V, NL), F32)),
        scratch_shapes=[pltpu.VMEM((K, V, LANES), F32)],
        compiler_params=_cparams(("parallel", "arbitrary")),
        name="rwkv_scan",
    )(r, w, k, kk, b, v, s0)


KQ = 4
CH = LANES // KQ


def _rwkv_scan2_kernel(rf, rm, wf, wm, kf, km, kkf, kkm, bf, bm, vf, vm, s0_ref, yf_ref, yb_ref, S, YQ, VT, SA, *,
                       TBLK):
    V = RWKV_HEADSIZE
    NQ = RWKV_HEADSIZE // KQ
    tb = pl.program_id(0)

    @pl.when(tb == 0)
    def _():
        S[...] = s0_ref[...]

    VH = V // 2

    def bc(ref, tt, q):
        return jnp.broadcast_to(ref[tt, pl.ds(q, 1), :], (VH, LANES))

    def all_quarters(x):
        return (x + pltpu.roll(x, CH, 1)) + (pltpu.roll(x, 2 * CH, 1) + pltpu.roll(x, 3 * CH, 1))

    dirs = ((rf, wf, kf, kkf, bf, vf), (rm, wm, km, kkm, bm, vm))
    time_of = (lambda t: t, lambda t: TBLK - 1 - t)
    chains = [(d, hv) for d in range(N_DIR) for hv in range(2)]

    def first_sa(d, hv):
        kk_ = dirs[d][3]
        tt = time_of[d](0)
        acc = [None, None]
        for q in range(NQ):
            p = S[d, q, pl.ds(hv * VH, VH), :] * bc(kk_, tt, q)
            acc[q % 2] = p if acc[q % 2] is None else acc[q % 2] + p
        return all_quarters(acc[0] + acc[1])

    TG = 8
    lane_c = lax.broadcasted_iota(jnp.int32, (CH, LANES), 1) % CH
    rep = jnp.where(lax.broadcasted_iota(jnp.int32, (CH, LANES), 0) == lane_c, 1.0, 0.0)
    lane_r = lax.broadcasted_iota(jnp.int32, (LANES, CH), 0) % CH
    fold = jnp.where(lax.broadcasted_iota(jnp.int32, (LANES, CH), 1) == lane_r, 1.0, 0.0)

    def spread(i, _):
        for d in range(N_DIR):
            v = dirs[d][5][pl.ds(i * TG, TG)].reshape(TG * V, CH)
            VT[d, pl.ds(i * TG, TG)] = _dot_exact_rhs(v, rep).reshape(TG, V, LANES)
        return 0
    lax.fori_loop(0, TBLK // TG, spread, 0)

    for c, (d, hv) in enumerate(chains):
        SA[c] = first_sa(d, hv)

    def step(t, _):
        for c, (d, hv) in enumerate(chains):
            r_, w_, k_, kk_, b_, _ = dirs[d]
            tt = time_of[d](t)
            tn = time_of[d](jnp.minimum(t + 1, TBLK - 1))
            rows = pl.ds(hv * VH, VH)
            sa = SA[c]
            vt = VT[d, tt, rows, :]
            yacc = None
            sacc = None
            for q in range(NQ):
                s_new = S[d, q, rows, :] * bc(w_, tt, q) - sa * bc(b_, tt, q) + vt * bc(k_, tt, q)
                S[d, q, rows, :] = s_new
                py = s_new * bc(r_, tt, q)
                ps = s_new * bc(kk_, tn, q)
                yacc = py if yacc is None else yacc + py
                sacc = ps if sacc is None else sacc + ps
            YQ[d, tt, rows, :] = yacc
            SA[c] = all_quarters(sacc)
        return 0
    lax.fori_loop(0, TBLK, step, 0)

    def finish(i, _):
        for d, y_ in enumerate((yf_ref, yb_ref)):
            x = YQ[d, pl.ds(i * TG, TG)].reshape(TG * V, LANES)
            y_[pl.ds(i * TG, TG)] = _dot_exact_rhs(x, fold).reshape(TG, V, CH)
        return 0
    lax.fori_loop(0, TBLK // TG, finish, 0)


def _rwkv_scan2(r, w0, w1, k, kk, b, v, s0):
    L, NQ, _ = r.shape
    V = RWKV_HEADSIZE
    TBLK = min(64, L)
    nblk = L // TBLK
    fspec = pl.BlockSpec((TBLK, NQ, LANES), lambda t: (t, 0, 0))
    mspec = pl.BlockSpec((TBLK, NQ, LANES), lambda t: (nblk - 1 - t, 0, 0))
    vfspec = pl.BlockSpec((TBLK, V, CH), lambda t: (t, 0, 0))
    vmspec = pl.BlockSpec((TBLK, V, CH), lambda t: (nblk - 1 - t, 0, 0))
    yf, yb = pl.pallas_call(
        functools.partial(_rwkv_scan2_kernel, TBLK=TBLK),
        grid=(nblk,),
        in_specs=[fspec, mspec] * 5 + [vfspec, vmspec, _full((N_DIR, NQ, V, LANES))],
        out_specs=(vfspec, vmspec),
        out_shape=(jax.ShapeDtypeStruct((L, V, CH), F32), jax.ShapeDtypeStruct((L, V, CH), F32)),
        scratch_shapes=[pltpu.VMEM((N_DIR, NQ, V, LANES), F32), pltpu.VMEM((N_DIR, TBLK, V, LANES), F32),
                        pltpu.VMEM((N_DIR, TBLK, V, LANES), F32), pltpu.VMEM((2 * N_DIR, V // 2, LANES), F32)],
        compiler_params=_cparams(("arbitrary",)),
        name="rwkv_scan2",
    )(r, r, w0, w1, k, k, kk, kk, b, b, v, v, s0)
    return yf, yb


HEADS_PAD = 8
SCAN_WIDTH = HEADS_PAD * RWKV_HEADSIZE


def _pad_chains(x, B):
    return x if B * HEADS_PAD == CH else jnp.pad(x, ((0, 0),) * (x.ndim - 1) + ((0, CH - B * HEADS_PAD),))


def _to_scan2_k(t, B, L):
    x = t.reshape(B, L, HEADS_PAD, RWKV_HEADSIZE // KQ, KQ).transpose(1, 3, 4, 0, 2)
    x = _pad_chains(x.reshape(L, RWKV_HEADSIZE // KQ, KQ, B * HEADS_PAD), B)
    return x.reshape(L, RWKV_HEADSIZE // KQ, LANES)


def _to_scan2_v(t, B, L):
    x = t.reshape(B, L, HEADS_PAD, RWKV_HEADSIZE).transpose(1, 3, 0, 2).reshape(L, RWKV_HEADSIZE, B * HEADS_PAD)
    return _pad_chains(x, B)


def _from_scan2(y, B, L):
    y = y[:, :, :B * HEADS_PAD]
    return y.reshape(L, RWKV_HEADSIZE, B, HEADS_PAD).transpose(2, 0, 3, 1).reshape(B * L, SCAN_WIDTH)


def _state_to_scan2(s):
    B = s.shape[0]
    s = jnp.pad(s, ((0, 0), (0, 0), (0, HEADS_PAD - RWKV_HEADS), (0, 0), (0, 0)))
    x = s.transpose(1, 4, 3, 0, 2).reshape(N_DIR, RWKV_HEADSIZE // KQ, KQ, RWKV_HEADSIZE, B * HEADS_PAD)
    x = _pad_chains(x, B)
    return x.transpose(0, 1, 3, 2, 4).reshape(N_DIR, RWKV_HEADSIZE // KQ, RWKV_HEADSIZE, LANES)


def _rwkv_layout(B, vsplit):
    chains = N_DIR * B * RWKV_HEADS
    cp = -(-chains // (LANES // vsplit)) * (LANES // vsplit)
    return chains, cp


def _to_scan_k(t_fwd, t_bwd, B, L, vsplit):
    chains, cp = _rwkv_layout(B, vsplit)

    def one(t):
        return t.reshape(B, L, RWKV_HEADS, RWKV_HEADSIZE).transpose(1, 3, 0, 2).reshape(L, RWKV_HEADSIZE, B * RWKV_HEADS)
    x = jnp.concatenate([one(t_fwd), jnp.flip(one(t_bwd), 0)], axis=-1)
    x = jnp.pad(x, ((0, 0), (0, 0), (0, cp - chains)))
    return jnp.tile(x, (1, 1, vsplit))


def _to_scan_v(t, B, L, vsplit):
    chains, cp = _rwkv_layout(B, vsplit)
    x = t.reshape(B, L, RWKV_HEADS, RWKV_HEADSIZE).transpose(1, 3, 0, 2).reshape(L, RWKV_HEADSIZE, B * RWKV_HEADS)
    x = jnp.concatenate([x, jnp.flip(x, 0)], axis=-1)
    x = jnp.pad(x, ((0, 0), (0, 0), (0, cp - chains)))
    vs = RWKV_HEADSIZE // vsplit
    return x.reshape(L, vsplit, vs, cp).transpose(0, 2, 1, 3).reshape(L, vs, vsplit * cp)


def _from_scan_v(y, B, L, vsplit):
    chains, cp = _rwkv_layout(B, vsplit)
    vs = RWKV_HEADSIZE // vsplit
    y = y.reshape(L, vs, vsplit, cp).transpose(0, 2, 1, 3).reshape(L, RWKV_HEADSIZE, cp)[:, :, :chains]
    half = chains // 2

    def back(t):
        return t.reshape(L, RWKV_HEADSIZE, B, RWKV_HEADS).transpose(2, 0, 3, 1).reshape(B * L, RWKV_WIDTH)
    return back(y[:, :, :half]), back(jnp.flip(y[:, :, half:], 0))


def _state_to_scan(s, vsplit):
    B = s.shape[0]
    chains, cp = _rwkv_layout(B, vsplit)
    x = s.transpose(4, 3, 1, 0, 2).reshape(RWKV_HEADSIZE, RWKV_HEADSIZE, chains)
    x = jnp.pad(x, ((0, 0), (0, 0), (0, cp - chains)))
    vs = RWKV_HEADSIZE // vsplit
    return x.reshape(RWKV_HEADSIZE, vsplit, vs, cp).transpose(0, 2, 1, 3).reshape(RWKV_HEADSIZE, vs, vsplit * cp)


def _state_from_scan(x, B, vsplit):
    chains, cp = _rwkv_layout(B, vsplit)
    vs = RWKV_HEADSIZE // vsplit
    x = x.reshape(RWKV_HEADSIZE, vs, vsplit, cp).transpose(0, 2, 1, 3).reshape(RWKV_HEADSIZE, RWKV_HEADSIZE, cp)
    x = x[:, :, :chains].reshape(RWKV_HEADSIZE, RWKV_HEADSIZE, N_DIR, B, RWKV_HEADS)
    return x.transpose(3, 2, 4, 1, 0)


def _post_kernel(x_ref, mod_ref, ys5_ref, yf_ref, yb_ref, z_ref, rf_ref, rb_ref, bonus_ref, g_ref,
                 gluw_ref, glub_ref, ssdg_ref, lng_ref, lnb_ref, seg_ref, wout_ref, n2g_ref, rw_ref,
                 x1_ref, hb_ref, aff_ref):
    m = mod_ref[...]
    D = D_MODEL
    zg = _gelu_tanh(ys5_ref[...])
    gate = jnp.dot(zg.astype(BF16), gluw_ref[...], preferred_element_type=F32) + glub_ref[...]
    y_a = zg * _sigmoid(gate)
    z = z_ref[...]
    yb = (yf_ref[...] + yb_ref[...]) * (z * _sigmoid(z))
    y_b = yb * lax.rsqrt(jnp.mean(yb * yb, axis=-1, keepdims=True) + EPS) * ssdg_ref[...]
    seg = seg_ref[...] * (1.0 / RWKV_HEADSIZE)
    yr = rf_ref[...] + rb_ref[...]
    mean = _dot_exact_rhs(yr, seg)
    cen = yr - mean
    var = _dot_exact_rhs(cen * cen, seg)
    yn = cen * lax.rsqrt(var + GN_EPS) * lng_ref[...] + lnb_ref[...]
    y_c = (yn + bonus_ref[...]) * g_ref[...]
    o = jnp.dot(y_a.astype(BF16), wout_ref[0:S5_WIDTH, :], preferred_element_type=F32)
    o = o + jnp.dot(y_b.astype(BF16), wout_ref[S5_WIDTH:S5_WIDTH + SSD_WIDTH, :], preferred_element_type=F32)
    o = o + jnp.dot(y_c.astype(BF16), wout_ref[S5_WIDTH + SSD_WIDTH:, :], preferred_element_type=F32)
    x1 = x_ref[...] + m[:, 2 * D:3 * D] * o
    x1_ref[...] = x1
    h2 = x1 * lax.rsqrt(jnp.mean(x1 * x1, axis=-1, keepdims=True) + EPS) * n2g_ref[...]
    h2 = h2 * (1.0 + m[:, 4 * D:5 * D]) + m[:, 3 * D:4 * D]
    hb_ref[...] = h2.astype(BF16)
    logits = _dot3(rw_ref[...], h2, (((1,), (1,)), ((), ())))
    mx = jnp.max(logits, axis=0, keepdims=True)
    ex = jnp.exp(logits - mx)
    aff_ref[...] = ex / jnp.sum(ex, axis=0, keepdims=True)


def _post_mixer(x, mod3, row_of_block, ys5, yssd, z, rf, rb, bonus, g, lp):
    ntok = x.shape[0]
    tm = TOKEN_BLOCK
    tok = lambda w: pl.BlockSpec((tm, w), lambda i: (i, 0))
    row = lambda t: t.reshape(1, -1)
    W = RWKV_WIDTH
    return pl.pallas_call(
        _post_kernel,
        grid=(ntok // tm,),
        in_specs=[
            tok(D_MODEL),
            pl.BlockSpec((None, 1, 6 * D_MODEL), lambda i: (row_of_block(i), 0, 0)),
            tok(S5_WIDTH),
            pl.BlockSpec((None, tm, SSD_WIDTH), lambda i: (0, i, 0)),
            pl.BlockSpec((None, tm, SSD_WIDTH), lambda i: (1, i, 0)),
            tok(SSD_WIDTH), tok(W), tok(W), tok(W), tok(W),
            _full((S5_WIDTH, S5_WIDTH)), _full((1, S5_WIDTH)), _full((1, SSD_WIDTH)), _full((1, W)), _full((1, W)),
            _full((W, W)), _full((D_MODEL, D_MODEL)), _full((1, D_MODEL)), _full((N_EXPERTS, D_MODEL)),
        ],
        out_specs=(tok(D_MODEL), tok(D_MODEL), pl.BlockSpec((N_EXPERTS, tm), lambda i: (0, i))),
        out_shape=(jax.ShapeDtypeStruct((ntok, D_MODEL), F32), jax.ShapeDtypeStruct((ntok, D_MODEL), BF16),
                   jax.ShapeDtypeStruct((N_EXPERTS, ntok), F32)),
        compiler_params=_cparams(("parallel",)),
        name="post_mixer",
    )(x, mod3, ys5, yssd, yssd, z, rf, rb, bonus, g,
      lp['s5_glu_w'], row(lp['s5_glu_b']), row(lp['ssd_norm_g']), row(lp['rwkv_ln_g']),
      row(lp['rwkv_ln_b']), lp['seg'], lp['w_out'], row(lp['norm2_g']),
      lp['router_w'].T)


def _select_kernel(aff_ref, slot_ref, affo_ref, *, n, cap):
    E = N_EXPERTS
    a = aff_ref[...]
    bits = pltpu.bitcast(a, jnp.int32)
    thr = jnp.zeros((E, 1), jnp.int32)
    capf = float(cap)
    for bit in range(30, -1, -1):
        cand = thr | (1 << bit)
        cnt = jnp.sum(jnp.where(bits >= cand, 1.0, 0.0), axis=1, keepdims=True)
        thr = jnp.where(cnt >= capf, cand, thr)
    gt = bits > thr
    eq = bits == thr
    need = capf - jnp.sum(jnp.where(gt, 1.0, 0.0), axis=1, keepdims=True)
    CW = min(256, n)
    ui = lax.broadcasted_iota(jnp.int32, (CW, CW), 0)
    uj = lax.broadcasted_iota(jnp.int32, (CW, CW), 1)
    upper = jnp.where(ui < uj, 1.0, 0.0).astype(BF16)

    def excl_cumsum(mask_f):
        outs = []
        off = jnp.zeros((E, 1), F32)
        for c in range(n // CW):
            mc = mask_f[:, c * CW:(c + 1) * CW]
            outs.append(jnp.dot(mc.astype(BF16), upper, preferred_element_type=F32) + off)
            off = off + jnp.sum(mc, axis=1, keepdims=True)
        return jnp.concatenate(outs, axis=1)

    eq_rank = excl_cumsum(jnp.where(eq, 1.0, 0.0))
    sel = jnp.where(gt, 1.0, jnp.where(eq, jnp.where(eq_rank < need, 1.0, 0.0), 0.0))
    pos = excl_cumsum(sel)
    slot = jnp.where(sel > 0.0, pos, -1.0)
    for e in range(E):
        slot_ref[e] = slot[e:e + 1, :]
        affo_ref[e] = a[e:e + 1, :]


def _select(aff, B, n, cap):
    spec = pl.BlockSpec((None, N_EXPERTS, 1, n), lambda b: (b, 0, 0, 0))
    return pl.pallas_call(
        functools.partial(_select_kernel, n=n, cap=cap),
        grid=(B,),
        in_specs=[pl.BlockSpec((N_EXPERTS, n), lambda b: (0, b))],
        out_specs=(spec, spec),
        out_shape=(jax.ShapeDtypeStruct((B, N_EXPERTS, 1, n), F32), jax.ShapeDtypeStruct((B, N_EXPERTS, 1, n), F32)),
        compiler_params=_cparams(("parallel",)),
        name="ec_select",
    )(aff)


def _slot_block_range(slots_f, cap, SB):
    lo = jnp.min(jnp.where(slots_f >= 0.0, slots_f, float(cap))).astype(jnp.int32)
    hi = jnp.max(slots_f).astype(jnp.int32)
    first = lo // SB
    count = jnp.where(hi >= 0, hi // SB - first + 1, 0)
    return first, count


EC_SLOT_BLOCK = 256


def _one_hot_all(slot_ref, aff_ref, n, cap):
    srow = lax.broadcasted_iota(jnp.int32, (cap, n), 0).astype(F32)
    ohs, gates = [], []
    for e in range(N_EXPERTS):
        hit = slot_ref[e] == srow
        ohs.append(jnp.where(hit, 1.0, 0.0).astype(BF16))
        if aff_ref is not None:
            gates.append(jnp.sum(jnp.where(hit, aff_ref[e], 0.0), axis=1, keepdims=True))
    return jnp.concatenate(ohs, axis=0), (jnp.concatenate(gates, axis=0) if gates else None)


def _gather_all_kernel(hb_ref, slot_ref, aff_ref, xs_ref, gate_ref, *, n, cap):
    oh, gate = _one_hot_all(slot_ref, aff_ref, n, cap)
    xs = jnp.dot(oh, hb_ref[...], preferred_element_type=F32)
    xs_ref[...] = xs.astype(BF16).reshape(N_EXPERTS, cap, D_MODEL)
    gate_ref[...] = gate.reshape(N_EXPERTS, cap, 1)


def _gather_kernel(hb_ref, slot_ref, aff_ref, xs_ref, gate_ref, acc, gacc, *, n, cap):
    NC = min(512, n)
    SB = min(EC_SLOT_BLOCK, cap)
    acc[...] = jnp.zeros_like(acc)
    gacc[...] = jnp.zeros_like(gacc)
    srow = lax.broadcasted_iota(jnp.int32, (SB, NC), 0).astype(F32)
    for c in range(n // NC):
        sl = slot_ref[:, c * NC:(c + 1) * NC]
        first, count = _slot_block_range(sl, cap, SB)
        for j in range(min(cap // SB, NC // SB + 1)):
            @pl.when(j < count)
            def _(c=c, j=j, sl=sl, first=first):
                base = pl.multiple_of((first + j) * SB, SB)
                hit = (sl - base.astype(F32)) == srow
                oh = jnp.where(hit, 1.0, 0.0).astype(BF16)
                acc[pl.ds(base, SB), :] += jnp.dot(oh, hb_ref[c * NC:(c + 1) * NC, :], preferred_element_type=F32)
                gacc[pl.ds(base, SB), :] += jnp.sum(jnp.where(hit, aff_ref[:, c * NC:(c + 1) * NC], 0.0), axis=1,
                                                    keepdims=True)
    xs_ref[...] = acc[...].astype(BF16)
    gate_ref[...] = gacc[...]


def _gather(hb, slot4, aff4, B, n, cap):
    E = N_EXPERTS
    if E * cap <= 512:
        all_spec = pl.BlockSpec((None, E, 1, n), lambda b: (b, 0, 0, 0))
        return pl.pallas_call(
            functools.partial(_gather_all_kernel, n=n, cap=cap),
            grid=(B,),
            in_specs=[pl.BlockSpec((n, D_MODEL), lambda b: (b, 0)), all_spec, all_spec],
            out_specs=(pl.BlockSpec((None, E, cap, D_MODEL), lambda b: (b, 0, 0, 0)),
                       pl.BlockSpec((None, E, cap, 1), lambda b: (b, 0, 0, 0))),
            out_shape=(jax.ShapeDtypeStruct((B, E, cap, D_MODEL), BF16), jax.ShapeDtypeStruct((B, E, cap, 1), F32)),
            compiler_params=_cparams(("parallel",)),
            name="ec_gather_all",
        )(hb, slot4, aff4)
    return pl.pallas_call(
        functools.partial(_gather_kernel, n=n, cap=cap),
        grid=(B, E),
        in_specs=[
            pl.BlockSpec((n, D_MODEL), lambda b, e: (b, 0)),
            pl.BlockSpec((None, None, 1, n), lambda b, e: (b, e, 0, 0)),
            pl.BlockSpec((None, None, 1, n), lambda b, e: (b, e, 0, 0)),
        ],
        out_specs=(pl.BlockSpec((None, None, cap, D_MODEL), lambda b, e: (b, e, 0, 0)),
                   pl.BlockSpec((None, None, cap, 1), lambda b, e: (b, e, 0, 0))),
        out_shape=(jax.ShapeDtypeStruct((B, E, cap, D_MODEL), BF16), jax.ShapeDtypeStruct((B, E, cap, 1), F32)),
        scratch_shapes=[pltpu.VMEM((cap, D_MODEL), F32), pltpu.VMEM((cap, 1), F32)],
        compiler_params=_cparams(("parallel", "arbitrary")),
        name="ec_gather",
    )(hb, slot4, aff4)


def _ffn_kernel(xs_ref, gate_ref, w1_ref, w3_ref, w2_ref, o_ref):
    bg, cap, _ = xs_ref.shape
    x = xs_ref[...].reshape(bg * cap, D_MODEL)
    h1 = jnp.dot(x, w1_ref[...], preferred_element_type=F32)
    h3 = jnp.dot(x, w3_ref[...], preferred_element_type=F32)
    hid = (h1 * _sigmoid(h1) * h3).astype(BF16)
    o = jnp.dot(hid, w2_ref[...], preferred_element_type=F32) * gate_ref[...].reshape(bg * cap, 1)
    o_ref[...] = o.astype(BF16).reshape(bg, cap, D_MODEL)


def _expert_ffn(xs, gate, w1, w3, w2, bg):
    B, E, cap, _ = xs.shape
    return pl.pallas_call(
        _ffn_kernel,
        grid=(E, B // bg),
        in_specs=[
            pl.BlockSpec((bg, None, cap, D_MODEL), lambda e, b: (b, e, 0, 0)),
            pl.BlockSpec((bg, None, cap, 1), lambda e, b: (b, e, 0, 0)),
            pl.BlockSpec((None, D_MODEL, D_EXPERT), lambda e, b: (e, 0, 0)),
            pl.BlockSpec((None, D_MODEL, D_EXPERT), lambda e, b: (e, 0, 0)),
            pl.BlockSpec((None, D_EXPERT, D_MODEL), lambda e, b: (e, 0, 0)),
        ],
        out_specs=pl.BlockSpec((bg, None, cap, D_MODEL), lambda e, b: (b, e, 0, 0)),
        out_shape=jax.ShapeDtypeStruct((B, E, cap, D_MODEL), BF16),
        compiler_params=_cparams(("parallel", "parallel")),
        name="ec_ffn",
    )(xs, gate, w1, w3, w2)


_TN_DIMS = (((0,), (0,)), ((), ()))


def _ec_residual(x1_ref, mod_ref, fg_ref, out_ref, ffn, final):
    x2 = x1_ref[...] + mod_ref[:, 5 * D_MODEL:6 * D_MODEL] * ffn
    if final:
        x2 = x2 * lax.rsqrt(jnp.mean(x2 * x2, axis=-1, keepdims=True) + EPS) * fg_ref[...]
    out_ref[...] = x2


def _scatter_all_kernel(slot_ref, o_ref, x1_ref, mod_ref, fg_ref, out_ref, *, n, cap, final):
    oh, _ = _one_hot_all(slot_ref, None, n, cap)
    ffn = lax.dot_general(oh, o_ref[...].reshape(N_EXPERTS * cap, D_MODEL), _TN_DIMS, preferred_element_type=F32)
    _ec_residual(x1_ref, mod_ref, fg_ref, out_ref, ffn, final)


def _scatter_kernel(slot_ref, o_ref, x1_ref, mod_ref, fg_ref, out_ref, acc, *, cap, final):
    e = pl.program_id(2)
    tn = x1_ref.shape[0]

    @pl.when(e == 0)
    def _():
        acc[...] = jnp.zeros_like(acc)

    sl = slot_ref[...]
    SB = min(EC_SLOT_BLOCK, cap)
    srow = lax.broadcasted_iota(jnp.int32, (SB, tn), 0).astype(F32)
    first, count = _slot_block_range(sl, cap, SB)
    for j in range(min(cap // SB, tn // SB + 1)):
        @pl.when(j < count)
        def _(j=j):
            base = pl.multiple_of((first + j) * SB, SB)
            oh = jnp.where((sl - base.astype(F32)) == srow, 1.0, 0.0).astype(BF16)
            acc[...] += lax.dot_general(oh, o_ref[pl.ds(base, SB), :], _TN_DIMS, preferred_element_type=F32)

    @pl.when(e == N_EXPERTS - 1)
    def _():
        _ec_residual(x1_ref, mod_ref, fg_ref, out_ref, acc[...], final)


def _scatter(slot4, o, x1, mod3, mod_row, final_g, B, n, cap, final):
    E = N_EXPERTS
    if E * cap <= 512:
        return pl.pallas_call(
            functools.partial(_scatter_all_kernel, n=n, cap=cap, final=final),
            grid=(B,),
            in_specs=[
                pl.BlockSpec((None, E, 1, n), lambda b: (b, 0, 0, 0)),
                pl.BlockSpec((None, E, cap, D_MODEL), lambda b: (b, 0, 0, 0)),
                pl.BlockSpec((n, D_MODEL), lambda b: (b, 0)),
                pl.BlockSpec((None, 1, 6 * D_MODEL), lambda b: (mod_row(b), 0, 0)),
                _full((1, D_MODEL)),
            ],
            out_specs=pl.BlockSpec((n, D_MODEL), lambda b: (b, 0)),
            out_shape=jax.ShapeDtypeStruct((B * n, D_MODEL), F32),
            compiler_params=_cparams(("parallel",)),
            name="ec_scatter_all",
        )(slot4, o, x1, mod3, final_g.reshape(1, D_MODEL))
    tn = min(1024, n)
    nt = n // tn
    return pl.pallas_call(
        functools.partial(_scatter_kernel, cap=cap, final=final),
        grid=(B, nt, N_EXPERTS),
        in_specs=[
            pl.BlockSpec((None, None, 1, tn), lambda b, t, e: (b, e, 0, t)),
            pl.BlockSpec((None, None, cap, D_MODEL), lambda b, t, e: (b, e, 0, 0)),
            pl.BlockSpec((tn, D_MODEL), lambda b, t, e: (b * nt + t, 0)),
            pl.BlockSpec((None, 1, 6 * D_MODEL), lambda b, t, e: (mod_row(b), 0, 0)),
            _full((1, D_MODEL)),
        ],
        out_specs=pl.BlockSpec((tn, D_MODEL), lambda b, t, e: (b * nt + t, 0)),
        out_shape=jax.ShapeDtypeStruct((B * n, D_MODEL), F32),
        scratch_shapes=[pltpu.VMEM((tn, D_MODEL), F32)],
        compiler_params=_cparams(("parallel", "parallel", "arbitrary")),
        name="ec_scatter",
    )(slot4, o, x1, mod3, final_g.reshape(1, D_MODEL))


def _expert_choice(hb, aff, x1, mod3, mod_row, final_g, w1, w3, w2, B, n, final):
    cap = EC_FACTOR * n // N_EXPERTS
    slot4, aff4 = _select(aff, B, n, cap)
    xs, gate = _gather(hb, slot4, aff4, B, n, cap)
    bg = max(1, min(B, 256 // cap))
    o = _expert_ffn(xs, gate, w1, w3, w2, bg)
    return _scatter(slot4, o, x1, mod3, mod_row, final_g, B, n, cap, final)


def kernel(x_prompt, x_sample, c, state_s5_re, state_s5_im, state_ssd, state_rwkv, c_ctx, ada_w, ada_b, norm1_g, norm2_g, w_in, w_out, s5_a_re, s5_a_im, s5_log_dt, s5_b_re, s5_b_im, s5_c_re, s5_c_im, s5_d, s5_glu_w, s5_glu_b, ssd_conv_w, ssd_conv_b, ssd_a_log, ssd_dt_bias, ssd_d, ssd_norm_g, rwkv_mu, rwkv_w0, rwkv_w_up, rwkv_a0, rwkv_a_up, rwkv_g_up, rwkv_k_k, rwkv_k_a, rwkv_r_k, rwkv_ln_g, rwkv_ln_b, router_w, exp_w1, exp_w3, exp_w2, final_g):
    Bp, Lp, D = x_prompt.shape
    Bs, Ls, _ = x_sample.shape
    depth = ada_w.shape[0]
    Np, Ns = Bp * Lp, Bs * Ls
    tm = TOKEN_BLOCK
    grid_rows = Ls // GRID_W
    nfs = S5_GROUPS // 8
    assert Lp % tm == 0 and Ls % tm == 0 and Lp % SSD_CHUNK == 0

    n_rows = 1 + Bs
    rows_pad = -(-n_rows // 8) * 8
    cond = jnp.zeros((rows_pad, D), F32).at[0].set(c_ctx).at[1:n_rows].set(c)
    mod = _modulation(cond, ada_w, ada_b)
    s_blocks_per_req = Ls // tm
    row_p = lambda i: 0
    row_s = lambda i: 1 + i // s_blocks_per_req

    ab_re, ab_im, bb_re, bb_im = _s5_discretize(s5_a_re, s5_a_im, s5_log_dt, s5_b_re, s5_b_im)
    w_in_pad = _pad_in_weight(w_in)
    w_out_b = w_out.astype(BF16)
    glu_w_b = s5_glu_w.astype(BF16)
    exp_w1_b, exp_w3_b, exp_w2_b = exp_w1.astype(BF16), exp_w3.astype(BF16), exp_w2.astype(BF16)
    seg = _segment_ones(RWKV_WIDTH, RWKV_HEADSIZE)

    xp = x_prompt.reshape(Np, D)
    xs = x_sample.reshape(Ns, D)
    new_s5_re, new_s5_im, new_ssd, new_rwkv = [], [], [], []
    QP = 2 if Bp % 2 == 0 and Bp >= 2 else 1
    RP = Bp // QP
    assert Bs * HEADS_PAD <= CH

    for l in range(depth):
        lp = {
            'rwkv_mu': rwkv_mu[l], 'rwkv_w0': rwkv_w0[l], 'rwkv_w_up': rwkv_w_up[l], 'rwkv_a0': rwkv_a0[l],
            'rwkv_a_up': rwkv_a_up[l], 'rwkv_g_up': rwkv_g_up[l], 'rwkv_k_k': rwkv_k_k[l], 'rwkv_k_a': rwkv_k_a[l],
            'rwkv_r_k': rwkv_r_k[l].reshape(-1), 'rwkv_ln_g': rwkv_ln_g[l], 'rwkv_ln_b': rwkv_ln_b[l],
            's5_glu_w': glu_w_b[l], 's5_glu_b': s5_glu_b[l], 'ssd_norm_g': ssd_norm_g[l], 'w_out': w_out_b[l],
            'norm2_g': norm2_g[l], 'router_w': router_w[l], 'seg': seg,
        }
        mod3 = mod[l].reshape(rows_pad, 1, 6 * D)
        us5_p, z_p, xbc_p, dt_p, rkv_p, lo_p = _in_projection(xp, mod3, row_p, norm1_g[l], w_in_pad[l])
        us5_s, z_s, xbc_s, dt_s, rkv_s, lo_s = _in_projection(xs, mod3, row_s, norm1_g[l], w_in_pad[l])

        tables = _s5_layer_tables(ab_re[l], ab_im[l], bb_re[l], bb_im[l], s5_c_re[l], s5_c_im[l])
        d_row = s5_d[l].reshape(1, S5_WIDTH)
        up = us5_p.reshape(QP, RP, Lp, S5_WIDTH).transpose(0, 2, 1, 3).reshape(QP, Lp * RP, S5_WIDTH)
        yp, hfin = _s5_scan(up, None, tables, d_row, R=RP, n_slab=Lp, chained=False, want_final=True)
        ys5_p = yp.reshape(QP, Lp, RP, S5_WIDTH).transpose(0, 2, 1, 3).reshape(Np, S5_WIDTH)
        hf = hfin.transpose(0, 3, 1, 2, 4).reshape(Bp, N_DIR, nfs, 2, 8, S5_STATE)
        new_s5_re.append(hf[:, :, :, 0].reshape(Bp, N_DIR, S5_GROUPS, S5_STATE))
        new_s5_im.append(hf[:, :, :, 1].reshape(Bp, N_DIR, S5_GROUPS, S5_STATE))
        h0 = jnp.concatenate([state_s5_re[:, l].reshape(Bs, N_DIR, nfs, 1, S5_SLICE_ST),
                              state_s5_im[:, l].reshape(Bs, N_DIR, nfs, 1, S5_SLICE_ST)], axis=-1)
        (ysm,) = _s5_scan(us5_s.reshape(Bs, Ls, S5_WIDTH), h0, tables, d_row, R=GRID_W, n_slab=grid_rows,
                          chained=True, want_final=False)
        ys5_s = ysm.reshape(Ns, S5_WIDTH)

        ssd_args = (ssd_conv_w[l], ssd_conv_b[l], ssd_dt_bias[l], ssd_a_log[l], ssd_d[l])
        yssd_p, hssd = _ssd_scan(xbc_p, dt_p, None, *ssd_args, B=Bp, L=Lp, want_final=True)
        new_ssd.append(hssd.transpose(0, 1, 2, 4, 3))
        (yssd_s,) = _ssd_scan(xbc_s, dt_s, state_ssd[:, l].transpose(0, 1, 2, 4, 3), *ssd_args,
                              B=Bs, L=Ls, want_final=False)

        r_, w0_, w1_, k_, v_, kk_, b_, g_p, bonus_p = _rwkv_prep(rkv_p, lo_p, lp, Lp // tm)
        zero_state = jnp.zeros((Bp, N_DIR, RWKV_HEADS, RWKV_HEADSIZE, RWKV_HEADSIZE), F32)
        y_, sfin = _rwkv_scan(_to_scan_k(r_, r_, Bp, Lp, 1), _to_scan_k(w0_, w1_, Bp, Lp, 1),
                              _to_scan_k(k_, k_, Bp, Lp, 1), _to_scan_k(kk_, kk_, Bp, Lp, 1),
                              _to_scan_k(b_, b_, Bp, Lp, 1), _to_scan_v(v_, Bp, Lp, 1),
                              _state_to_scan(zero_state, 1))
        rf_p, rb_p = _from_scan_v(y_, Bp, Lp, 1)
        new_rwkv.append(_state_from_scan(sfin, Bp, 1))

        r_, w0_, w1_, k_, v_, kk_, b_, g_s, bonus_s = _rwkv_prep(rkv_s, lo_s, lp, Ls // tm, SCAN_WIDTH)
        sc = lambda t: _to_scan2_k(t, Bs, Ls)
        yf_, yb_ = _rwkv_scan2(sc(r_), sc(w0_), sc(w1_), sc(k_), sc(kk_), sc(b_), _to_scan2_v(v_, Bs, Ls),
                               _state_to_scan2(state_rwkv[:, l]))
        rf_s = _from_scan2(yf_, Bs, Ls)
        rb_s = _from_scan2(yb_, Bs, Ls)

        x1_p, hb_p, aff_p = _post_mixer(xp, mod3, row_p, ys5_p, yssd_p, z_p, rf_p, rb_p, bonus_p, g_p, lp)
        x1_s, hb_s, aff_s = _post_mixer(xs, mod3, row_s, ys5_s, yssd_s, z_s, rf_s, rb_s, bonus_s, g_s, lp)

        final = l == depth - 1
        ew = (exp_w1_b[l], exp_w3_b[l], exp_w2_b[l])
        xp = _expert_choice(hb_p, aff_p, x1_p, mod3, lambda b: 0, final_g, *ew, Bp, Lp, final)
        xs = _expert_choice(hb_s, aff_s, x1_s, mod3, lambda b: 1 + b, final_g, *ew, Bs, Ls, final)

    y_prompt = xp.reshape(Bp, Lp, D)
    y_sample = xs.reshape(Bs, Ls, D)
    return (y_prompt, y_sample, jnp.stack(new_s5_re, axis=1), jnp.stack(new_s5_im, axis=1),
            jnp.stack(new_ssd, axis=1), jnp.stack(new_rwkv, axis=1))
```

```python
import functools
import math

import jax
import jax.numpy as jnp
from jax import lax
from jax.experimental import pallas as pl
from jax.experimental.pallas import tpu as pltpu

F32 = jnp.float32
BF16 = jnp.bfloat16
HIGHEST = lax.Precision.HIGHEST

D_MODEL = 1024
GRID_W = 64
N_DIR = 2
EPS = 1e-6
S5_WIDTH = 256
S5_CH = 16
S5_GROUPS = 16
S5_STATE = 64
SSD_HEADDIM = 64
SSD_HEADS = 6
SSD_WIDTH = 384
SSD_GROUPS = 2
SSD_STATE = 64
SSD_BC = 128
SSD_CONV_CH = 640
SSD_CHUNK = 128
RWKV_HEADSIZE = 64
RWKV_HEADS = 6
RWKV_WIDTH = 384
W_RANK = 32
A_RANK = 32
G_RANK = 64
DECAY_SCALE = math.exp(-0.5)
GN_EPS = 64e-5
N_EXPERTS = 16
D_EXPERT = 512
EC_FACTOR = 2
IN_SIZES = (S5_WIDTH, SSD_WIDTH, SSD_CONV_CH, N_DIR * SSD_HEADS, 3 * RWKV_WIDTH, N_DIR * W_RANK, A_RANK, G_RANK)

LANES = 128
TOKEN_BLOCK = 256
DT_PAD = 128
LO_PAD = 256
S5_SLICE_CH = 128
S5_SLICE_ST = 512
VMEM_LIMIT = 56 * 1024 * 1024


def _cparams(sem):
    return pltpu.CompilerParams(dimension_semantics=sem, vmem_limit_bytes=VMEM_LIMIT)


def _full(shape):
    nd = len(shape)
    return pl.BlockSpec(shape, lambda *_: (0,) * nd)


def _sigmoid(x):
    return 1.0 / (1.0 + jnp.exp(-x))


def _softplus(x):
    return jnp.maximum(x, 0.0) + jnp.log1p(jnp.exp(-jnp.abs(x)))


def _split_bf16(x):
    hi = x.astype(BF16)
    lo = (x - hi.astype(F32)).astype(BF16)
    return hi, lo


def _dot_exact_rhs(x, m, dims=None):
    hi, lo = _split_bf16(x)
    mb = m.astype(BF16)
    if dims is None:
        return jnp.dot(hi, mb, preferred_element_type=F32) + jnp.dot(lo, mb, preferred_element_type=F32)
    return (lax.dot_general(hi, mb, dims, preferred_element_type=F32)
            + lax.dot_general(lo, mb, dims, preferred_element_type=F32))


def _dot_exact_lhs(m, x):
    hi, lo = _split_bf16(x)
    mb = m.astype(BF16)
    return jnp.dot(mb, hi, preferred_element_type=F32) + jnp.dot(mb, lo, preferred_element_type=F32)


def _dot3(x, w, dims=(((1,), (0,)), ((), ()))):
    xh, xl = _split_bf16(x)
    wh, wl = _split_bf16(w)
    dg = functools.partial(lax.dot_general, dimension_numbers=dims, preferred_element_type=F32)
    return dg(xh, wh) + (dg(xl, wh) + dg(xh, wl))


def _gelu_tanh(x):
    return 0.5 * x * (1.0 + jnp.tanh(math.sqrt(2.0 / math.pi) * (x + 0.044715 * (x * x * x))))


def _mod_kernel(c_ref, w_ref, b_ref, o_ref):
    c = c_ref[...]
    s = (c * _sigmoid(c)).astype(BF16)
    o_ref[...] = jnp.dot(s, w_ref[...].astype(BF16), preferred_element_type=F32) + b_ref[...]


def _modulation(cond, ada_w, ada_b):
    depth = ada_w.shape[0]
    rows = cond.shape[0]
    tn = 1536
    return pl.pallas_call(
        _mod_kernel,
        grid=(depth, 6 * D_MODEL // tn),
        in_specs=[
            pl.BlockSpec((rows, D_MODEL), lambda l, j: (0, 0)),
            pl.BlockSpec((None, D_MODEL, tn), lambda l, j: (l, 0, j)),
            pl.BlockSpec((None, 1, tn), lambda l, j: (l, 0, j)),
        ],
        out_specs=pl.BlockSpec((None, rows, tn), lambda l, j: (l, 0, j)),
        out_shape=jax.ShapeDtypeStruct((depth, rows, 6 * D_MODEL), F32),
        compiler_params=_cparams(("parallel", "parallel")),
        name="adaln_mod",
    )(cond, ada_w, ada_b.reshape(depth, 1, 6 * D_MODEL))


IN_PAD_SIZES = (S5_WIDTH, SSD_WIDTH, SSD_CONV_CH, DT_PAD, 3 * RWKV_WIDTH, LO_PAD)


def _inproj_kernel(x_ref, mod_ref, g_ref, w_ref, us5_ref, z_ref, xbc_ref, dt_ref, rkv_ref, lo_ref):
    x = x_ref[...]
    y = x * lax.rsqrt(jnp.mean(x * x, axis=-1, keepdims=True) + EPS) * g_ref[...]
    m = mod_ref[...]
    h = y * (1.0 + m[:, D_MODEL:2 * D_MODEL]) + m[:, 0:D_MODEL]
    p = jnp.dot(h.astype(BF16), w_ref[...], preferred_element_type=F32)
    start = 0
    for ref, size in zip((us5_ref, z_ref, xbc_ref, dt_ref, rkv_ref, lo_ref), IN_PAD_SIZES):
        ref[...] = p[:, start:start + size]
        start += size


def _pad_in_weight(w_in):
    parts, start = [], 0
    for s in IN_SIZES:
        parts.append(w_in[..., start:start + s])
        start += s
    us5, z, xbc, dt, rkv, wlo, alo, glo = parts
    zeros = lambda n: jnp.zeros(w_in.shape[:-1] + (n,), w_in.dtype)
    lo_used = wlo.shape[-1] + alo.shape[-1] + glo.shape[-1]
    return jnp.concatenate([us5, z, xbc, dt, zeros(DT_PAD - dt.shape[-1]), rkv, wlo, alo, glo,
                            zeros(LO_PAD - lo_used)], axis=-1).astype(BF16)


def _in_projection(x, mod3, row_of_block, norm_g, w_pad):
    ntok = x.shape[0]
    tm = TOKEN_BLOCK
    width = w_pad.shape[1]
    outs = tuple(jax.ShapeDtypeStruct((ntok, s), F32) for s in IN_PAD_SIZES)
    return pl.pallas_call(
        _inproj_kernel,
        grid=(ntok // tm,),
        in_specs=[
            pl.BlockSpec((tm, D_MODEL), lambda i: (i, 0)),
            pl.BlockSpec((None, 1, 6 * D_MODEL), lambda i: (row_of_block(i), 0, 0)),
            _full((1, D_MODEL)),
            _full((D_MODEL, width)),
        ],
        out_specs=tuple(pl.BlockSpec((tm, s), lambda i: (i, 0)) for s in IN_PAD_SIZES),
        out_shape=outs,
        compiler_params=_cparams(("parallel",)),
        name="in_proj",
    )(x, mod3, norm_g.reshape(1, D_MODEL), w_pad)


def _s5_disc_kernel(are_ref, aim_ref, ldt_ref, bre_ref, bim_ref, abre_ref, abim_ref, bbre_ref, bbim_ref):
    lam_re = jnp.minimum(are_ref[...], -1e-4)
    lam_im = aim_ref[...]
    dt = jnp.exp(ldt_ref[...])
    mag = jnp.exp(lam_re * dt)
    ab_re = mag * jnp.cos(lam_im * dt)
    ab_im = mag * jnp.sin(lam_im * dt)
    num_re, num_im = ab_re - 1.0, ab_im
    den = lam_re * lam_re + lam_im * lam_im
    q_re = (num_re * lam_re + num_im * lam_im) / den
    q_im = (num_im * lam_re - num_re * lam_im) / den
    abre_ref[...] = ab_re
    abim_ref[...] = ab_im
    b_re = bre_ref[...]
    b_im = bim_ref[...]
    qr = q_re[:, None, :]
    qi = q_im[:, None, :]
    bbre_ref[...] = qr * b_re - qi * b_im
    bbim_ref[...] = qr * b_im + qi * b_re


def _s5_discretize(a_re, a_im, log_dt, b_re, b_im):
    lead = a_re.shape[:3]
    n = lead[0] * lead[1] * lead[2]
    a2 = lambda t: t.reshape(n, S5_STATE)
    ldt = jnp.broadcast_to(log_dt.reshape(n, 1), (n, S5_STATE))
    b3 = lambda t: t.reshape(n, S5_STATE, S5_CH).transpose(0, 2, 1)
    ab_re, ab_im, bb_re, bb_im = pl.pallas_call(
        _s5_disc_kernel,
        out_shape=(jax.ShapeDtypeStruct((n, S5_STATE), F32), jax.ShapeDtypeStruct((n, S5_STATE), F32),
                   jax.ShapeDtypeStruct((n, S5_CH, S5_STATE), F32), jax.ShapeDtypeStruct((n, S5_CH, S5_STATE), F32)),
        name="s5_discretize",
    )(a2(a_re), a2(a_im), ldt, b3(b_re), b3(b_im))
    return (ab_re.reshape(lead + (S5_STATE,)), ab_im.reshape(lead + (S5_STATE,)),
            bb_re.reshape(lead + (S5_CH, S5_STATE)), bb_im.reshape(lead + (S5_CH, S5_STATE)))


def _s5_layer_tables(ab_re, ab_im, bb_re, bb_im, c_re, c_im):
    nfs = S5_GROUPS // 8
    eye = jnp.eye(8, dtype=F32)

    def rows(t):
        return t.reshape(N_DIR, nfs, 8 * S5_STATE)

    ab_row = jnp.concatenate([rows(ab_re), rows(ab_im)], axis=-1).reshape(N_DIR, nfs, 1, 2 * S5_SLICE_ST)

    def bmat(t):
        t = t.reshape(N_DIR, nfs, 8, S5_CH, S5_STATE)
        return jnp.einsum('dfghp,gk->dfghkp', t, eye).reshape(N_DIR, nfs, S5_SLICE_CH, S5_SLICE_ST)

    b_mat = jnp.concatenate([bmat(bb_re), bmat(bb_im)], axis=-1).astype(BF16)

    def cmat(t):
        t = t.reshape(N_DIR, nfs, 8, S5_CH, S5_STATE)
        return jnp.einsum('dfghp,gk->dfgpkh', t, eye).reshape(N_DIR, nfs, S5_SLICE_ST, S5_SLICE_CH)

    c_mat = jnp.concatenate([cmat(c_re), -cmat(c_im)], axis=-2).astype(BF16)
    return ab_row, b_mat, c_mat


def _s5_kernel(*refs, R, n_slab, chained, want_final):
    if chained:
        u_ref, h0_ref, ab_ref, bm_ref, cm_ref, d_ref = refs[:6]
        rest = refs[6:]
    else:
        u_ref, ab_ref, bm_ref, cm_ref, d_ref = refs[:5]
        h0_ref = None
        rest = refs[5:]
    y_ref = rest[0]
    rest = rest[1:]
    if want_final:
        hfin_ref = rest[0]
        rest = rest[1:]
    H = rest[0]
    if chained:
        PW, CIN = rest[1], rest[2]
    NR = R * n_slab
    RC = min(512, NR)
    ST = S5_SLICE_ST
    nchunk = ST // LANES

    y_ref[...] = u_ref[...] * d_ref[...]

    for d in range(N_DIR):
        def slab_of(i, d=d):
            return i if d == 0 else n_slab - 1 - i

        def bu_body(i, _, d=d):
            r0 = pl.multiple_of(i * RC, RC)
            H[pl.ds(r0, RC), :] = jnp.dot(u_ref[pl.ds(r0, RC), :].astype(BF16), bm_ref[d],
                                           preferred_element_type=F32)
            return 0
        lax.fori_loop(0, NR // RC, bu_body, 0)

        for c in range(nchunk):
            lre = slice(c * LANES, (c + 1) * LANES)
            lim = slice(ST + c * LANES, ST + (c + 1) * LANES)
            a_re = jnp.broadcast_to(ab_ref[d, :, lre], (R, LANES))
            a_im = jnp.broadcast_to(ab_ref[d, :, lim], (R, LANES))

            def step(i, carry, lre=lre, lim=lim, a_re=a_re, a_im=a_im, slab_of=slab_of):
                cr, ci = carry
                r0 = pl.multiple_of(slab_of(i) * R, R)
                nr = a_re * cr - a_im * ci + H[pl.ds(r0, R), lre]
                ni = a_re * ci + a_im * cr + H[pl.ds(r0, R), lim]
                H[pl.ds(r0, R), lre] = nr
                H[pl.ds(r0, R), lim] = ni
                return nr, ni
            zero = jnp.zeros((R, LANES), F32)
            lax.fori_loop(0, n_slab, step, (zero, zero))

        last0 = (n_slab - 1) * R if d == 0 else 0
        if chained:
            a_re_row = ab_ref[d, :, 0:ST]
            a_im_row = ab_ref[d, :, ST:2 * ST]

            def pw_step(j, carry, a_re_row=a_re_row, a_im_row=a_im_row):
                pr, pi = carry
                PW[j, :, 0:ST] = jnp.broadcast_to(pr, (8, ST))
                PW[j, :, ST:2 * ST] = jnp.broadcast_to(pi, (8, ST))
                return pr * a_re_row - pi * a_im_row, pr * a_im_row + pi * a_re_row
            lax.fori_loop(0, n_slab, pw_step, (a_re_row, a_im_row))
            t_re = PW[n_slab - 1, 0:1, 0:ST]
            t_im = PW[n_slab - 1, 0:1, ST:2 * ST]

            cr = h0_ref[d, :, 0:ST]
            ci = h0_ref[d, :, ST:2 * ST]
            for i in range(R):
                c = i if d == 0 else R - 1 - i
                CIN[c:c + 1, 0:ST] = cr
                CIN[c:c + 1, ST:2 * ST] = ci
                er = H[last0 + c:last0 + c + 1, 0:ST]
                ei = H[last0 + c:last0 + c + 1, ST:2 * ST]
                cr, ci = t_re * cr - t_im * ci + er, t_re * ci + t_im * cr + ei

            def fix_step(i, _, slab_of=slab_of):
                r0 = pl.multiple_of(slab_of(i) * R, R)
                p = PW[i]
                for c in range(nchunk):
                    lre = slice(c * LANES, (c + 1) * LANES)
                    lim = slice(ST + c * LANES, ST + (c + 1) * LANES)
                    pr = p[0:1, lre]
                    pi = p[0:1, lim]
                    cr = CIN[:, lre]
                    ci = CIN[:, lim]
                    H[pl.ds(r0, R), lre] = H[pl.ds(r0, R), lre] + (pr * cr - pi * ci)
                    H[pl.ds(r0, R), lim] = H[pl.ds(r0, R), lim] + (pr * ci + pi * cr)
                return 0
            lax.fori_loop(0, n_slab, fix_step, 0)

        if want_final:
            hfin_ref[d] = H[last0:last0 + R, :]

        def y_body(i, _, d=d):
            r0 = pl.multiple_of(i * RC, RC)
            y_ref[pl.ds(r0, RC), :] = y_ref[pl.ds(r0, RC), :] + jnp.dot(
                H[pl.ds(r0, RC), :].astype(BF16), cm_ref[d], preferred_element_type=F32)
            return 0
        lax.fori_loop(0, NR // RC, y_body, 0)


def _s5_scan(u, h0, tables, d_row, *, R, n_slab, chained, want_final):
    ab_row, b_mat, c_mat = tables
    Q, NR, _ = u.shape
    nfs = S5_GROUPS // 8
    W2 = 2 * S5_SLICE_ST
    in_specs = [pl.BlockSpec((None, NR, S5_SLICE_CH), lambda q, f: (q, 0, f))]
    args = [u]
    if chained:
        in_specs.append(pl.BlockSpec((None, N_DIR, None, 1, W2), lambda q, f: (q, 0, f, 0, 0)))
        args.append(h0)
    in_specs += [
        pl.BlockSpec((N_DIR, None, 1, W2), lambda q, f: (0, f, 0, 0)),
        pl.BlockSpec((N_DIR, None, S5_SLICE_CH, W2), lambda q, f: (0, f, 0, 0)),
        pl.BlockSpec((N_DIR, None, W2, S5_SLICE_CH), lambda q, f: (0, f, 0, 0)),
        pl.BlockSpec((1, S5_SLICE_CH), lambda q, f: (0, f)),
    ]
    args += [ab_row, b_mat, c_mat, d_row]
    out_shape = [jax.ShapeDtypeStruct((Q, NR, S5_WIDTH), F32)]
    out_specs = [pl.BlockSpec((None, NR, S5_SLICE_CH), lambda q, f: (q, 0, f))]
    if want_final:
        out_shape.append(jax.ShapeDtypeStruct((Q, N_DIR, nfs, R, W2), F32))
        out_specs.append(pl.BlockSpec((None, N_DIR, None, R, W2), lambda q, f: (q, 0, f, 0, 0)))
    scratch = [pltpu.VMEM((NR, W2), F32)]
    if chained:
        scratch += [pltpu.VMEM((n_slab, 8, W2), F32), pltpu.VMEM((R, W2), F32)]
    res = pl.pallas_call(
        functools.partial(_s5_kernel, R=R, n_slab=n_slab, chained=chained, want_final=want_final),
        grid=(Q, nfs),
        in_specs=in_specs,
        out_specs=tuple(out_specs),
        out_shape=tuple(out_shape),
        scratch_shapes=scratch,
        compiler_params=_cparams(("parallel", "parallel")),
        name="s5_scan_chained" if chained else "s5_scan",
    )(*args)
    return res


def _ssd_kernel(*refs, TB, nb, has_h0, want_final):
    io = [refs[4 * d:4 * d + 4] for d in range(N_DIR)]
    refs = refs[8:]
    if has_h0:
        h0_ref = refs[0]
        refs = refs[1:]
    cw_ref, cb_ref, dtb_ref, arow_ref, sel_ref, drow_ref = refs[:6]
    refs = refs[6:]
    y_refs = refs[:2]
    refs = refs[2:]
    if want_final:
        hfin_ref = refs[0]
        refs = refs[1:]
    hst, xc_s, dt_s = refs
    CH = SSD_CHUNK
    P = SSD_HEADDIM
    j = pl.program_id(1)

    @pl.when(j == 0)
    def _():
        if has_h0:
            hst[...] = h0_ref[...]
        else:
            hst[...] = jnp.zeros_like(hst)

    li = lax.broadcasted_iota(jnp.int32, (CH, CH), 0)
    si = lax.broadcasted_iota(jnp.int32, (CH, CH), 1)
    tmats = ((si <= li).astype(F32), (si >= li).astype(F32))
    rows = lax.broadcasted_iota(jnp.int32, (TB, 1), 0)
    for d in range(N_DIR):
        xbc_ref, xp_ref, xn_ref, dt_ref = io[d]
        jj = j if d == 0 else nb - 1 - j
        x = xbc_ref[...]
        prev_row = xp_ref[7:8, :] * (jj > 0).astype(F32)
        next_row = xn_ref[0:1, :] * (jj < nb - 1).astype(F32)
        x_prev = jnp.where(rows == 0, prev_row, pltpu.roll(x, 1, 0))
        x_next = jnp.where(rows == TB - 1, next_row, pltpu.roll(x, TB - 1, 0))
        conv = cw_ref[0:1, :] * x_prev + cw_ref[1:2, :] * x + cw_ref[2:3, :] * x_next + cb_ref[...]
        xc_s[d] = conv * _sigmoid(conv)
        dtf = _softplus(dt_ref[...] + dtb_ref[...])
        dt_s[d] = _dot_exact_rhs(dtf, sel_ref[d])

    hrow = lax.broadcasted_iota(jnp.int32, (LANES, SSD_WIDTH), 0)
    e_head = jnp.where(lax.broadcasted_iota(jnp.int32, (LANES, SSD_WIDTH), 1) // P == hrow, 1.0, 0.0)
    hrow2 = lax.broadcasted_iota(jnp.int32, (LANES, SSD_HEADS * CH), 0)
    e_chunk = jnp.where(lax.broadcasted_iota(jnp.int32, (LANES, SSD_HEADS * CH), 1) // CH == hrow2, 1.0, 0.0)

    n_ch = TB // CH
    for i, d in [(i, d) for i in range(n_ch) for d in range(N_DIR)]:
        tmat = tmats[d]
        y_ref = y_refs[d]
        r0 = (i if d == 0 else n_ch - 1 - i) * CH
        dtc = dt_s[d, r0:r0 + CH, :]
        dA = dtc * arow_ref[d]
        cs = _dot_exact_lhs(tmat, dA)
        csT = cs.T
        dt_x = _dot_exact_rhs(dtc, e_head)
        cs_x = _dot_exact_rhs(cs, e_head)
        tot_x = jnp.sum(_dot_exact_rhs(dA, e_head), axis=0, keepdims=True)
        cs_xx = _dot_exact_rhs(cs, e_chunk)
        xs_all = xc_s[d, r0:r0 + CH, 0:SSD_WIDTH]
        xdt_all = xs_all * dt_x
        xd_all = xdt_all * jnp.exp(tot_x - cs_x)
        ecs_all = jnp.exp(cs_x)
        etot_all = jnp.exp(tot_x)
        Bm = xc_s[d, r0:r0 + CH, SSD_WIDTH:SSD_WIDTH + SSD_BC]
        Cm = xc_s[d, r0:r0 + CH, SSD_WIDTH + SSD_BC:SSD_WIDTH + 2 * SSD_BC]
        BmT = Bm.T
        for g in range(SSD_GROUPS):
            Cg = Cm[:, g * SSD_STATE:(g + 1) * SSD_STATE].astype(BF16)
            Bg = Bm[:, g * SSD_STATE:(g + 1) * SSD_STATE].astype(BF16)
            BgT = BmT[g * SSD_STATE:(g + 1) * SSD_STATE, :].astype(BF16)
            G = lax.dot_general(Cg, Bg, (((1,), (1,)), ((), ())), preferred_element_type=F32)
            for hh in range(SSD_HEADS // SSD_GROUPS):
                h = g * (SSD_HEADS // SSD_GROUPS) + hh
                hl = slice(h * P, (h + 1) * P)
                row = csT[h:h + 1, :]
                lm = jnp.exp(jnp.where(tmat > 0.0, cs_xx[:, h * CH:(h + 1) * CH] - row, -1e30))
                hprev = hst[d, h]
                y = jnp.dot((G * lm).astype(BF16), xdt_all[:, hl].astype(BF16), preferred_element_type=F32)
                y = y + jnp.dot(Cg, hprev.astype(BF16), preferred_element_type=F32) * ecs_all[:, hl]
                if d == 0:
                    y = y + drow_ref[:, hl] * xs_all[:, hl]
                y_ref[r0:r0 + CH, hl] = y
                hst[d, h] = etot_all[:, hl] * hprev + jnp.dot(BgT, xd_all[:, hl].astype(BF16),
                                                             preferred_element_type=F32)

    if want_final:
        @pl.when(j == nb - 1)
        def _():
            hfin_ref[...] = hst[...]


def _ssd_scan(xbc, dt, h0, conv_w, conv_b, dt_bias, a_log, d_skip, *, B, L, want_final):
    TB = min(512, L)
    nb = L // TB
    H = SSD_HEADS

    nrow8 = B * L // 8
    in_specs, args = [], []
    blks = (lambda b, j: b * nb + j, lambda b, j: b * nb + nb - 1 - j)
    for blk in blks:
        in_specs += [
            pl.BlockSpec((TB, SSD_CONV_CH), lambda b, j, blk=blk: (blk(b, j), 0)),
            pl.BlockSpec((8, SSD_CONV_CH), lambda b, j, blk=blk: (jnp.maximum(blk(b, j) * (TB // 8) - 1, 0), 0)),
            pl.BlockSpec((8, SSD_CONV_CH),
                         lambda b, j, blk=blk: (jnp.minimum((blk(b, j) + 1) * (TB // 8), nrow8 - 1), 0)),
            pl.BlockSpec((TB, DT_PAD), lambda b, j, blk=blk: (blk(b, j), 0)),
        ]
        args += [xbc, xbc, xbc, dt]
    if h0 is not None:
        in_specs.append(pl.BlockSpec((None, N_DIR, H, SSD_STATE, SSD_HEADDIM), lambda b, j: (b, 0, 0, 0, 0)))
        args.append(h0)
    dtb = jnp.pad(dt_bias.reshape(1, N_DIR * H), ((0, 0), (0, DT_PAD - N_DIR * H)))
    arow = jnp.pad(-jnp.exp(a_log), ((0, 0), (0, LANES - H))).reshape(N_DIR, 1, LANES)
    lane = jnp.arange(LANES)
    sel = jnp.stack([(lane[:, None] == (dd * H + lane[None, :])) & (lane[None, :] < H) for dd in range(N_DIR)]).astype(F32)
    drow = jnp.repeat(d_skip, SSD_HEADDIM).reshape(1, SSD_WIDTH)
    in_specs += [
        _full((3, SSD_CONV_CH)), _full((1, SSD_CONV_CH)), _full((1, DT_PAD)),
        _full((N_DIR, 1, LANES)), _full((N_DIR, LANES, LANES)), _full((1, SSD_WIDTH)),
    ]
    args += [conv_w, conv_b.reshape(1, SSD_CONV_CH), dtb, arow, sel, drow]
    out_shape = [jax.ShapeDtypeStruct((B * L, SSD_WIDTH), F32)] * N_DIR
    out_specs = [pl.BlockSpec((TB, SSD_WIDTH), lambda b, j, blk=blk: (blk(b, j), 0)) for blk in blks]
    if want_final:
        out_shape.append(jax.ShapeDtypeStruct((B, N_DIR, H, SSD_STATE, SSD_HEADDIM), F32))
        out_specs.append(pl.BlockSpec((None, N_DIR, H, SSD_STATE, SSD_HEADDIM), lambda b, j: (b, 0, 0, 0, 0)))
    return pl.pallas_call(
        functools.partial(_ssd_kernel, TB=TB, nb=nb, has_h0=h0 is not None, want_final=want_final),
        grid=(B, nb),
        in_specs=in_specs,
        out_specs=tuple(out_specs),
        out_shape=tuple(out_shape),
        scratch_shapes=[pltpu.VMEM((N_DIR, H, SSD_STATE, SSD_HEADDIM), F32),
                        pltpu.VMEM((N_DIR, TB, SSD_CONV_CH), F32), pltpu.VMEM((N_DIR, TB, DT_PAD), F32)],
        compiler_params=_cparams(("parallel", "arbitrary")),
        name="ssd_scan",
    )(*args)


def _rwkv_prep_kernel(rkv_ref, rp_ref, rn_ref, lo_ref, mu_ref, a0_ref, aup_ref, gup_ref, w0_ref, wup_ref,
                      kkw_ref, ka_ref, rk_ref, seg_ref,
                      r_ref, w0o_ref, w1o_ref, k_ref, v_ref, kk_ref, b_ref, g_ref, bonus_ref, *, nbs):
    tm = rkv_ref.shape[0]
    W = RWKV_WIDTH
    jj = pl.program_id(0) % nbs
    x = rkv_ref[...]
    prev_row = rp_ref[7:8, :] * (jj > 0).astype(F32)
    next_row = rn_ref[0:1, :] * (jj < nbs - 1).astype(F32)
    rows = lax.broadcasted_iota(jnp.int32, (tm, 1), 0)
    xp = jnp.where(rows == 0, prev_row, pltpu.roll(x, 1, 0))
    xn = jnp.where(rows == tm - 1, next_row, pltpu.roll(x, tm - 1, 0))
    x = x + mu_ref[0:1, :] * (xp - x) + mu_ref[1:2, :] * (xn - x)
    r = x[:, 0:W]
    k = x[:, W:2 * W]
    v = x[:, 2 * W:3 * W]
    lo = lo_ref[...]
    seg = seg_ref[...]

    def put(ref, val):
        ref[:, 0:W] = val
        if ref.shape[1] > W:
            ref[:, W:] = jnp.zeros((tm, ref.shape[1] - W), F32)

    a = _sigmoid(a0_ref[...] + _dot3(lo, aup_ref[...]))
    g_ref[...] = _dot3(_sigmoid(lo), gup_ref[...])
    tlo = jnp.tanh(lo)
    for d, o_ref in enumerate((w0o_ref, w1o_ref)):
        zw = w0_ref[d:d + 1, :] + _dot3(tlo, wup_ref[d])
        put(o_ref, jnp.exp(-DECAY_SCALE * _sigmoid(zw)))
    kk = k * kkw_ref[...]
    kk = kk * lax.rsqrt(jnp.maximum(_dot_exact_rhs(kk * kk, seg), 1e-24))
    k2 = k * (1.0 + (a - 1.0) * ka_ref[...])
    put(r_ref, r)
    put(k_ref, k2)
    put(v_ref, v)
    put(kk_ref, kk)
    put(b_ref, kk * a)
    bonus_ref[...] = _dot_exact_rhs(r * k2 * rk_ref[...], seg) * v


def _segment_ones(width, seg):
    i = jnp.arange(width)
    return (i[:, None] // seg == i[None, :] // seg).astype(F32)


def _rwkv_prep(rkv, lo, lp, seq_blocks, scan_width=RWKV_WIDTH):
    ntok = rkv.shape[0]
    tm = TOKEN_BLOCK
    W = RWKV_WIDTH
    nrow8 = ntok // 8
    nbs_of = seq_blocks
    widths = (scan_width,) * 7 + (W, W)
    pad_rows = lambda t, r0: jnp.zeros((LO_PAD, W), F32).at[r0:r0 + t.shape[0]].set(t)
    aup = pad_rows(lp['rwkv_a_up'], N_DIR * W_RANK)
    gup = pad_rows(lp['rwkv_g_up'], N_DIR * W_RANK + A_RANK)
    wup = jnp.stack([pad_rows(lp['rwkv_w_up'][d], d * W_RANK) for d in range(N_DIR)])
    row = lambda t: t.reshape(1, W)
    outs = tuple(jax.ShapeDtypeStruct((ntok, w), F32) for w in widths)
    return pl.pallas_call(
        functools.partial(_rwkv_prep_kernel, nbs=nbs_of),
        grid=(ntok // tm,),
        in_specs=[
            pl.BlockSpec((tm, 3 * W), lambda i: (i, 0)),
            pl.BlockSpec((8, 3 * W), lambda i: (jnp.maximum(i * (tm // 8) - 1, 0), 0)),
            pl.BlockSpec((8, 3 * W), lambda i: (jnp.minimum((i + 1) * (tm // 8), nrow8 - 1), 0)),
            pl.BlockSpec((tm, LO_PAD), lambda i: (i, 0)),
            _full((2, 3 * W)), _full((1, W)), _full((LO_PAD, W)), _full((LO_PAD, W)), _full((2, W)),
            _full((N_DIR, LO_PAD, W)), _full((1, W)), _full((1, W)), _full((1, W)), _full((W, W)),
        ],
        out_specs=tuple(pl.BlockSpec((tm, w), lambda i: (i, 0)) for w in widths),
        out_shape=outs,
        compiler_params=_cparams(("parallel",)),
        name="rwkv_prep",
    )(rkv, rkv, rkv, lo, lp['rwkv_mu'], row(lp['rwkv_a0']), aup, gup, lp['rwkv_w0'], wup,
      row(lp['rwkv_k_k']), row(lp['rwkv_k_a']), row(lp['rwkv_r_k']), lp['seg'])


def _rwkv_scan_kernel(r_ref, w_ref, k_ref, kk_ref, b_ref, v_ref, s0_ref, y_ref, sfin_ref, S, *, TBLK, V, nblk):
    K = RWKV_HEADSIZE
    tb = pl.program_id(1)

    @pl.when(tb == 0)
    def _():
        S[...] = s0_ref[...]

    def bc(ref, t, kx):
        return jnp.broadcast_to(ref[t, pl.ds(kx, 1), :], (V, LANES))

    NACC = 4

    def step(t, _):
        accs = [None] * NACC
        for kx in range(K):
            p = S[kx] * bc(kk_ref, t, kx)
            a = kx % NACC
            accs[a] = p if accs[a] is None else accs[a] + p
        sa = (accs[0] + accs[1]) + (accs[2] + accs[3])
        vt = v_ref[t]
        yacc = [None] * NACC
        for kx in range(K):
            s_new = S[kx] * bc(w_ref, t, kx) - sa * bc(b_ref, t, kx) + vt * bc(k_ref, t, kx)
            S[kx] = s_new
            p = s_new * bc(r_ref, t, kx)
            a = kx % NACC
            yacc[a] = p if yacc[a] is None else yacc[a] + p
        y_ref[t] = (yacc[0] + yacc[1]) + (yacc[2] + yacc[3])
        return 0
    lax.fori_loop(0, TBLK, step, 0)

    @pl.when(tb == nblk - 1)
    def _():
        sfin_ref[...] = S[...]


def _rwkv_scan(r, w, k, kk, b, v, s0):
    L, K, NL = r.shape
    V = v.shape[1]
    TBLK = min(64, L)
    nblk = L // TBLK
    ngrp = NL // LANES
    kspec = pl.BlockSpec((TBLK, K, LANES), lambda g, t: (t, 0, g))
    vspec = pl.BlockSpec((TBLK, V, LANES), lambda g, t: (t, 0, g))
    sspec = pl.BlockSpec((K, V, LANES), lambda g, t: (0, 0, g))
    return pl.pallas_call(
        functools.partial(_rwkv_scan_kernel, TBLK=TBLK, V=V, nblk=nblk),
        grid=(ngrp, nblk),
        in_specs=[kspec, kspec, kspec, kspec, kspec, vspec, sspec],
        out_specs=(vspec, sspec),
        out_shape=(jax.ShapeDtypeStruct((L, V, NL), F32), jax.ShapeDtypeStruct((K, V, NL), F32)),
        scratch_shapes=[pltpu.VMEM((K, V, LANES), F32)],
        compiler_params=_cparams(("parallel", "arbitrary")),
        name="rwkv_scan",
    )(r, w, k, kk, b, v, s0)


KQ = 4
CH = LANES // KQ


def _rwkv_scan2_kernel(rf, rm, wf, wm, kf, km, kkf, kkm, bf, bm, vf, vm, s0_ref, yf_ref, yb_ref, S, YQ, VT, SA, *,
                       TBLK):
    V = RWKV_HEADSIZE
    NQ = RWKV_HEADSIZE // KQ
    tb = pl.program_id(0)

    @pl.when(tb == 0)
    def _():
        S[...] = s0_ref[...]

    VH = V // 2

    def bc(ref, tt, q):
        return jnp.broadcast_to(ref[tt, pl.ds(q, 1), :], (VH, LANES))

    def all_quarters(x):
        return (x + pltpu.roll(x, CH, 1)) + (pltpu.roll(x, 2 * CH, 1) + pltpu.roll(x, 3 * CH, 1))

    dirs = ((rf, wf, kf, kkf, bf, vf), (rm, wm, km, kkm, bm, vm))
    time_of = (lambda t: t, lambda t: TBLK - 1 - t)
    chains = [(d, hv) for d in range(N_DIR) for hv in range(2)]

    def first_sa(d, hv):
        kk_ = dirs[d][3]
        tt = time_of[d](0)
        acc = [None, None]
        for q in range(NQ):
            p = S[d, q, pl.ds(hv * VH, VH), :] * bc(kk_, tt, q)
            acc[q % 2] = p if acc[q % 2] is None else acc[q % 2] + p
        return acc[0] + acc[1]

    TG = 8
    lane_c = lax.broadcasted_iota(jnp.int32, (CH, LANES), 1) % CH
    rep = jnp.where(lax.broadcasted_iota(jnp.int32, (CH, LANES), 0) == lane_c, 1.0, 0.0)
    lane_r = lax.broadcasted_iota(jnp.int32, (LANES, CH), 0) % CH
    fold = jnp.where(lax.broadcasted_iota(jnp.int32, (LANES, CH), 1) == lane_r, 1.0, 0.0)

    def spread(i, _):
        for d in range(N_DIR):
            v = dirs[d][5][pl.ds(i * TG, TG)].reshape(TG * V, CH)
            VT[d, pl.ds(i * TG, TG)] = _dot_exact_rhs(v, rep).reshape(TG, V, LANES)
        return 0
    lax.fori_loop(0, TBLK // TG, spread, 0)

    last = len(chains) - 1
    for c, (d, hv) in enumerate(chains):
        part = first_sa(d, hv)
        SA[c] = part if c == last else all_quarters(part)

    def step(t, _):
        sa_last = all_quarters(SA[last])
        for c, (d, hv) in enumerate(chains):
            r_, w_, k_, kk_, b_, _ = dirs[d]
            tt = time_of[d](t)
            tn = time_of[d](jnp.minimum(t + 1, TBLK - 1))
            rows = pl.ds(hv * VH, VH)
            sa = sa_last if c == last else SA[c]
            vt = VT[d, tt, rows, :]
            yacc = None
            sacc = None
            for q in range(NQ):
                s_new = S[d, q, rows, :] * bc(w_, tt, q) - sa * bc(b_, tt, q) + vt * bc(k_, tt, q)
                S[d, q, rows, :] = s_new
                py = s_new * bc(r_, tt, q)
                ps = s_new * bc(kk_, tn, q)
                yacc = py if yacc is None else yacc + py
                sacc = ps if sacc is None else sacc + ps
            YQ[d, tt, rows, :] = yacc
            SA[c] = sacc if c == last else all_quarters(sacc)
        return 0
    lax.fori_loop(0, TBLK, step, 0)

    def finish(i, _):
        for d, y_ in enumerate((yf_ref, yb_ref)):
            x = YQ[d, pl.ds(i * TG, TG)].reshape(TG * V, LANES)
            y_[pl.ds(i * TG, TG)] = _dot_exact_rhs(x, fold).reshape(TG, V, CH)
        return 0
    lax.fori_loop(0, TBLK // TG, finish, 0)


def _rwkv_scan2(r, w0, w1, k, kk, b, v, s0):
    L, NQ, _ = r.shape
    V = RWKV_HEADSIZE
    TBLK = min(64, L)
    nblk = L // TBLK
    fspec = pl.BlockSpec((TBLK, NQ, LANES), lambda t: (t, 0, 0))
    mspec = pl.BlockSpec((TBLK, NQ, LANES), lambda t: (nblk - 1 - t, 0, 0))
    vfspec = pl.BlockSpec((TBLK, V, CH), lambda t: (t, 0, 0))
    vmspec = pl.BlockSpec((TBLK, V, CH), lambda t: (nblk - 1 - t, 0, 0))
    yf, yb = pl.pallas_call(
        functools.partial(_rwkv_scan2_kernel, TBLK=TBLK),
        grid=(nblk,),
        in_specs=[fspec, mspec] * 5 + [vfspec, vmspec, _full((N_DIR, NQ, V, LANES))],
        out_specs=(vfspec, vmspec),
        out_shape=(jax.ShapeDtypeStruct((L, V, CH), F32), jax.ShapeDtypeStruct((L, V, CH), F32)),
        scratch_shapes=[pltpu.VMEM((N_DIR, NQ, V, LANES), F32), pltpu.VMEM((N_DIR, TBLK, V, LANES), F32),
                        pltpu.VMEM((N_DIR, TBLK, V, LANES), F32), pltpu.VMEM((2 * N_DIR, V // 2, LANES), F32)],
        compiler_params=_cparams(("arbitrary",)),
        name="rwkv_scan2",
    )(r, r, w0, w1, k, k, kk, kk, b, b, v, v, s0)
    return yf, yb


HEADS_PAD = 8
SCAN_WIDTH = HEADS_PAD * RWKV_HEADSIZE


def _pad_chains(x, B):
    return x if B * HEADS_PAD == CH else jnp.pad(x, ((0, 0),) * (x.ndim - 1) + ((0, CH - B * HEADS_PAD),))


def _to_scan2_k(t, B, L):
    x = t.reshape(B, L, HEADS_PAD, RWKV_HEADSIZE // KQ, KQ).transpose(1, 3, 4, 0, 2)
    x = _pad_chains(x.reshape(L, RWKV_HEADSIZE // KQ, KQ, B * HEADS_PAD), B)
    return x.reshape(L, RWKV_HEADSIZE // KQ, LANES)


def _to_scan2_v(t, B, L):
    x = t.reshape(B, L, HEADS_PAD, RWKV_HEADSIZE).transpose(1, 3, 0, 2).reshape(L, RWKV_HEADSIZE, B * HEADS_PAD)
    return _pad_chains(x, B)


def _from_scan2(y, B, L):
    y = y[:, :, :B * HEADS_PAD]
    return y.reshape(L, RWKV_HEADSIZE, B, HEADS_PAD).transpose(2, 0, 3, 1).reshape(B * L, SCAN_WIDTH)


def _state_to_scan2(s):
    B = s.shape[0]
    s = jnp.pad(s, ((0, 0), (0, 0), (0, HEADS_PAD - RWKV_HEADS), (0, 0), (0, 0)))
    x = s.transpose(1, 4, 3, 0, 2).reshape(N_DIR, RWKV_HEADSIZE // KQ, KQ, RWKV_HEADSIZE, B * HEADS_PAD)
    x = _pad_chains(x, B)
    return x.transpose(0, 1, 3, 2, 4).reshape(N_DIR, RWKV_HEADSIZE // KQ, RWKV_HEADSIZE, LANES)


def _rwkv_layout(B, vsplit):
    chains = N_DIR * B * RWKV_HEADS
    cp = -(-chains // (LANES // vsplit)) * (LANES // vsplit)
    return chains, cp


def _to_scan_k(t_fwd, t_bwd, B, L, vsplit):
    chains, cp = _rwkv_layout(B, vsplit)

    def one(t):
        return t.reshape(B, L, RWKV_HEADS, RWKV_HEADSIZE).transpose(1, 3, 0, 2).reshape(L, RWKV_HEADSIZE, B * RWKV_HEADS)
    x = jnp.concatenate([one(t_fwd), jnp.flip(one(t_bwd), 0)], axis=-1)
    x = jnp.pad(x, ((0, 0), (0, 0), (0, cp - chains)))
    return jnp.tile(x, (1, 1, vsplit))


def _to_scan_v(t, B, L, vsplit):
    chains, cp = _rwkv_layout(B, vsplit)
    x = t.reshape(B, L, RWKV_HEADS, RWKV_HEADSIZE).transpose(1, 3, 0, 2).reshape(L, RWKV_HEADSIZE, B * RWKV_HEADS)
    x = jnp.concatenate([x, jnp.flip(x, 0)], axis=-1)
    x = jnp.pad(x, ((0, 0), (0, 0), (0, cp - chains)))
    vs = RWKV_HEADSIZE // vsplit
    return x.reshape(L, vsplit, vs, cp).transpose(0, 2, 1, 3).reshape(L, vs, vsplit * cp)


def _from_scan_v(y, B, L, vsplit):
    chains, cp = _rwkv_layout(B, vsplit)
    vs = RWKV_HEADSIZE // vsplit
    y = y.reshape(L, vs, vsplit, cp).transpose(0, 2, 1, 3).reshape(L, RWKV_HEADSIZE, cp)[:, :, :chains]
    half = chains // 2

    def back(t):
        return t.reshape(L, RWKV_HEADSIZE, B, RWKV_HEADS).transpose(2, 0, 3, 1).reshape(B * L, RWKV_WIDTH)
    return back(y[:, :, :half]), back(jnp.flip(y[:, :, half:], 0))


def _state_to_scan(s, vsplit):
    B = s.shape[0]
    chains, cp = _rwkv_layout(B, vsplit)
    x = s.transpose(4, 3, 1, 0, 2).reshape(RWKV_HEADSIZE, RWKV_HEADSIZE, chains)
    x = jnp.pad(x, ((0, 0), (0, 0), (0, cp - chains)))
    vs = RWKV_HEADSIZE // vsplit
    return x.reshape(RWKV_HEADSIZE, vsplit, vs, cp).transpose(0, 2, 1, 3).reshape(RWKV_HEADSIZE, vs, vsplit * cp)


def _state_from_scan(x, B, vsplit):
    chains, cp = _rwkv_layout(B, vsplit)
    vs = RWKV_HEADSIZE // vsplit
    x = x.reshape(RWKV_HEADSIZE, vs, vsplit, cp).transpose(0, 2, 1, 3).reshape(RWKV_HEADSIZE, RWKV_HEADSIZE, cp)
    x = x[:, :, :chains].reshape(RWKV_HEADSIZE, RWKV_HEADSIZE, N_DIR, B, RWKV_HEADS)
    return x.transpose(3, 2, 4, 1, 0)


def _post_kernel(x_ref, mod_ref, ys5_ref, yf_ref, yb_ref, z_ref, rf_ref, rb_ref, bonus_ref, g_ref,
                 gluw_ref, glub_ref, ssdg_ref, lng_ref, lnb_ref, seg_ref, wout_ref, n2g_ref, rw_ref,
                 x1_ref, hb_ref, aff_ref):
    m = mod_ref[...]
    D = D_MODEL
    zg = _gelu_tanh(ys5_ref[...])
    gate = jnp.dot(zg.astype(BF16), gluw_ref[...], preferred_element_type=F32) + glub_ref[...]
    y_a = zg * _sigmoid(gate)
    z = z_ref[...]
    yb = (yf_ref[...] + yb_ref[...]) * (z * _sigmoid(z))
    y_b = yb * lax.rsqrt(jnp.mean(yb * yb, axis=-1, keepdims=True) + EPS) * ssdg_ref[...]
    seg = seg_ref[...] * (1.0 / RWKV_HEADSIZE)
    yr = rf_ref[...] + rb_ref[...]
    mean = _dot_exact_rhs(yr, seg)
    cen = yr - mean
    var = _dot_exact_rhs(cen * cen, seg)
    yn = cen * lax.rsqrt(var + GN_EPS) * lng_ref[...] + lnb_ref[...]
    y_c = (yn + bonus_ref[...]) * g_ref[...]
    o = jnp.dot(y_a.astype(BF16), wout_ref[0:S5_WIDTH, :], preferred_element_type=F32)
    o = o + jnp.dot(y_b.astype(BF16), wout_ref[S5_WIDTH:S5_WIDTH + SSD_WIDTH, :], preferred_element_type=F32)
    o = o + jnp.dot(y_c.astype(BF16), wout_ref[S5_WIDTH + SSD_WIDTH:, :], preferred_element_type=F32)
    x1 = x_ref[...] + m[:, 2 * D:3 * D] * o
    x1_ref[...] = x1
    h2 = x1 * lax.rsqrt(jnp.mean(x1 * x1, axis=-1, keepdims=True) + EPS) * n2g_ref[...]
    h2 = h2 * (1.0 + m[:, 4 * D:5 * D]) + m[:, 3 * D:4 * D]
    hb_ref[...] = h2.astype(BF16)
    logits = _dot3(rw_ref[...], h2, (((1,), (1,)), ((), ())))
    mx = jnp.max(logits, axis=0, keepdims=True)
    ex = jnp.exp(logits - mx)
    aff_ref[...] = ex / jnp.sum(ex, axis=0, keepdims=True)


def _post_mixer(x, mod3, row_of_block, ys5, yssd, z, rf, rb, bonus, g, lp):
    ntok = x.shape[0]
    tm = TOKEN_BLOCK
    tok = lambda w: pl.BlockSpec((tm, w), lambda i: (i, 0))
    row = lambda t: t.reshape(1, -1)
    W = RWKV_WIDTH
    return pl.pallas_call(
        _post_kernel,
        grid=(ntok // tm,),
        in_specs=[
            tok(D_MODEL),
            pl.BlockSpec((None, 1, 6 * D_MODEL), lambda i: (row_of_block(i), 0, 0)),
            tok(S5_WIDTH),
            tok(SSD_WIDTH), tok(SSD_WIDTH),
            tok(SSD_WIDTH), tok(W), tok(W), tok(W), tok(W),
            _full((S5_WIDTH, S5_WIDTH)), _full((1, S5_WIDTH)), _full((1, SSD_WIDTH)), _full((1, W)), _full((1, W)),
            _full((W, W)), _full((D_MODEL, D_MODEL)), _full((1, D_MODEL)), _full((N_EXPERTS, D_MODEL)),
        ],
        out_specs=(tok(D_MODEL), tok(D_MODEL), pl.BlockSpec((N_EXPERTS, tm), lambda i: (0, i))),
        out_shape=(jax.ShapeDtypeStruct((ntok, D_MODEL), F32), jax.ShapeDtypeStruct((ntok, D_MODEL), BF16),
                   jax.ShapeDtypeStruct((N_EXPERTS, ntok), F32)),
        compiler_params=_cparams(("parallel",)),
        name="post_mixer",
    )(x, mod3, ys5, yssd[0], yssd[1], z, rf, rb, bonus, g,
      lp['s5_glu_w'], row(lp['s5_glu_b']), row(lp['ssd_norm_g']), row(lp['rwkv_ln_g']),
      row(lp['rwkv_ln_b']), lp['seg'], lp['w_out'], row(lp['norm2_g']),
      lp['router_w'].T)


def _select_kernel(aff_ref, slot_ref, affo_ref, *, n, cap):
    E = N_EXPERTS
    a = aff_ref[...]
    bits = pltpu.bitcast(a, jnp.int32)
    thr = jnp.zeros((E, 1), jnp.int32)
    capf = float(cap)
    for bit in range(30, -1, -1):
        cand = thr | (1 << bit)
        cnt = jnp.sum(jnp.where(bits >= cand, 1.0, 0.0), axis=1, keepdims=True)
        thr = jnp.where(cnt >= capf, cand, thr)
    gt = bits > thr
    eq = bits == thr
    need = capf - jnp.sum(jnp.where(gt, 1.0, 0.0), axis=1, keepdims=True)
    CW = min(256, n)
    ui = lax.broadcasted_iota(jnp.int32, (CW, CW), 0)
    uj = lax.broadcasted_iota(jnp.int32, (CW, CW), 1)
    upper = jnp.where(ui < uj, 1.0, 0.0).astype(BF16)

    def excl_cumsum(mask_f):
        outs = []
        off = jnp.zeros((E, 1), F32)
        for c in range(n // CW):
            mc = mask_f[:, c * CW:(c + 1) * CW]
            outs.append(jnp.dot(mc.astype(BF16), upper, preferred_element_type=F32) + off)
            off = off + jnp.sum(mc, axis=1, keepdims=True)
        return jnp.concatenate(outs, axis=1)

    eq_rank = excl_cumsum(jnp.where(eq, 1.0, 0.0))
    sel = jnp.where(gt, 1.0, jnp.where(eq, jnp.where(eq_rank < need, 1.0, 0.0), 0.0))
    pos = excl_cumsum(sel)
    slot = jnp.where(sel > 0.0, pos, -1.0)
    for e in range(E):
        slot_ref[e] = slot[e:e + 1, :]
        affo_ref[e] = a[e:e + 1, :]


def _select(aff, B, n, cap):
    spec = pl.BlockSpec((None, N_EXPERTS, 1, n), lambda b: (b, 0, 0, 0))
    return pl.pallas_call(
        functools.partial(_select_kernel, n=n, cap=cap),
        grid=(B,),
        in_specs=[pl.BlockSpec((N_EXPERTS, n), lambda b: (0, b))],
        out_specs=(spec, spec),
        out_shape=(jax.ShapeDtypeStruct((B, N_EXPERTS, 1, n), F32), jax.ShapeDtypeStruct((B, N_EXPERTS, 1, n), F32)),
        compiler_params=_cparams(("parallel",)),
        name="ec_select",
    )(aff)


def _slot_block_range(slots_f, cap, SB):
    lo = jnp.min(jnp.where(slots_f >= 0.0, slots_f, float(cap))).astype(jnp.int32)
    hi = jnp.max(slots_f).astype(jnp.int32)
    first = lo // SB
    count = jnp.where(hi >= 0, hi // SB - first + 1, 0)
    return first, count


EC_SLOT_BLOCK = 256


def _one_hot_all(slot_ref, aff_ref, n, cap):
    srow = lax.broadcasted_iota(jnp.int32, (cap, n), 0).astype(F32)
    ohs, gates = [], []
    for e in range(N_EXPERTS):
        hit = slot_ref[e] == srow
        ohs.append(jnp.where(hit, 1.0, 0.0).astype(BF16))
        if aff_ref is not None:
            gates.append(jnp.sum(jnp.where(hit, aff_ref[e], 0.0), axis=1, keepdims=True))
    return jnp.concatenate(ohs, axis=0), (jnp.concatenate(gates, axis=0) if gates else None)


def _gather_all_kernel(hb_ref, slot_ref, aff_ref, xs_ref, gate_ref, *, n, cap):
    oh, gate = _one_hot_all(slot_ref, aff_ref, n, cap)
    xs = jnp.dot(oh, hb_ref[...], preferred_element_type=F32)
    xs_ref[...] = xs.astype(BF16).reshape(N_EXPERTS, cap, D_MODEL)
    gate_ref[...] = gate.reshape(N_EXPERTS, cap, 1)


def _gather_kernel(hb_ref, slot_ref, aff_ref, xs_ref, gate_ref, acc, gacc, *, n, cap):
    NC = min(512, n)
    SB = min(EC_SLOT_BLOCK // 2, cap)
    acc[...] = jnp.zeros_like(acc)
    gacc[...] = jnp.zeros_like(gacc)
    srow = lax.broadcasted_iota(jnp.int32, (SB, NC), 0).astype(F32)
    for c in range(n // NC):
        sl = slot_ref[:, c * NC:(c + 1) * NC]
        first, count = _slot_block_range(sl, cap, SB)
        for j in range(min(cap // SB, NC // SB + 1)):
            @pl.when(j < count)
            def _(c=c, j=j, sl=sl, first=first):
                base = pl.multiple_of((first + j) * SB, SB)
                hit = (sl - base.astype(F32)) == srow
                oh = jnp.where(hit, 1.0, 0.0).astype(BF16)
                acc[pl.ds(base, SB), :] += jnp.dot(oh, hb_ref[c * NC:(c + 1) * NC, :], preferred_element_type=F32)
                gacc[pl.ds(base, SB), :] += jnp.sum(jnp.where(hit, aff_ref[:, c * NC:(c + 1) * NC], 0.0), axis=1,
                                                    keepdims=True)
    xs_ref[...] = acc[...].astype(BF16)
    gate_ref[...] = gacc[...]


def _gather(hb, slot4, aff4, B, n, cap):
    E = N_EXPERTS
    if E * cap <= 512:
        all_spec = pl.BlockSpec((None, E, 1, n), lambda b: (b, 0, 0, 0))
        return pl.pallas_call(
            functools.partial(_gather_all_kernel, n=n, cap=cap),
            grid=(B,),
            in_specs=[pl.BlockSpec((n, D_MODEL), lambda b: (b, 0)), all_spec, all_spec],
            out_specs=(pl.BlockSpec((None, E, cap, D_MODEL), lambda b: (b, 0, 0, 0)),
                       pl.BlockSpec((None, E, cap, 1), lambda b: (b, 0, 0, 0))),
            out_shape=(jax.ShapeDtypeStruct((B, E, cap, D_MODEL), BF16), jax.ShapeDtypeStruct((B, E, cap, 1), F32)),
            compiler_params=_cparams(("parallel",)),
            name="ec_gather_all",
        )(hb, slot4, aff4)
    return pl.pallas_call(
        functools.partial(_gather_kernel, n=n, cap=cap),
        grid=(B, E),
        in_specs=[
            pl.BlockSpec((n, D_MODEL), lambda b, e: (b, 0)),
            pl.BlockSpec((None, None, 1, n), lambda b, e: (b, e, 0, 0)),
            pl.BlockSpec((None, None, 1, n), lambda b, e: (b, e, 0, 0)),
        ],
        out_specs=(pl.BlockSpec((None, None, cap, D_MODEL), lambda b, e: (b, e, 0, 0)),
                   pl.BlockSpec((None, None, cap, 1), lambda b, e: (b, e, 0, 0))),
        out_shape=(jax.ShapeDtypeStruct((B, E, cap, D_MODEL), BF16), jax.ShapeDtypeStruct((B, E, cap, 1), F32)),
        scratch_shapes=[pltpu.VMEM((cap, D_MODEL), F32), pltpu.VMEM((cap, 1), F32)],
        compiler_params=_cparams(("parallel", "arbitrary")),
        name="ec_gather",
    )(hb, slot4, aff4)


def _ffn_kernel(xs_ref, gate_ref, w1_ref, w3_ref, w2_ref, o_ref):
    bg, cap, _ = xs_ref.shape
    x = xs_ref[...].reshape(bg * cap, D_MODEL)
    h1 = jnp.dot(x, w1_ref[...], preferred_element_type=F32)
    h3 = jnp.dot(x, w3_ref[...], preferred_element_type=F32)
    hid = (h1 * _sigmoid(h1) * h3).astype(BF16)
    o = jnp.dot(hid, w2_ref[...], preferred_element_type=F32) * gate_ref[...].reshape(bg * cap, 1)
    o_ref[...] = o.astype(BF16).reshape(bg, cap, D_MODEL)


def _expert_ffn(xs, gate, w1, w3, w2, bg):
    B, E, cap, _ = xs.shape
    return pl.pallas_call(
        _ffn_kernel,
        grid=(E, B // bg),
        in_specs=[
            pl.BlockSpec((bg, None, cap, D_MODEL), lambda e, b: (b, e, 0, 0)),
            pl.BlockSpec((bg, None, cap, 1), lambda e, b: (b, e, 0, 0)),
            pl.BlockSpec((None, D_MODEL, D_EXPERT), lambda e, b: (e, 0, 0)),
            pl.BlockSpec((None, D_MODEL, D_EXPERT), lambda e, b: (e, 0, 0)),
            pl.BlockSpec((None, D_EXPERT, D_MODEL), lambda e, b: (e, 0, 0)),
        ],
        out_specs=pl.BlockSpec((bg, None, cap, D_MODEL), lambda e, b: (b, e, 0, 0)),
        out_shape=jax.ShapeDtypeStruct((B, E, cap, D_MODEL), BF16),
        compiler_params=_cparams(("parallel", "parallel")),
        name="ec_ffn",
    )(xs, gate, w1, w3, w2)


_TN_DIMS = (((0,), (0,)), ((), ()))


def _ec_residual(x1_ref, mod_ref, fg_ref, out_ref, ffn, final):
    x2 = x1_ref[...] + mod_ref[:, 5 * D_MODEL:6 * D_MODEL] * ffn
    if final:
        x2 = x2 * lax.rsqrt(jnp.mean(x2 * x2, axis=-1, keepdims=True) + EPS) * fg_ref[...]
    out_ref[...] = x2


def _scatter_all_kernel(slot_ref, o_ref, x1_ref, mod_ref, fg_ref, out_ref, *, n, cap, final):
    oh, _ = _one_hot_all(slot_ref, None, n, cap)
    ffn = lax.dot_general(oh, o_ref[...].reshape(N_EXPERTS * cap, D_MODEL), _TN_DIMS, preferred_element_type=F32)
    _ec_residual(x1_ref, mod_ref, fg_ref, out_ref, ffn, final)


def _scatter_kernel(slot_ref, o_ref, x1_ref, mod_ref, fg_ref, out_ref, acc, *, cap, final):
    e = pl.program_id(2)
    tn = x1_ref.shape[0]

    @pl.when(e == 0)
    def _():
        acc[...] = jnp.zeros_like(acc)

    sl = slot_ref[...]
    SB = min(EC_SLOT_BLOCK, cap)
    srow = lax.broadcasted_iota(jnp.int32, (SB, tn), 0).astype(F32)
    first, count = _slot_block_range(sl, cap, SB)
    for j in range(min(cap // SB, tn // SB + 1)):
        @pl.when(j < count)
        def _(j=j):
            base = pl.multiple_of((first + j) * SB, SB)
            oh = jnp.where((sl - base.astype(F32)) == srow, 1.0, 0.0).astype(BF16)
            acc[...] += lax.dot_general(oh, o_ref[pl.ds(base, SB), :], _TN_DIMS, preferred_element_type=F32)

    @pl.when(e == N_EXPERTS - 1)
    def _():
        _ec_residual(x1_ref, mod_ref, fg_ref, out_ref, acc[...], final)


def _scatter(slot4, o, x1, mod3, mod_row, final_g, B, n, cap, final):
    E = N_EXPERTS
    if E * cap <= 512:
        return pl.pallas_call(
            functools.partial(_scatter_all_kernel, n=n, cap=cap, final=final),
            grid=(B,),
            in_specs=[
                pl.BlockSpec((None, E, 1, n), lambda b: (b, 0, 0, 0)),
                pl.BlockSpec((None, E, cap, D_MODEL), lambda b: (b, 0, 0, 0)),
                pl.BlockSpec((n, D_MODEL), lambda b: (b, 0)),
                pl.BlockSpec((None, 1, 6 * D_MODEL), lambda b: (mod_row(b), 0, 0)),
                _full((1, D_MODEL)),
            ],
            out_specs=pl.BlockSpec((n, D_MODEL), lambda b: (b, 0)),
            out_shape=jax.ShapeDtypeStruct((B * n, D_MODEL), F32),
            compiler_params=_cparams(("parallel",)),
            name="ec_scatter_all",
        )(slot4, o, x1, mod3, final_g.reshape(1, D_MODEL))
    tn = min(1024, n)
    nt = n // tn
    return pl.pallas_call(
        functools.partial(_scatter_kernel, cap=cap, final=final),
        grid=(B, nt, N_EXPERTS),
        in_specs=[
            pl.BlockSpec((None, None, 1, tn), lambda b, t, e: (b, e, 0, t)),
            pl.BlockSpec((None, None, cap, D_MODEL), lambda b, t, e: (b, e, 0, 0)),
            pl.BlockSpec((tn, D_MODEL), lambda b, t, e: (b * nt + t, 0)),
            pl.BlockSpec((None, 1, 6 * D_MODEL), lambda b, t, e: (mod_row(b), 0, 0)),
            _full((1, D_MODEL)),
        ],
        out_specs=pl.BlockSpec((tn, D_MODEL), lambda b, t, e: (b * nt + t, 0)),
        out_shape=jax.ShapeDtypeStruct((B * n, D_MODEL), F32),
        scratch_shapes=[pltpu.VMEM((tn, D_MODEL), F32)],
        compiler_params=_cparams(("parallel", "parallel", "arbitrary")),
        name="ec_scatter",
    )(slot4, o, x1, mod3, final_g.reshape(1, D_MODEL))


def _expert_choice(hb, aff, x1, mod3, mod_row, final_g, w1, w3, w2, B, n, final):
    cap = EC_FACTOR * n // N_EXPERTS
    slot4, aff4 = _select(aff, B, n, cap)
    xs, gate = _gather(hb, slot4, aff4, B, n, cap)
    bg = max(1, min(B, 256 // cap))
    o = _expert_ffn(xs, gate, w1, w3, w2, bg)
    return _scatter(slot4, o, x1, mod3, mod_row, final_g, B, n, cap, final)


def kernel(x_prompt, x_sample, c, state_s5_re, state_s5_im, state_ssd, state_rwkv, c_ctx, ada_w, ada_b, norm1_g, norm2_g, w_in, w_out, s5_a_re, s5_a_im, s5_log_dt, s5_b_re, s5_b_im, s5_c_re, s5_c_im, s5_d, s5_glu_w, s5_glu_b, ssd_conv_w, ssd_conv_b, ssd_a_log, ssd_dt_bias, ssd_d, ssd_norm_g, rwkv_mu, rwkv_w0, rwkv_w_up, rwkv_a0, rwkv_a_up, rwkv_g_up, rwkv_k_k, rwkv_k_a, rwkv_r_k, rwkv_ln_g, rwkv_ln_b, router_w, exp_w1, exp_w3, exp_w2, final_g):
    Bp, Lp, D = x_prompt.shape
    Bs, Ls, _ = x_sample.shape
    depth = ada_w.shape[0]
    Np, Ns = Bp * Lp, Bs * Ls
    tm = TOKEN_BLOCK
    grid_rows = Ls // GRID_W
    nfs = S5_GROUPS // 8
    assert Lp % tm == 0 and Ls % tm == 0 and Lp % SSD_CHUNK == 0

    n_rows = 1 + Bs
    rows_pad = -(-n_rows // 8) * 8
    cond = jnp.zeros((rows_pad, D), F32).at[0].set(c_ctx).at[1:n_rows].set(c)
    mod = _modulation(cond, ada_w, ada_b)
    s_blocks_per_req = Ls // tm
    row_p = lambda i: 0
    row_s = lambda i: 1 + i // s_blocks_per_req

    ab_re, ab_im, bb_re, bb_im = _s5_discretize(s5_a_re, s5_a_im, s5_log_dt, s5_b_re, s5_b_im)
    w_in_pad = _pad_in_weight(w_in)
    w_out_b = w_out.astype(BF16)
    glu_w_b = s5_glu_w.astype(BF16)
    exp_w1_b, exp_w3_b, exp_w2_b = exp_w1.astype(BF16), exp_w3.astype(BF16), exp_w2.astype(BF16)
    seg = _segment_ones(RWKV_WIDTH, RWKV_HEADSIZE)

    xp = x_prompt.reshape(Np, D)
    xs = x_sample.reshape(Ns, D)
    new_s5_re, new_s5_im, new_ssd, new_rwkv = [], [], [], []
    QP = 2 if Bp % 2 == 0 and Bp >= 2 else 1
    RP = Bp // QP
    assert Bs * HEADS_PAD <= CH

    for l in range(depth):
        lp = {
            'rwkv_mu': rwkv_mu[l], 'rwkv_w0': rwkv_w0[l], 'rwkv_w_up': rwkv_w_up[l], 'rwkv_a0': rwkv_a0[l],
            'rwkv_a_up': rwkv_a_up[l], 'rwkv_g_up': rwkv_g_up[l], 'rwkv_k_k': rwkv_k_k[l], 'rwkv_k_a': rwkv_k_a[l],
            'rwkv_r_k': rwkv_r_k[l].reshape(-1), 'rwkv_ln_g': rwkv_ln_g[l], 'rwkv_ln_b': rwkv_ln_b[l],
            's5_glu_w': glu_w_b[l], 's5_glu_b': s5_glu_b[l], 'ssd_norm_g': ssd_norm_g[l], 'w_out': w_out_b[l],
            'norm2_g': norm2_g[l], 'router_w': router_w[l], 'seg': seg,
        }
        mod3 = mod[l].reshape(rows_pad, 1, 6 * D)
        us5_p, z_p, xbc_p, dt_p, rkv_p, lo_p = _in_projection(xp, mod3, row_p, norm1_g[l], w_in_pad[l])
        us5_s, z_s, xbc_s, dt_s, rkv_s, lo_s = _in_projection(xs, mod3, row_s, norm1_g[l], w_in_pad[l])

        tables = _s5_layer_tables(ab_re[l], ab_im[l], bb_re[l], bb_im[l], s5_c_re[l], s5_c_im[l])
        d_row = s5_d[l].reshape(1, S5_WIDTH)
        up = us5_p.reshape(QP, RP, Lp, S5_WIDTH).transpose(0, 2, 1, 3).reshape(QP, Lp * RP, S5_WIDTH)
        yp, hfin = _s5_scan(up, None, tables, d_row, R=RP, n_slab=Lp, chained=False, want_final=True)
        ys5_p = yp.reshape(QP, Lp, RP, S5_WIDTH).transpose(0, 2, 1, 3).reshape(Np, S5_WIDTH)
        hf = hfin.transpose(0, 3, 1, 2, 4).reshape(Bp, N_DIR, nfs, 2, 8, S5_STATE)
        new_s5_re.append(hf[:, :, :, 0].reshape(Bp, N_DIR, S5_GROUPS, S5_STATE))
        new_s5_im.append(hf[:, :, :, 1].reshape(Bp, N_DIR, S5_GROUPS, S5_STATE))
        h0 = jnp.concatenate([state_s5_re[:, l].reshape(Bs, N_DIR, nfs, 1, S5_SLICE_ST),
                              state_s5_im[:, l].reshape(Bs, N_DIR, nfs, 1, S5_SLICE_ST)], axis=-1)
        (ysm,) = _s5_scan(us5_s.reshape(Bs, Ls, S5_WIDTH), h0, tables, d_row, R=GRID_W, n_slab=grid_rows,
                          chained=True, want_final=False)
        ys5_s = ysm.reshape(Ns, S5_WIDTH)

        ssd_args = (ssd_conv_w[l], ssd_conv_b[l], ssd_dt_bias[l], ssd_a_log[l], ssd_d[l])
        *yssd_p, hssd = _ssd_scan(xbc_p, dt_p, None, *ssd_args, B=Bp, L=Lp, want_final=True)
        new_ssd.append(hssd.transpose(0, 1, 2, 4, 3))
        yssd_s = _ssd_scan(xbc_s, dt_s, state_ssd[:, l].transpose(0, 1, 2, 4, 3), *ssd_args,
                           B=Bs, L=Ls, want_final=False)

        r_, w0_, w1_, k_, v_, kk_, b_, g_p, bonus_p = _rwkv_prep(rkv_p, lo_p, lp, Lp // tm)
        zero_state = jnp.zeros((Bp, N_DIR, RWKV_HEADS, RWKV_HEADSIZE, RWKV_HEADSIZE), F32)
        y_, sfin = _rwkv_scan(_to_scan_k(r_, r_, Bp, Lp, 1), _to_scan_k(w0_, w1_, Bp, Lp, 1),
                              _to_scan_k(k_, k_, Bp, Lp, 1), _to_scan_k(kk_, kk_, Bp, Lp, 1),
                              _to_scan_k(b_, b_, Bp, Lp, 1), _to_scan_v(v_, Bp, Lp, 1),
                              _state_to_scan(zero_state, 1))
        rf_p, rb_p = _from_scan_v(y_, Bp, Lp, 1)
        new_rwkv.append(_state_from_scan(sfin, Bp, 1))

        r_, w0_, w1_, k_, v_, kk_, b_, g_s, bonus_s = _rwkv_prep(rkv_s, lo_s, lp, Ls // tm, SCAN_WIDTH)
        sc = lambda t: _to_scan2_k(t, Bs, Ls)
        yf_, yb_ = _rwkv_scan2(sc(r_), sc(w0_), sc(w1_), sc(k_), sc(kk_), sc(b_), _to_scan2_v(v_, Bs, Ls),
                               _state_to_scan2(state_rwkv[:, l]))
        rf_s = _from_scan2(yf_, Bs, Ls)
        rb_s = _from_scan2(yb_, Bs, Ls)

        x1_p, hb_p, aff_p = _post_mixer(xp, mod3, row_p, ys5_p, yssd_p, z_p, rf_p, rb_p, bonus_p, g_p, lp)
        x1_s, hb_s, aff_s = _post_mixer(xs, mod3, row_s, ys5_s, yssd_s, z_s, rf_s, rb_s, bonus_s, g_s, lp)

        final = l == depth - 1
        ew = (exp_w1_b[l], exp_w3_b[l], exp_w2_b[l])
        xp = _expert_choice(hb_p, aff_p, x1_p, mod3, lambda b: 0, final_g, *ew, Bp, Lp, final)
        xs = _expert_choice(hb_s, aff_s, x1_s, mod3, lambda b: 1 + b, final_g, *ew, Bs, Ls, final)

    y_prompt = xp.reshape(Bp, Lp, D)
    y_sample = xs.reshape(Bs, Ls, D)
    return (y_prompt, y_sample, jnp.stack(new_s5_re, axis=1), jnp.stack(new_s5_im, axis=1),
            jnp.stack(new_ssd, axis=1), jnp.stack(new_rwkv, axis=1))
```

```python
import functools
import math

import jax
import jax.numpy as jnp
from jax import lax
from jax.experimental import pallas as pl
from jax.experimental.pallas import tpu as pltpu

F32 = jnp.float32
BF16 = jnp.bfloat16
HIGHEST = lax.Precision.HIGHEST

D_MODEL = 1024
GRID_W = 64
N_DIR = 2
EPS = 1e-6
S5_WIDTH = 256
S5_CH = 16
S5_GROUPS = 16
S5_STATE = 64
SSD_HEADDIM = 64
SSD_HEADS = 6
SSD_WIDTH = 384
SSD_GROUPS = 2
SSD_STATE = 64
SSD_BC = 128
SSD_CONV_CH = 640
SSD_CHUNK = 128
RWKV_HEADSIZE = 64
RWKV_HEADS = 6
RWKV_WIDTH = 384
W_RANK = 32
A_RANK = 32
G_RANK = 64
DECAY_SCALE = math.exp(-0.5)
GN_EPS = 64e-5
N_EXPERTS = 16
D_EXPERT = 512
EC_FACTOR = 2
IN_SIZES = (S5_WIDTH, SSD_WIDTH, SSD_CONV_CH, N_DIR * SSD_HEADS, 3 * RWKV_WIDTH, N_DIR * W_RANK, A_RANK, G_RANK)

LANES = 128
TOKEN_BLOCK = 256
DT_PAD = 128
LO_PAD = 256
S5_SLICE_CH = 128
S5_SLICE_ST = 512
VMEM_LIMIT = 56 * 1024 * 1024


def _cparams(sem):
    return pltpu.CompilerParams(dimension_semantics=sem, vmem_limit_bytes=VMEM_LIMIT)


def _full(shape):
    nd = len(shape)
    return pl.BlockSpec(shape, lambda *_: (0,) * nd)


def _sigmoid(x):
    return 1.0 / (1.0 + jnp.exp(-x))


def _softplus(x):
    return jnp.maximum(x, 0.0) + jnp.log1p(jnp.exp(-jnp.abs(x)))


def _split_bf16(x):
    hi = x.astype(BF16)
    lo = (x - hi.astype(F32)).astype(BF16)
    return hi, lo


def _dot_exact_rhs(x, m, dims=None):
    hi, lo = _split_bf16(x)
    mb = m.astype(BF16)
    if dims is None:
        return jnp.dot(hi, mb, preferred_element_type=F32) + jnp.dot(lo, mb, preferred_element_type=F32)
    return (lax.dot_general(hi, mb, dims, preferred_element_type=F32)
            + lax.dot_general(lo, mb, dims, preferred_element_type=F32))


def _dot_exact_lhs(m, x):
    hi, lo = _split_bf16(x)
    mb = m.astype(BF16)
    return jnp.dot(mb, hi, preferred_element_type=F32) + jnp.dot(mb, lo, preferred_element_type=F32)


def _dot3(x, w, dims=(((1,), (0,)), ((), ()))):
    xh, xl = _split_bf16(x)
    wh, wl = _split_bf16(w)
    dg = functools.partial(lax.dot_general, dimension_numbers=dims, preferred_element_type=F32)
    return dg(xh, wh) + (dg(xl, wh) + dg(xh, wl))


def _gelu_tanh(x):
    return 0.5 * x * (1.0 + jnp.tanh(math.sqrt(2.0 / math.pi) * (x + 0.044715 * (x * x * x))))


def _mod_kernel(c_ref, w_ref, b_ref, o_ref):
    c = c_ref[...]
    s = (c * _sigmoid(c)).astype(BF16)
    o_ref[...] = jnp.dot(s, w_ref[...].astype(BF16), preferred_element_type=F32) + b_ref[...]


def _modulation(cond, ada_w, ada_b):
    depth = ada_w.shape[0]
    rows = cond.shape[0]
    tn = 1536
    return pl.pallas_call(
        _mod_kernel,
        grid=(depth, 6 * D_MODEL // tn),
        in_specs=[
            pl.BlockSpec((rows, D_MODEL), lambda l, j: (0, 0)),
            pl.BlockSpec((None, D_MODEL, tn), lambda l, j: (l, 0, j)),
            pl.BlockSpec((None, 1, tn), lambda l, j: (l, 0, j)),
        ],
        out_specs=pl.BlockSpec((None, rows, tn), lambda l, j: (l, 0, j)),
        out_shape=jax.ShapeDtypeStruct((depth, rows, 6 * D_MODEL), F32),
        compiler_params=_cparams(("parallel", "parallel")),
        name="adaln_mod",
    )(cond, ada_w, ada_b.reshape(depth, 1, 6 * D_MODEL))


IN_PAD_SIZES = (S5_WIDTH, SSD_WIDTH, SSD_CONV_CH, DT_PAD, 3 * RWKV_WIDTH, LO_PAD)


def _inproj_kernel(x_ref, mod_ref, g_ref, w_ref, us5_ref, z_ref, xbc_ref, dt_ref, rkv_ref, lo_ref):
    x = x_ref[...]
    y = x * lax.rsqrt(jnp.mean(x * x, axis=-1, keepdims=True) + EPS) * g_ref[...]
    m = mod_ref[...]
    h = y * (1.0 + m[:, D_MODEL:2 * D_MODEL]) + m[:, 0:D_MODEL]
    p = jnp.dot(h.astype(BF16), w_ref[...], preferred_element_type=F32)
    start = 0
    for ref, size in zip((us5_ref, z_ref, xbc_ref, dt_ref, rkv_ref, lo_ref), IN_PAD_SIZES):
        ref[...] = p[:, start:start + size]
        start += size


def _pad_in_weight(w_in):
    parts, start = [], 0
    for s in IN_SIZES:
        parts.append(w_in[..., start:start + s])
        start += s
    us5, z, xbc, dt, rkv, wlo, alo, glo = parts
    zeros = lambda n: jnp.zeros(w_in.shape[:-1] + (n,), w_in.dtype)
    lo_used = wlo.shape[-1] + alo.shape[-1] + glo.shape[-1]
    return jnp.concatenate([us5, z, xbc, dt, zeros(DT_PAD - dt.shape[-1]), rkv, wlo, alo, glo,
                            zeros(LO_PAD - lo_used)], axis=-1).astype(BF16)


def _in_projection(x, mod3, row_of_block, norm_g, w_pad):
    ntok = x.shape[0]
    tm = TOKEN_BLOCK
    width = w_pad.shape[1]
    outs = tuple(jax.ShapeDtypeStruct((ntok, s), F32) for s in IN_PAD_SIZES)
    return pl.pallas_call(
        _inproj_kernel,
        grid=(ntok // tm,),
        in_specs=[
            pl.BlockSpec((tm, D_MODEL), lambda i: (i, 0)),
            pl.BlockSpec((None, 1, 6 * D_MODEL), lambda i: (row_of_block(i), 0, 0)),
            _full((1, D_MODEL)),
            _full((D_MODEL, width)),
        ],
        out_specs=tuple(pl.BlockSpec((tm, s), lambda i: (i, 0)) for s in IN_PAD_SIZES),
        out_shape=outs,
        compiler_params=_cparams(("parallel",)),
        name="in_proj",
    )(x, mod3, norm_g.reshape(1, D_MODEL), w_pad)


def _s5_disc_kernel(are_ref, aim_ref, ldt_ref, bre_ref, bim_ref, abre_ref, abim_ref, bbre_ref, bbim_ref):
    lam_re = jnp.minimum(are_ref[...], -1e-4)
    lam_im = aim_ref[...]
    dt = jnp.exp(ldt_ref[...])
    mag = jnp.exp(lam_re * dt)
    ab_re = mag * jnp.cos(lam_im * dt)
    ab_im = mag * jnp.sin(lam_im * dt)
    num_re, num_im = ab_re - 1.0, ab_im
    den = lam_re * lam_re + lam_im * lam_im
    q_re = (num_re * lam_re + num_im * lam_im) / den
    q_im = (num_im * lam_re - num_re * lam_im) / den
    abre_ref[...] = ab_re
    abim_ref[...] = ab_im
    b_re = bre_ref[...]
    b_im = bim_ref[...]
    qr = q_re[:, None, :]
    qi = q_im[:, None, :]
    bbre_ref[...] = qr * b_re - qi * b_im
    bbim_ref[...] = qr * b_im + qi * b_re


def _s5_discretize(a_re, a_im, log_dt, b_re, b_im):
    lead = a_re.shape[:3]
    n = lead[0] * lead[1] * lead[2]
    a2 = lambda t: t.reshape(n, S5_STATE)
    ldt = jnp.broadcast_to(log_dt.reshape(n, 1), (n, S5_STATE))
    b3 = lambda t: t.reshape(n, S5_STATE, S5_CH).transpose(0, 2, 1)
    ab_re, ab_im, bb_re, bb_im = pl.pallas_call(
        _s5_disc_kernel,
        out_shape=(jax.ShapeDtypeStruct((n, S5_STATE), F32), jax.ShapeDtypeStruct((n, S5_STATE), F32),
                   jax.ShapeDtypeStruct((n, S5_CH, S5_STATE), F32), jax.ShapeDtypeStruct((n, S5_CH, S5_STATE), F32)),
        name="s5_discretize",
    )(a2(a_re), a2(a_im), ldt, b3(b_re), b3(b_im))
    return (ab_re.reshape(lead + (S5_STATE,)), ab_im.reshape(lead + (S5_STATE,)),
            bb_re.reshape(lead + (S5_CH, S5_STATE)), bb_im.reshape(lead + (S5_CH, S5_STATE)))


def _s5_layer_tables(ab_re, ab_im, bb_re, bb_im, c_re, c_im):
    nfs = S5_GROUPS // 8
    eye = jnp.eye(8, dtype=F32)

    def rows(t):
        return t.reshape(N_DIR, nfs, 8 * S5_STATE)

    ab_row = jnp.concatenate([rows(ab_re), rows(ab_im)], axis=-1).reshape(N_DIR, nfs, 1, 2 * S5_SLICE_ST)

    def bmat(t):
        t = t.reshape(N_DIR, nfs, 8, S5_CH, S5_STATE)
        return jnp.einsum('dfghp,gk->dfghkp', t, eye).reshape(N_DIR, nfs, S5_SLICE_CH, S5_SLICE_ST)

    b_mat = jnp.concatenate([bmat(bb_re), bmat(bb_im)], axis=-1).astype(BF16)

    def cmat(t):
        t = t.reshape(N_DIR, nfs, 8, S5_CH, S5_STATE)
        return jnp.einsum('dfghp,gk->dfgpkh', t, eye).reshape(N_DIR, nfs, S5_SLICE_ST, S5_SLICE_CH)

    c_mat = jnp.concatenate([cmat(c_re), -cmat(c_im)], axis=-2).astype(BF16)
    return ab_row, b_mat, c_mat


def _s5_kernel(*refs, R, n_slab, chained, want_final):
    if chained:
        u_ref, h0_ref, ab_ref, bm_ref, cm_ref, d_ref = refs[:6]
        rest = refs[6:]
    else:
        u_ref, ab_ref, bm_ref, cm_ref, d_ref = refs[:5]
        h0_ref = None
        rest = refs[5:]
    y_ref = rest[0]
    rest = rest[1:]
    if want_final:
        hfin_ref = rest[0]
        rest = rest[1:]
    H = rest[0]
    if chained:
        PW, CIN = rest[1], rest[2]
    NR = R * n_slab
    RC = min(512, NR)
    ST = S5_SLICE_ST
    nchunk = ST // LANES

    y_ref[...] = u_ref[...] * d_ref[...]

    for d in range(N_DIR):
        def slab_of(i, d=d):
            return i if d == 0 else n_slab - 1 - i

        def bu_body(i, _, d=d):
            r0 = pl.multiple_of(i * RC, RC)
            H[pl.ds(r0, RC), :] = jnp.dot(u_ref[pl.ds(r0, RC), :].astype(BF16), bm_ref[d],
                                           preferred_element_type=F32)
            return 0
        lax.fori_loop(0, NR // RC, bu_body, 0)

        for c in range(nchunk):
            lre = slice(c * LANES, (c + 1) * LANES)
            lim = slice(ST + c * LANES, ST + (c + 1) * LANES)
            a_re = jnp.broadcast_to(ab_ref[d, :, lre], (R, LANES))
            a_im = jnp.broadcast_to(ab_ref[d, :, lim], (R, LANES))

            def step(i, carry, lre=lre, lim=lim, a_re=a_re, a_im=a_im, slab_of=slab_of):
                cr, ci = carry
                r0 = pl.multiple_of(slab_of(i) * R, R)
                nr = a_re * cr - a_im * ci + H[pl.ds(r0, R), lre]
                ni = a_re * ci + a_im * cr + H[pl.ds(r0, R), lim]
                H[pl.ds(r0, R), lre] = nr
                H[pl.ds(r0, R), lim] = ni
                return nr, ni
            zero = jnp.zeros((R, LANES), F32)
            lax.fori_loop(0, n_slab, step, (zero, zero))

        last0 = (n_slab - 1) * R if d == 0 else 0
        if chained:
            a_re_row = ab_ref[d, :, 0:ST]
            a_im_row = ab_ref[d, :, ST:2 * ST]

            def pw_step(j, carry, a_re_row=a_re_row, a_im_row=a_im_row):
                pr, pi = carry
                PW[j, :, 0:ST] = jnp.broadcast_to(pr, (8, ST))
                PW[j, :, ST:2 * ST] = jnp.broadcast_to(pi, (8, ST))
                return pr * a_re_row - pi * a_im_row, pr * a_im_row + pi * a_re_row
            lax.fori_loop(0, n_slab, pw_step, (a_re_row, a_im_row))
            t_re = PW[n_slab - 1, 0:1, 0:ST]
            t_im = PW[n_slab - 1, 0:1, ST:2 * ST]

            cr = h0_ref[d, :, 0:ST]
            ci = h0_ref[d, :, ST:2 * ST]
            for i in range(R):
                c = i if d == 0 else R - 1 - i
                CIN[c:c + 1, 0:ST] = cr
                CIN[c:c + 1, ST:2 * ST] = ci
                er = H[last0 + c:last0 + c + 1, 0:ST]
                ei = H[last0 + c:last0 + c + 1, ST:2 * ST]
                cr, ci = t_re * cr - t_im * ci + er, t_re * ci + t_im * cr + ei

            def fix_step(i, _, slab_of=slab_of):
                r0 = pl.multiple_of(slab_of(i) * R, R)
                p = PW[i]
                for c in range(nchunk):
                    lre = slice(c * LANES, (c + 1) * LANES)
                    lim = slice(ST + c * LANES, ST + (c + 1) * LANES)
                    pr = p[0:1, lre]
                    pi = p[0:1, lim]
                    cr = CIN[:, lre]
                    ci = CIN[:, lim]
                    H[pl.ds(r0, R), lre] = H[pl.ds(r0, R), lre] + (pr * cr - pi * ci)
                    H[pl.ds(r0, R), lim] = H[pl.ds(r0, R), lim] + (pr * ci + pi * cr)
                return 0
            lax.fori_loop(0, n_slab, fix_step, 0)

        if want_final:
            hfin_ref[d] = H[last0:last0 + R, :]

        def y_body(i, _, d=d):
            r0 = pl.multiple_of(i * RC, RC)
            y_ref[pl.ds(r0, RC), :] = y_ref[pl.ds(r0, RC), :] + jnp.dot(
                H[pl.ds(r0, RC), :].astype(BF16), cm_ref[d], preferred_element_type=F32)
            return 0
        lax.fori_loop(0, NR // RC, y_body, 0)


def _s5_scan(u, h0, tables, d_row, *, R, n_slab, chained, want_final):
    ab_row, b_mat, c_mat = tables
    Q, NR, _ = u.shape
    nfs = S5_GROUPS // 8
    W2 = 2 * S5_SLICE_ST
    in_specs = [pl.BlockSpec((None, NR, S5_SLICE_CH), lambda q, f: (q, 0, f))]
    args = [u]
    if chained:
        in_specs.append(pl.BlockSpec((None, N_DIR, None, 1, W2), lambda q, f: (q, 0, f, 0, 0)))
        args.append(h0)
    in_specs += [
        pl.BlockSpec((N_DIR, None, 1, W2), lambda q, f: (0, f, 0, 0)),
        pl.BlockSpec((N_DIR, None, S5_SLICE_CH, W2), lambda q, f: (0, f, 0, 0)),
        pl.BlockSpec((N_DIR, None, W2, S5_SLICE_CH), lambda q, f: (0, f, 0, 0)),
        pl.BlockSpec((1, S5_SLICE_CH), lambda q, f: (0, f)),
    ]
    args += [ab_row, b_mat, c_mat, d_row]
    out_shape = [jax.ShapeDtypeStruct((Q, NR, S5_WIDTH), F32)]
    out_specs = [pl.BlockSpec((None, NR, S5_SLICE_CH), lambda q, f: (q, 0, f))]
    if want_final:
        out_shape.append(jax.ShapeDtypeStruct((Q, N_DIR, nfs, R, W2), F32))
        out_specs.append(pl.BlockSpec((None, N_DIR, None, R, W2), lambda q, f: (q, 0, f, 0, 0)))
    scratch = [pltpu.VMEM((NR, W2), F32)]
    if chained:
        scratch += [pltpu.VMEM((n_slab, 8, W2), F32), pltpu.VMEM((R, W2), F32)]
    res = pl.pallas_call(
        functools.partial(_s5_kernel, R=R, n_slab=n_slab, chained=chained, want_final=want_final),
        grid=(Q, nfs),
        in_specs=in_specs,
        out_specs=tuple(out_specs),
        out_shape=tuple(out_shape),
        scratch_shapes=scratch,
        compiler_params=_cparams(("parallel", "parallel")),
        name="s5_scan_chained" if chained else "s5_scan",
    )(*args)
    return res


def _ssd_kernel(*refs, TB, nb, has_h0, want_final):
    io = [refs[4 * d:4 * d + 4] for d in range(N_DIR)]
    refs = refs[8:]
    if has_h0:
        h0_ref = refs[0]
        refs = refs[1:]
    cw_ref, cb_ref, dtb_ref, arow_ref, sel_ref, drow_ref = refs[:6]
    refs = refs[6:]
    y_refs = refs[:2]
    refs = refs[2:]
    if want_final:
        hfin_ref = refs[0]
        refs = refs[1:]
    hst, xc_s, dt_s = refs
    CH = SSD_CHUNK
    P = SSD_HEADDIM
    j = pl.program_id(1)

    @pl.when(j == 0)
    def _():
        if has_h0:
            hst[...] = h0_ref[...]
        else:
            hst[...] = jnp.zeros_like(hst)

    li = lax.broadcasted_iota(jnp.int32, (CH, CH), 0)
    si = lax.broadcasted_iota(jnp.int32, (CH, CH), 1)
    tmats = ((si <= li).astype(F32), (si >= li).astype(F32))
    rows = lax.broadcasted_iota(jnp.int32, (TB, 1), 0)
    for d in range(N_DIR):
        xbc_ref, xp_ref, xn_ref, dt_ref = io[d]
        jj = j if d == 0 else nb - 1 - j
        x = xbc_ref[...]
        prev_row = xp_ref[7:8, :] * (jj > 0).astype(F32)
        next_row = xn_ref[0:1, :] * (jj < nb - 1).astype(F32)
        x_prev = jnp.where(rows == 0, prev_row, pltpu.roll(x, 1, 0))
        x_next = jnp.where(rows == TB - 1, next_row, pltpu.roll(x, TB - 1, 0))
        conv = cw_ref[0:1, :] * x_prev + cw_ref[1:2, :] * x + cw_ref[2:3, :] * x_next + cb_ref[...]
        xc_s[d] = conv * _sigmoid(conv)
        dtf = _softplus(dt_ref[...] + dtb_ref[...])
        dt_s[d] = _dot_exact_rhs(dtf, sel_ref[d])

    hrow = lax.broadcasted_iota(jnp.int32, (LANES, SSD_WIDTH), 0)
    e_head = jnp.where(lax.broadcasted_iota(jnp.int32, (LANES, SSD_WIDTH), 1) // P == hrow, 1.0, 0.0)
    hrow2 = lax.broadcasted_iota(jnp.int32, (LANES, SSD_HEADS * CH), 0)
    e_chunk = jnp.where(lax.broadcasted_iota(jnp.int32, (LANES, SSD_HEADS * CH), 1) // CH == hrow2, 1.0, 0.0)

    n_ch = TB // CH
    for i, d in [(i, d) for i in range(n_ch) for d in range(N_DIR)]:
        tmat = tmats[d]
        y_ref = y_refs[d]
        r0 = (i if d == 0 else n_ch - 1 - i) * CH
        dtc = dt_s[d, r0:r0 + CH, :]
        dA = dtc * arow_ref[d]
        cs = _dot_exact_lhs(tmat, dA)
        csT = cs.T
        dt_x = _dot_exact_rhs(dtc, e_head)
        cs_x = _dot_exact_rhs(cs, e_head)
        tot_x = jnp.sum(_dot_exact_rhs(dA, e_head), axis=0, keepdims=True)
        cs_xx = _dot_exact_rhs(cs, e_chunk)
        xs_all = xc_s[d, r0:r0 + CH, 0:SSD_WIDTH]
        xdt_all = xs_all * dt_x
        xd_all = xdt_all * jnp.exp(tot_x - cs_x)
        ecs_all = jnp.exp(cs_x)
        etot_all = jnp.exp(tot_x)
        Bm = xc_s[d, r0:r0 + CH, SSD_WIDTH:SSD_WIDTH + SSD_BC]
        Cm = xc_s[d, r0:r0 + CH, SSD_WIDTH + SSD_BC:SSD_WIDTH + 2 * SSD_BC]
        BmT = Bm.T
        for g in range(SSD_GROUPS):
            Cg = Cm[:, g * SSD_STATE:(g + 1) * SSD_STATE].astype(BF16)
            Bg = Bm[:, g * SSD_STATE:(g + 1) * SSD_STATE].astype(BF16)
            BgT = BmT[g * SSD_STATE:(g + 1) * SSD_STATE, :].astype(BF16)
            G = lax.dot_general(Cg, Bg, (((1,), (1,)), ((), ())), preferred_element_type=F32)
            for hh in range(SSD_HEADS // SSD_GROUPS):
                h = g * (SSD_HEADS // SSD_GROUPS) + hh
                hl = slice(h * P, (h + 1) * P)
                row = csT[h:h + 1, :]
                lm = jnp.exp(jnp.where(tmat > 0.0, cs_xx[:, h * CH:(h + 1) * CH] - row, -1e30))
                hprev = hst[d, h]
                y = jnp.dot((G * lm).astype(BF16), xdt_all[:, hl].astype(BF16), preferred_element_type=F32)
                y = y + jnp.dot(Cg, hprev.astype(BF16), preferred_element_type=F32) * ecs_all[:, hl]
                if d == 0:
                    y = y + drow_ref[:, hl] * xs_all[:, hl]
                y_ref[r0:r0 + CH, hl] = y
                hst[d, h] = etot_all[:, hl] * hprev + jnp.dot(BgT, xd_all[:, hl].astype(BF16),
                                                             preferred_element_type=F32)

    if want_final:
        @pl.when(j == nb - 1)
        def _():
            hfin_ref[...] = hst[...]


def _ssd_scan(xbc, dt, h0, conv_w, conv_b, dt_bias, a_log, d_skip, *, B, L, want_final):
    TB = min(512, L)
    nb = L // TB
    H = SSD_HEADS

    nrow8 = B * L // 8
    in_specs, args = [], []
    blks = (lambda b, j: b * nb + j, lambda b, j: b * nb + nb - 1 - j)
    for blk in blks:
        in_specs += [
            pl.BlockSpec((TB, SSD_CONV_CH), lambda b, j, blk=blk: (blk(b, j), 0)),
            pl.BlockSpec((8, SSD_CONV_CH), lambda b, j, blk=blk: (jnp.maximum(blk(b, j) * (TB // 8) - 1, 0), 0)),
            pl.BlockSpec((8, SSD_CONV_CH),
                         lambda b, j, blk=blk: (jnp.minimum((blk(b, j) + 1) * (TB // 8), nrow8 - 1), 0)),
            pl.BlockSpec((TB, DT_PAD), lambda b, j, blk=blk: (blk(b, j), 0)),
        ]
        args += [xbc, xbc, xbc, dt]
    if h0 is not None:
        in_specs.append(pl.BlockSpec((None, N_DIR, H, SSD_STATE, SSD_HEADDIM), lambda b, j: (b, 0, 0, 0, 0)))
        args.append(h0)
    dtb = jnp.pad(dt_bias.reshape(1, N_DIR * H), ((0, 0), (0, DT_PAD - N_DIR * H)))
    arow = jnp.pad(-jnp.exp(a_log), ((0, 0), (0, LANES - H))).reshape(N_DIR, 1, LANES)
    lane = jnp.arange(LANES)
    sel = jnp.stack([(lane[:, None] == (dd * H + lane[None, :])) & (lane[None, :] < H) for dd in range(N_DIR)]).astype(F32)
    drow = jnp.repeat(d_skip, SSD_HEADDIM).reshape(1, SSD_WIDTH)
    in_specs += [
        _full((3, SSD_CONV_CH)), _full((1, SSD_CONV_CH)), _full((1, DT_PAD)),
        _full((N_DIR, 1, LANES)), _full((N_DIR, LANES, LANES)), _full((1, SSD_WIDTH)),
    ]
    args += [conv_w, conv_b.reshape(1, SSD_CONV_CH), dtb, arow, sel, drow]
    out_shape = [jax.ShapeDtypeStruct((B * L, SSD_WIDTH), F32)] * N_DIR
    out_specs = [pl.BlockSpec((TB, SSD_WIDTH), lambda b, j, blk=blk: (blk(b, j), 0)) for blk in blks]
    if want_final:
        out_shape.append(jax.ShapeDtypeStruct((B, N_DIR, H, SSD_STATE, SSD_HEADDIM), F32))
        out_specs.append(pl.BlockSpec((None, N_DIR, H, SSD_STATE, SSD_HEADDIM), lambda b, j: (b, 0, 0, 0, 0)))
    return pl.pallas_call(
        functools.partial(_ssd_kernel, TB=TB, nb=nb, has_h0=h0 is not None, want_final=want_final),
        grid=(B, nb),
        in_specs=in_specs,
        out_specs=tuple(out_specs),
        out_shape=tuple(out_shape),
        scratch_shapes=[pltpu.VMEM((N_DIR, H, SSD_STATE, SSD_HEADDIM), F32),
                        pltpu.VMEM((N_DIR, TB, SSD_CONV_CH), F32), pltpu.VMEM((N_DIR, TB, DT_PAD), F32)],
        compiler_params=_cparams(("parallel", "arbitrary")),
        name="ssd_scan",
    )(*args)


def _rwkv_prep_kernel(rkv_ref, rp_ref, rn_ref, lo_ref, mu_ref, a0_ref, aup_ref, gup_ref, w0_ref, wup_ref,
                      kkw_ref, ka_ref, rk_ref, seg_ref,
                      r_ref, w0o_ref, w1o_ref, k_ref, v_ref, kk_ref, b_ref, g_ref, bonus_ref, *, nbs):
    tm = rkv_ref.shape[0]
    W = RWKV_WIDTH
    jj = pl.program_id(0) % nbs
    x = rkv_ref[...]
    prev_row = rp_ref[7:8, :] * (jj > 0).astype(F32)
    next_row = rn_ref[0:1, :] * (jj < nbs - 1).astype(F32)
    rows = lax.broadcasted_iota(jnp.int32, (tm, 1), 0)
    xp = jnp.where(rows == 0, prev_row, pltpu.roll(x, 1, 0))
    xn = jnp.where(rows == tm - 1, next_row, pltpu.roll(x, tm - 1, 0))
    x = x + mu_ref[0:1, :] * (xp - x) + mu_ref[1:2, :] * (xn - x)
    r = x[:, 0:W]
    k = x[:, W:2 * W]
    v = x[:, 2 * W:3 * W]
    lo = lo_ref[...]
    seg = seg_ref[...]

    def put(ref, val):
        ref[:, 0:W] = val
        if ref.shape[1] > W:
            ref[:, W:] = jnp.zeros((tm, ref.shape[1] - W), F32)

    a = _sigmoid(a0_ref[...] + _dot3(lo, aup_ref[...]))
    g_ref[...] = _dot3(_sigmoid(lo), gup_ref[...])
    tlo = jnp.tanh(lo)
    for d, o_ref in enumerate((w0o_ref, w1o_ref)):
        zw = w0_ref[d:d + 1, :] + _dot3(tlo, wup_ref[d])
        put(o_ref, jnp.exp(-DECAY_SCALE * _sigmoid(zw)))
    kk = k * kkw_ref[...]
    kk = kk * lax.rsqrt(jnp.maximum(_dot_exact_rhs(kk * kk, seg), 1e-24))
    k2 = k * (1.0 + (a - 1.0) * ka_ref[...])
    put(r_ref, r)
    put(k_ref, k2)
    put(v_ref, v)
    put(kk_ref, kk)
    put(b_ref, kk * a)
    bonus_ref[...] = _dot_exact_rhs(r * k2 * rk_ref[...], seg) * v


def _segment_ones(width, seg):
    i = jnp.arange(width)
    return (i[:, None] // seg == i[None, :] // seg).astype(F32)


def _rwkv_prep(rkv, lo, lp, seq_blocks, scan_width=RWKV_WIDTH):
    ntok = rkv.shape[0]
    tm = TOKEN_BLOCK
    W = RWKV_WIDTH
    nrow8 = ntok // 8
    nbs_of = seq_blocks
    widths = (scan_width,) * 7 + (W, W)
    pad_rows = lambda t, r0: jnp.zeros((LO_PAD, W), F32).at[r0:r0 + t.shape[0]].set(t)
    aup = pad_rows(lp['rwkv_a_up'], N_DIR * W_RANK)
    gup = pad_rows(lp['rwkv_g_up'], N_DIR * W_RANK + A_RANK)
    wup = jnp.stack([pad_rows(lp['rwkv_w_up'][d], d * W_RANK) for d in range(N_DIR)])
    row = lambda t: t.reshape(1, W)
    outs = tuple(jax.ShapeDtypeStruct((ntok, w), F32) for w in widths)
    return pl.pallas_call(
        functools.partial(_rwkv_prep_kernel, nbs=nbs_of),
        grid=(ntok // tm,),
        in_specs=[
            pl.BlockSpec((tm, 3 * W), lambda i: (i, 0)),
            pl.BlockSpec((8, 3 * W), lambda i: (jnp.maximum(i * (tm // 8) - 1, 0), 0)),
            pl.BlockSpec((8, 3 * W), lambda i: (jnp.minimum((i + 1) * (tm // 8), nrow8 - 1), 0)),
            pl.BlockSpec((tm, LO_PAD), lambda i: (i, 0)),
            _full((2, 3 * W)), _full((1, W)), _full((LO_PAD, W)), _full((LO_PAD, W)), _full((2, W)),
            _full((N_DIR, LO_PAD, W)), _full((1, W)), _full((1, W)), _full((1, W)), _full((W, W)),
        ],
        out_specs=tuple(pl.BlockSpec((tm, w), lambda i: (i, 0)) for w in widths),
        out_shape=outs,
        compiler_params=_cparams(("parallel",)),
        name="rwkv_prep",
    )(rkv, rkv, rkv, lo, lp['rwkv_mu'], row(lp['rwkv_a0']), aup, gup, lp['rwkv_w0'], wup,
      row(lp['rwkv_k_k']), row(lp['rwkv_k_a']), row(lp['rwkv_r_k']), lp['seg'])


def _rwkv_scan_kernel(r_ref, w_ref, k_ref, kk_ref, b_ref, v_ref, s0_ref, y_ref, sfin_ref, S, *, TBLK, V, nblk):
    K = RWKV_HEADSIZE
    tb = pl.program_id(1)

    @pl.when(tb == 0)
    def _():
        S[...] = s0_ref[...]

    def bc(ref, t, kx):
        return jnp.broadcast_to(ref[t, pl.ds(kx, 1), :], (V, LANES))

    NACC = 4

    def step(t, _):
        accs = [None] * NACC
        for kx in range(K):
            p = S[kx] * bc(kk_ref, t, kx)
            a = kx % NACC
            accs[a] = p if accs[a] is None else accs[a] + p
        sa = (accs[0] + accs[1]) + (accs[2] + accs[3])
        vt = v_ref[t]
        yacc = [None] * NACC
        for kx in range(K):
            s_new = S[kx] * bc(w_ref, t, kx) - sa * bc(b_ref, t, kx) + vt * bc(k_ref, t, kx)
            S[kx] = s_new
            p = s_new * bc(r_ref, t, kx)
            a = kx % NACC
            yacc[a] = p if yacc[a] is None else yacc[a] + p
        y_ref[t] = (yacc[0] + yacc[1]) + (yacc[2] + yacc[3])
        return 0
    lax.fori_loop(0, TBLK, step, 0)

    @pl.when(tb == nblk - 1)
    def _():
        sfin_ref[...] = S[...]


def _rwkv_scan(r, w, k, kk, b, v, s0):
    L, K, NL = r.shape
    V = v.shape[1]
    TBLK = min(64, L)
    nblk = L // TBLK
    ngrp = NL // LANES
    kspec = pl.BlockSpec((TBLK, K, LANES), lambda g, t: (t, 0, g))
    vspec = pl.BlockSpec((TBLK, V, LANES), lambda g, t: (t, 0, g))
    sspec = pl.BlockSpec((K, V, LANES), lambda g, t: (0, 0, g))
    return pl.pallas_call(
        functools.partial(_rwkv_scan_kernel, TBLK=TBLK, V=V, nblk=nblk),
        grid=(ngrp, nblk),
        in_specs=[kspec, kspec, kspec, kspec, kspec, vspec, sspec],
        out_specs=(vspec, sspec),
        out_shape=(jax.ShapeDtypeStruct((L, V, NL), F32), jax.ShapeDtypeStruct((K, V, NL), F32)),
        scratch_shapes=[pltpu.VMEM((K, V, LANES), F32)],
        compiler_params=_cparams(("parallel", "arbitrary")),
        name="rwkv_scan",
    )(r, w, k, kk, b, v, s0)


KQ = 4
CH = LANES // KQ


def _rwkv_scan2_kernel(rf, rm, wf, wm, kf, km, kkf, kkm, bf, bm, vf, vm, s0_ref, yf_ref, yb_ref, S, YQ, SA, *, TBLK):
    V = RWKV_HEADSIZE
    NQ = RWKV_HEADSIZE // KQ
    tb = pl.program_id(0)

    @pl.when(tb == 0)
    def _():
        S[...] = s0_ref[...]

    VH = V // 2

    def bc(ref, tt, q):
        return jnp.broadcast_to(ref[tt, pl.ds(q, 1), :], (VH, LANES))

    def all_quarters(x):
        return (x + pltpu.roll(x, CH, 1)) + (pltpu.roll(x, 2 * CH, 1) + pltpu.roll(x, 3 * CH, 1))

    dirs = ((rf, wf, kf, kkf, bf, vf), (rm, wm, km, kkm, bm, vm))
    time_of = (lambda t: t, lambda t: TBLK - 1 - t)
    chains = [(d, hv) for d in range(N_DIR) for hv in range(2)]

    def first_sa(d, hv):
        kk_ = dirs[d][3]
        tt = time_of[d](0)
        acc = [None, None]
        for q in range(NQ):
            p = S[d, q, pl.ds(hv * VH, VH), :] * bc(kk_, tt, q)
            acc[q % 2] = p if acc[q % 2] is None else acc[q % 2] + p
        return acc[0] + acc[1]

    TG = min(16, TBLK)
    lane_r = lax.broadcasted_iota(jnp.int32, (2 * LANES, CH), 0) % CH
    fold2 = jnp.where(lax.broadcasted_iota(jnp.int32, (2 * LANES, CH), 1) == lane_r, 1.0, 0.0).astype(BF16)

    last = len(chains) - 1
    for c, (d, hv) in enumerate(chains):
        part = first_sa(d, hv)
        SA[c] = part if c == last else all_quarters(part)

    def step(t, _):
        sa_last = all_quarters(SA[last])
        for c, (d, hv) in enumerate(chains):
            r_, w_, k_, kk_, b_, v_ = dirs[d]
            tt = time_of[d](t)
            tn = time_of[d](jnp.minimum(t + 1, TBLK - 1))
            rows = pl.ds(hv * VH, VH)
            sa = sa_last if c == last else SA[c]
            vt = v_[tt, rows, :]
            yacc = None
            sacc = None
            for q in range(NQ):
                s_new = S[d, q, rows, :] * bc(w_, tt, q) - sa * bc(b_, tt, q) + vt * bc(k_, tt, q)
                S[d, q, rows, :] = s_new
                py = s_new * bc(r_, tt, q)
                ps = s_new * bc(kk_, tn, q)
                yacc = py if yacc is None else yacc + py
                sacc = ps if sacc is None else sacc + ps
            YQ[d, tt, rows, :] = yacc
            SA[c] = sacc if c == last else all_quarters(sacc)
        return 0
    lax.fori_loop(0, TBLK, step, 0)

    def finish(i, _):
        for d, y_ in enumerate((yf_ref, yb_ref)):
            hi, lo = _split_bf16(YQ[d, pl.ds(i * TG, TG)].reshape(TG * V, LANES))
            y = jnp.dot(jnp.concatenate([hi, lo], axis=-1), fold2, preferred_element_type=F32)
            y_[pl.ds(i * TG, TG)] = y.reshape(TG, V, CH)
        return 0
    lax.fori_loop(0, TBLK // TG, finish, 0)


def _rwkv_scan2(r, w0, w1, k, kk, b, v, s0):
    L, NQ, _ = r.shape
    V = RWKV_HEADSIZE
    TBLK = min(64, L)
    nblk = L // TBLK
    fspec = pl.BlockSpec((TBLK, NQ, LANES), lambda t: (t, 0, 0))
    mspec = pl.BlockSpec((TBLK, NQ, LANES), lambda t: (nblk - 1 - t, 0, 0))
    vfspec = pl.BlockSpec((TBLK, V, LANES), lambda t: (t, 0, 0))
    vmspec = pl.BlockSpec((TBLK, V, LANES), lambda t: (nblk - 1 - t, 0, 0))
    yfspec = pl.BlockSpec((TBLK, V, CH), lambda t: (t, 0, 0))
    ymspec = pl.BlockSpec((TBLK, V, CH), lambda t: (nblk - 1 - t, 0, 0))
    yf, yb = pl.pallas_call(
        functools.partial(_rwkv_scan2_kernel, TBLK=TBLK),
        grid=(nblk,),
        in_specs=[fspec, mspec] * 5 + [vfspec, vmspec, _full((N_DIR, NQ, V, LANES))],
        out_specs=(yfspec, ymspec),
        out_shape=(jax.ShapeDtypeStruct((L, V, CH), F32), jax.ShapeDtypeStruct((L, V, CH), F32)),
        scratch_shapes=[pltpu.VMEM((N_DIR, NQ, V, LANES), F32), pltpu.VMEM((N_DIR, TBLK, V, LANES), F32),
                        pltpu.VMEM((2 * N_DIR, V // 2, LANES), F32)],
        compiler_params=_cparams(("arbitrary",)),
        name="rwkv_scan2",
    )(r, r, w0, w1, k, k, kk, kk, b, b, v, v, s0)
    return yf, yb


HEADS_PAD = 8
SCAN_WIDTH = HEADS_PAD * RWKV_HEADSIZE


def _pad_chains(x, B):
    return x if B * HEADS_PAD == CH else jnp.pad(x, ((0, 0),) * (x.ndim - 1) + ((0, CH - B * HEADS_PAD),))


def _to_scan2_k(t, B, L):
    x = t.reshape(B, L, HEADS_PAD, RWKV_HEADSIZE // KQ, KQ).transpose(1, 3, 4, 0, 2)
    x = _pad_chains(x.reshape(L, RWKV_HEADSIZE // KQ, KQ, B * HEADS_PAD), B)
    return x.reshape(L, RWKV_HEADSIZE // KQ, LANES)


def _to_scan2_v(t, B, L):
    x = t.reshape(B, L, HEADS_PAD, RWKV_HEADSIZE).transpose(1, 3, 0, 2).reshape(L, RWKV_HEADSIZE, B * HEADS_PAD)
    return jnp.tile(_pad_chains(x, B), (1, 1, KQ))


def _from_scan2(y, B, L):
    y = y[:, :, :B * HEADS_PAD]
    return y.reshape(L, RWKV_HEADSIZE, B, HEADS_PAD).transpose(2, 0, 3, 1).reshape(B * L, SCAN_WIDTH)


def _state_to_scan2(s):
    B = s.shape[0]
    s = jnp.pad(s, ((0, 0), (0, 0), (0, HEADS_PAD - RWKV_HEADS), (0, 0), (0, 0)))
    x = s.transpose(1, 4, 3, 0, 2).reshape(N_DIR, RWKV_HEADSIZE // KQ, KQ, RWKV_HEADSIZE, B * HEADS_PAD)
    x = _pad_chains(x, B)
    return x.transpose(0, 1, 3, 2, 4).reshape(N_DIR, RWKV_HEADSIZE // KQ, RWKV_HEADSIZE, LANES)


def _rwkv_layout(B, vsplit):
    chains = N_DIR * B * RWKV_HEADS
    cp = -(-chains // (LANES // vsplit)) * (LANES // vsplit)
    return chains, cp


def _to_scan_k(t_fwd, t_bwd, B, L, vsplit):
    chains, cp = _rwkv_layout(B, vsplit)

    def one(t):
        return t.reshape(B, L, RWKV_HEADS, RWKV_HEADSIZE).transpose(1, 3, 0, 2).reshape(L, RWKV_HEADSIZE, B * RWKV_HEADS)
    x = jnp.concatenate([one(t_fwd), jnp.flip(one(t_bwd), 0)], axis=-1)
    x = jnp.pad(x, ((0, 0), (0, 0), (0, cp - chains)))
    return jnp.tile(x, (1, 1, vsplit))


def _to_scan_v(t, B, L, vsplit):
    chains, cp = _rwkv_layout(B, vsplit)
    x = t.reshape(B, L, RWKV_HEADS, RWKV_HEADSIZE).transpose(1, 3, 0, 2).reshape(L, RWKV_HEADSIZE, B * RWKV_HEADS)
    x = jnp.concatenate([x, jnp.flip(x, 0)], axis=-1)
    x = jnp.pad(x, ((0, 0), (0, 0), (0, cp - chains)))
    vs = RWKV_HEADSIZE // vsplit
    return x.reshape(L, vsplit, vs, cp).transpose(0, 2, 1, 3).reshape(L, vs, vsplit * cp)


def _from_scan_v(y, B, L, vsplit):
    chains, cp = _rwkv_layout(B, vsplit)
    vs = RWKV_HEADSIZE // vsplit
    y = y.reshape(L, vs, vsplit, cp).transpose(0, 2, 1, 3).reshape(L, RWKV_HEADSIZE, cp)[:, :, :chains]
    half = chains // 2

    def back(t):
        return t.reshape(L, RWKV_HEADSIZE, B, RWKV_HEADS).transpose(2, 0, 3, 1).reshape(B * L, RWKV_WIDTH)
    return back(y[:, :, :half]), back(jnp.flip(y[:, :, half:], 0))


def _state_to_scan(s, vsplit):
    B = s.shape[0]
    chains, cp = _rwkv_layout(B, vsplit)
    x = s.transpose(4, 3, 1, 0, 2).reshape(RWKV_HEADSIZE, RWKV_HEADSIZE, chains)
    x = jnp.pad(x, ((0, 0), (0, 0), (0, cp - chains)))
    vs = RWKV_HEADSIZE // vsplit
    return x.reshape(RWKV_HEADSIZE, vsplit, vs, cp).transpose(0, 2, 1, 3).reshape(RWKV_HEADSIZE, vs, vsplit * cp)


def _state_from_scan(x, B, vsplit):
    chains, cp = _rwkv_layout(B, vsplit)
    vs = RWKV_HEADSIZE // vsplit
    x = x.reshape(RWKV_HEADSIZE, vs, vsplit, cp).transpose(0, 2, 1, 3).reshape(RWKV_HEADSIZE, RWKV_HEADSIZE, cp)
    x = x[:, :, :chains].reshape(RWKV_HEADSIZE, RWKV_HEADSIZE, N_DIR, B, RWKV_HEADS)
    return x.transpose(3, 2, 4, 1, 0)


def _post_kernel(x_ref, mod_ref, ys5_ref, yf_ref, yb_ref, z_ref, rf_ref, rb_ref, bonus_ref, g_ref,
                 gluw_ref, glub_ref, ssdg_ref, lng_ref, lnb_ref, seg_ref, wout_ref, n2g_ref, rw_ref,
                 x1_ref, hb_ref, aff_ref):
    m = mod_ref[...]
    D = D_MODEL
    zg = _gelu_tanh(ys5_ref[...])
    gate = jnp.dot(zg.astype(BF16), gluw_ref[...], preferred_element_type=F32) + glub_ref[...]
    y_a = zg * _sigmoid(gate)
    z = z_ref[...]
    yb = (yf_ref[...] + yb_ref[...]) * (z * _sigmoid(z))
    y_b = yb * lax.rsqrt(jnp.mean(yb * yb, axis=-1, keepdims=True) + EPS) * ssdg_ref[...]
    seg = seg_ref[...] * (1.0 / RWKV_HEADSIZE)
    yr = rf_ref[...] + rb_ref[...]
    mean = _dot_exact_rhs(yr, seg)
    cen = yr - mean
    var = jnp.dot((cen * cen).astype(BF16), seg.astype(BF16), preferred_element_type=F32)
    yn = cen * lax.rsqrt(var + GN_EPS) * lng_ref[...] + lnb_ref[...]
    y_c = (yn + bonus_ref[...]) * g_ref[...]
    o = jnp.dot(y_a.astype(BF16), wout_ref[0:S5_WIDTH, :], preferred_element_type=F32)
    o = o + jnp.dot(y_b.astype(BF16), wout_ref[S5_WIDTH:S5_WIDTH + SSD_WIDTH, :], preferred_element_type=F32)
    o = o + jnp.dot(y_c.astype(BF16), wout_ref[S5_WIDTH + SSD_WIDTH:, :], preferred_element_type=F32)
    x1 = x_ref[...] + m[:, 2 * D:3 * D] * o
    x1_ref[...] = x1
    h2 = x1 * lax.rsqrt(jnp.mean(x1 * x1, axis=-1, keepdims=True) + EPS) * n2g_ref[...]
    h2 = h2 * (1.0 + m[:, 4 * D:5 * D]) + m[:, 3 * D:4 * D]
    hb_ref[...] = h2.astype(BF16)
    logits = _dot3(rw_ref[...], h2, (((1,), (1,)), ((), ())))
    mx = jnp.max(logits, axis=0, keepdims=True)
    ex = jnp.exp(logits - mx)
    aff_ref[...] = ex / jnp.sum(ex, axis=0, keepdims=True)


def _post_mixer(x, mod3, row_of_block, ys5, yssd, z, rf, rb, bonus, g, lp):
    ntok = x.shape[0]
    tm = TOKEN_BLOCK
    tok = lambda w: pl.BlockSpec((tm, w), lambda i: (i, 0))
    row = lambda t: t.reshape(1, -1)
    W = RWKV_WIDTH
    return pl.pallas_call(
        _post_kernel,
        grid=(ntok // tm,),
        in_specs=[
            tok(D_MODEL),
            pl.BlockSpec((None, 1, 6 * D_MODEL), lambda i: (row_of_block(i), 0, 0)),
            tok(S5_WIDTH),
            tok(SSD_WIDTH), tok(SSD_WIDTH),
            tok(SSD_WIDTH), tok(W), tok(W), tok(W), tok(W),
            _full((S5_WIDTH, S5_WIDTH)), _full((1, S5_WIDTH)), _full((1, SSD_WIDTH)), _full((1, W)), _full((1, W)),
            _full((W, W)), _full((D_MODEL, D_MODEL)), _full((1, D_MODEL)), _full((N_EXPERTS, D_MODEL)),
        ],
        out_specs=(tok(D_MODEL), tok(D_MODEL), pl.BlockSpec((N_EXPERTS, tm), lambda i: (0, i))),
        out_shape=(jax.ShapeDtypeStruct((ntok, D_MODEL), F32), jax.ShapeDtypeStruct((ntok, D_MODEL), BF16),
                   jax.ShapeDtypeStruct((N_EXPERTS, ntok), F32)),
        compiler_params=_cparams(("parallel",)),
        name="post_mixer",
    )(x, mod3, ys5, yssd[0], yssd[1], z, rf, rb, bonus, g,
      lp['s5_glu_w'], row(lp['s5_glu_b']), row(lp['ssd_norm_g']), row(lp['rwkv_ln_g']),
      row(lp['rwkv_ln_b']), lp['seg'], lp['w_out'], row(lp['norm2_g']),
      lp['router_w'].T)


def _select_kernel(aff_ref, slot_ref, affo_ref, *, n, cap):
    E = N_EXPERTS
    a = aff_ref[...]
    bits = pltpu.bitcast(a, jnp.int32)
    thr = jnp.zeros((E, 1), jnp.int32)
    capf = float(cap)
    for bit in range(30, -1, -1):
        cand = thr | (1 << bit)
        cnt = jnp.sum(jnp.where(bits >= cand, 1.0, 0.0), axis=1, keepdims=True)
        thr = jnp.where(cnt >= capf, cand, thr)
    gt = bits > thr
    eq = bits == thr
    need = capf - jnp.sum(jnp.where(gt, 1.0, 0.0), axis=1, keepdims=True)
    CW = min(256, n)
    ui = lax.broadcasted_iota(jnp.int32, (CW, CW), 0)
    uj = lax.broadcasted_iota(jnp.int32, (CW, CW), 1)
    upper = jnp.where(ui < uj, 1.0, 0.0).astype(BF16)

    def excl_cumsum(mask_f):
        outs = []
        off = jnp.zeros((E, 1), F32)
        for c in range(n // CW):
            mc = mask_f[:, c * CW:(c + 1) * CW]
            outs.append(jnp.dot(mc.astype(BF16), upper, preferred_element_type=F32) + off)
            off = off + jnp.sum(mc, axis=1, keepdims=True)
        return jnp.concatenate(outs, axis=1)

    eq_rank = excl_cumsum(jnp.where(eq, 1.0, 0.0))
    sel = jnp.where(gt, 1.0, jnp.where(eq, jnp.where(eq_rank < need, 1.0, 0.0), 0.0))
    pos = excl_cumsum(sel)
    slot = jnp.where(sel > 0.0, pos, -1.0)
    for e in range(E):
        slot_ref[e] = slot[e:e + 1, :]
        affo_ref[e] = a[e:e + 1, :]


def _select(aff, B, n, cap):
    spec = pl.BlockSpec((None, N_EXPERTS, 1, n), lambda b: (b, 0, 0, 0))
    return pl.pallas_call(
        functools.partial(_select_kernel, n=n, cap=cap),
        grid=(B,),
        in_specs=[pl.BlockSpec((N_EXPERTS, n), lambda b: (0, b))],
        out_specs=(spec, spec),
        out_shape=(jax.ShapeDtypeStruct((B, N_EXPERTS, 1, n), F32), jax.ShapeDtypeStruct((B, N_EXPERTS, 1, n), F32)),
        compiler_params=_cparams(("parallel",)),
        name="ec_select",
    )(aff)


def _slot_block_range(slots_f, cap, SB):
    lo = jnp.min(jnp.where(slots_f >= 0.0, slots_f, float(cap))).astype(jnp.int32)
    hi = jnp.max(slots_f).astype(jnp.int32)
    first = lo // SB
    count = jnp.where(hi >= 0, hi // SB - first + 1, 0)
    return first, count


EC_SLOT_BLOCK = 256


def _one_hot_all(slot_ref, aff_ref, n, cap):
    srow = lax.broadcasted_iota(jnp.int32, (cap, n), 0).astype(F32)
    ohs, gates = [], []
    for e in range(N_EXPERTS):
        hit = slot_ref[e] == srow
        ohs.append(jnp.where(hit, 1.0, 0.0).astype(BF16))
        if aff_ref is not None:
            gates.append(jnp.sum(jnp.where(hit, aff_ref[e], 0.0), axis=1, keepdims=True))
    return jnp.concatenate(ohs, axis=0), (jnp.concatenate(gates, axis=0) if gates else None)


def _gather_all_kernel(hb_ref, slot_ref, aff_ref, xs_ref, gate_ref, *, n, cap):
    oh, gate = _one_hot_all(slot_ref, aff_ref, n, cap)
    xs = jnp.dot(oh, hb_ref[...], preferred_element_type=F32)
    xs_ref[...] = xs.astype(BF16).reshape(N_EXPERTS, cap, D_MODEL)
    gate_ref[...] = gate.reshape(N_EXPERTS, cap, 1)


def _gather_kernel(hb_ref, slot_ref, aff_ref, xs_ref, gate_ref, acc, gacc, *, n, cap):
    NC = min(512, n)
    SB = min(EC_SLOT_BLOCK // 2, cap)
    acc[...] = jnp.zeros_like(acc)
    gacc[...] = jnp.zeros_like(gacc)
    srow = lax.broadcasted_iota(jnp.int32, (SB, NC), 0).astype(F32)
    for c in range(n // NC):
        sl = slot_ref[:, c * NC:(c + 1) * NC]
        first, count = _slot_block_range(sl, cap, SB)
        for j in range(min(cap // SB, NC // SB + 1)):
            @pl.when(j < count)
            def _(c=c, j=j, sl=sl, first=first):
                base = pl.multiple_of((first + j) * SB, SB)
                hit = (sl - base.astype(F32)) == srow
                oh = jnp.where(hit, 1.0, 0.0).astype(BF16)
                acc[pl.ds(base, SB), :] += jnp.dot(oh, hb_ref[c * NC:(c + 1) * NC, :], preferred_element_type=F32)
                gacc[pl.ds(base, SB), :] += jnp.sum(jnp.where(hit, aff_ref[:, c * NC:(c + 1) * NC], 0.0), axis=1,
                                                    keepdims=True)
    xs_ref[...] = acc[...].astype(BF16)
    gate_ref[...] = gacc[...]


def _gather(hb, slot4, aff4, B, n, cap):
    E = N_EXPERTS
    if E * cap <= 512:
        all_spec = pl.BlockSpec((None, E, 1, n), lambda b: (b, 0, 0, 0))
        return pl.pallas_call(
            functools.partial(_gather_all_kernel, n=n, cap=cap),
            grid=(B,),
            in_specs=[pl.BlockSpec((n, D_MODEL), lambda b: (b, 0)), all_spec, all_spec],
            out_specs=(pl.BlockSpec((None, E, cap, D_MODEL), lambda b: (b, 0, 0, 0)),
                       pl.BlockSpec((None, E, cap, 1), lambda b: (b, 0, 0, 0))),
            out_shape=(jax.ShapeDtypeStruct((B, E, cap, D_MODEL), BF16), jax.ShapeDtypeStruct((B, E, cap, 1), F32)),
            compiler_params=_cparams(("parallel",)),
            name="ec_gather_all",
        )(hb, slot4, aff4)
    return pl.pallas_call(
        functools.partial(_gather_kernel, n=n, cap=cap),
        grid=(B, E),
        in_specs=[
            pl.BlockSpec((n, D_MODEL), lambda b, e: (b, 0)),
            pl.BlockSpec((None, None, 1, n), lambda b, e: (b, e, 0, 0)),
            pl.BlockSpec((None, None, 1, n), lambda b, e: (b, e, 0, 0)),
        ],
        out_specs=(pl.BlockSpec((None, None, cap, D_MODEL), lambda b, e: (b, e, 0, 0)),
                   pl.BlockSpec((None, None, cap, 1), lambda b, e: (b, e, 0, 0))),
        out_shape=(jax.ShapeDtypeStruct((B, E, cap, D_MODEL), BF16), jax.ShapeDtypeStruct((B, E, cap, 1), F32)),
        scratch_shapes=[pltpu.VMEM((cap, D_MODEL), F32), pltpu.VMEM((cap, 1), F32)],
        compiler_params=_cparams(("parallel", "arbitrary")),
        name="ec_gather",
    )(hb, slot4, aff4)


def _ffn_kernel(xs_ref, gate_ref, w1_ref, w3_ref, w2_ref, o_ref):
    bg, cap, _ = xs_ref.shape
    x = xs_ref[...].reshape(bg * cap, D_MODEL)
    h1 = jnp.dot(x, w1_ref[...], preferred_element_type=F32)
    h3 = jnp.dot(x, w3_ref[...], preferred_element_type=F32)
    hid = (h1 * _sigmoid(h1) * h3).astype(BF16)
    o = jnp.dot(hid, w2_ref[...], preferred_element_type=F32) * gate_ref[...].reshape(bg * cap, 1)
    o_ref[...] = o.astype(BF16).reshape(bg, cap, D_MODEL)


def _expert_ffn(xs, gate, w1, w3, w2, bg):
    B, E, cap, _ = xs.shape
    return pl.pallas_call(
        _ffn_kernel,
        grid=(E, B // bg),
        in_specs=[
            pl.BlockSpec((bg, None, cap, D_MODEL), lambda e, b: (b, e, 0, 0)),
            pl.BlockSpec((bg, None, cap, 1), lambda e, b: (b, e, 0, 0)),
            pl.BlockSpec((None, D_MODEL, D_EXPERT), lambda e, b: (e, 0, 0)),
            pl.BlockSpec((None, D_MODEL, D_EXPERT), lambda e, b: (e, 0, 0)),
            pl.BlockSpec((None, D_EXPERT, D_MODEL), lambda e, b: (e, 0, 0)),
        ],
        out_specs=pl.BlockSpec((bg, None, cap, D_MODEL), lambda e, b: (b, e, 0, 0)),
        out_shape=jax.ShapeDtypeStruct((B, E, cap, D_MODEL), BF16),
        compiler_params=_cparams(("parallel", "parallel")),
        name="ec_ffn",
    )(xs, gate, w1, w3, w2)


_TN_DIMS = (((0,), (0,)), ((), ()))


def _ec_residual(x1_ref, mod_ref, fg_ref, out_ref, ffn, final):
    x2 = x1_ref[...] + mod_ref[:, 5 * D_MODEL:6 * D_MODEL] * ffn
    if final:
        x2 = x2 * lax.rsqrt(jnp.mean(x2 * x2, axis=-1, keepdims=True) + EPS) * fg_ref[...]
    out_ref[...] = x2


def _scatter_all_kernel(slot_ref, o_ref, x1_ref, mod_ref, fg_ref, out_ref, *, n, cap, final):
    oh, _ = _one_hot_all(slot_ref, None, n, cap)
    ffn = lax.dot_general(oh, o_ref[...].reshape(N_EXPERTS * cap, D_MODEL), _TN_DIMS, preferred_element_type=F32)
    _ec_residual(x1_ref, mod_ref, fg_ref, out_ref, ffn, final)


def _scatter_kernel(slot_ref, o_ref, x1_ref, mod_ref, fg_ref, out_ref, acc, *, cap, final):
    e = pl.program_id(2)
    tn = x1_ref.shape[0]

    @pl.when(e == 0)
    def _():
        acc[...] = jnp.zeros_like(acc)

    sl = slot_ref[...]
    SB = min(EC_SLOT_BLOCK, cap)
    srow = lax.broadcasted_iota(jnp.int32, (SB, tn), 0).astype(F32)
    first, count = _slot_block_range(sl, cap, SB)
    for j in range(min(cap // SB, tn // SB + 1)):
        @pl.when(j < count)
        def _(j=j):
            base = pl.multiple_of((first + j) * SB, SB)
            oh = jnp.where((sl - base.astype(F32)) == srow, 1.0, 0.0).astype(BF16)
            acc[...] += lax.dot_general(oh, o_ref[pl.ds(base, SB), :], _TN_DIMS, preferred_element_type=F32)

    @pl.when(e == N_EXPERTS - 1)
    def _():
        _ec_residual(x1_ref, mod_ref, fg_ref, out_ref, acc[...], final)


def _scatter(slot4, o, x1, mod3, mod_row, final_g, B, n, cap, final):
    E = N_EXPERTS
    if E * cap <= 512:
        return pl.pallas_call(
            functools.partial(_scatter_all_kernel, n=n, cap=cap, final=final),
            grid=(B,),
            in_specs=[
                pl.BlockSpec((None, E, 1, n), lambda b: (b, 0, 0, 0)),
                pl.BlockSpec((None, E, cap, D_MODEL), lambda b: (b, 0, 0, 0)),
                pl.BlockSpec((n, D_MODEL), lambda b: (b, 0)),
                pl.BlockSpec((None, 1, 6 * D_MODEL), lambda b: (mod_row(b), 0, 0)),
                _full((1, D_MODEL)),
            ],
            out_specs=pl.BlockSpec((n, D_MODEL), lambda b: (b, 0)),
            out_shape=jax.ShapeDtypeStruct((B * n, D_MODEL), F32),
            compiler_params=_cparams(("parallel",)),
            name="ec_scatter_all",
        )(slot4, o, x1, mod3, final_g.reshape(1, D_MODEL))
    tn = min(1024, n)
    nt = n // tn
    return pl.pallas_call(
        functools.partial(_scatter_kernel, cap=cap, final=final),
        grid=(B, nt, N_EXPERTS),
        in_specs=[
            pl.BlockSpec((None, None, 1, tn), lambda b, t, e: (b, e, 0, t)),
            pl.BlockSpec((None, None, cap, D_MODEL), lambda b, t, e: (b, e, 0, 0)),
            pl.BlockSpec((tn, D_MODEL), lambda b, t, e: (b * nt + t, 0)),
            pl.BlockSpec((None, 1, 6 * D_MODEL), lambda b, t, e: (mod_row(b), 0, 0)),
            _full((1, D_MODEL)),
        ],
        out_specs=pl.BlockSpec((tn, D_MODEL), lambda b, t, e: (b * nt + t, 0)),
        out_shape=jax.ShapeDtypeStruct((B * n, D_MODEL), F32),
        scratch_shapes=[pltpu.VMEM((tn, D_MODEL), F32)],
        compiler_params=_cparams(("parallel", "parallel", "arbitrary")),
        name="ec_scatter",
    )(slot4, o, x1, mod3, final_g.reshape(1, D_MODEL))


def _expert_choice(hb, aff, x1, mod3, mod_row, final_g, w1, w3, w2, B, n, final):
    cap = EC_FACTOR * n // N_EXPERTS
    slot4, aff4 = _select(aff, B, n, cap)
    xs, gate = _gather(hb, slot4, aff4, B, n, cap)
    bg = max(1, min(B, 256 // cap))
    o = _expert_ffn(xs, gate, w1, w3, w2, bg)
    return _scatter(slot4, o, x1, mod3, mod_row, final_g, B, n, cap, final)


def kernel(x_prompt, x_sample, c, state_s5_re, state_s5_im, state_ssd, state_rwkv, c_ctx, ada_w, ada_b, norm1_g, norm2_g, w_in, w_out, s5_a_re, s5_a_im, s5_log_dt, s5_b_re, s5_b_im, s5_c_re, s5_c_im, s5_d, s5_glu_w, s5_glu_b, ssd_conv_w, ssd_conv_b, ssd_a_log, ssd_dt_bias, ssd_d, ssd_norm_g, rwkv_mu, rwkv_w0, rwkv_w_up, rwkv_a0, rwkv_a_up, rwkv_g_up, rwkv_k_k, rwkv_k_a, rwkv_r_k, rwkv_ln_g, rwkv_ln_b, router_w, exp_w1, exp_w3, exp_w2, final_g):
    Bp, Lp, D = x_prompt.shape
    Bs, Ls, _ = x_sample.shape
    depth = ada_w.shape[0]
    Np, Ns = Bp * Lp, Bs * Ls
    tm = TOKEN_BLOCK
    grid_rows = Ls // GRID_W
    nfs = S5_GROUPS // 8
    assert Lp % tm == 0 and Ls % tm == 0 and Lp % SSD_CHUNK == 0

    n_rows = 1 + Bs
    rows_pad = -(-n_rows // 8) * 8
    cond = jnp.zeros((rows_pad, D), F32).at[0].set(c_ctx).at[1:n_rows].set(c)
    mod = _modulation(cond, ada_w, ada_b)
    s_blocks_per_req = Ls // tm
    row_p = lambda i: 0
    row_s = lambda i: 1 + i // s_blocks_per_req

    ab_re, ab_im, bb_re, bb_im = _s5_discretize(s5_a_re, s5_a_im, s5_log_dt, s5_b_re, s5_b_im)
    w_in_pad = _pad_in_weight(w_in)
    w_out_b = w_out.astype(BF16)
    glu_w_b = s5_glu_w.astype(BF16)
    exp_w1_b, exp_w3_b, exp_w2_b = exp_w1.astype(BF16), exp_w3.astype(BF16), exp_w2.astype(BF16)
    seg = _segment_ones(RWKV_WIDTH, RWKV_HEADSIZE)

    xp = x_prompt.reshape(Np, D)
    xs = x_sample.reshape(Ns, D)
    new_s5_re, new_s5_im, new_ssd, new_rwkv = [], [], [], []
    QP = 2 if Bp % 2 == 0 and Bp >= 2 else 1
    RP = Bp // QP
    assert Bs * HEADS_PAD <= CH

    for l in range(depth):
        lp = {
            'rwkv_mu': rwkv_mu[l], 'rwkv_w0': rwkv_w0[l], 'rwkv_w_up': rwkv_w_up[l], 'rwkv_a0': rwkv_a0[l],
            'rwkv_a_up': rwkv_a_up[l], 'rwkv_g_up': rwkv_g_up[l], 'rwkv_k_k': rwkv_k_k[l], 'rwkv_k_a': rwkv_k_a[l],
            'rwkv_r_k': rwkv_r_k[l].reshape(-1), 'rwkv_ln_g': rwkv_ln_g[l], 'rwkv_ln_b': rwkv_ln_b[l],
            's5_glu_w': glu_w_b[l], 's5_glu_b': s5_glu_b[l], 'ssd_norm_g': ssd_norm_g[l], 'w_out': w_out_b[l],
            'norm2_g': norm2_g[l], 'router_w': router_w[l], 'seg': seg,
        }
        mod3 = mod[l].reshape(rows_pad, 1, 6 * D)
        us5_p, z_p, xbc_p, dt_p, rkv_p, lo_p = _in_projection(xp, mod3, row_p, norm1_g[l], w_in_pad[l])
        us5_s, z_s, xbc_s, dt_s, rkv_s, lo_s = _in_projection(xs, mod3, row_s, norm1_g[l], w_in_pad[l])

        tables = _s5_layer_tables(ab_re[l], ab_im[l], bb_re[l], bb_im[l], s5_c_re[l], s5_c_im[l])
        d_row = s5_d[l].reshape(1, S5_WIDTH)
        up = us5_p.reshape(QP, RP, Lp, S5_WIDTH).transpose(0, 2, 1, 3).reshape(QP, Lp * RP, S5_WIDTH)
        yp, hfin = _s5_scan(up, None, tables, d_row, R=RP, n_slab=Lp, chained=False, want_final=True)
        ys5_p = yp.reshape(QP, Lp, RP, S5_WIDTH).transpose(0, 2, 1, 3).reshape(Np, S5_WIDTH)
        hf = hfin.transpose(0, 3, 1, 2, 4).reshape(Bp, N_DIR, nfs, 2, 8, S5_STATE)
        new_s5_re.append(hf[:, :, :, 0].reshape(Bp, N_DIR, S5_GROUPS, S5_STATE))
        new_s5_im.append(hf[:, :, :, 1].reshape(Bp, N_DIR, S5_GROUPS, S5_STATE))
        h0 = jnp.concatenate([state_s5_re[:, l].reshape(Bs, N_DIR, nfs, 1, S5_SLICE_ST),
                              state_s5_im[:, l].reshape(Bs, N_DIR, nfs, 1, S5_SLICE_ST)], axis=-1)
        (ysm,) = _s5_scan(us5_s.reshape(Bs, Ls, S5_WIDTH), h0, tables, d_row, R=GRID_W, n_slab=grid_rows,
                          chained=True, want_final=False)
        ys5_s = ysm.reshape(Ns, S5_WIDTH)

        ssd_args = (ssd_conv_w[l], ssd_conv_b[l], ssd_dt_bias[l], ssd_a_log[l], ssd_d[l])
        *yssd_p, hssd = _ssd_scan(xbc_p, dt_p, None, *ssd_args, B=Bp, L=Lp, want_final=True)
        new_ssd.append(hssd.transpose(0, 1, 2, 4, 3))
        yssd_s = _ssd_scan(xbc_s, dt_s, state_ssd[:, l].transpose(0, 1, 2, 4, 3), *ssd_args,
                           B=Bs, L=Ls, want_final=False)

        r_, w0_, w1_, k_, v_, kk_, b_, g_p, bonus_p = _rwkv_prep(rkv_p, lo_p, lp, Lp // tm)
        zero_state = jnp.zeros((Bp, N_DIR, RWKV_HEADS, RWKV_HEADSIZE, RWKV_HEADSIZE), F32)
        y_, sfin = _rwkv_scan(_to_scan_k(r_, r_, Bp, Lp, 1), _to_scan_k(w0_, w1_, Bp, Lp, 1),
                              _to_scan_k(k_, k_, Bp, Lp, 1), _to_scan_k(kk_, kk_, Bp, Lp, 1),
                              _to_scan_k(b_, b_, Bp, Lp, 1), _to_scan_v(v_, Bp, Lp, 1),
                              _state_to_scan(zero_state, 1))
        rf_p, rb_p = _from_scan_v(y_, Bp, Lp, 1)
        new_rwkv.append(_state_from_scan(sfin, Bp, 1))

        r_, w0_, w1_, k_, v_, kk_, b_, g_s, bonus_s = _rwkv_prep(rkv_s, lo_s, lp, Ls // tm, SCAN_WIDTH)
        sc = lambda t: _to_scan2_k(t, Bs, Ls)
        yf_, yb_ = _rwkv_scan2(sc(r_), sc(w0_), sc(w1_), sc(k_), sc(kk_), sc(b_), _to_scan2_v(v_, Bs, Ls),
                               _state_to_scan2(state_rwkv[:, l]))
        rf_s = _from_scan2(yf_, Bs, Ls)
        rb_s = _from_scan2(yb_, Bs, Ls)

        x1_p, hb_p, aff_p = _post_mixer(xp, mod3, row_p, ys5_p, yssd_p, z_p, rf_p, rb_p, bonus_p, g_p, lp)
        x1_s, hb_s, aff_s = _post_mixer(xs, mod3, row_s, ys5_s, yssd_s, z_s, rf_s, rb_s, bonus_s, g_s, lp)

        final = l == depth - 1
        ew = (exp_w1_b[l], exp_w3_b[l], exp_w2_b[l])
        xp = _expert_choice(hb_p, aff_p, x1_p, mod3, lambda b: 0, final_g, *ew, Bp, Lp, final)
        xs = _expert_choice(hb_s, aff_s, x1_s, mod3, lambda b: 1 + b, final_g, *ew, Bs, Ls, final)

    y_prompt = xp.reshape(Bp, Lp, D)
    y_sample = xs.reshape(Bs, Ls, D)
    return (y_prompt, y_sample, jnp.stack(new_s5_re, axis=1), jnp.stack(new_s5_im, axis=1),
            jnp.stack(new_ssd, axis=1), jnp.stack(new_rwkv, axis=1))
```

```python
import functools
import math

import jax
import jax.numpy as jnp
from jax import lax
from jax.experimental import pallas as pl
from jax.experimental.pallas import tpu as pltpu

F32 = jnp.float32
BF16 = jnp.bfloat16
HIGHEST = lax.Precision.HIGHEST

D_MODEL = 1024
GRID_W = 64
N_DIR = 2
EPS = 1e-6
S5_WIDTH = 256
S5_CH = 16
S5_GROUPS = 16
S5_STATE = 64
SSD_HEADDIM = 64
SSD_HEADS = 6
SSD_WIDTH = 384
SSD_GROUPS = 2
SSD_STATE = 64
SSD_BC = 128
SSD_CONV_CH = 640
SSD_CHUNK = 128
RWKV_HEADSIZE = 64
RWKV_HEADS = 6
RWKV_WIDTH = 384
W_RANK = 32
A_RANK = 32
G_RANK = 64
DECAY_SCALE = math.exp(-0.5)
GN_EPS = 64e-5
N_EXPERTS = 16
D_EXPERT = 512
EC_FACTOR = 2
IN_SIZES = (S5_WIDTH, SSD_WIDTH, SSD_CONV_CH, N_DIR * SSD_HEADS, 3 * RWKV_WIDTH, N_DIR * W_RANK, A_RANK, G_RANK)

LANES = 128
TOKEN_BLOCK = 256
DT_PAD = 128
LO_PAD = 256
S5_SLICE_CH = 128
S5_SLICE_ST = 512
VMEM_LIMIT = 56 * 1024 * 1024


def _cparams(sem):
    return pltpu.CompilerParams(dimension_semantics=sem, vmem_limit_bytes=VMEM_LIMIT)


def _full(shape):
    nd = len(shape)
    return pl.BlockSpec(shape, lambda *_: (0,) * nd)


def _sigmoid(x):
    return 1.0 / (1.0 + jnp.exp(-x))


def _softplus(x):
    return jnp.maximum(x, 0.0) + jnp.log1p(jnp.exp(-jnp.abs(x)))


def _split_bf16(x):
    hi = x.astype(BF16)
    lo = (x - hi.astype(F32)).astype(BF16)
    return hi, lo


def _dot_exact_rhs(x, m, dims=None):
    hi, lo = _split_bf16(x)
    mb = m.astype(BF16)
    if dims is None:
        return jnp.dot(hi, mb, preferred_element_type=F32) + jnp.dot(lo, mb, preferred_element_type=F32)
    return (lax.dot_general(hi, mb, dims, preferred_element_type=F32)
            + lax.dot_general(lo, mb, dims, preferred_element_type=F32))


def _dot_exact_lhs(m, x):
    hi, lo = _split_bf16(x)
    mb = m.astype(BF16)
    return jnp.dot(mb, hi, preferred_element_type=F32) + jnp.dot(mb, lo, preferred_element_type=F32)


def _dot3(x, w, dims=(((1,), (0,)), ((), ()))):
    xh, xl = _split_bf16(x)
    wh, wl = _split_bf16(w)
    dg = functools.partial(lax.dot_general, dimension_numbers=dims, preferred_element_type=F32)
    return dg(xh, wh) + (dg(xl, wh) + dg(xh, wl))


def _gelu_tanh(x):
    return 0.5 * x * (1.0 + jnp.tanh(math.sqrt(2.0 / math.pi) * (x + 0.044715 * (x * x * x))))


def _mod_kernel(c_ref, w_ref, b_ref, o_ref):
    c = c_ref[...]
    s = (c * _sigmoid(c)).astype(BF16)
    o_ref[...] = jnp.dot(s, w_ref[...].astype(BF16), preferred_element_type=F32) + b_ref[...]


def _modulation(cond, ada_w, ada_b):
    depth = ada_w.shape[0]
    rows = cond.shape[0]
    tn = 1536
    return pl.pallas_call(
        _mod_kernel,
        grid=(depth, 6 * D_MODEL // tn),
        in_specs=[
            pl.BlockSpec((rows, D_MODEL), lambda l, j: (0, 0)),
            pl.BlockSpec((None, D_MODEL, tn), lambda l, j: (l, 0, j)),
            pl.BlockSpec((None, 1, tn), lambda l, j: (l, 0, j)),
        ],
        out_specs=pl.BlockSpec((None, rows, tn), lambda l, j: (l, 0, j)),
        out_shape=jax.ShapeDtypeStruct((depth, rows, 6 * D_MODEL), F32),
        compiler_params=_cparams(("parallel", "parallel")),
        name="adaln_mod",
    )(cond, ada_w, ada_b.reshape(depth, 1, 6 * D_MODEL))


IN_PAD_SIZES = (S5_WIDTH, SSD_WIDTH, SSD_CONV_CH, DT_PAD, 3 * RWKV_WIDTH, LO_PAD)


def _inproj_kernel(x_ref, mod_ref, g_ref, w_ref, us5_ref, z_ref, xbc_ref, dt_ref, rkv_ref, lo_ref):
    x = x_ref[...]
    y = x * lax.rsqrt(jnp.mean(x * x, axis=-1, keepdims=True) + EPS) * g_ref[...]
    m = mod_ref[...]
    h = y * (1.0 + m[:, D_MODEL:2 * D_MODEL]) + m[:, 0:D_MODEL]
    p = jnp.dot(h.astype(BF16), w_ref[...], preferred_element_type=F32)
    start = 0
    for ref, size in zip((us5_ref, z_ref, xbc_ref, dt_ref, rkv_ref, lo_ref), IN_PAD_SIZES):
        ref[...] = p[:, start:start + size]
        start += size


def _pad_in_weight(w_in):
    parts, start = [], 0
    for s in IN_SIZES:
        parts.append(w_in[..., start:start + s])
        start += s
    us5, z, xbc, dt, rkv, wlo, alo, glo = parts
    zeros = lambda n: jnp.zeros(w_in.shape[:-1] + (n,), w_in.dtype)
    lo_used = wlo.shape[-1] + alo.shape[-1] + glo.shape[-1]
    return jnp.concatenate([us5, z, xbc, dt, zeros(DT_PAD - dt.shape[-1]), rkv, wlo, alo, glo,
                            zeros(LO_PAD - lo_used)], axis=-1).astype(BF16)


def _in_projection(x, mod3, row_of_block, norm_g, w_pad):
    ntok = x.shape[0]
    tm = TOKEN_BLOCK
    width = w_pad.shape[1]
    outs = tuple(jax.ShapeDtypeStruct((ntok, s), F32) for s in IN_PAD_SIZES)
    return pl.pallas_call(
        _inproj_kernel,
        grid=(ntok // tm,),
        in_specs=[
            pl.BlockSpec((tm, D_MODEL), lambda i: (i, 0)),
            pl.BlockSpec((None, 1, 6 * D_MODEL), lambda i: (row_of_block(i), 0, 0)),
            _full((1, D_MODEL)),
            _full((D_MODEL, width)),
        ],
        out_specs=tuple(pl.BlockSpec((tm, s), lambda i: (i, 0)) for s in IN_PAD_SIZES),
        out_shape=outs,
        compiler_params=_cparams(("parallel",)),
        name="in_proj",
    )(x, mod3, norm_g.reshape(1, D_MODEL), w_pad)


def _s5_disc_kernel(are_ref, aim_ref, ldt_ref, bre_ref, bim_ref, abre_ref, abim_ref, bbre_ref, bbim_ref):
    lam_re = jnp.minimum(are_ref[...], -1e-4)
    lam_im = aim_ref[...]
    dt = jnp.exp(ldt_ref[...])
    mag = jnp.exp(lam_re * dt)
    ab_re = mag * jnp.cos(lam_im * dt)
    ab_im = mag * jnp.sin(lam_im * dt)
    num_re, num_im = ab_re - 1.0, ab_im
    den = lam_re * lam_re + lam_im * lam_im
    q_re = (num_re * lam_re + num_im * lam_im) / den
    q_im = (num_im * lam_re - num_re * lam_im) / den
    abre_ref[...] = ab_re
    abim_ref[...] = ab_im
    b_re = bre_ref[...]
    b_im = bim_ref[...]
    qr = q_re[:, None, :]
    qi = q_im[:, None, :]
    bbre_ref[...] = qr * b_re - qi * b_im
    bbim_ref[...] = qr * b_im + qi * b_re


def _s5_discretize(a_re, a_im, log_dt, b_re, b_im):
    lead = a_re.shape[:3]
    n = lead[0] * lead[1] * lead[2]
    a2 = lambda t: t.reshape(n, S5_STATE)
    ldt = jnp.broadcast_to(log_dt.reshape(n, 1), (n, S5_STATE))
    b3 = lambda t: t.reshape(n, S5_STATE, S5_CH).transpose(0, 2, 1)
    ab_re, ab_im, bb_re, bb_im = pl.pallas_call(
        _s5_disc_kernel,
        out_shape=(jax.ShapeDtypeStruct((n, S5_STATE), F32), jax.ShapeDtypeStruct((n, S5_STATE), F32),
                   jax.ShapeDtypeStruct((n, S5_CH, S5_STATE), F32), jax.ShapeDtypeStruct((n, S5_CH, S5_STATE), F32)),
        name="s5_discretize",
    )(a2(a_re), a2(a_im), ldt, b3(b_re), b3(b_im))
    return (ab_re.reshape(lead + (S5_STATE,)), ab_im.reshape(lead + (S5_STATE,)),
            bb_re.reshape(lead + (S5_CH, S5_STATE)), bb_im.reshape(lead + (S5_CH, S5_STATE)))


def _s5_layer_tables(ab_re, ab_im, bb_re, bb_im, c_re, c_im):
    nfs = S5_GROUPS // 8
    eye = jnp.eye(8, dtype=F32)

    def rows(t):
        return t.reshape(N_DIR, nfs, 8 * S5_STATE)

    ab_row = jnp.concatenate([rows(ab_re), rows(ab_im)], axis=-1).reshape(N_DIR, nfs, 1, 2 * S5_SLICE_ST)

    def bmat(t):
        t = t.reshape(N_DIR, nfs, 8, S5_CH, S5_STATE)
        return jnp.einsum('dfghp,gk->dfghkp', t, eye).reshape(N_DIR, nfs, S5_SLICE_CH, S5_SLICE_ST)

    b_mat = jnp.concatenate([bmat(bb_re), bmat(bb_im)], axis=-1).astype(BF16)

    def cmat(t):
        t = t.reshape(N_DIR, nfs, 8, S5_CH, S5_STATE)
        return jnp.einsum('dfghp,gk->dfgpkh', t, eye).reshape(N_DIR, nfs, S5_SLICE_ST, S5_SLICE_CH)

    c_mat = jnp.concatenate([cmat(c_re), -cmat(c_im)], axis=-2).astype(BF16)
    return ab_row, b_mat, c_mat


def _s5_kernel(*refs, R, n_slab, chained, want_final):
    if chained:
        u_ref, h0_ref, ab_ref, bm_ref, cm_ref, d_ref = refs[:6]
        rest = refs[6:]
    else:
        u_ref, ab_ref, bm_ref, cm_ref, d_ref = refs[:5]
        h0_ref = None
        rest = refs[5:]
    y_ref = rest[0]
    rest = rest[1:]
    if want_final:
        hfin_ref = rest[0]
        rest = rest[1:]
    H = rest[0]
    if chained:
        PW, CIN = rest[1], rest[2]
    NR = R * n_slab
    RC = min(512, NR)
    ST = S5_SLICE_ST
    nchunk = ST // LANES

    y_ref[...] = u_ref[...] * d_ref[...]

    for d in range(N_DIR):
        def slab_of(i, d=d):
            return i if d == 0 else n_slab - 1 - i

        def bu_body(i, _, d=d):
            r0 = pl.multiple_of(i * RC, RC)
            H[pl.ds(r0, RC), :] = jnp.dot(u_ref[pl.ds(r0, RC), :].astype(BF16), bm_ref[d],
                                           preferred_element_type=F32)
            return 0
        lax.fori_loop(0, NR // RC, bu_body, 0)

        for c in range(nchunk):
            lre = slice(c * LANES, (c + 1) * LANES)
            lim = slice(ST + c * LANES, ST + (c + 1) * LANES)
            a_re = jnp.broadcast_to(ab_ref[d, :, lre], (R, LANES))
            a_im = jnp.broadcast_to(ab_ref[d, :, lim], (R, LANES))

            def step(i, carry, lre=lre, lim=lim, a_re=a_re, a_im=a_im, slab_of=slab_of):
                cr, ci = carry
                r0 = pl.multiple_of(slab_of(i) * R, R)
                nr = a_re * cr - a_im * ci + H[pl.ds(r0, R), lre]
                ni = a_re * ci + a_im * cr + H[pl.ds(r0, R), lim]
                H[pl.ds(r0, R), lre] = nr
                H[pl.ds(r0, R), lim] = ni
                return nr, ni
            zero = jnp.zeros((R, LANES), F32)
            lax.fori_loop(0, n_slab, step, (zero, zero))

        last0 = (n_slab - 1) * R if d == 0 else 0
        if chained:
            a_re_row = ab_ref[d, :, 0:ST]
            a_im_row = ab_ref[d, :, ST:2 * ST]

            def pw_step(j, carry, a_re_row=a_re_row, a_im_row=a_im_row):
                pr, pi = carry
                PW[j, :, 0:ST] = jnp.broadcast_to(pr, (8, ST))
                PW[j, :, ST:2 * ST] = jnp.broadcast_to(pi, (8, ST))
                return pr * a_re_row - pi * a_im_row, pr * a_im_row + pi * a_re_row
            lax.fori_loop(0, n_slab, pw_step, (a_re_row, a_im_row))
            t_re = PW[n_slab - 1, 0:1, 0:ST]
            t_im = PW[n_slab - 1, 0:1, ST:2 * ST]

            cr = h0_ref[d, :, 0:ST]
            ci = h0_ref[d, :, ST:2 * ST]
            for i in range(R):
                c = i if d == 0 else R - 1 - i
                CIN[c:c + 1, 0:ST] = cr
                CIN[c:c + 1, ST:2 * ST] = ci
                er = H[last0 + c:last0 + c + 1, 0:ST]
                ei = H[last0 + c:last0 + c + 1, ST:2 * ST]
                cr, ci = t_re * cr - t_im * ci + er, t_re * ci + t_im * cr + ei

            def fix_step(i, _, slab_of=slab_of):
                r0 = pl.multiple_of(slab_of(i) * R, R)
                p = PW[i]
                for c in range(nchunk):
                    lre = slice(c * LANES, (c + 1) * LANES)
                    lim = slice(ST + c * LANES, ST + (c + 1) * LANES)
                    pr = p[0:1, lre]
                    pi = p[0:1, lim]
                    cr = CIN[:, lre]
                    ci = CIN[:, lim]
                    H[pl.ds(r0, R), lre] = H[pl.ds(r0, R), lre] + (pr * cr - pi * ci)
                    H[pl.ds(r0, R), lim] = H[pl.ds(r0, R), lim] + (pr * ci + pi * cr)
                return 0
            lax.fori_loop(0, n_slab, fix_step, 0)

        if want_final:
            hfin_ref[d] = H[last0:last0 + R, :]

        def y_body(i, _, d=d):
            r0 = pl.multiple_of(i * RC, RC)
            y_ref[pl.ds(r0, RC), :] = y_ref[pl.ds(r0, RC), :] + jnp.dot(
                H[pl.ds(r0, RC), :].astype(BF16), cm_ref[d], preferred_element_type=F32)
            return 0
        lax.fori_loop(0, NR // RC, y_body, 0)


def _s5_scan(u, h0, tables, d_row, *, R, n_slab, chained, want_final):
    ab_row, b_mat, c_mat = tables
    Q, NR, _ = u.shape
    nfs = S5_GROUPS // 8
    W2 = 2 * S5_SLICE_ST
    in_specs = [pl.BlockSpec((None, NR, S5_SLICE_CH), lambda q, f: (q, 0, f))]
    args = [u]
    if chained:
        in_specs.append(pl.BlockSpec((None, N_DIR, None, 1, W2), lambda q, f: (q, 0, f, 0, 0)))
        args.append(h0)
    in_specs += [
        pl.BlockSpec((N_DIR, None, 1, W2), lambda q, f: (0, f, 0, 0)),
        pl.BlockSpec((N_DIR, None, S5_SLICE_CH, W2), lambda q, f: (0, f, 0, 0)),
        pl.BlockSpec((N_DIR, None, W2, S5_SLICE_CH), lambda q, f: (0, f, 0, 0)),
        pl.BlockSpec((1, S5_SLICE_CH), lambda q, f: (0, f)),
    ]
    args += [ab_row, b_mat, c_mat, d_row]
    out_shape = [jax.ShapeDtypeStruct((Q, NR, S5_WIDTH), F32)]
    out_specs = [pl.BlockSpec((None, NR, S5_SLICE_CH), lambda q, f: (q, 0, f))]
    if want_final:
        out_shape.append(jax.ShapeDtypeStruct((Q, N_DIR, nfs, R, W2), F32))
        out_specs.append(pl.BlockSpec((None, N_DIR, None, R, W2), lambda q, f: (q, 0, f, 0, 0)))
    scratch = [pltpu.VMEM((NR, W2), F32)]
    if chained:
        scratch += [pltpu.VMEM((n_slab, 8, W2), F32), pltpu.VMEM((R, W2), F32)]
    res = pl.pallas_call(
        functools.partial(_s5_kernel, R=R, n_slab=n_slab, chained=chained, want_final=want_final),
        grid=(Q, nfs),
        in_specs=in_specs,
        out_specs=tuple(out_specs),
        out_shape=tuple(out_shape),
        scratch_shapes=scratch,
        compiler_params=_cparams(("parallel", "parallel")),
        name="s5_scan_chained" if chained else "s5_scan",
    )(*args)
    return res


def _ssd_kernel(*refs, TB, nb, has_h0, want_final):
    io = [refs[4 * d:4 * d + 4] for d in range(N_DIR)]
    refs = refs[8:]
    if has_h0:
        h0_ref = refs[0]
        refs = refs[1:]
    cw_ref, cb_ref, dtb_ref, arow_ref, sel_ref, drow_ref = refs[:6]
    refs = refs[6:]
    y_refs = refs[:2]
    refs = refs[2:]
    if want_final:
        hfin_ref = refs[0]
        refs = refs[1:]
    hst, xc_s, dt_s = refs
    CH = SSD_CHUNK
    P = SSD_HEADDIM
    j = pl.program_id(1)

    @pl.when(j == 0)
    def _():
        if has_h0:
            hst[...] = h0_ref[...]
        else:
            hst[...] = jnp.zeros_like(hst)

    li = lax.broadcasted_iota(jnp.int32, (CH, CH), 0)
    si = lax.broadcasted_iota(jnp.int32, (CH, CH), 1)
    tmats = ((si <= li).astype(F32), (si >= li).astype(F32))
    rows = lax.broadcasted_iota(jnp.int32, (TB, 1), 0)
    for d in range(N_DIR):
        xbc_ref, xp_ref, xn_ref, dt_ref = io[d]
        jj = j if d == 0 else nb - 1 - j
        x = xbc_ref[...]
        prev_row = xp_ref[7:8, :] * (jj > 0).astype(F32)
        next_row = xn_ref[0:1, :] * (jj < nb - 1).astype(F32)
        x_prev = jnp.where(rows == 0, prev_row, pltpu.roll(x, 1, 0))
        x_next = jnp.where(rows == TB - 1, next_row, pltpu.roll(x, TB - 1, 0))
        conv = cw_ref[0:1, :] * x_prev + cw_ref[1:2, :] * x + cw_ref[2:3, :] * x_next + cb_ref[...]
        xc_s[d] = conv * _sigmoid(conv)
        dtf = _softplus(dt_ref[...] + dtb_ref[...])
        dt_s[d] = _dot_exact_rhs(dtf, sel_ref[d])

    hrow = lax.broadcasted_iota(jnp.int32, (LANES, SSD_WIDTH), 0)
    e_head = jnp.where(lax.broadcasted_iota(jnp.int32, (LANES, SSD_WIDTH), 1) // P == hrow, 1.0, 0.0)
    hrow2 = lax.broadcasted_iota(jnp.int32, (LANES, SSD_HEADS * CH), 0)
    e_chunk = jnp.where(lax.broadcasted_iota(jnp.int32, (LANES, SSD_HEADS * CH), 1) // CH == hrow2, 1.0, 0.0)

    n_ch = TB // CH
    for i, d in [(i, d) for i in range(n_ch) for d in range(N_DIR)]:
        tmat = tmats[d]
        y_ref = y_refs[d]
        r0 = (i if d == 0 else n_ch - 1 - i) * CH
        dtc = dt_s[d, r0:r0 + CH, :]
        dA = dtc * arow_ref[d]
        cs = _dot_exact_lhs(tmat, dA)
        csT = cs.T
        dt_x = _dot_exact_rhs(dtc, e_head)
        cs_x = _dot_exact_rhs(cs, e_head)
        tot_x = jnp.sum(_dot_exact_rhs(dA, e_head), axis=0, keepdims=True)
        cs_xx = _dot_exact_rhs(cs, e_chunk)
        xs_all = xc_s[d, r0:r0 + CH, 0:SSD_WIDTH]
        xdt_all = xs_all * dt_x
        xd_all = xdt_all * jnp.exp(tot_x - cs_x)
        ecs_all = jnp.exp(cs_x)
        etot_all = jnp.exp(tot_x)
        Bm = xc_s[d, r0:r0 + CH, SSD_WIDTH:SSD_WIDTH + SSD_BC]
        Cm = xc_s[d, r0:r0 + CH, SSD_WIDTH + SSD_BC:SSD_WIDTH + 2 * SSD_BC]
        BmT = Bm.T
        for g in range(SSD_GROUPS):
            Cg = Cm[:, g * SSD_STATE:(g + 1) * SSD_STATE].astype(BF16)
            Bg = Bm[:, g * SSD_STATE:(g + 1) * SSD_STATE].astype(BF16)
            BgT = BmT[g * SSD_STATE:(g + 1) * SSD_STATE, :].astype(BF16)
            G = lax.dot_general(Cg, Bg, (((1,), (1,)), ((), ())), preferred_element_type=F32)
            for hh in range(SSD_HEADS // SSD_GROUPS):
                h = g * (SSD_HEADS // SSD_GROUPS) + hh
                hl = slice(h * P, (h + 1) * P)
                row = csT[h:h + 1, :]
                lm = jnp.exp(jnp.where(tmat > 0.0, cs_xx[:, h * CH:(h + 1) * CH] - row, -1e30))
                hprev = hst[d, h]
                y = jnp.dot((G * lm).astype(BF16), xdt_all[:, hl].astype(BF16), preferred_element_type=F32)
                y = y + jnp.dot(Cg, hprev.astype(BF16), preferred_element_type=F32) * ecs_all[:, hl]
                if d == 0:
                    y = y + drow_ref[:, hl] * xs_all[:, hl]
                y_ref[r0:r0 + CH, hl] = y
                hst[d, h] = etot_all[:, hl] * hprev + jnp.dot(BgT, xd_all[:, hl].astype(BF16),
                                                             preferred_element_type=F32)

    if want_final:
        @pl.when(j == nb - 1)
        def _():
            hfin_ref[...] = hst[...]


def _ssd_scan(xbc, dt, h0, conv_w, conv_b, dt_bias, a_log, d_skip, *, B, L, want_final):
    TB = min(512, L)
    nb = L // TB
    H = SSD_HEADS

    nrow8 = B * L // 8
    in_specs, args = [], []
    blks = (lambda b, j: b * nb + j, lambda b, j: b * nb + nb - 1 - j)
    for blk in blks:
        in_specs += [
            pl.BlockSpec((TB, SSD_CONV_CH), lambda b, j, blk=blk: (blk(b, j), 0)),
            pl.BlockSpec((8, SSD_CONV_CH), lambda b, j, blk=blk: (jnp.maximum(blk(b, j) * (TB // 8) - 1, 0), 0)),
            pl.BlockSpec((8, SSD_CONV_CH),
                         lambda b, j, blk=blk: (jnp.minimum((blk(b, j) + 1) * (TB // 8), nrow8 - 1), 0)),
            pl.BlockSpec((TB, DT_PAD), lambda b, j, blk=blk: (blk(b, j), 0)),
        ]
        args += [xbc, xbc, xbc, dt]
    if h0 is not None:
        in_specs.append(pl.BlockSpec((None, N_DIR, H, SSD_STATE, SSD_HEADDIM), lambda b, j: (b, 0, 0, 0, 0)))
        args.append(h0)
    dtb = jnp.pad(dt_bias.reshape(1, N_DIR * H), ((0, 0), (0, DT_PAD - N_DIR * H)))
    arow = jnp.pad(-jnp.exp(a_log), ((0, 0), (0, LANES - H))).reshape(N_DIR, 1, LANES)
    lane = jnp.arange(LANES)
    sel = jnp.stack([(lane[:, None] == (dd * H + lane[None, :])) & (lane[None, :] < H) for dd in range(N_DIR)]).astype(F32)
    drow = jnp.repeat(d_skip, SSD_HEADDIM).reshape(1, SSD_WIDTH)
    in_specs += [
        _full((3, SSD_CONV_CH)), _full((1, SSD_CONV_CH)), _full((1, DT_PAD)),
        _full((N_DIR, 1, LANES)), _full((N_DIR, LANES, LANES)), _full((1, SSD_WIDTH)),
    ]
    args += [conv_w, conv_b.reshape(1, SSD_CONV_CH), dtb, arow, sel, drow]
    out_shape = [jax.ShapeDtypeStruct((B * L, SSD_WIDTH), F32)] * N_DIR
    out_specs = [pl.BlockSpec((TB, SSD_WIDTH), lambda b, j, blk=blk: (blk(b, j), 0)) for blk in blks]
    if want_final:
        out_shape.append(jax.ShapeDtypeStruct((B, N_DIR, H, SSD_STATE, SSD_HEADDIM), F32))
        out_specs.append(pl.BlockSpec((None, N_DIR, H, SSD_STATE, SSD_HEADDIM), lambda b, j: (b, 0, 0, 0, 0)))
    return pl.pallas_call(
        functools.partial(_ssd_kernel, TB=TB, nb=nb, has_h0=h0 is not None, want_final=want_final),
        grid=(B, nb),
        in_specs=in_specs,
        out_specs=tuple(out_specs),
        out_shape=tuple(out_shape),
        scratch_shapes=[pltpu.VMEM((N_DIR, H, SSD_STATE, SSD_HEADDIM), F32),
                        pltpu.VMEM((N_DIR, TB, SSD_CONV_CH), F32), pltpu.VMEM((N_DIR, TB, DT_PAD), F32)],
        compiler_params=_cparams(("parallel", "arbitrary")),
        name="ssd_scan",
    )(*args)


def _rwkv_prep_kernel(rkv_ref, rp_ref, rn_ref, lo_ref, mu_ref, a0_ref, aup_ref, gup_ref, w0_ref, wup_ref,
                      kkw_ref, ka_ref, rk_ref, seg_ref,
                      r_ref, w0o_ref, w1o_ref, k_ref, v_ref, kk_ref, b_ref, g_ref, bonus_ref, *, nbs):
    tm = rkv_ref.shape[0]
    W = RWKV_WIDTH
    jj = pl.program_id(0) % nbs
    x = rkv_ref[...]
    prev_row = rp_ref[7:8, :] * (jj > 0).astype(F32)
    next_row = rn_ref[0:1, :] * (jj < nbs - 1).astype(F32)
    rows = lax.broadcasted_iota(jnp.int32, (tm, 1), 0)
    xp = jnp.where(rows == 0, prev_row, pltpu.roll(x, 1, 0))
    xn = jnp.where(rows == tm - 1, next_row, pltpu.roll(x, tm - 1, 0))
    x = x + mu_ref[0:1, :] * (xp - x) + mu_ref[1:2, :] * (xn - x)
    r = x[:, 0:W]
    k = x[:, W:2 * W]
    v = x[:, 2 * W:3 * W]
    lo = lo_ref[...]
    seg = seg_ref[...]

    def put(ref, val):
        ref[:, 0:W] = val
        if ref.shape[1] > W:
            ref[:, W:] = jnp.zeros((tm, ref.shape[1] - W), F32)

    a = _sigmoid(a0_ref[...] + _dot3(lo, aup_ref[...]))
    g_ref[...] = _dot3(_sigmoid(lo), gup_ref[...])
    tlo = jnp.tanh(lo)
    for d, o_ref in enumerate((w0o_ref, w1o_ref)):
        zw = w0_ref[d:d + 1, :] + _dot3(tlo, wup_ref[d])
        put(o_ref, jnp.exp(-DECAY_SCALE * _sigmoid(zw)))
    kk = k * kkw_ref[...]
    kk = kk * lax.rsqrt(jnp.maximum(_dot_exact_rhs(kk * kk, seg), 1e-24))
    k2 = k * (1.0 + (a - 1.0) * ka_ref[...])
    put(r_ref, r)
    put(k_ref, k2)
    put(v_ref, v)
    put(kk_ref, kk)
    put(b_ref, kk * a)
    bonus_ref[...] = _dot_exact_rhs(r * k2 * rk_ref[...], seg) * v


def _segment_ones(width, seg):
    i = jnp.arange(width)
    return (i[:, None] // seg == i[None, :] // seg).astype(F32)


def _rwkv_prep(rkv, lo, lp, seq_blocks, scan_width=RWKV_WIDTH):
    ntok = rkv.shape[0]
    tm = TOKEN_BLOCK
    W = RWKV_WIDTH
    nrow8 = ntok // 8
    nbs_of = seq_blocks
    widths = (scan_width,) * 7 + (W, W)
    pad_rows = lambda t, r0: jnp.zeros((LO_PAD, W), F32).at[r0:r0 + t.shape[0]].set(t)
    aup = pad_rows(lp['rwkv_a_up'], N_DIR * W_RANK)
    gup = pad_rows(lp['rwkv_g_up'], N_DIR * W_RANK + A_RANK)
    wup = jnp.stack([pad_rows(lp['rwkv_w_up'][d], d * W_RANK) for d in range(N_DIR)])
    row = lambda t: t.reshape(1, W)
    outs = tuple(jax.ShapeDtypeStruct((ntok, w), F32) for w in widths)
    return pl.pallas_call(
        functools.partial(_rwkv_prep_kernel, nbs=nbs_of),
        grid=(ntok // tm,),
        in_specs=[
            pl.BlockSpec((tm, 3 * W), lambda i: (i, 0)),
            pl.BlockSpec((8, 3 * W), lambda i: (jnp.maximum(i * (tm // 8) - 1, 0), 0)),
            pl.BlockSpec((8, 3 * W), lambda i: (jnp.minimum((i + 1) * (tm // 8), nrow8 - 1), 0)),
            pl.BlockSpec((tm, LO_PAD), lambda i: (i, 0)),
            _full((2, 3 * W)), _full((1, W)), _full((LO_PAD, W)), _full((LO_PAD, W)), _full((2, W)),
            _full((N_DIR, LO_PAD, W)), _full((1, W)), _full((1, W)), _full((1, W)), _full((W, W)),
        ],
        out_specs=tuple(pl.BlockSpec((tm, w), lambda i: (i, 0)) for w in widths),
        out_shape=outs,
        compiler_params=_cparams(("parallel",)),
        name="rwkv_prep",
    )(rkv, rkv, rkv, lo, lp['rwkv_mu'], row(lp['rwkv_a0']), aup, gup, lp['rwkv_w0'], wup,
      row(lp['rwkv_k_k']), row(lp['rwkv_k_a']), row(lp['rwkv_r_k']), lp['seg'])


def _rwkv_scan_kernel(r_ref, w_ref, k_ref, kk_ref, b_ref, v_ref, s0_ref, y_ref, sfin_ref, S, SA, *, TBLK, V, nblk):
    K = RWKV_HEADSIZE
    VH = V // 2
    tb = pl.program_id(1)

    @pl.when(tb == 0)
    def _():
        S[...] = s0_ref[...]

    def bc(ref, t, kx):
        return jnp.broadcast_to(ref[t, pl.ds(kx, 1), :], (VH, LANES))

    def tree(parts):
        return (parts[0] + parts[1]) + (parts[2] + parts[3])

    for hv in range(2):
        rows = pl.ds(hv * VH, VH)
        accs = [None] * 4
        for kx in range(K):
            p = S[kx, rows, :] * bc(kk_ref, 0, kx)
            accs[kx % 4] = p if accs[kx % 4] is None else accs[kx % 4] + p
        SA[hv] = tree(accs)

    def step(t, _):
        tn = jnp.minimum(t + 1, TBLK - 1)
        for hv in range(2):
            rows = pl.ds(hv * VH, VH)
            sa = SA[hv]
            vt = v_ref[t, rows, :]
            yacc = [None] * 4
            sacc = [None] * 4
            for kx in range(K):
                s_new = S[kx, rows, :] * bc(w_ref, t, kx) - sa * bc(b_ref, t, kx) + vt * bc(k_ref, t, kx)
                S[kx, rows, :] = s_new
                py = s_new * bc(r_ref, t, kx)
                ps = s_new * bc(kk_ref, tn, kx)
                a = kx % 4
                yacc[a] = py if yacc[a] is None else yacc[a] + py
                sacc[a] = ps if sacc[a] is None else sacc[a] + ps
            y_ref[t, rows, :] = tree(yacc)
            SA[hv] = tree(sacc)
        return 0
    lax.fori_loop(0, TBLK, step, 0)

    @pl.when(tb == nblk - 1)
    def _():
        sfin_ref[...] = S[...]


def _rwkv_scan(r, w, k, kk, b, v, s0):
    L, K, NL = r.shape
    V = v.shape[1]
    TBLK = min(64, L)
    nblk = L // TBLK
    ngrp = NL // LANES
    kspec = pl.BlockSpec((TBLK, K, LANES), lambda g, t: (t, 0, g))
    vspec = pl.BlockSpec((TBLK, V, LANES), lambda g, t: (t, 0, g))
    sspec = pl.BlockSpec((K, V, LANES), lambda g, t: (0, 0, g))
    return pl.pallas_call(
        functools.partial(_rwkv_scan_kernel, TBLK=TBLK, V=V, nblk=nblk),
        grid=(ngrp, nblk),
        in_specs=[kspec, kspec, kspec, kspec, kspec, vspec, sspec],
        out_specs=(vspec, sspec),
        out_shape=(jax.ShapeDtypeStruct((L, V, NL), F32), jax.ShapeDtypeStruct((K, V, NL), F32)),
        scratch_shapes=[pltpu.VMEM((K, V, LANES), F32), pltpu.VMEM((2, V // 2, LANES), F32)],
        compiler_params=_cparams(("parallel", "arbitrary")),
        name="rwkv_scan",
    )(r, w, k, kk, b, v, s0)


KQ = 4
CH = LANES // KQ


def _rwkv_scan2_kernel(rf, rm, wf, wm, kf, km, kkf, kkm, bf, bm, vf, vm, s0_ref, yf_ref, yb_ref, S, YQ, SA, *, TBLK):
    V = RWKV_HEADSIZE
    NQ = RWKV_HEADSIZE // KQ
    tb = pl.program_id(0)

    @pl.when(tb == 0)
    def _():
        S[...] = s0_ref[...]

    VH = V // 2

    def bc(ref, tt, q):
        return jnp.broadcast_to(ref[tt, pl.ds(q, 1), :], (VH, LANES))

    def all_quarters(x):
        return (x + pltpu.roll(x, CH, 1)) + (pltpu.roll(x, 2 * CH, 1) + pltpu.roll(x, 3 * CH, 1))

    dirs = ((rf, wf, kf, kkf, bf, vf), (rm, wm, km, kkm, bm, vm))
    time_of = (lambda t: t, lambda t: TBLK - 1 - t)
    chains = [(d, hv) for d in range(N_DIR) for hv in range(2)]

    def first_sa(d, hv):
        kk_ = dirs[d][3]
        tt = time_of[d](0)
        acc = [None, None]
        for q in range(NQ):
            p = S[d, q, pl.ds(hv * VH, VH), :] * bc(kk_, tt, q)
            acc[q % 2] = p if acc[q % 2] is None else acc[q % 2] + p
        return acc[0] + acc[1]

    TG = min(16, TBLK)
    lane_r = lax.broadcasted_iota(jnp.int32, (2 * LANES, CH), 0) % CH
    fold2 = jnp.where(lax.broadcasted_iota(jnp.int32, (2 * LANES, CH), 1) == lane_r, 1.0, 0.0).astype(BF16)

    last = len(chains) - 1
    for c, (d, hv) in enumerate(chains):
        part = first_sa(d, hv)
        SA[c] = part if c == last else all_quarters(part)

    def step(t, _):
        sa_last = all_quarters(SA[last])
        for c, (d, hv) in enumerate(chains):
            r_, w_, k_, kk_, b_, v_ = dirs[d]
            tt = time_of[d](t)
            tn = time_of[d](jnp.minimum(t + 1, TBLK - 1))
            rows = pl.ds(hv * VH, VH)
            sa = sa_last if c == last else SA[c]
            vt = v_[tt, rows, :]
            yacc = None
            sacc = None
            for q in range(NQ):
                s_new = S[d, q, rows, :] * bc(w_, tt, q) - sa * bc(b_, tt, q) + vt * bc(k_, tt, q)
                S[d, q, rows, :] = s_new
                py = s_new * bc(r_, tt, q)
                ps = s_new * bc(kk_, tn, q)
                yacc = py if yacc is None else yacc + py
                sacc = ps if sacc is None else sacc + ps
            YQ[d, tt, rows, :] = yacc
            SA[c] = sacc if c == last else all_quarters(sacc)
        return 0
    lax.fori_loop(0, TBLK, step, 0)

    def finish(i, _):
        for d, y_ in enumerate((yf_ref, yb_ref)):
            hi, lo = _split_bf16(YQ[d, pl.ds(i * TG, TG)].reshape(TG * V, LANES))
            y = jnp.dot(jnp.concatenate([hi, lo], axis=-1), fold2, preferred_element_type=F32)
            y_[pl.ds(i * TG, TG)] = y.reshape(TG, V, CH)
        return 0
    lax.fori_loop(0, TBLK // TG, finish, 0)


def _rwkv_scan2(r, w0, w1, k, kk, b, v, s0):
    L, NQ, _ = r.shape
    V = RWKV_HEADSIZE
    TBLK = min(64, L)
    nblk = L // TBLK
    fspec = pl.BlockSpec((TBLK, NQ, LANES), lambda t: (t, 0, 0))
    mspec = pl.BlockSpec((TBLK, NQ, LANES), lambda t: (nblk - 1 - t, 0, 0))
    vfspec = pl.BlockSpec((TBLK, V, LANES), lambda t: (t, 0, 0))
    vmspec = pl.BlockSpec((TBLK, V, LANES), lambda t: (nblk - 1 - t, 0, 0))
    yfspec = pl.BlockSpec((TBLK, V, CH), lambda t: (t, 0, 0))
    ymspec = pl.BlockSpec((TBLK, V, CH), lambda t: (nblk - 1 - t, 0, 0))
    yf, yb = pl.pallas_call(
        functools.partial(_rwkv_scan2_kernel, TBLK=TBLK),
        grid=(nblk,),
        in_specs=[fspec, mspec] * 5 + [vfspec, vmspec, _full((N_DIR, NQ, V, LANES))],
        out_specs=(yfspec, ymspec),
        out_shape=(jax.ShapeDtypeStruct((L, V, CH), F32), jax.ShapeDtypeStruct((L, V, CH), F32)),
        scratch_shapes=[pltpu.VMEM((N_DIR, NQ, V, LANES), F32), pltpu.VMEM((N_DIR, TBLK, V, LANES), F32),
                        pltpu.VMEM((2 * N_DIR, V // 2, LANES), F32)],
        compiler_params=_cparams(("arbitrary",)),
        name="rwkv_scan2",
    )(r, r, w0, w1, k, k, kk, kk, b, b, v, v, s0)
    return yf, yb


HEADS_PAD = 8
SCAN_WIDTH = HEADS_PAD * RWKV_HEADSIZE


def _pad_chains(x, B):
    return x if B * HEADS_PAD == CH else jnp.pad(x, ((0, 0),) * (x.ndim - 1) + ((0, CH - B * HEADS_PAD),))


def _to_scan2_k(t, B, L):
    x = t.reshape(B, L, HEADS_PAD, RWKV_HEADSIZE // KQ, KQ).transpose(1, 3, 4, 0, 2)
    x = _pad_chains(x.reshape(L, RWKV_HEADSIZE // KQ, KQ, B * HEADS_PAD), B)
    return x.reshape(L, RWKV_HEADSIZE // KQ, LANES)


def _to_scan2_v(t, B, L):
    x = t.reshape(B, L, HEADS_PAD, RWKV_HEADSIZE).transpose(1, 3, 0, 2).reshape(L, RWKV_HEADSIZE, B * HEADS_PAD)
    return jnp.tile(_pad_chains(x, B), (1, 1, KQ))


def _from_scan2(y, B, L):
    y = y[:, :, :B * HEADS_PAD]
    return y.reshape(L, RWKV_HEADSIZE, B, HEADS_PAD).transpose(2, 0, 3, 1).reshape(B * L, SCAN_WIDTH)


def _state_to_scan2(s):
    B = s.shape[0]
    s = jnp.pad(s, ((0, 0), (0, 0), (0, HEADS_PAD - RWKV_HEADS), (0, 0), (0, 0)))
    x = s.transpose(1, 4, 3, 0, 2).reshape(N_DIR, RWKV_HEADSIZE // KQ, KQ, RWKV_HEADSIZE, B * HEADS_PAD)
    x = _pad_chains(x, B)
    return x.transpose(0, 1, 3, 2, 4).reshape(N_DIR, RWKV_HEADSIZE // KQ, RWKV_HEADSIZE, LANES)


def _rwkv_layout(B, vsplit):
    chains = N_DIR * B * RWKV_HEADS
    cp = -(-chains // (LANES // vsplit)) * (LANES // vsplit)
    return chains, cp


def _to_scan_k(t_fwd, t_bwd, B, L, vsplit):
    chains, cp = _rwkv_layout(B, vsplit)

    def one(t):
        return t.reshape(B, L, RWKV_HEADS, RWKV_HEADSIZE).transpose(1, 3, 0, 2).reshape(L, RWKV_HEADSIZE, B * RWKV_HEADS)
    x = jnp.concatenate([one(t_fwd), jnp.flip(one(t_bwd), 0)], axis=-1)
    x = jnp.pad(x, ((0, 0), (0, 0), (0, cp - chains)))
    return jnp.tile(x, (1, 1, vsplit))


def _to_scan_v(t, B, L, vsplit):
    chains, cp = _rwkv_layout(B, vsplit)
    x = t.reshape(B, L, RWKV_HEADS, RWKV_HEADSIZE).transpose(1, 3, 0, 2).reshape(L, RWKV_HEADSIZE, B * RWKV_HEADS)
    x = jnp.concatenate([x, jnp.flip(x, 0)], axis=-1)
    x = jnp.pad(x, ((0, 0), (0, 0), (0, cp - chains)))
    vs = RWKV_HEADSIZE // vsplit
    return x.reshape(L, vsplit, vs, cp).transpose(0, 2, 1, 3).reshape(L, vs, vsplit * cp)


def _from_scan_v(y, B, L, vsplit):
    chains, cp = _rwkv_layout(B, vsplit)
    vs = RWKV_HEADSIZE // vsplit
    y = y.reshape(L, vs, vsplit, cp).transpose(0, 2, 1, 3).reshape(L, RWKV_HEADSIZE, cp)[:, :, :chains]
    half = chains // 2

    def back(t):
        return t.reshape(L, RWKV_HEADSIZE, B, RWKV_HEADS).transpose(2, 0, 3, 1).reshape(B * L, RWKV_WIDTH)
    return back(y[:, :, :half]), back(jnp.flip(y[:, :, half:], 0))


def _state_to_scan(s, vsplit):
    B = s.shape[0]
    chains, cp = _rwkv_layout(B, vsplit)
    x = s.transpose(4, 3, 1, 0, 2).reshape(RWKV_HEADSIZE, RWKV_HEADSIZE, chains)
    x = jnp.pad(x, ((0, 0), (0, 0), (0, cp - chains)))
    vs = RWKV_HEADSIZE // vsplit
    return x.reshape(RWKV_HEADSIZE, vsplit, vs, cp).transpose(0, 2, 1, 3).reshape(RWKV_HEADSIZE, vs, vsplit * cp)


def _state_from_scan(x, B, vsplit):
    chains, cp = _rwkv_layout(B, vsplit)
    vs = RWKV_HEADSIZE // vsplit
    x = x.reshape(RWKV_HEADSIZE, vs, vsplit, cp).transpose(0, 2, 1, 3).reshape(RWKV_HEADSIZE, RWKV_HEADSIZE, cp)
    x = x[:, :, :chains].reshape(RWKV_HEADSIZE, RWKV_HEADSIZE, N_DIR, B, RWKV_HEADS)
    return x.transpose(3, 2, 4, 1, 0)


def _post_kernel(x_ref, mod_ref, ys5_ref, yf_ref, yb_ref, z_ref, rf_ref, rb_ref, bonus_ref, g_ref,
                 gluw_ref, glub_ref, ssdg_ref, lng_ref, lnb_ref, seg_ref, wout_ref, n2g_ref, rw_ref,
                 x1_ref, hb_ref, aff_ref):
    m = mod_ref[...]
    D = D_MODEL
    zg = _gelu_tanh(ys5_ref[...])
    gate = jnp.dot(zg.astype(BF16), gluw_ref[...], preferred_element_type=F32) + glub_ref[...]
    y_a = zg * _sigmoid(gate)
    z = z_ref[...]
    yb = (yf_ref[...] + yb_ref[...]) * (z * _sigmoid(z))
    y_b = yb * lax.rsqrt(jnp.mean(yb * yb, axis=-1, keepdims=True) + EPS) * ssdg_ref[...]
    seg = seg_ref[...] * (1.0 / RWKV_HEADSIZE)
    yr = rf_ref[...] + rb_ref[...]
    mean = _dot_exact_rhs(yr, seg)
    cen = yr - mean
    var = jnp.dot((cen * cen).astype(BF16), seg.astype(BF16), preferred_element_type=F32)
    yn = cen * lax.rsqrt(var + GN_EPS) * lng_ref[...] + lnb_ref[...]
    y_c = (yn + bonus_ref[...]) * g_ref[...]
    o = jnp.dot(y_a.astype(BF16), wout_ref[0:S5_WIDTH, :], preferred_element_type=F32)
    o = o + jnp.dot(y_b.astype(BF16), wout_ref[S5_WIDTH:S5_WIDTH + SSD_WIDTH, :], preferred_element_type=F32)
    o = o + jnp.dot(y_c.astype(BF16), wout_ref[S5_WIDTH + SSD_WIDTH:, :], preferred_element_type=F32)
    x1 = x_ref[...] + m[:, 2 * D:3 * D] * o
    x1_ref[...] = x1
    h2 = x1 * lax.rsqrt(jnp.mean(x1 * x1, axis=-1, keepdims=True) + EPS) * n2g_ref[...]
    h2 = h2 * (1.0 + m[:, 4 * D:5 * D]) + m[:, 3 * D:4 * D]
    hb_ref[...] = h2.astype(BF16)
    logits = _dot3(rw_ref[...], h2, (((1,), (1,)), ((), ())))
    mx = jnp.max(logits, axis=0, keepdims=True)
    ex = jnp.exp(logits - mx)
    aff_ref[...] = ex / jnp.sum(ex, axis=0, keepdims=True)


def _post_mixer(x, mod3, row_of_block, ys5, yssd, z, rf, rb, bonus, g, lp):
    ntok = x.shape[0]
    tm = TOKEN_BLOCK
    tok = lambda w: pl.BlockSpec((tm, w), lambda i: (i, 0))
    row = lambda t: t.reshape(1, -1)
    W = RWKV_WIDTH
    return pl.pallas_call(
        _post_kernel,
        grid=(ntok // tm,),
        in_specs=[
            tok(D_MODEL),
            pl.BlockSpec((None, 1, 6 * D_MODEL), lambda i: (row_of_block(i), 0, 0)),
            tok(S5_WIDTH),
            tok(SSD_WIDTH), tok(SSD_WIDTH),
            tok(SSD_WIDTH), tok(W), tok(W), tok(W), tok(W),
            _full((S5_WIDTH, S5_WIDTH)), _full((1, S5_WIDTH)), _full((1, SSD_WIDTH)), _full((1, W)), _full((1, W)),
            _full((W, W)), _full((D_MODEL, D_MODEL)), _full((1, D_MODEL)), _full((N_EXPERTS, D_MODEL)),
        ],
        out_specs=(tok(D_MODEL), tok(D_MODEL), pl.BlockSpec((N_EXPERTS, tm), lambda i: (0, i))),
        out_shape=(jax.ShapeDtypeStruct((ntok, D_MODEL), F32), jax.ShapeDtypeStruct((ntok, D_MODEL), BF16),
                   jax.ShapeDtypeStruct((N_EXPERTS, ntok), F32)),
        compiler_params=_cparams(("parallel",)),
        name="post_mixer",
    )(x, mod3, ys5, yssd[0], yssd[1], z, rf, rb, bonus, g,
      lp['s5_glu_w'], row(lp['s5_glu_b']), row(lp['ssd_norm_g']), row(lp['rwkv_ln_g']),
      row(lp['rwkv_ln_b']), lp['seg'], lp['w_out'], row(lp['norm2_g']),
      lp['router_w'].T)


def _select_kernel(aff_ref, slot_ref, affo_ref, *, n, cap, bg):
    for g in range(bg):
        _select_one(aff_ref[:, g * n:(g + 1) * n], slot_ref.at[g], affo_ref.at[g], n, cap)


def _select_one(a, slot_ref, affo_ref, n, cap):
    E = N_EXPERTS
    bits = pltpu.bitcast(a, jnp.int32)
    thr = jnp.zeros((E, 1), jnp.int32)
    capf = float(cap)
    for bit in range(30, -1, -1):
        cand = thr | (1 << bit)
        cnt = jnp.sum(jnp.where(bits >= cand, 1.0, 0.0), axis=1, keepdims=True)
        thr = jnp.where(cnt >= capf, cand, thr)
    gt = bits > thr
    eq = bits == thr
    need = capf - jnp.sum(jnp.where(gt, 1.0, 0.0), axis=1, keepdims=True)
    CW = min(256, n)
    ui = lax.broadcasted_iota(jnp.int32, (CW, CW), 0)
    uj = lax.broadcasted_iota(jnp.int32, (CW, CW), 1)
    upper = jnp.where(ui < uj, 1.0, 0.0).astype(BF16)

    def excl_cumsum(mask_f):
        outs = []
        off = jnp.zeros((E, 1), F32)
        for c in range(n // CW):
            mc = mask_f[:, c * CW:(c + 1) * CW]
            outs.append(jnp.dot(mc.astype(BF16), upper, preferred_element_type=F32) + off)
            off = off + jnp.sum(mc, axis=1, keepdims=True)
        return jnp.concatenate(outs, axis=1)

    eq_rank = excl_cumsum(jnp.where(eq, 1.0, 0.0))
    sel = jnp.where(gt, 1.0, jnp.where(eq, jnp.where(eq_rank < need, 1.0, 0.0), 0.0))
    pos = excl_cumsum(sel)
    slot = jnp.where(sel > 0.0, pos, -1.0)
    for e in range(E):
        slot_ref[e] = slot[e:e + 1, :]
        affo_ref[e] = a[e:e + 1, :]


def _select(aff, B, n, cap):
    bg = 4 if (B % 4 == 0 and n <= 512) else (2 if B % 2 == 0 else 1)
    spec = pl.BlockSpec((bg, N_EXPERTS, 1, n), lambda b: (b, 0, 0, 0))
    return pl.pallas_call(
        functools.partial(_select_kernel, n=n, cap=cap, bg=bg),
        grid=(B // bg,),
        in_specs=[pl.BlockSpec((N_EXPERTS, bg * n), lambda b: (0, b))],
        out_specs=(spec, spec),
        out_shape=(jax.ShapeDtypeStruct((B, N_EXPERTS, 1, n), F32), jax.ShapeDtypeStruct((B, N_EXPERTS, 1, n), F32)),
        compiler_params=_cparams(("parallel",)),
        name="ec_select",
    )(aff)


def _slot_block_range(slots_f, cap, SB):
    lo = jnp.min(jnp.where(slots_f >= 0.0, slots_f, float(cap))).astype(jnp.int32)
    hi = jnp.max(slots_f).astype(jnp.int32)
    first = lo // SB
    count = jnp.where(hi >= 0, hi // SB - first + 1, 0)
    return first, count


EC_SLOT_BLOCK = 256


def _one_hot_all(slot_ref, aff_ref, n, cap):
    srow = lax.broadcasted_iota(jnp.int32, (cap, n), 0).astype(F32)
    ohs, gates = [], []
    for e in range(N_EXPERTS):
        hit = slot_ref[e] == srow
        ohs.append(jnp.where(hit, 1.0, 0.0).astype(BF16))
        if aff_ref is not None:
            gates.append(jnp.sum(jnp.where(hit, aff_ref[e], 0.0), axis=1, keepdims=True))
    return jnp.concatenate(ohs, axis=0), (jnp.concatenate(gates, axis=0) if gates else None)


def _gather_all_kernel(hb_ref, slot_ref, aff_ref, xs_ref, gate_ref, *, n, cap):
    oh, gate = _one_hot_all(slot_ref, aff_ref, n, cap)
    xs = jnp.dot(oh, hb_ref[...], preferred_element_type=F32)
    xs_ref[...] = xs.astype(BF16).reshape(N_EXPERTS, cap, D_MODEL)
    gate_ref[...] = gate.reshape(N_EXPERTS, cap, 1)


def _gather_kernel(hb_ref, slot_ref, aff_ref, xs_ref, gate_ref, acc, gacc, *, n, cap):
    NC = min(512, n)
    SB = min(EC_SLOT_BLOCK // 2, cap)
    acc[...] = jnp.zeros_like(acc)
    gacc[...] = jnp.zeros_like(gacc)
    srow = lax.broadcasted_iota(jnp.int32, (SB, NC), 0).astype(F32)
    for c in range(n // NC):
        sl = slot_ref[:, c * NC:(c + 1) * NC]
        first, count = _slot_block_range(sl, cap, SB)
        for j in range(min(cap // SB, NC // SB + 1)):
            @pl.when(j < count)
            def _(c=c, j=j, sl=sl, first=first):
                base = pl.multiple_of((first + j) * SB, SB)
                hit = (sl - base.astype(F32)) == srow
                oh = jnp.where(hit, 1.0, 0.0).astype(BF16)
                acc[pl.ds(base, SB), :] += jnp.dot(oh, hb_ref[c * NC:(c + 1) * NC, :], preferred_element_type=F32)
                gacc[pl.ds(base, SB), :] += jnp.sum(jnp.where(hit, aff_ref[:, c * NC:(c + 1) * NC], 0.0), axis=1,
                                                    keepdims=True)
    xs_ref[...] = acc[...].astype(BF16)
    gate_ref[...] = gacc[...]


def _gather(hb, slot4, aff4, B, n, cap):
    E = N_EXPERTS
    if E * cap <= 512:
        all_spec = pl.BlockSpec((None, E, 1, n), lambda b: (b, 0, 0, 0))
        return pl.pallas_call(
            functools.partial(_gather_all_kernel, n=n, cap=cap),
            grid=(B,),
            in_specs=[pl.BlockSpec((n, D_MODEL), lambda b: (b, 0)), all_spec, all_spec],
            out_specs=(pl.BlockSpec((None, E, cap, D_MODEL), lambda b: (b, 0, 0, 0)),
                       pl.BlockSpec((None, E, cap, 1), lambda b: (b, 0, 0, 0))),
            out_shape=(jax.ShapeDtypeStruct((B, E, cap, D_MODEL), BF16), jax.ShapeDtypeStruct((B, E, cap, 1), F32)),
            compiler_params=_cparams(("parallel",)),
            name="ec_gather_all",
        )(hb, slot4, aff4)
    return pl.pallas_call(
        functools.partial(_gather_kernel, n=n, cap=cap),
        grid=(B, E),
        in_specs=[
            pl.BlockSpec((n, D_MODEL), lambda b, e: (b, 0)),
            pl.BlockSpec((None, None, 1, n), lambda b, e: (b, e, 0, 0)),
            pl.BlockSpec((None, None, 1, n), lambda b, e: (b, e, 0, 0)),
        ],
        out_specs=(pl.BlockSpec((None, None, cap, D_MODEL), lambda b, e: (b, e, 0, 0)),
                   pl.BlockSpec((None, None, cap, 1), lambda b, e: (b, e, 0, 0))),
        out_shape=(jax.ShapeDtypeStruct((B, E, cap, D_MODEL), BF16), jax.ShapeDtypeStruct((B, E, cap, 1), F32)),
        scratch_shapes=[pltpu.VMEM((cap, D_MODEL), F32), pltpu.VMEM((cap, 1), F32)],
        compiler_params=_cparams(("parallel", "arbitrary")),
        name="ec_gather",
    )(hb, slot4, aff4)


def _ffn_kernel(xs_ref, gate_ref, w1_ref, w3_ref, w2_ref, o_ref):
    bg, cap, _ = xs_ref.shape
    x = xs_ref[...].reshape(bg * cap, D_MODEL)
    h1 = jnp.dot(x, w1_ref[...], preferred_element_type=F32)
    h3 = jnp.dot(x, w3_ref[...], preferred_element_type=F32)
    hid = (h1 * _sigmoid(h1) * h3).astype(BF16)
    o = jnp.dot(hid, w2_ref[...], preferred_element_type=F32) * gate_ref[...].reshape(bg * cap, 1)
    o_ref[...] = o.astype(BF16).reshape(bg, cap, D_MODEL)


def _expert_ffn(xs, gate, w1, w3, w2, bg):
    B, E, cap, _ = xs.shape
    return pl.pallas_call(
        _ffn_kernel,
        grid=(E, B // bg),
        in_specs=[
            pl.BlockSpec((bg, None, cap, D_MODEL), lambda e, b: (b, e, 0, 0)),
            pl.BlockSpec((bg, None, cap, 1), lambda e, b: (b, e, 0, 0)),
            pl.BlockSpec((None, D_MODEL, D_EXPERT), lambda e, b: (e, 0, 0)),
            pl.BlockSpec((None, D_MODEL, D_EXPERT), lambda e, b: (e, 0, 0)),
            pl.BlockSpec((None, D_EXPERT, D_MODEL), lambda e, b: (e, 0, 0)),
        ],
        out_specs=pl.BlockSpec((bg, None, cap, D_MODEL), lambda e, b: (b, e, 0, 0)),
        out_shape=jax.ShapeDtypeStruct((B, E, cap, D_MODEL), BF16),
        compiler_params=_cparams(("parallel", "parallel")),
        name="ec_ffn",
    )(xs, gate, w1, w3, w2)


_TN_DIMS = (((0,), (0,)), ((), ()))


def _ec_residual(x1_ref, mod_ref, fg_ref, out_ref, ffn, final):
    x2 = x1_ref[...] + mod_ref[:, 5 * D_MODEL:6 * D_MODEL] * ffn
    if final:
        x2 = x2 * lax.rsqrt(jnp.mean(x2 * x2, axis=-1, keepdims=True) + EPS) * fg_ref[...]
    out_ref[...] = x2


def _scatter_all_kernel(slot_ref, o_ref, x1_ref, mod_ref, fg_ref, out_ref, *, n, cap, final):
    oh, _ = _one_hot_all(slot_ref, None, n, cap)
    ffn = lax.dot_general(oh, o_ref[...].reshape(N_EXPERTS * cap, D_MODEL), _TN_DIMS, preferred_element_type=F32)
    _ec_residual(x1_ref, mod_ref, fg_ref, out_ref, ffn, final)


def _scatter_kernel(slot_ref, o_ref, x1_ref, mod_ref, fg_ref, out_ref, acc, *, cap, final):
    e = pl.program_id(2)
    tn = x1_ref.shape[0]

    @pl.when(e == 0)
    def _():
        acc[...] = jnp.zeros_like(acc)

    sl = slot_ref[...]
    SB = min(EC_SLOT_BLOCK, cap)
    srow = lax.broadcasted_iota(jnp.int32, (SB, tn), 0).astype(F32)
    first, count = _slot_block_range(sl, cap, SB)
    for j in range(min(cap // SB, tn // SB + 1)):
        @pl.when(j < count)
        def _(j=j):
            base = pl.multiple_of((first + j) * SB, SB)
            oh = jnp.where((sl - base.astype(F32)) == srow, 1.0, 0.0).astype(BF16)
            acc[...] += lax.dot_general(oh, o_ref[pl.ds(base, SB), :], _TN_DIMS, preferred_element_type=F32)

    @pl.when(e == N_EXPERTS - 1)
    def _():
        _ec_residual(x1_ref, mod_ref, fg_ref, out_ref, acc[...], final)


def _scatter(slot4, o, x1, mod3, mod_row, final_g, B, n, cap, final):
    E = N_EXPERTS
    if E * cap <= 512:
        return pl.pallas_call(
            functools.partial(_scatter_all_kernel, n=n, cap=cap, final=final),
            grid=(B,),
            in_specs=[
                pl.BlockSpec((None, E, 1, n), lambda b: (b, 0, 0, 0)),
                pl.BlockSpec((None, E, cap, D_MODEL), lambda b: (b, 0, 0, 0)),
                pl.BlockSpec((n, D_MODEL), lambda b: (b, 0)),
                pl.BlockSpec((None, 1, 6 * D_MODEL), lambda b: (mod_row(b), 0, 0)),
                _full((1, D_MODEL)),
            ],
            out_specs=pl.BlockSpec((n, D_MODEL), lambda b: (b, 0)),
            out_shape=jax.ShapeDtypeStruct((B * n, D_MODEL), F32),
            compiler_params=_cparams(("parallel",)),
            name="ec_scatter_all",
        )(slot4, o, x1, mod3, final_g.reshape(1, D_MODEL))
    tn = min(1024, n)
    nt = n // tn
    return pl.pallas_call(
        functools.partial(_scatter_kernel, cap=cap, final=final),
        grid=(B, nt, N_EXPERTS),
        in_specs=[
            pl.BlockSpec((None, None, 1, tn), lambda b, t, e: (b, e, 0, t)),
            pl.BlockSpec((None, None, cap, D_MODEL), lambda b, t, e: (b, e, 0, 0)),
            pl.BlockSpec((tn, D_MODEL), lambda b, t, e: (b * nt + t, 0)),
            pl.BlockSpec((None, 1, 6 * D_MODEL), lambda b, t, e: (mod_row(b), 0, 0)),
            _full((1, D_MODEL)),
        ],
        out_specs=pl.BlockSpec((tn, D_MODEL), lambda b, t, e: (b * nt + t, 0)),
        out_shape=jax.ShapeDtypeStruct((B * n, D_MODEL), F32),
        scratch_shapes=[pltpu.VMEM((tn, D_MODEL), F32)],
        compiler_params=_cparams(("parallel", "parallel", "arbitrary")),
        name="ec_scatter",
    )(slot4, o, x1, mod3, final_g.reshape(1, D_MODEL))


def _expert_choice(hb, aff, x1, mod3, mod_row, final_g, w1, w3, w2, B, n, final):
    cap = EC_FACTOR * n // N_EXPERTS
    slot4, aff4 = _select(aff, B, n, cap)
    xs, gate = _gather(hb, slot4, aff4, B, n, cap)
    bg = max(1, min(B, 256 // cap))
    o = _expert_ffn(xs, gate, w1, w3, w2, bg)
    return _scatter(slot4, o, x1, mod3, mod_row, final_g, B, n, cap, final)


def kernel(x_prompt, x_sample, c, state_s5_re, state_s5_im, state_ssd, state_rwkv, c_ctx, ada_w, ada_b, norm1_g, norm2_g, w_in, w_out, s5_a_re, s5_a_im, s5_log_dt, s5_b_re, s5_b_im, s5_c_re, s5_c_im, s5_d, s5_glu_w, s5_glu_b, ssd_conv_w, ssd_conv_b, ssd_a_log, ssd_dt_bias, ssd_d, ssd_norm_g, rwkv_mu, rwkv_w0, rwkv_w_up, rwkv_a0, rwkv_a_up, rwkv_g_up, rwkv_k_k, rwkv_k_a, rwkv_r_k, rwkv_ln_g, rwkv_ln_b, router_w, exp_w1, exp_w3, exp_w2, final_g):
    Bp, Lp, D = x_prompt.shape
    Bs, Ls, _ = x_sample.shape
    depth = ada_w.shape[0]
    Np, Ns = Bp * Lp, Bs * Ls
    tm = TOKEN_BLOCK
    grid_rows = Ls // GRID_W
    nfs = S5_GROUPS // 8
    assert Lp % tm == 0 and Ls % tm == 0 and Lp % SSD_CHUNK == 0

    n_rows = 1 + Bs
    rows_pad = -(-n_rows // 8) * 8
    cond = jnp.zeros((rows_pad, D), F32).at[0].set(c_ctx).at[1:n_rows].set(c)
    mod = _modulation(cond, ada_w, ada_b)
    s_blocks_per_req = Ls // tm
    row_p = lambda i: 0
    row_s = lambda i: 1 + i // s_blocks_per_req

    ab_re, ab_im, bb_re, bb_im = _s5_discretize(s5_a_re, s5_a_im, s5_log_dt, s5_b_re, s5_b_im)
    w_in_pad = _pad_in_weight(w_in)
    w_out_b = w_out.astype(BF16)
    glu_w_b = s5_glu_w.astype(BF16)
    exp_w1_b, exp_w3_b, exp_w2_b = exp_w1.astype(BF16), exp_w3.astype(BF16), exp_w2.astype(BF16)
    seg = _segment_ones(RWKV_WIDTH, RWKV_HEADSIZE)

    xp = x_prompt.reshape(Np, D)
    xs = x_sample.reshape(Ns, D)
    new_s5_re, new_s5_im, new_ssd, new_rwkv = [], [], [], []
    QP = 2 if Bp % 2 == 0 and Bp >= 2 else 1
    RP = Bp // QP
    assert Bs * HEADS_PAD <= CH

    for l in range(depth):
        lp = {
            'rwkv_mu': rwkv_mu[l], 'rwkv_w0': rwkv_w0[l], 'rwkv_w_up': rwkv_w_up[l], 'rwkv_a0': rwkv_a0[l],
            'rwkv_a_up': rwkv_a_up[l], 'rwkv_g_up': rwkv_g_up[l], 'rwkv_k_k': rwkv_k_k[l], 'rwkv_k_a': rwkv_k_a[l],
            'rwkv_r_k': rwkv_r_k[l].reshape(-1), 'rwkv_ln_g': rwkv_ln_g[l], 'rwkv_ln_b': rwkv_ln_b[l],
            's5_glu_w': glu_w_b[l], 's5_glu_b': s5_glu_b[l], 'ssd_norm_g': ssd_norm_g[l], 'w_out': w_out_b[l],
            'norm2_g': norm2_g[l], 'router_w': router_w[l], 'seg': seg,
        }
        mod3 = mod[l].reshape(rows_pad, 1, 6 * D)
        us5_p, z_p, xbc_p, dt_p, rkv_p, lo_p = _in_projection(xp, mod3, row_p, norm1_g[l], w_in_pad[l])
        us5_s, z_s, xbc_s, dt_s, rkv_s, lo_s = _in_projection(xs, mod3, row_s, norm1_g[l], w_in_pad[l])

        tables = _s5_layer_tables(ab_re[l], ab_im[l], bb_re[l], bb_im[l], s5_c_re[l], s5_c_im[l])
        d_row = s5_d[l].reshape(1, S5_WIDTH)
        up = us5_p.reshape(QP, RP, Lp, S5_WIDTH).transpose(0, 2, 1, 3).reshape(QP, Lp * RP, S5_WIDTH)
        yp, hfin = _s5_scan(up, None, tables, d_row, R=RP, n_slab=Lp, chained=False, want_final=True)
        ys5_p = yp.reshape(QP, Lp, RP, S5_WIDTH).transpose(0, 2, 1, 3).reshape(Np, S5_WIDTH)
        hf = hfin.transpose(0, 3, 1, 2, 4).reshape(Bp, N_DIR, nfs, 2, 8, S5_STATE)
        new_s5_re.append(hf[:, :, :, 0].reshape(Bp, N_DIR, S5_GROUPS, S5_STATE))
        new_s5_im.append(hf[:, :, :, 1].reshape(Bp, N_DIR, S5_GROUPS, S5_STATE))
        h0 = jnp.concatenate([state_s5_re[:, l].reshape(Bs, N_DIR, nfs, 1, S5_SLICE_ST),
                              state_s5_im[:, l].reshape(Bs, N_DIR, nfs, 1, S5_SLICE_ST)], axis=-1)
        (ysm,) = _s5_scan(us5_s.reshape(Bs, Ls, S5_WIDTH), h0, tables, d_row, R=GRID_W, n_slab=grid_rows,
                          chained=True, want_final=False)
        ys5_s = ysm.reshape(Ns, S5_WIDTH)

        ssd_args = (ssd_conv_w[l], ssd_conv_b[l], ssd_dt_bias[l], ssd_a_log[l], ssd_d[l])
        *yssd_p, hssd = _ssd_scan(xbc_p, dt_p, None, *ssd_args, B=Bp, L=Lp, want_final=True)
        new_ssd.append(hssd.transpose(0, 1, 2, 4, 3))
        yssd_s = _ssd_scan(xbc_s, dt_s, state_ssd[:, l].transpose(0, 1, 2, 4, 3), *ssd_args,
                           B=Bs, L=Ls, want_final=False)

        r_, w0_, w1_, k_, v_, kk_, b_, g_p, bonus_p = _rwkv_prep(rkv_p, lo_p, lp, Lp // tm)
        zero_state = jnp.zeros((Bp, N_DIR, RWKV_HEADS, RWKV_HEADSIZE, RWKV_HEADSIZE), F32)
        y_, sfin = _rwkv_scan(_to_scan_k(r_, r_, Bp, Lp, 1), _to_scan_k(w0_, w1_, Bp, Lp, 1),
                              _to_scan_k(k_, k_, Bp, Lp, 1), _to_scan_k(kk_, kk_, Bp, Lp, 1),
                              _to_scan_k(b_, b_, Bp, Lp, 1), _to_scan_v(v_, Bp, Lp, 1),
                              _state_to_scan(zero_state, 1))
        rf_p, rb_p = _from_scan_v(y_, Bp, Lp, 1)
        new_rwkv.append(_state_from_scan(sfin, Bp, 1))

        r_, w0_, w1_, k_, v_, kk_, b_, g_s, bonus_s = _rwkv_prep(rkv_s, lo_s, lp, Ls // tm, SCAN_WIDTH)
        sc = lambda t: _to_scan2_k(t, Bs, Ls)
        yf_, yb_ = _rwkv_scan2(sc(r_), sc(w0_), sc(w1_), sc(k_), sc(kk_), sc(b_), _to_scan2_v(v_, Bs, Ls),
                               _state_to_scan2(state_rwkv[:, l]))
        rf_s = _from_scan2(yf_, Bs, Ls)
        rb_s = _from_scan2(yb_, Bs, Ls)

        x1_p, hb_p, aff_p = _post_mixer(xp, mod3, row_p, ys5_p, yssd_p, z_p, rf_p, rb_p, bonus_p, g_p, lp)
        x1_s, hb_s, aff_s = _post_mixer(xs, mod3, row_s, ys5_s, yssd_s, z_s, rf_s, rb_s, bonus_s, g_s, lp)

        final = l == depth - 1
        ew = (exp_w1_b[l], exp_w3_b[l], exp_w2_b[l])
        xp = _expert_choice(hb_p, aff_p, x1_p, mod3, lambda b: 0, final_g, *ew, Bp, Lp, final)
        xs = _expert_choice(hb_s, aff_s, x1_s, mod3, lambda b: 1 + b, final_g, *ew, Bs, Ls, final)

    y_prompt = xp.reshape(Bp, Lp, D)
    y_sample = xs.reshape(Bs, Ls, D)
    return (y_prompt, y_sample, jnp.stack(new_s5_re, axis=1), jnp.stack(new_s5_im, axis=1),
            jnp.stack(new_ssd, axis=1), jnp.stack(new_rwkv, axis=1))
```

```python
import functools
import math

import jax
import jax.numpy as jnp
from jax import lax
from jax.experimental import pallas as pl
from jax.experimental.pallas import tpu as pltpu

F32 = jnp.float32
BF16 = jnp.bfloat16
HIGHEST = lax.Precision.HIGHEST

D_MODEL = 1024
GRID_W = 64
N_DIR = 2
EPS = 1e-6
S5_WIDTH = 256
S5_CH = 16
S5_GROUPS = 16
S5_STATE = 64
SSD_HEADDIM = 64
SSD_HEADS = 6
SSD_WIDTH = 384
SSD_GROUPS = 2
SSD_STATE = 64
SSD_BC = 128
SSD_CONV_CH = 640
SSD_CHUNK = 128
RWKV_HEADSIZE = 64
RWKV_HEADS = 6
RWKV_WIDTH = 384
W_RANK = 32
A_RANK = 32
G_RANK = 64
DECAY_SCALE = math.exp(-0.5)
GN_EPS = 64e-5
N_EXPERTS = 16
D_EXPERT = 512
EC_FACTOR = 2
IN_SIZES = (S5_WIDTH, SSD_WIDTH, SSD_CONV_CH, N_DIR * SSD_HEADS, 3 * RWKV_WIDTH, N_DIR * W_RANK, A_RANK, G_RANK)

LANES = 128
TOKEN_BLOCK = 256
DT_PAD = 128
LO_PAD = 256
S5_SLICE_CH = 128
S5_SLICE_ST = 512
VMEM_LIMIT = 56 * 1024 * 1024


def _cparams(sem):
    return pltpu.CompilerParams(dimension_semantics=sem, vmem_limit_bytes=VMEM_LIMIT)


def _full(shape):
    nd = len(shape)
    return pl.BlockSpec(shape, lambda *_: (0,) * nd)


def _sigmoid(x):
    return 1.0 / (1.0 + jnp.exp(-x))


def _softplus(x):
    return jnp.maximum(x, 0.0) + jnp.log1p(jnp.exp(-jnp.abs(x)))


def _split_bf16(x):
    hi = x.astype(BF16)
    lo = (x - hi.astype(F32)).astype(BF16)
    return hi, lo


def _dot_exact_rhs(x, m, dims=None):
    hi, lo = _split_bf16(x)
    mb = m.astype(BF16)
    if dims is None:
        return jnp.dot(hi, mb, preferred_element_type=F32) + jnp.dot(lo, mb, preferred_element_type=F32)
    return (lax.dot_general(hi, mb, dims, preferred_element_type=F32)
            + lax.dot_general(lo, mb, dims, preferred_element_type=F32))


def _dot_exact_lhs(m, x):
    hi, lo = _split_bf16(x)
    mb = m.astype(BF16)
    return jnp.dot(mb, hi, preferred_element_type=F32) + jnp.dot(mb, lo, preferred_element_type=F32)


def _dot3(x, w, dims=(((1,), (0,)), ((), ()))):
    xh, xl = _split_bf16(x)
    wh, wl = _split_bf16(w)
    dg = functools.partial(lax.dot_general, dimension_numbers=dims, preferred_element_type=F32)
    return dg(xh, wh) + (dg(xl, wh) + dg(xh, wl))


def _gelu_tanh(x):
    return 0.5 * x * (1.0 + jnp.tanh(math.sqrt(2.0 / math.pi) * (x + 0.044715 * (x * x * x))))


def _mod_kernel(c_ref, w_ref, b_ref, o_ref):
    c = c_ref[...]
    s = (c * _sigmoid(c)).astype(BF16)
    o_ref[...] = jnp.dot(s, w_ref[...].astype(BF16), preferred_element_type=F32) + b_ref[...]


def _modulation(cond, ada_w, ada_b):
    depth = ada_w.shape[0]
    rows = cond.shape[0]
    tn = 1536
    return pl.pallas_call(
        _mod_kernel,
        grid=(depth, 6 * D_MODEL // tn),
        in_specs=[
            pl.BlockSpec((rows, D_MODEL), lambda l, j: (0, 0)),
            pl.BlockSpec((None, D_MODEL, tn), lambda l, j: (l, 0, j)),
            pl.BlockSpec((None, 1, tn), lambda l, j: (l, 0, j)),
        ],
        out_specs=pl.BlockSpec((None, rows, tn), lambda l, j: (l, 0, j)),
        out_shape=jax.ShapeDtypeStruct((depth, rows, 6 * D_MODEL), F32),
        compiler_params=_cparams(("parallel", "parallel")),
        name="adaln_mod",
    )(cond, ada_w, ada_b.reshape(depth, 1, 6 * D_MODEL))


IN_PAD_SIZES = (S5_WIDTH, SSD_WIDTH, SSD_CONV_CH, DT_PAD, 3 * RWKV_WIDTH, LO_PAD)


def _inproj_kernel(x_ref, mod_ref, g_ref, w_ref, us5_ref, z_ref, xbc_ref, dt_ref, rkv_ref, lo_ref):
    x = x_ref[...]
    y = x * lax.rsqrt(jnp.mean(x * x, axis=-1, keepdims=True) + EPS) * g_ref[...]
    m = mod_ref[...]
    h = y * (1.0 + m[:, D_MODEL:2 * D_MODEL]) + m[:, 0:D_MODEL]
    p = jnp.dot(h.astype(BF16), w_ref[...], preferred_element_type=F32)
    start = 0
    for ref, size in zip((us5_ref, z_ref, xbc_ref, dt_ref, rkv_ref, lo_ref), IN_PAD_SIZES):
        ref[...] = p[:, start:start + size]
        start += size


def _pad_in_weight(w_in):
    parts, start = [], 0
    for s in IN_SIZES:
        parts.append(w_in[..., start:start + s])
        start += s
    us5, z, xbc, dt, rkv, wlo, alo, glo = parts
    zeros = lambda n: jnp.zeros(w_in.shape[:-1] + (n,), w_in.dtype)
    lo_used = wlo.shape[-1] + alo.shape[-1] + glo.shape[-1]
    return jnp.concatenate([us5, z, xbc, dt, zeros(DT_PAD - dt.shape[-1]), rkv, wlo, alo, glo,
                            zeros(LO_PAD - lo_used)], axis=-1).astype(BF16)


def _in_projection(x, mod3, row_of_block, norm_g, w_pad):
    ntok = x.shape[0]
    tm = TOKEN_BLOCK
    width = w_pad.shape[1]
    outs = tuple(jax.ShapeDtypeStruct((ntok, s), F32) for s in IN_PAD_SIZES)
    return pl.pallas_call(
        _inproj_kernel,
        grid=(ntok // tm,),
        in_specs=[
            pl.BlockSpec((tm, D_MODEL), lambda i: (i, 0)),
            pl.BlockSpec((None, 1, 6 * D_MODEL), lambda i: (row_of_block(i), 0, 0)),
            _full((1, D_MODEL)),
            _full((D_MODEL, width)),
        ],
        out_specs=tuple(pl.BlockSpec((tm, s), lambda i: (i, 0)) for s in IN_PAD_SIZES),
        out_shape=outs,
        compiler_params=_cparams(("parallel",)),
        name="in_proj",
    )(x, mod3, norm_g.reshape(1, D_MODEL), w_pad)


def _s5_disc_kernel(are_ref, aim_ref, ldt_ref, bre_ref, bim_ref, abre_ref, abim_ref, bbre_ref, bbim_ref):
    lam_re = jnp.minimum(are_ref[...], -1e-4)
    lam_im = aim_ref[...]
    dt = jnp.exp(ldt_ref[...])
    mag = jnp.exp(lam_re * dt)
    ab_re = mag * jnp.cos(lam_im * dt)
    ab_im = mag * jnp.sin(lam_im * dt)
    num_re, num_im = ab_re - 1.0, ab_im
    den = lam_re * lam_re + lam_im * lam_im
    q_re = (num_re * lam_re + num_im * lam_im) / den
    q_im = (num_im * lam_re - num_re * lam_im) / den
    abre_ref[...] = ab_re
    abim_ref[...] = ab_im
    b_re = bre_ref[...]
    b_im = bim_ref[...]
    qr = q_re[:, None, :]
    qi = q_im[:, None, :]
    bbre_ref[...] = qr * b_re - qi * b_im
    bbim_ref[...] = qr * b_im + qi * b_re


def _s5_discretize(a_re, a_im, log_dt, b_re, b_im):
    lead = a_re.shape[:3]
    n = lead[0] * lead[1] * lead[2]
    a2 = lambda t: t.reshape(n, S5_STATE)
    ldt = jnp.broadcast_to(log_dt.reshape(n, 1), (n, S5_STATE))
    b3 = lambda t: t.reshape(n, S5_STATE, S5_CH).transpose(0, 2, 1)
    ab_re, ab_im, bb_re, bb_im = pl.pallas_call(
        _s5_disc_kernel,
        out_shape=(jax.ShapeDtypeStruct((n, S5_STATE), F32), jax.ShapeDtypeStruct((n, S5_STATE), F32),
                   jax.ShapeDtypeStruct((n, S5_CH, S5_STATE), F32), jax.ShapeDtypeStruct((n, S5_CH, S5_STATE), F32)),
        name="s5_discretize",
    )(a2(a_re), a2(a_im), ldt, b3(b_re), b3(b_im))
    return (ab_re.reshape(lead + (S5_STATE,)), ab_im.reshape(lead + (S5_STATE,)),
            bb_re.reshape(lead + (S5_CH, S5_STATE)), bb_im.reshape(lead + (S5_CH, S5_STATE)))


def _s5_layer_tables(ab_re, ab_im, bb_re, bb_im, c_re, c_im):
    nfs = S5_GROUPS // 8
    eye = jnp.eye(8, dtype=F32)

    def rows(t):
        return t.reshape(N_DIR, nfs, 8 * S5_STATE)

    ab_row = jnp.concatenate([rows(ab_re), rows(ab_im)], axis=-1).reshape(N_DIR, nfs, 1, 2 * S5_SLICE_ST)

    def bmat(t):
        t = t.reshape(N_DIR, nfs, 8, S5_CH, S5_STATE)
        return jnp.einsum('dfghp,gk->dfghkp', t, eye).reshape(N_DIR, nfs, S5_SLICE_CH, S5_SLICE_ST)

    b_mat = jnp.concatenate([bmat(bb_re), bmat(bb_im)], axis=-1).astype(BF16)

    def cmat(t):
        t = t.reshape(N_DIR, nfs, 8, S5_CH, S5_STATE)
        return jnp.einsum('dfghp,gk->dfgpkh', t, eye).reshape(N_DIR, nfs, S5_SLICE_ST, S5_SLICE_CH)

    c_mat = jnp.concatenate([cmat(c_re), -cmat(c_im)], axis=-2).astype(BF16)
    return ab_row, b_mat, c_mat


def _s5_kernel(*refs, R, n_slab, chained, want_final):
    if chained:
        u_ref, h0_ref, ab_ref, bm_ref, cm_ref, d_ref = refs[:6]
        rest = refs[6:]
    else:
        u_ref, ab_ref, bm_ref, cm_ref, d_ref = refs[:5]
        h0_ref = None
        rest = refs[5:]
    y_ref = rest[0]
    rest = rest[1:]
    if want_final:
        hfin_ref = rest[0]
        rest = rest[1:]
    H = rest[0]
    if chained:
        PW, CIN = rest[1], rest[2]
    NR = R * n_slab
    RC = min(512, NR)
    ST = S5_SLICE_ST
    nchunk = ST // LANES

    y_ref[...] = u_ref[...] * d_ref[...]

    for d in range(N_DIR):
        def slab_of(i, d=d):
            return i if d == 0 else n_slab - 1 - i

        def bu_body(i, _, d=d):
            r0 = pl.multiple_of(i * RC, RC)
            H[pl.ds(r0, RC), :] = jnp.dot(u_ref[pl.ds(r0, RC), :].astype(BF16), bm_ref[d],
                                           preferred_element_type=F32)
            return 0
        lax.fori_loop(0, NR // RC, bu_body, 0)

        for c in range(nchunk):
            lre = slice(c * LANES, (c + 1) * LANES)
            lim = slice(ST + c * LANES, ST + (c + 1) * LANES)
            a_re = jnp.broadcast_to(ab_ref[d, :, lre], (R, LANES))
            a_im = jnp.broadcast_to(ab_ref[d, :, lim], (R, LANES))

            def step(i, carry, lre=lre, lim=lim, a_re=a_re, a_im=a_im, slab_of=slab_of):
                cr, ci = carry
                r0 = pl.multiple_of(slab_of(i) * R, R)
                nr = a_re * cr - a_im * ci + H[pl.ds(r0, R), lre]
                ni = a_re * ci + a_im * cr + H[pl.ds(r0, R), lim]
                H[pl.ds(r0, R), lre] = nr
                H[pl.ds(r0, R), lim] = ni
                return nr, ni
            zero = jnp.zeros((R, LANES), F32)
            lax.fori_loop(0, n_slab, step, (zero, zero))

        last0 = (n_slab - 1) * R if d == 0 else 0
        if chained:
            a_re_row = ab_ref[d, :, 0:ST]
            a_im_row = ab_ref[d, :, ST:2 * ST]

            def pw_step(j, carry, a_re_row=a_re_row, a_im_row=a_im_row):
                pr, pi = carry
                PW[j, :, 0:ST] = jnp.broadcast_to(pr, (8, ST))
                PW[j, :, ST:2 * ST] = jnp.broadcast_to(pi, (8, ST))
                return pr * a_re_row - pi * a_im_row, pr * a_im_row + pi * a_re_row
            lax.fori_loop(0, n_slab, pw_step, (a_re_row, a_im_row))
            t_re = PW[n_slab - 1, 0:1, 0:ST]
            t_im = PW[n_slab - 1, 0:1, ST:2 * ST]

            cr = h0_ref[d, :, 0:ST]
            ci = h0_ref[d, :, ST:2 * ST]
            for i in range(R):
                c = i if d == 0 else R - 1 - i
                CIN[c:c + 1, 0:ST] = cr
                CIN[c:c + 1, ST:2 * ST] = ci
                er = H[last0 + c:last0 + c + 1, 0:ST]
                ei = H[last0 + c:last0 + c + 1, ST:2 * ST]
                cr, ci = t_re * cr - t_im * ci + er, t_re * ci + t_im * cr + ei

            def fix_step(i, _, slab_of=slab_of):
                r0 = pl.multiple_of(slab_of(i) * R, R)
                p = PW[i]
                for c in range(nchunk):
                    lre = slice(c * LANES, (c + 1) * LANES)
                    lim = slice(ST + c * LANES, ST + (c + 1) * LANES)
                    pr = p[0:1, lre]
                    pi = p[0:1, lim]
                    cr = CIN[:, lre]
                    ci = CIN[:, lim]
                    H[pl.ds(r0, R), lre] = H[pl.ds(r0, R), lre] + (pr * cr - pi * ci)
                    H[pl.ds(r0, R), lim] = H[pl.ds(r0, R), lim] + (pr * ci + pi * cr)
                return 0
            lax.fori_loop(0, n_slab, fix_step, 0)

        if want_final:
            hfin_ref[d] = H[last0:last0 + R, :]

        def y_body(i, _, d=d):
            r0 = pl.multiple_of(i * RC, RC)
            y_ref[pl.ds(r0, RC), :] = y_ref[pl.ds(r0, RC), :] + jnp.dot(
                H[pl.ds(r0, RC), :].astype(BF16), cm_ref[d], preferred_element_type=F32)
            return 0
        lax.fori_loop(0, NR // RC, y_body, 0)


def _s5_scan(u, h0, tables, d_row, *, R, n_slab, chained, want_final):
    ab_row, b_mat, c_mat = tables
    Q, NR, _ = u.shape
    nfs = S5_GROUPS // 8
    W2 = 2 * S5_SLICE_ST
    in_specs = [pl.BlockSpec((None, NR, S5_SLICE_CH), lambda q, f: (q, 0, f))]
    args = [u]
    if chained:
        in_specs.append(pl.BlockSpec((None, N_DIR, None, 1, W2), lambda q, f: (q, 0, f, 0, 0)))
        args.append(h0)
    in_specs += [
        pl.BlockSpec((N_DIR, None, 1, W2), lambda q, f: (0, f, 0, 0)),
        pl.BlockSpec((N_DIR, None, S5_SLICE_CH, W2), lambda q, f: (0, f, 0, 0)),
        pl.BlockSpec((N_DIR, None, W2, S5_SLICE_CH), lambda q, f: (0, f, 0, 0)),
        pl.BlockSpec((1, S5_SLICE_CH), lambda q, f: (0, f)),
    ]
    args += [ab_row, b_mat, c_mat, d_row]
    out_shape = [jax.ShapeDtypeStruct((Q, NR, S5_WIDTH), F32)]
    out_specs = [pl.BlockSpec((None, NR, S5_SLICE_CH), lambda q, f: (q, 0, f))]
    if want_final:
        out_shape.append(jax.ShapeDtypeStruct((Q, N_DIR, nfs, R, W2), F32))
        out_specs.append(pl.BlockSpec((None, N_DIR, None, R, W2), lambda q, f: (q, 0, f, 0, 0)))
    scratch = [pltpu.VMEM((NR, W2), F32)]
    if chained:
        scratch += [pltpu.VMEM((n_slab, 8, W2), F32), pltpu.VMEM((R, W2), F32)]
    res = pl.pallas_call(
        functools.partial(_s5_kernel, R=R, n_slab=n_slab, chained=chained, want_final=want_final),
        grid=(Q, nfs),
        in_specs=in_specs,
        out_specs=tuple(out_specs),
        out_shape=tuple(out_shape),
        scratch_shapes=scratch,
        compiler_params=_cparams(("parallel", "parallel")),
        name="s5_scan_chained" if chained else "s5_scan",
    )(*args)
    return res


def _ssd_kernel(*refs, TB, nb, has_h0, want_final):
    io = [refs[4 * d:4 * d + 4] for d in range(N_DIR)]
    refs = refs[8:]
    if has_h0:
        h0_ref = refs[0]
        refs = refs[1:]
    cw_ref, cb_ref, dtb_ref, arow_ref, sel_ref, drow_ref = refs[:6]
    refs = refs[6:]
    y_refs = refs[:2]
    refs = refs[2:]
    if want_final:
        hfin_ref = refs[0]
        refs = refs[1:]
    hst, xc_s, dt_s = refs
    CH = SSD_CHUNK
    P = SSD_HEADDIM
    j = pl.program_id(1)

    @pl.when(j == 0)
    def _():
        if has_h0:
            hst[...] = h0_ref[...]
        else:
            hst[...] = jnp.zeros_like(hst)

    li = lax.broadcasted_iota(jnp.int32, (CH, CH), 0)
    si = lax.broadcasted_iota(jnp.int32, (CH, CH), 1)
    tmats = ((si <= li).astype(F32), (si >= li).astype(F32))
    rows = lax.broadcasted_iota(jnp.int32, (TB, 1), 0)
    for d in range(N_DIR):
        xbc_ref, xp_ref, xn_ref, dt_ref = io[d]
        jj = j if d == 0 else nb - 1 - j
        x = xbc_ref[...]
        prev_row = xp_ref[7:8, :] * (jj > 0).astype(F32)
        next_row = xn_ref[0:1, :] * (jj < nb - 1).astype(F32)
        x_prev = jnp.where(rows == 0, prev_row, pltpu.roll(x, 1, 0))
        x_next = jnp.where(rows == TB - 1, next_row, pltpu.roll(x, TB - 1, 0))
        conv = cw_ref[0:1, :] * x_prev + cw_ref[1:2, :] * x + cw_ref[2:3, :] * x_next + cb_ref[...]
        xc_s[d] = conv * _sigmoid(conv)
        dtf = _softplus(dt_ref[...] + dtb_ref[...])
        dt_s[d] = _dot_exact_rhs(dtf, sel_ref[d])

    hrow = lax.broadcasted_iota(jnp.int32, (LANES, SSD_WIDTH), 0)
    e_head = jnp.where(lax.broadcasted_iota(jnp.int32, (LANES, SSD_WIDTH), 1) // P == hrow, 1.0, 0.0)
    hrow2 = lax.broadcasted_iota(jnp.int32, (LANES, SSD_HEADS * CH), 0)
    e_chunk = jnp.where(lax.broadcasted_iota(jnp.int32, (LANES, SSD_HEADS * CH), 1) // CH == hrow2, 1.0, 0.0)

    n_ch = TB // CH
    for i, d in [(i, d) for i in range(n_ch) for d in range(N_DIR)]:
        tmat = tmats[d]
        y_ref = y_refs[d]
        r0 = (i if d == 0 else n_ch - 1 - i) * CH
        dtc = dt_s[d, r0:r0 + CH, :]
        dA = dtc * arow_ref[d]
        cs = _dot_exact_lhs(tmat, dA)
        csT = cs.T
        dt_x = _dot_exact_rhs(dtc, e_head)
        cs_x = _dot_exact_rhs(cs, e_head)
        tot_x = jnp.sum(_dot_exact_rhs(dA, e_head), axis=0, keepdims=True)
        cs_xx = _dot_exact_rhs(cs, e_chunk)
        xs_all = xc_s[d, r0:r0 + CH, 0:SSD_WIDTH]
        xdt_all = xs_all * dt_x
        xd_all = xdt_all * jnp.exp(tot_x - cs_x)
        ecs_all = jnp.exp(cs_x)
        etot_all = jnp.exp(tot_x)
        Bm = xc_s[d, r0:r0 + CH, SSD_WIDTH:SSD_WIDTH + SSD_BC]
        Cm = xc_s[d, r0:r0 + CH, SSD_WIDTH + SSD_BC:SSD_WIDTH + 2 * SSD_BC]
        BmT = Bm.T
        for g in range(SSD_GROUPS):
            Cg = Cm[:, g * SSD_STATE:(g + 1) * SSD_STATE].astype(BF16)
            Bg = Bm[:, g * SSD_STATE:(g + 1) * SSD_STATE].astype(BF16)
            BgT = BmT[g * SSD_STATE:(g + 1) * SSD_STATE, :].astype(BF16)
            G = lax.dot_general(Cg, Bg, (((1,), (1,)), ((), ())), preferred_element_type=F32)
            for hh in range(SSD_HEADS // SSD_GROUPS):
                h = g * (SSD_HEADS // SSD_GROUPS) + hh
                hl = slice(h * P, (h + 1) * P)
                row = csT[h:h + 1, :]
                lm = jnp.exp(jnp.where(tmat > 0.0, cs_xx[:, h * CH:(h + 1) * CH] - row, -1e30))
                hprev = hst[d, h]
                y = jnp.dot((G * lm).astype(BF16), xdt_all[:, hl].astype(BF16), preferred_element_type=F32)
                y = y + jnp.dot(Cg, hprev.astype(BF16), preferred_element_type=F32) * ecs_all[:, hl]
                if d == 0:
                    y = y + drow_ref[:, hl] * xs_all[:, hl]
                y_ref[r0:r0 + CH, hl] = y
                hst[d, h] = etot_all[:, hl] * hprev + jnp.dot(BgT, xd_all[:, hl].astype(BF16),
                                                             preferred_element_type=F32)

    if want_final:
        @pl.when(j == nb - 1)
        def _():
            hfin_ref[...] = hst[...]


def _ssd_scan(xbc, dt, h0, conv_w, conv_b, dt_bias, a_log, d_skip, *, B, L, want_final):
    TB = min(512, L)
    nb = L // TB
    H = SSD_HEADS

    nrow8 = B * L // 8
    in_specs, args = [], []
    blks = (lambda b, j: b * nb + j, lambda b, j: b * nb + nb - 1 - j)
    for blk in blks:
        in_specs += [
            pl.BlockSpec((TB, SSD_CONV_CH), lambda b, j, blk=blk: (blk(b, j), 0)),
            pl.BlockSpec((8, SSD_CONV_CH), lambda b, j, blk=blk: (jnp.maximum(blk(b, j) * (TB // 8) - 1, 0), 0)),
            pl.BlockSpec((8, SSD_CONV_CH),
                         lambda b, j, blk=blk: (jnp.minimum((blk(b, j) + 1) * (TB // 8), nrow8 - 1), 0)),
            pl.BlockSpec((TB, DT_PAD), lambda b, j, blk=blk: (blk(b, j), 0)),
        ]
        args += [xbc, xbc, xbc, dt]
    if h0 is not None:
        in_specs.append(pl.BlockSpec((None, N_DIR, H, SSD_STATE, SSD_HEADDIM), lambda b, j: (b, 0, 0, 0, 0)))
        args.append(h0)
    dtb = jnp.pad(dt_bias.reshape(1, N_DIR * H), ((0, 0), (0, DT_PAD - N_DIR * H)))
    arow = jnp.pad(-jnp.exp(a_log), ((0, 0), (0, LANES - H))).reshape(N_DIR, 1, LANES)
    lane = jnp.arange(LANES)
    sel = jnp.stack([(lane[:, None] == (dd * H + lane[None, :])) & (lane[None, :] < H) for dd in range(N_DIR)]).astype(F32)
    drow = jnp.repeat(d_skip, SSD_HEADDIM).reshape(1, SSD_WIDTH)
    in_specs += [
        _full((3, SSD_CONV_CH)), _full((1, SSD_CONV_CH)), _full((1, DT_PAD)),
        _full((N_DIR, 1, LANES)), _full((N_DIR, LANES, LANES)), _full((1, SSD_WIDTH)),
    ]
    args += [conv_w, conv_b.reshape(1, SSD_CONV_CH), dtb, arow, sel, drow]
    out_shape = [jax.ShapeDtypeStruct((B * L, SSD_WIDTH), F32)] * N_DIR
    out_specs = [pl.BlockSpec((TB, SSD_WIDTH), lambda b, j, blk=blk: (blk(b, j), 0)) for blk in blks]
    if want_final:
        out_shape.append(jax.ShapeDtypeStruct((B, N_DIR, H, SSD_STATE, SSD_HEADDIM), F32))
        out_specs.append(pl.BlockSpec((None, N_DIR, H, SSD_STATE, SSD_HEADDIM), lambda b, j: (b, 0, 0, 0, 0)))
    return pl.pallas_call(
        functools.partial(_ssd_kernel, TB=TB, nb=nb, has_h0=h0 is not None, want_final=want_final),
        grid=(B, nb),
        in_specs=in_specs,
        out_specs=tuple(out_specs),
        out_shape=tuple(out_shape),
        scratch_shapes=[pltpu.VMEM((N_DIR, H, SSD_STATE, SSD_HEADDIM), F32),
                        pltpu.VMEM((N_DIR, TB, SSD_CONV_CH), F32), pltpu.VMEM((N_DIR, TB, DT_PAD), F32)],
        compiler_params=_cparams(("parallel", "arbitrary")),
        name="ssd_scan",
    )(*args)


def _rwkv_prep_kernel(rkv_ref, rp_ref, rn_ref, lo_ref, mu_ref, a0_ref, aup_ref, gup_ref, w0_ref, wup_ref,
                      kkw_ref, ka_ref, rk_ref, seg_ref,
                      r_ref, w0o_ref, w1o_ref, k_ref, v_ref, kk_ref, b_ref, g_ref, bonus_ref, *, nbs):
    tm = rkv_ref.shape[0]
    W = RWKV_WIDTH
    jj = pl.program_id(0) % nbs
    x = rkv_ref[...]
    prev_row = rp_ref[7:8, :] * (jj > 0).astype(F32)
    next_row = rn_ref[0:1, :] * (jj < nbs - 1).astype(F32)
    rows = lax.broadcasted_iota(jnp.int32, (tm, 1), 0)
    xp = jnp.where(rows == 0, prev_row, pltpu.roll(x, 1, 0))
    xn = jnp.where(rows == tm - 1, next_row, pltpu.roll(x, tm - 1, 0))
    x = x + mu_ref[0:1, :] * (xp - x) + mu_ref[1:2, :] * (xn - x)
    r = x[:, 0:W]
    k = x[:, W:2 * W]
    v = x[:, 2 * W:3 * W]
    lo = lo_ref[...]
    seg = seg_ref[...]

    def put(ref, val):
        ref[:, 0:W] = val
        if ref.shape[1] > W:
            ref[:, W:] = jnp.zeros((tm, ref.shape[1] - W), F32)

    a = _sigmoid(a0_ref[...] + _dot3(lo, aup_ref[...]))
    g_ref[...] = _dot3(_sigmoid(lo), gup_ref[...])
    tlo = jnp.tanh(lo)
    for d, o_ref in enumerate((w0o_ref, w1o_ref)):
        zw = w0_ref[d:d + 1, :] + _dot3(tlo, wup_ref[d])
        put(o_ref, jnp.exp(-DECAY_SCALE * _sigmoid(zw)))
    kk = k * kkw_ref[...]
    kk = kk * lax.rsqrt(jnp.maximum(_dot_exact_rhs(kk * kk, seg), 1e-24))
    k2 = k * (1.0 + (a - 1.0) * ka_ref[...])
    put(r_ref, r)
    put(k_ref, k2)
    put(v_ref, v)
    put(kk_ref, kk)
    put(b_ref, kk * a)
    bonus_ref[...] = _dot_exact_rhs(r * k2 * rk_ref[...], seg) * v


def _segment_ones(width, seg):
    i = jnp.arange(width)
    return (i[:, None] // seg == i[None, :] // seg).astype(F32)


def _rwkv_prep(rkv, lo, lp, seq_blocks, scan_width=RWKV_WIDTH):
    ntok = rkv.shape[0]
    tm = TOKEN_BLOCK
    W = RWKV_WIDTH
    nrow8 = ntok // 8
    nbs_of = seq_blocks
    widths = (scan_width,) * 7 + (W, W)
    pad_rows = lambda t, r0: jnp.zeros((LO_PAD, W), F32).at[r0:r0 + t.shape[0]].set(t)
    aup = pad_rows(lp['rwkv_a_up'], N_DIR * W_RANK)
    gup = pad_rows(lp['rwkv_g_up'], N_DIR * W_RANK + A_RANK)
    wup = jnp.stack([pad_rows(lp['rwkv_w_up'][d], d * W_RANK) for d in range(N_DIR)])
    row = lambda t: t.reshape(1, W)
    outs = tuple(jax.ShapeDtypeStruct((ntok, w), F32) for w in widths)
    return pl.pallas_call(
        functools.partial(_rwkv_prep_kernel, nbs=nbs_of),
        grid=(ntok // tm,),
        in_specs=[
            pl.BlockSpec((tm, 3 * W), lambda i: (i, 0)),
            pl.BlockSpec((8, 3 * W), lambda i: (jnp.maximum(i * (tm // 8) - 1, 0), 0)),
            pl.BlockSpec((8, 3 * W), lambda i: (jnp.minimum((i + 1) * (tm // 8), nrow8 - 1), 0)),
            pl.BlockSpec((tm, LO_PAD), lambda i: (i, 0)),
            _full((2, 3 * W)), _full((1, W)), _full((LO_PAD, W)), _full((LO_PAD, W)), _full((2, W)),
            _full((N_DIR, LO_PAD, W)), _full((1, W)), _full((1, W)), _full((1, W)), _full((W, W)),
        ],
        out_specs=tuple(pl.BlockSpec((tm, w), lambda i: (i, 0)) for w in widths),
        out_shape=outs,
        compiler_params=_cparams(("parallel",)),
        name="rwkv_prep",
    )(rkv, rkv, rkv, lo, lp['rwkv_mu'], row(lp['rwkv_a0']), aup, gup, lp['rwkv_w0'], wup,
      row(lp['rwkv_k_k']), row(lp['rwkv_k_a']), row(lp['rwkv_r_k']), lp['seg'])


def _rwkv_scan_kernel(rf, rm, wf, wm, kf, km, kkf, kkm, bf, bm, vf, vm, mask_ref, s0_ref, yf_ref, yb_ref, sfin_ref,
                      S, SA, OPS, VTS, *, TBLK, V, nblk):
    K = RWKV_HEADSIZE
    VH = V // 2
    tb = pl.program_id(1)

    @pl.when(tb == 0)
    def _():
        S[...] = s0_ref[...]

    is_fwd = mask_ref[...] > 0.5
    fwd_k = jnp.broadcast_to(is_fwd, (K, LANES))
    fwd_v = jnp.broadcast_to(is_fwd, (V, LANES))

    def pick(t, _):
        tm_ = TBLK - 1 - t
        for j, (f, m) in enumerate(((rf, rm), (wf, wm), (kf, km), (kkf, kkm), (bf, bm))):
            OPS[t, j] = jnp.where(fwd_k, f[t], m[tm_])
        VTS[t] = jnp.where(fwd_v, vf[t], vm[tm_])
        return 0
    lax.fori_loop(0, TBLK, pick, 0)
    R_, W_, K_, KK_, B_ = range(5)

    def bc(j, t, kx):
        return jnp.broadcast_to(OPS[t, j, pl.ds(kx, 1), :], (VH, LANES))

    def tree(parts):
        return (parts[0] + parts[1]) + (parts[2] + parts[3])

    for hv in range(2):
        rows = pl.ds(hv * VH, VH)
        accs = [None] * 4
        for kx in range(K):
            p = S[kx, rows, :] * bc(KK_, 0, kx)
            accs[kx % 4] = p if accs[kx % 4] is None else accs[kx % 4] + p
        SA[hv] = tree(accs)

    def step(t, _):
        tn = jnp.minimum(t + 1, TBLK - 1)
        tm_ = TBLK - 1 - t
        for hv in range(2):
            rows = pl.ds(hv * VH, VH)
            sa = SA[hv]
            vt = VTS[t, rows, :]
            yacc = [None] * 4
            sacc = [None] * 4
            for kx in range(K):
                s_new = S[kx, rows, :] * bc(W_, t, kx) - sa * bc(B_, t, kx) + vt * bc(K_, t, kx)
                S[kx, rows, :] = s_new
                py = s_new * bc(R_, t, kx)
                ps = s_new * bc(KK_, tn, kx)
                a = kx % 4
                yacc[a] = py if yacc[a] is None else yacc[a] + py
                sacc[a] = ps if sacc[a] is None else sacc[a] + ps
            y = tree(yacc)
            yf_ref[t, rows, :] = y
            yb_ref[tm_, rows, :] = y
            SA[hv] = tree(sacc)
        return 0
    lax.fori_loop(0, TBLK, step, 0)

    @pl.when(tb == nblk - 1)
    def _():
        sfin_ref[...] = S[...]


def _rwkv_scan(r, w, k, kk, b, v, fwd_mask, s0):
    L, K, NL = r.shape
    V = v.shape[1]
    TBLK = min(32, L)
    nblk = L // TBLK
    ngrp = NL // LANES
    kf = pl.BlockSpec((TBLK, K, LANES), lambda g, t: (t, 0, g))
    km = pl.BlockSpec((TBLK, K, LANES), lambda g, t: (nblk - 1 - t, 0, g))
    vf = pl.BlockSpec((TBLK, V, LANES), lambda g, t: (t, 0, g))
    vm = pl.BlockSpec((TBLK, V, LANES), lambda g, t: (nblk - 1 - t, 0, g))
    sspec = pl.BlockSpec((K, V, LANES), lambda g, t: (0, 0, g))
    return pl.pallas_call(
        functools.partial(_rwkv_scan_kernel, TBLK=TBLK, V=V, nblk=nblk),
        grid=(ngrp, nblk),
        in_specs=[kf, km] * 5 + [vf, vm, pl.BlockSpec((1, LANES), lambda g, t: (0, g)), sspec],
        out_specs=(vf, vm, sspec),
        out_shape=(jax.ShapeDtypeStruct((L, V, NL), F32), jax.ShapeDtypeStruct((L, V, NL), F32),
                   jax.ShapeDtypeStruct((K, V, NL), F32)),
        scratch_shapes=[pltpu.VMEM((K, V, LANES), F32), pltpu.VMEM((2, V // 2, LANES), F32),
                        pltpu.VMEM((TBLK, 5, K, LANES), F32), pltpu.VMEM((TBLK, V, LANES), F32)],
        compiler_params=_cparams(("parallel", "arbitrary")),
        name="rwkv_scan",
    )(r, r, w, w, k, k, kk, kk, b, b, v, v, fwd_mask, s0)


KQ = 4
CH = LANES // KQ


def _rwkv_scan2_kernel(rf, rm, wf, wm, kf, km, kkf, kkm, bf, bm, vf, vm, s0_ref, yf_ref, yb_ref, S, YQ, SA, *, TBLK):
    V = RWKV_HEADSIZE
    NQ = RWKV_HEADSIZE // KQ
    tb = pl.program_id(0)

    @pl.when(tb == 0)
    def _():
        S[...] = s0_ref[...]

    VH = V // 2

    def bc(ref, tt, q):
        return jnp.broadcast_to(ref[tt, pl.ds(q, 1), :], (VH, LANES))

    def all_quarters(x):
        return (x + pltpu.roll(x, CH, 1)) + (pltpu.roll(x, 2 * CH, 1) + pltpu.roll(x, 3 * CH, 1))

    dirs = ((rf, wf, kf, kkf, bf, vf), (rm, wm, km, kkm, bm, vm))
    time_of = (lambda t: t, lambda t: TBLK - 1 - t)
    chains = [(d, hv) for d in range(N_DIR) for hv in range(2)]

    def first_sa(d, hv):
        kk_ = dirs[d][3]
        tt = time_of[d](0)
        acc = [None, None]
        for q in range(NQ):
            p = S[d, q, pl.ds(hv * VH, VH), :] * bc(kk_, tt, q)
            acc[q % 2] = p if acc[q % 2] is None else acc[q % 2] + p
        return acc[0] + acc[1]

    TG = min(16, TBLK)
    lane_r = lax.broadcasted_iota(jnp.int32, (2 * LANES, CH), 0) % CH
    fold2 = jnp.where(lax.broadcasted_iota(jnp.int32, (2 * LANES, CH), 1) == lane_r, 1.0, 0.0).astype(BF16)

    last = len(chains) - 1
    for c, (d, hv) in enumerate(chains):
        part = first_sa(d, hv)
        SA[c] = part if c == last else all_quarters(part)

    def step(t, _):
        sa_last = all_quarters(SA[last])
        for c, (d, hv) in enumerate(chains):
            r_, w_, k_, kk_, b_, v_ = dirs[d]
            tt = time_of[d](t)
            tn = time_of[d](jnp.minimum(t + 1, TBLK - 1))
            rows = pl.ds(hv * VH, VH)
            sa = sa_last if c == last else SA[c]
            vt = v_[tt, rows, :]
            yacc = None
            sacc = None
            for q in range(NQ):
                s_new = S[d, q, rows, :] * bc(w_, tt, q) - sa * bc(b_, tt, q) + vt * bc(k_, tt, q)
                S[d, q, rows, :] = s_new
                py = s_new * bc(r_, tt, q)
                ps = s_new * bc(kk_, tn, q)
                yacc = py if yacc is None else yacc + py
                sacc = ps if sacc is None else sacc + ps
            YQ[d, tt, rows, :] = yacc
            SA[c] = sacc if c == last else all_quarters(sacc)
        return 0
    lax.fori_loop(0, TBLK, step, 0)

    def finish(i, _):
        for d, y_ in enumerate((yf_ref, yb_ref)):
            hi, lo = _split_bf16(YQ[d, pl.ds(i * TG, TG)].reshape(TG * V, LANES))
            y = jnp.dot(jnp.concatenate([hi, lo], axis=-1), fold2, preferred_element_type=F32)
            y_[pl.ds(i * TG, TG)] = y.reshape(TG, V, CH)
        return 0
    lax.fori_loop(0, TBLK // TG, finish, 0)


def _rwkv_scan2(r, w0, w1, k, kk, b, v, s0):
    L, NQ, _ = r.shape
    V = RWKV_HEADSIZE
    TBLK = min(64, L)
    nblk = L // TBLK
    fspec = pl.BlockSpec((TBLK, NQ, LANES), lambda t: (t, 0, 0))
    mspec = pl.BlockSpec((TBLK, NQ, LANES), lambda t: (nblk - 1 - t, 0, 0))
    vfspec = pl.BlockSpec((TBLK, V, LANES), lambda t: (t, 0, 0))
    vmspec = pl.BlockSpec((TBLK, V, LANES), lambda t: (nblk - 1 - t, 0, 0))
    yfspec = pl.BlockSpec((TBLK, V, CH), lambda t: (t, 0, 0))
    ymspec = pl.BlockSpec((TBLK, V, CH), lambda t: (nblk - 1 - t, 0, 0))
    yf, yb = pl.pallas_call(
        functools.partial(_rwkv_scan2_kernel, TBLK=TBLK),
        grid=(nblk,),
        in_specs=[fspec, mspec] * 5 + [vfspec, vmspec, _full((N_DIR, NQ, V, LANES))],
        out_specs=(yfspec, ymspec),
        out_shape=(jax.ShapeDtypeStruct((L, V, CH), F32), jax.ShapeDtypeStruct((L, V, CH), F32)),
        scratch_shapes=[pltpu.VMEM((N_DIR, NQ, V, LANES), F32), pltpu.VMEM((N_DIR, TBLK, V, LANES), F32),
                        pltpu.VMEM((2 * N_DIR, V // 2, LANES), F32)],
        compiler_params=_cparams(("arbitrary",)),
        name="rwkv_scan2",
    )(r, r, w0, w1, k, k, kk, kk, b, b, v, v, s0)
    return yf, yb


HEADS_PAD = 8
SCAN_WIDTH = HEADS_PAD * RWKV_HEADSIZE


def _pad_chains(x, B):
    return x if B * HEADS_PAD == CH else jnp.pad(x, ((0, 0),) * (x.ndim - 1) + ((0, CH - B * HEADS_PAD),))


def _to_scan2_k(t, B, L):
    x = t.reshape(B, L, HEADS_PAD, RWKV_HEADSIZE // KQ, KQ).transpose(1, 3, 4, 0, 2)
    x = _pad_chains(x.reshape(L, RWKV_HEADSIZE // KQ, KQ, B * HEADS_PAD), B)
    return x.reshape(L, RWKV_HEADSIZE // KQ, LANES)


def _to_scan2_v(t, B, L):
    x = t.reshape(B, L, HEADS_PAD, RWKV_HEADSIZE).transpose(1, 3, 0, 2).reshape(L, RWKV_HEADSIZE, B * HEADS_PAD)
    return jnp.tile(_pad_chains(x, B), (1, 1, KQ))


def _from_scan2(y, B, L):
    y = y[:, :, :B * HEADS_PAD]
    return y.reshape(L, RWKV_HEADSIZE, B, HEADS_PAD).transpose(2, 0, 3, 1).reshape(B * L, SCAN_WIDTH)


def _state_to_scan2(s):
    B = s.shape[0]
    s = jnp.pad(s, ((0, 0), (0, 0), (0, HEADS_PAD - RWKV_HEADS), (0, 0), (0, 0)))
    x = s.transpose(1, 4, 3, 0, 2).reshape(N_DIR, RWKV_HEADSIZE // KQ, KQ, RWKV_HEADSIZE, B * HEADS_PAD)
    x = _pad_chains(x, B)
    return x.transpose(0, 1, 3, 2, 4).reshape(N_DIR, RWKV_HEADSIZE // KQ, RWKV_HEADSIZE, LANES)


def _ctx_lanes(B):
    chains = B * N_DIR * RWKV_HEADS
    return chains, -(-chains // LANES) * LANES


def _to_ctx(t_fwd, t_bwd, B, L):
    chains, lanes = _ctx_lanes(B)
    x = jnp.stack([t_fwd, t_bwd], 0).reshape(N_DIR, B, L, RWKV_HEADS, RWKV_HEADSIZE)
    x = x.transpose(2, 4, 1, 0, 3).reshape(L, RWKV_HEADSIZE, chains)
    return x if lanes == chains else jnp.pad(x, ((0, 0), (0, 0), (0, lanes - chains)))


def _ctx_fwd_mask(B):
    chains, lanes = _ctx_lanes(B)
    lane = jnp.arange(lanes)
    return (((lane // RWKV_HEADS) % N_DIR == 0) & (lane < chains)).astype(F32).reshape(1, lanes)


def _from_ctx(y, B, L, d):
    chains, _ = _ctx_lanes(B)
    y = y[:, :, :chains].reshape(L, RWKV_HEADSIZE, B, N_DIR, RWKV_HEADS)[:, :, :, d]
    return y.transpose(2, 0, 3, 1).reshape(B * L, RWKV_WIDTH)


def _ctx_state_from(x, B):
    chains, _ = _ctx_lanes(B)
    x = x[:, :, :chains].reshape(RWKV_HEADSIZE, RWKV_HEADSIZE, B, N_DIR, RWKV_HEADS)
    return x.transpose(2, 3, 4, 1, 0)


def _post_kernel(x_ref, mod_ref, ys5_ref, yf_ref, yb_ref, z_ref, rf_ref, rb_ref, bonus_ref, g_ref,
                 gluw_ref, glub_ref, ssdg_ref, lng_ref, lnb_ref, seg_ref, wout_ref, n2g_ref, rw_ref,
                 x1_ref, hb_ref, aff_ref):
    m = mod_ref[...]
    D = D_MODEL
    zg = _gelu_tanh(ys5_ref[...])
    gate = jnp.dot(zg.astype(BF16), gluw_ref[...], preferred_element_type=F32) + glub_ref[...]
    y_a = zg * _sigmoid(gate)
    z = z_ref[...]
    yb = (yf_ref[...] + yb_ref[...]) * (z * _sigmoid(z))
    y_b = yb * lax.rsqrt(jnp.mean(yb * yb, axis=-1, keepdims=True) + EPS) * ssdg_ref[...]
    seg = seg_ref[...] * (1.0 / RWKV_HEADSIZE)
    yr = rf_ref[...] + rb_ref[...]
    mean = _dot_exact_rhs(yr, seg)
    cen = yr - mean
    var = jnp.dot((cen * cen).astype(BF16), seg.astype(BF16), preferred_element_type=F32)
    yn = cen * lax.rsqrt(var + GN_EPS) * lng_ref[...] + lnb_ref[...]
    y_c = (yn + bonus_ref[...]) * g_ref[...]
    o = jnp.dot(y_a.astype(BF16), wout_ref[0:S5_WIDTH, :], preferred_element_type=F32)
    o = o + jnp.dot(y_b.astype(BF16), wout_ref[S5_WIDTH:S5_WIDTH + SSD_WIDTH, :], preferred_element_type=F32)
    o = o + jnp.dot(y_c.astype(BF16), wout_ref[S5_WIDTH + SSD_WIDTH:, :], preferred_element_type=F32)
    x1 = x_ref[...] + m[:, 2 * D:3 * D] * o
    x1_ref[...] = x1
    h2 = x1 * lax.rsqrt(jnp.mean(x1 * x1, axis=-1, keepdims=True) + EPS) * n2g_ref[...]
    h2 = h2 * (1.0 + m[:, 4 * D:5 * D]) + m[:, 3 * D:4 * D]
    hb_ref[...] = h2.astype(BF16)
    logits = _dot3(rw_ref[...], h2, (((1,), (1,)), ((), ())))
    mx = jnp.max(logits, axis=0, keepdims=True)
    ex = jnp.exp(logits - mx)
    aff_ref[...] = ex / jnp.sum(ex, axis=0, keepdims=True)


def _post_mixer(x, mod3, row_of_block, ys5, yssd, z, rf, rb, bonus, g, lp):
    ntok = x.shape[0]
    tm = TOKEN_BLOCK
    tok = lambda w: pl.BlockSpec((tm, w), lambda i: (i, 0))
    row = lambda t: t.reshape(1, -1)
    W = RWKV_WIDTH
    return pl.pallas_call(
        _post_kernel,
        grid=(ntok // tm,),
        in_specs=[
            tok(D_MODEL),
            pl.BlockSpec((None, 1, 6 * D_MODEL), lambda i: (row_of_block(i), 0, 0)),
            tok(S5_WIDTH),
            tok(SSD_WIDTH), tok(SSD_WIDTH),
            tok(SSD_WIDTH), tok(W), tok(W), tok(W), tok(W),
            _full((S5_WIDTH, S5_WIDTH)), _full((1, S5_WIDTH)), _full((1, SSD_WIDTH)), _full((1, W)), _full((1, W)),
            _full((W, W)), _full((D_MODEL, D_MODEL)), _full((1, D_MODEL)), _full((N_EXPERTS, D_MODEL)),
        ],
        out_specs=(tok(D_MODEL), tok(D_MODEL), pl.BlockSpec((N_EXPERTS, tm), lambda i: (0, i))),
        out_shape=(jax.ShapeDtypeStruct((ntok, D_MODEL), F32), jax.ShapeDtypeStruct((ntok, D_MODEL), BF16),
                   jax.ShapeDtypeStruct((N_EXPERTS, ntok), F32)),
        compiler_params=_cparams(("parallel",)),
        name="post_mixer",
    )(x, mod3, ys5, yssd[0], yssd[1], z, rf, rb, bonus, g,
      lp['s5_glu_w'], row(lp['s5_glu_b']), row(lp['ssd_norm_g']), row(lp['rwkv_ln_g']),
      row(lp['rwkv_ln_b']), lp['seg'], lp['w_out'], row(lp['norm2_g']),
      lp['router_w'].T)


def _select_kernel(aff_ref, slot_ref, affo_ref, *, n, cap, bg):
    for g in range(bg):
        _select_one(aff_ref[:, g * n:(g + 1) * n], slot_ref.at[g], affo_ref.at[g], n, cap)


def _select_one(a, slot_ref, affo_ref, n, cap):
    E = N_EXPERTS
    bits = pltpu.bitcast(a, jnp.int32)
    thr = jnp.zeros((E, 1), jnp.int32)
    capf = float(cap)
    for bit in range(30, -1, -1):
        cand = thr | (1 << bit)
        cnt = jnp.sum(jnp.where(bits >= cand, 1.0, 0.0), axis=1, keepdims=True)
        thr = jnp.where(cnt >= capf, cand, thr)
    gt = bits > thr
    eq = bits == thr
    need = capf - jnp.sum(jnp.where(gt, 1.0, 0.0), axis=1, keepdims=True)
    CW = min(256, n)
    ui = lax.broadcasted_iota(jnp.int32, (CW, CW), 0)
    uj = lax.broadcasted_iota(jnp.int32, (CW, CW), 1)
    upper = jnp.where(ui < uj, 1.0, 0.0).astype(BF16)

    def excl_cumsum(mask_f):
        outs = []
        off = jnp.zeros((E, 1), F32)
        for c in range(n // CW):
            mc = mask_f[:, c * CW:(c + 1) * CW]
            outs.append(jnp.dot(mc.astype(BF16), upper, preferred_element_type=F32) + off)
            off = off + jnp.sum(mc, axis=1, keepdims=True)
        return jnp.concatenate(outs, axis=1)

    eq_rank = excl_cumsum(jnp.where(eq, 1.0, 0.0))
    sel = jnp.where(gt, 1.0, jnp.where(eq, jnp.where(eq_rank < need, 1.0, 0.0), 0.0))
    pos = excl_cumsum(sel)
    slot = jnp.where(sel > 0.0, pos, -1.0)
    for e in range(E):
        slot_ref[e] = slot[e:e + 1, :]
        affo_ref[e] = a[e:e + 1, :]


def _select(aff, B, n, cap):
    bg = 4 if (B % 4 == 0 and n <= 512) else (2 if B % 2 == 0 else 1)
    spec = pl.BlockSpec((bg, N_EXPERTS, 1, n), lambda b: (b, 0, 0, 0))
    return pl.pallas_call(
        functools.partial(_select_kernel, n=n, cap=cap, bg=bg),
        grid=(B // bg,),
        in_specs=[pl.BlockSpec((N_EXPERTS, bg * n), lambda b: (0, b))],
        out_specs=(spec, spec),
        out_shape=(jax.ShapeDtypeStruct((B, N_EXPERTS, 1, n), F32), jax.ShapeDtypeStruct((B, N_EXPERTS, 1, n), F32)),
        compiler_params=_cparams(("parallel",)),
        name="ec_select",
    )(aff)


def _slot_block_range(slots_f, cap, SB):
    lo = jnp.min(jnp.where(slots_f >= 0.0, slots_f, float(cap))).astype(jnp.int32)
    hi = jnp.max(slots_f).astype(jnp.int32)
    first = lo // SB
    count = jnp.where(hi >= 0, hi // SB - first + 1, 0)
    return first, count


EC_SLOT_BLOCK = 256


def _one_hot_all(slot_ref, aff_ref, n, cap):
    srow = lax.broadcasted_iota(jnp.int32, (cap, n), 0).astype(F32)
    ohs, gates = [], []
    for e in range(N_EXPERTS):
        hit = slot_ref[e] == srow
        ohs.append(jnp.where(hit, 1.0, 0.0).astype(BF16))
        if aff_ref is not None:
            gates.append(jnp.sum(jnp.where(hit, aff_ref[e], 0.0), axis=1, keepdims=True))
    return jnp.concatenate(ohs, axis=0), (jnp.concatenate(gates, axis=0) if gates else None)


def _gather_all_kernel(hb_ref, slot_ref, aff_ref, xs_ref, gate_ref, *, n, cap):
    oh, gate = _one_hot_all(slot_ref, aff_ref, n, cap)
    xs = jnp.dot(oh, hb_ref[...], preferred_element_type=F32)
    xs_ref[...] = xs.astype(BF16).reshape(N_EXPERTS, cap, D_MODEL)
    gate_ref[...] = gate.reshape(N_EXPERTS, cap, 1)


def _gather_kernel(hb_ref, slot_ref, aff_ref, xs_ref, gate_ref, acc, gacc, *, n, cap):
    NC = min(512, n)
    SB = min(EC_SLOT_BLOCK // 2, cap)
    acc[...] = jnp.zeros_like(acc)
    gacc[...] = jnp.zeros_like(gacc)
    srow = lax.broadcasted_iota(jnp.int32, (SB, NC), 0).astype(F32)
    for c in range(n // NC):
        sl = slot_ref[:, c * NC:(c + 1) * NC]
        first, count = _slot_block_range(sl, cap, SB)
        for j in range(min(cap // SB, NC // SB + 1)):
            @pl.when(j < count)
            def _(c=c, j=j, sl=sl, first=first):
                base = pl.multiple_of((first + j) * SB, SB)
                hit = (sl - base.astype(F32)) == srow
                oh = jnp.where(hit, 1.0, 0.0).astype(BF16)
                acc[pl.ds(base, SB), :] += jnp.dot(oh, hb_ref[c * NC:(c + 1) * NC, :], preferred_element_type=F32)
                gacc[pl.ds(base, SB), :] += jnp.sum(jnp.where(hit, aff_ref[:, c * NC:(c + 1) * NC], 0.0), axis=1,
                                                    keepdims=True)
    xs_ref[...] = acc[...].astype(BF16)
    gate_ref[...] = gacc[...]


def _gather(hb, slot4, aff4, B, n, cap):
    E = N_EXPERTS
    if E * cap <= 512:
        all_spec = pl.BlockSpec((None, E, 1, n), lambda b: (b, 0, 0, 0))
        return pl.pallas_call(
            functools.partial(_gather_all_kernel, n=n, cap=cap),
            grid=(B,),
            in_specs=[pl.BlockSpec((n, D_MODEL), lambda b: (b, 0)), all_spec, all_spec],
            out_specs=(pl.BlockSpec((None, E, cap, D_MODEL), lambda b: (b, 0, 0, 0)),
                       pl.BlockSpec((None, E, cap, 1), lambda b: (b, 0, 0, 0))),
            out_shape=(jax.ShapeDtypeStruct((B, E, cap, D_MODEL), BF16), jax.ShapeDtypeStruct((B, E, cap, 1), F32)),
            compiler_params=_cparams(("parallel",)),
            name="ec_gather_all",
        )(hb, slot4, aff4)
    return pl.pallas_call(
        functools.partial(_gather_kernel, n=n, cap=cap),
        grid=(B, E),
        in_specs=[
            pl.BlockSpec((n, D_MODEL), lambda b, e: (b, 0)),
            pl.BlockSpec((None, None, 1, n), lambda b, e: (b, e, 0, 0)),
            pl.BlockSpec((None, None, 1, n), lambda b, e: (b, e, 0, 0)),
        ],
        out_specs=(pl.BlockSpec((None, None, cap, D_MODEL), lambda b, e: (b, e, 0, 0)),
                   pl.BlockSpec((None, None, cap, 1), lambda b, e: (b, e, 0, 0))),
        out_shape=(jax.ShapeDtypeStruct((B, E, cap, D_MODEL), BF16), jax.ShapeDtypeStruct((B, E, cap, 1), F32)),
        scratch_shapes=[pltpu.VMEM((cap, D_MODEL), F32), pltpu.VMEM((cap, 1), F32)],
        compiler_params=_cparams(("parallel", "arbitrary")),
        name="ec_gather",
    )(hb, slot4, aff4)


def _ffn_kernel(xs_ref, gate_ref, w1_ref, w3_ref, w2_ref, o_ref):
    bg, cap, _ = xs_ref.shape
    x = xs_ref[...].reshape(bg * cap, D_MODEL)
    h1 = jnp.dot(x, w1_ref[...], preferred_element_type=F32)
    h3 = jnp.dot(x, w3_ref[...], preferred_element_type=F32)
    hid = (h1 * _sigmoid(h1) * h3).astype(BF16)
    o = jnp.dot(hid, w2_ref[...], preferred_element_type=F32) * gate_ref[...].reshape(bg * cap, 1)
    o_ref[...] = o.astype(BF16).reshape(bg, cap, D_MODEL)


def _expert_ffn(xs, gate, w1, w3, w2, bg):
    B, E, cap, _ = xs.shape
    return pl.pallas_call(
        _ffn_kernel,
        grid=(E, B // bg),
        in_specs=[
            pl.BlockSpec((bg, None, cap, D_MODEL), lambda e, b: (b, e, 0, 0)),
            pl.BlockSpec((bg, None, cap, 1), lambda e, b: (b, e, 0, 0)),
            pl.BlockSpec((None, D_MODEL, D_EXPERT), lambda e, b: (e, 0, 0)),
            pl.BlockSpec((None, D_MODEL, D_EXPERT), lambda e, b: (e, 0, 0)),
            pl.BlockSpec((None, D_EXPERT, D_MODEL), lambda e, b: (e, 0, 0)),
        ],
        out_specs=pl.BlockSpec((bg, None, cap, D_MODEL), lambda e, b: (b, e, 0, 0)),
        out_shape=jax.ShapeDtypeStruct((B, E, cap, D_MODEL), BF16),
        compiler_params=_cparams(("parallel", "parallel")),
        name="ec_ffn",
    )(xs, gate, w1, w3, w2)


_TN_DIMS = (((0,), (0,)), ((), ()))


def _ec_residual(x1_ref, mod_ref, fg_ref, out_ref, ffn, final):
    x2 = x1_ref[...] + mod_ref[:, 5 * D_MODEL:6 * D_MODEL] * ffn
    if final:
        x2 = x2 * lax.rsqrt(jnp.mean(x2 * x2, axis=-1, keepdims=True) + EPS) * fg_ref[...]
    out_ref[...] = x2


def _scatter_all_kernel(slot_ref, o_ref, x1_ref, mod_ref, fg_ref, out_ref, *, n, cap, final):
    oh, _ = _one_hot_all(slot_ref, None, n, cap)
    ffn = lax.dot_general(oh, o_ref[...].reshape(N_EXPERTS * cap, D_MODEL), _TN_DIMS, preferred_element_type=F32)
    _ec_residual(x1_ref, mod_ref, fg_ref, out_ref, ffn, final)


def _scatter_kernel(slot_ref, o_ref, x1_ref, mod_ref, fg_ref, out_ref, acc, *, cap, final):
    e = pl.program_id(2)
    tn = x1_ref.shape[0]

    @pl.when(e == 0)
    def _():
        acc[...] = jnp.zeros_like(acc)

    sl = slot_ref[...]
    SB = min(EC_SLOT_BLOCK, cap)
    srow = lax.broadcasted_iota(jnp.int32, (SB, tn), 0).astype(F32)
    first, count = _slot_block_range(sl, cap, SB)
    for j in range(min(cap // SB, tn // SB + 1)):
        @pl.when(j < count)
        def _(j=j):
            base = pl.multiple_of((first + j) * SB, SB)
            oh = jnp.where((sl - base.astype(F32)) == srow, 1.0, 0.0).astype(BF16)
            acc[...] += lax.dot_general(oh, o_ref[pl.ds(base, SB), :], _TN_DIMS, preferred_element_type=F32)

    @pl.when(e == N_EXPERTS - 1)
    def _():
        _ec_residual(x1_ref, mod_ref, fg_ref, out_ref, acc[...], final)


def _scatter(slot4, o, x1, mod3, mod_row, final_g, B, n, cap, final):
    E = N_EXPERTS
    if E * cap <= 512:
        return pl.pallas_call(
            functools.partial(_scatter_all_kernel, n=n, cap=cap, final=final),
            grid=(B,),
            in_specs=[
                pl.BlockSpec((None, E, 1, n), lambda b: (b, 0, 0, 0)),
                pl.BlockSpec((None, E, cap, D_MODEL), lambda b: (b, 0, 0, 0)),
                pl.BlockSpec((n, D_MODEL), lambda b: (b, 0)),
                pl.BlockSpec((None, 1, 6 * D_MODEL), lambda b: (mod_row(b), 0, 0)),
                _full((1, D_MODEL)),
            ],
            out_specs=pl.BlockSpec((n, D_MODEL), lambda b: (b, 0)),
            out_shape=jax.ShapeDtypeStruct((B * n, D_MODEL), F32),
            compiler_params=_cparams(("parallel",)),
            name="ec_scatter_all",
        )(slot4, o, x1, mod3, final_g.reshape(1, D_MODEL))
    tn = min(1024, n)
    nt = n // tn
    return pl.pallas_call(
        functools.partial(_scatter_kernel, cap=cap, final=final),
        grid=(B, nt, N_EXPERTS),
        in_specs=[
            pl.BlockSpec((None, None, 1, tn), lambda b, t, e: (b, e, 0, t)),
            pl.BlockSpec((None, None, cap, D_MODEL), lambda b, t, e: (b, e, 0, 0)),
            pl.BlockSpec((tn, D_MODEL), lambda b, t, e: (b * nt + t, 0)),
            pl.BlockSpec((None, 1, 6 * D_MODEL), lambda b, t, e: (mod_row(b), 0, 0)),
            _full((1, D_MODEL)),
        ],
        out_specs=pl.BlockSpec((tn, D_MODEL), lambda b, t, e: (b * nt + t, 0)),
        out_shape=jax.ShapeDtypeStruct((B * n, D_MODEL), F32),
        scratch_shapes=[pltpu.VMEM((tn, D_MODEL), F32)],
        compiler_params=_cparams(("parallel", "parallel", "arbitrary")),
        name="ec_scatter",
    )(slot4, o, x1, mod3, final_g.reshape(1, D_MODEL))


def _expert_choice(hb, aff, x1, mod3, mod_row, final_g, w1, w3, w2, B, n, final):
    cap = EC_FACTOR * n // N_EXPERTS
    slot4, aff4 = _select(aff, B, n, cap)
    xs, gate = _gather(hb, slot4, aff4, B, n, cap)
    bg = max(1, min(B, 256 // cap))
    o = _expert_ffn(xs, gate, w1, w3, w2, bg)
    return _scatter(slot4, o, x1, mod3, mod_row, final_g, B, n, cap, final)


def kernel(x_prompt, x_sample, c, state_s5_re, state_s5_im, state_ssd, state_rwkv, c_ctx, ada_w, ada_b, norm1_g, norm2_g, w_in, w_out, s5_a_re, s5_a_im, s5_log_dt, s5_b_re, s5_b_im, s5_c_re, s5_c_im, s5_d, s5_glu_w, s5_glu_b, ssd_conv_w, ssd_conv_b, ssd_a_log, ssd_dt_bias, ssd_d, ssd_norm_g, rwkv_mu, rwkv_w0, rwkv_w_up, rwkv_a0, rwkv_a_up, rwkv_g_up, rwkv_k_k, rwkv_k_a, rwkv_r_k, rwkv_ln_g, rwkv_ln_b, router_w, exp_w1, exp_w3, exp_w2, final_g):
    Bp, Lp, D = x_prompt.shape
    Bs, Ls, _ = x_sample.shape
    depth = ada_w.shape[0]
    Np, Ns = Bp * Lp, Bs * Ls
    tm = TOKEN_BLOCK
    grid_rows = Ls // GRID_W
    nfs = S5_GROUPS // 8
    assert Lp % tm == 0 and Ls % tm == 0 and Lp % SSD_CHUNK == 0

    n_rows = 1 + Bs
    rows_pad = -(-n_rows // 8) * 8
    cond = jnp.zeros((rows_pad, D), F32).at[0].set(c_ctx).at[1:n_rows].set(c)
    mod = _modulation(cond, ada_w, ada_b)
    s_blocks_per_req = Ls // tm
    row_p = lambda i: 0
    row_s = lambda i: 1 + i // s_blocks_per_req

    ab_re, ab_im, bb_re, bb_im = _s5_discretize(s5_a_re, s5_a_im, s5_log_dt, s5_b_re, s5_b_im)
    w_in_pad = _pad_in_weight(w_in)
    w_out_b = w_out.astype(BF16)
    glu_w_b = s5_glu_w.astype(BF16)
    exp_w1_b, exp_w3_b, exp_w2_b = exp_w1.astype(BF16), exp_w3.astype(BF16), exp_w2.astype(BF16)
    seg = _segment_ones(RWKV_WIDTH, RWKV_HEADSIZE)

    xp = x_prompt.reshape(Np, D)
    xs = x_sample.reshape(Ns, D)
    new_s5_re, new_s5_im, new_ssd, new_rwkv = [], [], [], []
    QP = 2 if Bp % 2 == 0 and Bp >= 2 else 1
    RP = Bp // QP
    assert Bs * HEADS_PAD <= CH

    for l in range(depth):
        lp = {
            'rwkv_mu': rwkv_mu[l], 'rwkv_w0': rwkv_w0[l], 'rwkv_w_up': rwkv_w_up[l], 'rwkv_a0': rwkv_a0[l],
            'rwkv_a_up': rwkv_a_up[l], 'rwkv_g_up': rwkv_g_up[l], 'rwkv_k_k': rwkv_k_k[l], 'rwkv_k_a': rwkv_k_a[l],
            'rwkv_r_k': rwkv_r_k[l].reshape(-1), 'rwkv_ln_g': rwkv_ln_g[l], 'rwkv_ln_b': rwkv_ln_b[l],
            's5_glu_w': glu_w_b[l], 's5_glu_b': s5_glu_b[l], 'ssd_norm_g': ssd_norm_g[l], 'w_out': w_out_b[l],
            'norm2_g': norm2_g[l], 'router_w': router_w[l], 'seg': seg,
        }
        mod3 = mod[l].reshape(rows_pad, 1, 6 * D)
        us5_p, z_p, xbc_p, dt_p, rkv_p, lo_p = _in_projection(xp, mod3, row_p, norm1_g[l], w_in_pad[l])
        us5_s, z_s, xbc_s, dt_s, rkv_s, lo_s = _in_projection(xs, mod3, row_s, norm1_g[l], w_in_pad[l])

        tables = _s5_layer_tables(ab_re[l], ab_im[l], bb_re[l], bb_im[l], s5_c_re[l], s5_c_im[l])
        d_row = s5_d[l].reshape(1, S5_WIDTH)
        up = us5_p.reshape(QP, RP, Lp, S5_WIDTH).transpose(0, 2, 1, 3).reshape(QP, Lp * RP, S5_WIDTH)
        yp, hfin = _s5_scan(up, None, tables, d_row, R=RP, n_slab=Lp, chained=False, want_final=True)
        ys5_p = yp.reshape(QP, Lp, RP, S5_WIDTH).transpose(0, 2, 1, 3).reshape(Np, S5_WIDTH)
        hf = hfin.transpose(0, 3, 1, 2, 4).reshape(Bp, N_DIR, nfs, 2, 8, S5_STATE)
        new_s5_re.append(hf[:, :, :, 0].reshape(Bp, N_DIR, S5_GROUPS, S5_STATE))
        new_s5_im.append(hf[:, :, :, 1].reshape(Bp, N_DIR, S5_GROUPS, S5_STATE))
        h0 = jnp.concatenate([state_s5_re[:, l].reshape(Bs, N_DIR, nfs, 1, S5_SLICE_ST),
                              state_s5_im[:, l].reshape(Bs, N_DIR, nfs, 1, S5_SLICE_ST)], axis=-1)
        (ysm,) = _s5_scan(us5_s.reshape(Bs, Ls, S5_WIDTH), h0, tables, d_row, R=GRID_W, n_slab=grid_rows,
                          chained=True, want_final=False)
        ys5_s = ysm.reshape(Ns, S5_WIDTH)

        ssd_args = (ssd_conv_w[l], ssd_conv_b[l], ssd_dt_bias[l], ssd_a_log[l], ssd_d[l])
        *yssd_p, hssd = _ssd_scan(xbc_p, dt_p, None, *ssd_args, B=Bp, L=Lp, want_final=True)
        new_ssd.append(hssd.transpose(0, 1, 2, 4, 3))
        yssd_s = _ssd_scan(xbc_s, dt_s, state_ssd[:, l].transpose(0, 1, 2, 4, 3), *ssd_args,
                           B=Bs, L=Ls, want_final=False)

        r_, w0_, w1_, k_, v_, kk_, b_, g_p, bonus_p = _rwkv_prep(rkv_p, lo_p, lp, Lp // tm)
        cx = lambda t: _to_ctx(t, t, Bp, Lp)
        zero_state = jnp.zeros((RWKV_HEADSIZE, RWKV_HEADSIZE, _ctx_lanes(Bp)[1]), F32)
        yf_, yb_, sfin = _rwkv_scan(cx(r_), _to_ctx(w0_, w1_, Bp, Lp), cx(k_), cx(kk_), cx(b_), cx(v_),
                                    _ctx_fwd_mask(Bp), zero_state)
        rf_p = _from_ctx(yf_, Bp, Lp, 0)
        rb_p = _from_ctx(yb_, Bp, Lp, 1)
        new_rwkv.append(_ctx_state_from(sfin, Bp))

        r_, w0_, w1_, k_, v_, kk_, b_, g_s, bonus_s = _rwkv_prep(rkv_s, lo_s, lp, Ls // tm, SCAN_WIDTH)
        sc = lambda t: _to_scan2_k(t, Bs, Ls)
        yf_, yb_ = _rwkv_scan2(sc(r_), sc(w0_), sc(w1_), sc(k_), sc(kk_), sc(b_), _to_scan2_v(v_, Bs, Ls),
                               _state_to_scan2(state_rwkv[:, l]))
        rf_s = _from_scan2(yf_, Bs, Ls)
        rb_s = _from_scan2(yb_, Bs, Ls)

        x1_p, hb_p, aff_p = _post_mixer(xp, mod3, row_p, ys5_p, yssd_p, z_p, rf_p, rb_p, bonus_p, g_p, lp)
        x1_s, hb_s, aff_s = _post_mixer(xs, mod3, row_s, ys5_s, yssd_s, z_s, rf_s, rb_s, bonus_s, g_s, lp)

        final = l == depth - 1
        ew = (exp_w1_b[l], exp_w3_b[l], exp_w2_b[l])
        xp = _expert_choice(hb_p, aff_p, x1_p, mod3, lambda b: 0, final_g, *ew, Bp, Lp, final)
        xs = _expert_choice(hb_s, aff_s, x1_s, mod3, lambda b: 1 + b, final_g, *ew, Bs, Ls, final)

    y_prompt = xp.reshape(Bp, Lp, D)
    y_sample = xs.reshape(Bs, Ls, D)
    return (y_prompt, y_sample, jnp.stack(new_s5_re, axis=1), jnp.stack(new_s5_im, axis=1),
            jnp.stack(new_ssd, axis=1), jnp.stack(new_rwkv, axis=1))
```

```python
import functools
import math

import jax
import jax.numpy as jnp
from jax import lax
from jax.experimental import pallas as pl
from jax.experimental.pallas import tpu as pltpu

F32 = jnp.float32
BF16 = jnp.bfloat16
HIGHEST = lax.Precision.HIGHEST

D_MODEL = 1024
GRID_W = 64
N_DIR = 2
EPS = 1e-6
S5_WIDTH = 256
S5_CH = 16
S5_GROUPS = 16
S5_STATE = 64
SSD_HEADDIM = 64
SSD_HEADS = 6
SSD_WIDTH = 384
SSD_GROUPS = 2
SSD_STATE = 64
SSD_BC = 128
SSD_CONV_CH = 640
SSD_CHUNK = 128
RWKV_HEADSIZE = 64
RWKV_HEADS = 6
RWKV_WIDTH = 384
W_RANK = 32
A_RANK = 32
G_RANK = 64
DECAY_SCALE = math.exp(-0.5)
GN_EPS = 64e-5
N_EXPERTS = 16
D_EXPERT = 512
EC_FACTOR = 2
IN_SIZES = (S5_WIDTH, SSD_WIDTH, SSD_CONV_CH, N_DIR * SSD_HEADS, 3 * RWKV_WIDTH, N_DIR * W_RANK, A_RANK, G_RANK)

LANES = 128
TOKEN_BLOCK = 256
DT_PAD = 128
LO_PAD = 256
S5_SLICE_CH = 128
S5_SLICE_ST = 512
VMEM_LIMIT = 56 * 1024 * 1024


def _cparams(sem):
    return pltpu.CompilerParams(dimension_semantics=sem, vmem_limit_bytes=VMEM_LIMIT)


def _full(shape):
    nd = len(shape)
    return pl.BlockSpec(shape, lambda *_: (0,) * nd)


def _sigmoid(x):
    return 1.0 / (1.0 + jnp.exp(-x))


def _softplus(x):
    return jnp.maximum(x, 0.0) + jnp.log1p(jnp.exp(-jnp.abs(x)))


def _split_bf16(x):
    hi = x.astype(BF16)
    lo = (x - hi.astype(F32)).astype(BF16)
    return hi, lo


def _dot_exact_rhs(x, m, dims=None):
    hi, lo = _split_bf16(x)
    mb = m.astype(BF16)
    if dims is None:
        return jnp.dot(hi, mb, preferred_element_type=F32) + jnp.dot(lo, mb, preferred_element_type=F32)
    return (lax.dot_general(hi, mb, dims, preferred_element_type=F32)
            + lax.dot_general(lo, mb, dims, preferred_element_type=F32))


def _dot_exact_lhs(m, x):
    hi, lo = _split_bf16(x)
    mb = m.astype(BF16)
    return jnp.dot(mb, hi, preferred_element_type=F32) + jnp.dot(mb, lo, preferred_element_type=F32)


def _dot3(x, w, dims=(((1,), (0,)), ((), ()))):
    xh, xl = _split_bf16(x)
    wh, wl = _split_bf16(w)
    dg = functools.partial(lax.dot_general, dimension_numbers=dims, preferred_element_type=F32)
    return dg(xh, wh) + (dg(xl, wh) + dg(xh, wl))


def _gelu_tanh(x):
    return 0.5 * x * (1.0 + jnp.tanh(math.sqrt(2.0 / math.pi) * (x + 0.044715 * (x * x * x))))


def _mod_kernel(c_ref, w_ref, b_ref, o_ref):
    c = c_ref[...]
    s = (c * _sigmoid(c)).astype(BF16)
    o_ref[...] = jnp.dot(s, w_ref[...].astype(BF16), preferred_element_type=F32) + b_ref[...]


def _modulation(cond, ada_w, ada_b):
    depth = ada_w.shape[0]
    rows = cond.shape[0]
    tn = 1536
    return pl.pallas_call(
        _mod_kernel,
        grid=(depth, 6 * D_MODEL // tn),
        in_specs=[
            pl.BlockSpec((rows, D_MODEL), lambda l, j: (0, 0)),
            pl.BlockSpec((None, D_MODEL, tn), lambda l, j: (l, 0, j)),
            pl.BlockSpec((None, 1, tn), lambda l, j: (l, 0, j)),
        ],
        out_specs=pl.BlockSpec((None, rows, tn), lambda l, j: (l, 0, j)),
        out_shape=jax.ShapeDtypeStruct((depth, rows, 6 * D_MODEL), F32),
        compiler_params=_cparams(("parallel", "parallel")),
        name="adaln_mod",
    )(cond, ada_w, ada_b.reshape(depth, 1, 6 * D_MODEL))


IN_PAD_SIZES = (S5_WIDTH, SSD_WIDTH, SSD_CONV_CH, DT_PAD, 3 * RWKV_WIDTH, LO_PAD)


def _inproj_kernel(x_ref, mod_ref, g_ref, w_ref, us5_ref, z_ref, xbc_ref, dt_ref, rkv_ref, lo_ref):
    x = x_ref[...]
    y = x * lax.rsqrt(jnp.mean(x * x, axis=-1, keepdims=True) + EPS) * g_ref[...]
    m = mod_ref[...]
    h = y * (1.0 + m[:, D_MODEL:2 * D_MODEL]) + m[:, 0:D_MODEL]
    p = jnp.dot(h.astype(BF16), w_ref[...], preferred_element_type=F32)
    start = 0
    for ref, size in zip((us5_ref, z_ref, xbc_ref, dt_ref, rkv_ref, lo_ref), IN_PAD_SIZES):
        ref[...] = p[:, start:start + size]
        start += size


def _pad_in_weight(w_in):
    parts, start = [], 0
    for s in IN_SIZES:
        parts.append(w_in[..., start:start + s])
        start += s
    us5, z, xbc, dt, rkv, wlo, alo, glo = parts
    zeros = lambda n: jnp.zeros(w_in.shape[:-1] + (n,), w_in.dtype)
    lo_used = wlo.shape[-1] + alo.shape[-1] + glo.shape[-1]
    return jnp.concatenate([us5, z, xbc, dt, zeros(DT_PAD - dt.shape[-1]), rkv, wlo, alo, glo,
                            zeros(LO_PAD - lo_used)], axis=-1).astype(BF16)


def _in_projection(x, mod3, row_of_block, norm_g, w_pad):
    ntok = x.shape[0]
    tm = TOKEN_BLOCK
    width = w_pad.shape[1]
    outs = tuple(jax.ShapeDtypeStruct((ntok, s), F32) for s in IN_PAD_SIZES)
    return pl.pallas_call(
        _inproj_kernel,
        grid=(ntok // tm,),
        in_specs=[
            pl.BlockSpec((tm, D_MODEL), lambda i: (i, 0)),
            pl.BlockSpec((None, 1, 6 * D_MODEL), lambda i: (row_of_block(i), 0, 0)),
            _full((1, D_MODEL)),
            _full((D_MODEL, width)),
        ],
        out_specs=tuple(pl.BlockSpec((tm, s), lambda i: (i, 0)) for s in IN_PAD_SIZES),
        out_shape=outs,
        compiler_params=_cparams(("parallel",)),
        name="in_proj",
    )(x, mod3, norm_g.reshape(1, D_MODEL), w_pad)


def _s5_disc_kernel(are_ref, aim_ref, ldt_ref, bre_ref, bim_ref, abre_ref, abim_ref, bbre_ref, bbim_ref):
    lam_re = jnp.minimum(are_ref[...], -1e-4)
    lam_im = aim_ref[...]
    dt = jnp.exp(ldt_ref[...])
    mag = jnp.exp(lam_re * dt)
    ab_re = mag * jnp.cos(lam_im * dt)
    ab_im = mag * jnp.sin(lam_im * dt)
    num_re, num_im = ab_re - 1.0, ab_im
    den = lam_re * lam_re + lam_im * lam_im
    q_re = (num_re * lam_re + num_im * lam_im) / den
    q_im = (num_im * lam_re - num_re * lam_im) / den
    abre_ref[...] = ab_re
    abim_ref[...] = ab_im
    b_re = bre_ref[...]
    b_im = bim_ref[...]
    qr = q_re[:, None, :]
    qi = q_im[:, None, :]
    bbre_ref[...] = qr * b_re - qi * b_im
    bbim_ref[...] = qr * b_im + qi * b_re


def _s5_discretize(a_re, a_im, log_dt, b_re, b_im):
    lead = a_re.shape[:3]
    n = lead[0] * lead[1] * lead[2]
    a2 = lambda t: t.reshape(n, S5_STATE)
    ldt = jnp.broadcast_to(log_dt.reshape(n, 1), (n, S5_STATE))
    b3 = lambda t: t.reshape(n, S5_STATE, S5_CH).transpose(0, 2, 1)
    ab_re, ab_im, bb_re, bb_im = pl.pallas_call(
        _s5_disc_kernel,
        out_shape=(jax.ShapeDtypeStruct((n, S5_STATE), F32), jax.ShapeDtypeStruct((n, S5_STATE), F32),
                   jax.ShapeDtypeStruct((n, S5_CH, S5_STATE), F32), jax.ShapeDtypeStruct((n, S5_CH, S5_STATE), F32)),
        name="s5_discretize",
    )(a2(a_re), a2(a_im), ldt, b3(b_re), b3(b_im))
    return (ab_re.reshape(lead + (S5_STATE,)), ab_im.reshape(lead + (S5_STATE,)),
            bb_re.reshape(lead + (S5_CH, S5_STATE)), bb_im.reshape(lead + (S5_CH, S5_STATE)))


def _s5_layer_tables(ab_re, ab_im, bb_re, bb_im, c_re, c_im):
    nfs = S5_GROUPS // 8
    eye = jnp.eye(8, dtype=F32)

    def rows(t):
        return t.reshape(N_DIR, nfs, 8 * S5_STATE)

    ab_row = jnp.concatenate([rows(ab_re), rows(ab_im)], axis=-1).reshape(N_DIR, nfs, 1, 2 * S5_SLICE_ST)

    def bmat(t):
        t = t.reshape(N_DIR, nfs, 8, S5_CH, S5_STATE)
        return jnp.einsum('dfghp,gk->dfghkp', t, eye).reshape(N_DIR, nfs, S5_SLICE_CH, S5_SLICE_ST)

    b_mat = jnp.concatenate([bmat(bb_re), bmat(bb_im)], axis=-1).astype(BF16)

    def cmat(t):
        t = t.reshape(N_DIR, nfs, 8, S5_CH, S5_STATE)
        return jnp.einsum('dfghp,gk->dfgpkh', t, eye).reshape(N_DIR, nfs, S5_SLICE_ST, S5_SLICE_CH)

    c_mat = jnp.concatenate([cmat(c_re), -cmat(c_im)], axis=-2).astype(BF16)
    return ab_row, b_mat, c_mat


def _s5_kernel(*refs, R, n_slab, chained, want_final):
    if chained:
        u_ref, h0_ref, ab_ref, bm_ref, cm_ref, d_ref = refs[:6]
        rest = refs[6:]
    else:
        u_ref, ab_ref, bm_ref, cm_ref, d_ref = refs[:5]
        h0_ref = None
        rest = refs[5:]
    y_ref = rest[0]
    rest = rest[1:]
    if want_final:
        hfin_ref = rest[0]
        rest = rest[1:]
    H = rest[0]
    if chained:
        PW, CIN = rest[1], rest[2]
    NR = R * n_slab
    RC = min(512, NR)
    ST = S5_SLICE_ST
    nchunk = ST // LANES

    y_ref[...] = u_ref[...] * d_ref[...]

    for d in range(N_DIR):
        def slab_of(i, d=d):
            return i if d == 0 else n_slab - 1 - i

        def bu_body(i, _, d=d):
            r0 = pl.multiple_of(i * RC, RC)
            H[pl.ds(r0, RC), :] = jnp.dot(u_ref[pl.ds(r0, RC), :].astype(BF16), bm_ref[d],
                                           preferred_element_type=F32)
            return 0
        lax.fori_loop(0, NR // RC, bu_body, 0)

        for c in range(nchunk):
            lre = slice(c * LANES, (c + 1) * LANES)
            lim = slice(ST + c * LANES, ST + (c + 1) * LANES)
            a_re = jnp.broadcast_to(ab_ref[d, :, lre], (R, LANES))
            a_im = jnp.broadcast_to(ab_ref[d, :, lim], (R, LANES))

            def step(i, carry, lre=lre, lim=lim, a_re=a_re, a_im=a_im, slab_of=slab_of):
                cr, ci = carry
                r0 = pl.multiple_of(slab_of(i) * R, R)
                nr = a_re * cr - a_im * ci + H[pl.ds(r0, R), lre]
                ni = a_re * ci + a_im * cr + H[pl.ds(r0, R), lim]
                H[pl.ds(r0, R), lre] = nr
                H[pl.ds(r0, R), lim] = ni
                return nr, ni
            zero = jnp.zeros((R, LANES), F32)
            lax.fori_loop(0, n_slab, step, (zero, zero))

        last0 = (n_slab - 1) * R if d == 0 else 0
        if chained:
            a_re_row = ab_ref[d, :, 0:ST]
            a_im_row = ab_ref[d, :, ST:2 * ST]

            def pw_step(j, carry, a_re_row=a_re_row, a_im_row=a_im_row):
                pr, pi = carry
                PW[j, :, 0:ST] = jnp.broadcast_to(pr, (8, ST))
                PW[j, :, ST:2 * ST] = jnp.broadcast_to(pi, (8, ST))
                return pr * a_re_row - pi * a_im_row, pr * a_im_row + pi * a_re_row
            lax.fori_loop(0, n_slab, pw_step, (a_re_row, a_im_row))
            t_re = PW[n_slab - 1, 0:1, 0:ST]
            t_im = PW[n_slab - 1, 0:1, ST:2 * ST]

            cr = h0_ref[d, :, 0:ST]
            ci = h0_ref[d, :, ST:2 * ST]
            for i in range(R):
                c = i if d == 0 else R - 1 - i
                CIN[c:c + 1, 0:ST] = cr
                CIN[c:c + 1, ST:2 * ST] = ci
                er = H[last0 + c:last0 + c + 1, 0:ST]
                ei = H[last0 + c:last0 + c + 1, ST:2 * ST]
                cr, ci = t_re * cr - t_im * ci + er, t_re * ci + t_im * cr + ei

            def fix_step(i, _, slab_of=slab_of):
                r0 = pl.multiple_of(slab_of(i) * R, R)
                p = PW[i]
                for c in range(nchunk):
                    lre = slice(c * LANES, (c + 1) * LANES)
                    lim = slice(ST + c * LANES, ST + (c + 1) * LANES)
                    pr = p[0:1, lre]
                    pi = p[0:1, lim]
                    cr = CIN[:, lre]
                    ci = CIN[:, lim]
                    H[pl.ds(r0, R), lre] = H[pl.ds(r0, R), lre] + (pr * cr - pi * ci)
                    H[pl.ds(r0, R), lim] = H[pl.ds(r0, R), lim] + (pr * ci + pi * cr)
                return 0
            lax.fori_loop(0, n_slab, fix_step, 0)

        if want_final:
            hfin_ref[d] = H[last0:last0 + R, :]

        def y_body(i, _, d=d):
            r0 = pl.multiple_of(i * RC, RC)
            y_ref[pl.ds(r0, RC), :] = y_ref[pl.ds(r0, RC), :] + jnp.dot(
                H[pl.ds(r0, RC), :].astype(BF16), cm_ref[d], preferred_element_type=F32)
            return 0
        lax.fori_loop(0, NR // RC, y_body, 0)


def _s5_scan(u, h0, tables, d_row, *, R, n_slab, chained, want_final):
    ab_row, b_mat, c_mat = tables
    Q, NR, _ = u.shape
    nfs = S5_GROUPS // 8
    W2 = 2 * S5_SLICE_ST
    in_specs = [pl.BlockSpec((None, NR, S5_SLICE_CH), lambda q, f: (q, 0, f))]
    args = [u]
    if chained:
        in_specs.append(pl.BlockSpec((None, N_DIR, None, 1, W2), lambda q, f: (q, 0, f, 0, 0)))
        args.append(h0)
    in_specs += [
        pl.BlockSpec((N_DIR, None, 1, W2), lambda q, f: (0, f, 0, 0)),
        pl.BlockSpec((N_DIR, None, S5_SLICE_CH, W2), lambda q, f: (0, f, 0, 0)),
        pl.BlockSpec((N_DIR, None, W2, S5_SLICE_CH), lambda q, f: (0, f, 0, 0)),
        pl.BlockSpec((1, S5_SLICE_CH), lambda q, f: (0, f)),
    ]
    args += [ab_row, b_mat, c_mat, d_row]
    out_shape = [jax.ShapeDtypeStruct((Q, NR, S5_WIDTH), F32)]
    out_specs = [pl.BlockSpec((None, NR, S5_SLICE_CH), lambda q, f: (q, 0, f))]
    if want_final:
        out_shape.append(jax.ShapeDtypeStruct((Q, N_DIR, nfs, R, W2), F32))
        out_specs.append(pl.BlockSpec((None, N_DIR, None, R, W2), lambda q, f: (q, 0, f, 0, 0)))
    scratch = [pltpu.VMEM((NR, W2), F32)]
    if chained:
        scratch += [pltpu.VMEM((n_slab, 8, W2), F32), pltpu.VMEM((R, W2), F32)]
    res = pl.pallas_call(
        functools.partial(_s5_kernel, R=R, n_slab=n_slab, chained=chained, want_final=want_final),
        grid=(Q, nfs),
        in_specs=in_specs,
        out_specs=tuple(out_specs),
        out_shape=tuple(out_shape),
        scratch_shapes=scratch,
        compiler_params=_cparams(("parallel", "parallel")),
        name="s5_scan_chained" if chained else "s5_scan",
    )(*args)
    return res


def _ssd_kernel(*refs, TB, nb, has_h0, want_final):
    io = [refs[4 * d:4 * d + 4] for d in range(N_DIR)]
    refs = refs[8:]
    if has_h0:
        h0_ref = refs[0]
        refs = refs[1:]
    cw_ref, cb_ref, dtb_ref, arow_ref, sel_ref, drow_ref = refs[:6]
    refs = refs[6:]
    y_refs = refs[:2]
    refs = refs[2:]
    if want_final:
        hfin_ref = refs[0]
        refs = refs[1:]
    hst, xc_s, dt_s = refs
    CH = SSD_CHUNK
    P = SSD_HEADDIM
    j = pl.program_id(1)

    @pl.when(j == 0)
    def _():
        if has_h0:
            hst[...] = h0_ref[...]
        else:
            hst[...] = jnp.zeros_like(hst)

    li = lax.broadcasted_iota(jnp.int32, (CH, CH), 0)
    si = lax.broadcasted_iota(jnp.int32, (CH, CH), 1)
    tmats = ((si <= li).astype(F32), (si >= li).astype(F32))
    rows = lax.broadcasted_iota(jnp.int32, (TB, 1), 0)
    for d in range(N_DIR):
        xbc_ref, xp_ref, xn_ref, dt_ref = io[d]
        jj = j if d == 0 else nb - 1 - j
        x = xbc_ref[...]
        prev_row = xp_ref[7:8, :] * (jj > 0).astype(F32)
        next_row = xn_ref[0:1, :] * (jj < nb - 1).astype(F32)
        x_prev = jnp.where(rows == 0, prev_row, pltpu.roll(x, 1, 0))
        x_next = jnp.where(rows == TB - 1, next_row, pltpu.roll(x, TB - 1, 0))
        conv = cw_ref[0:1, :] * x_prev + cw_ref[1:2, :] * x + cw_ref[2:3, :] * x_next + cb_ref[...]
        xc_s[d] = conv * _sigmoid(conv)
        dtf = _softplus(dt_ref[...] + dtb_ref[...])
        dt_s[d] = _dot_exact_rhs(dtf, sel_ref[d])

    hrow = lax.broadcasted_iota(jnp.int32, (LANES, SSD_WIDTH), 0)
    e_head = jnp.where(lax.broadcasted_iota(jnp.int32, (LANES, SSD_WIDTH), 1) // P == hrow, 1.0, 0.0)
    hrow2 = lax.broadcasted_iota(jnp.int32, (LANES, SSD_HEADS * CH), 0)
    e_chunk = jnp.where(lax.broadcasted_iota(jnp.int32, (LANES, SSD_HEADS * CH), 1) // CH == hrow2, 1.0, 0.0)

    n_ch = TB // CH
    for i, d in [(i, d) for i in range(n_ch) for d in range(N_DIR)]:
        tmat = tmats[d]
        y_ref = y_refs[d]
        r0 = (i if d == 0 else n_ch - 1 - i) * CH
        dtc = dt_s[d, r0:r0 + CH, :]
        dA = dtc * arow_ref[d]
        cs = _dot_exact_lhs(tmat, dA)
        csT = cs.T
        dt_x = _dot_exact_rhs(dtc, e_head)
        cs_x = _dot_exact_rhs(cs, e_head)
        tot_x = jnp.sum(_dot_exact_rhs(dA, e_head), axis=0, keepdims=True)
        cs_xx = _dot_exact_rhs(cs, e_chunk)
        xs_all = xc_s[d, r0:r0 + CH, 0:SSD_WIDTH]
        xdt_all = xs_all * dt_x
        xd_all = xdt_all * jnp.exp(tot_x - cs_x)
        ecs_all = jnp.exp(cs_x)
        etot_all = jnp.exp(tot_x)
        Bm = xc_s[d, r0:r0 + CH, SSD_WIDTH:SSD_WIDTH + SSD_BC]
        Cm = xc_s[d, r0:r0 + CH, SSD_WIDTH + SSD_BC:SSD_WIDTH + 2 * SSD_BC]
        BmT = Bm.T
        for g in range(SSD_GROUPS):
            Cg = Cm[:, g * SSD_STATE:(g + 1) * SSD_STATE].astype(BF16)
            Bg = Bm[:, g * SSD_STATE:(g + 1) * SSD_STATE].astype(BF16)
            BgT = BmT[g * SSD_STATE:(g + 1) * SSD_STATE, :].astype(BF16)
            G = lax.dot_general(Cg, Bg, (((1,), (1,)), ((), ())), preferred_element_type=F32)
            for hh in range(SSD_HEADS // SSD_GROUPS):
                h = g * (SSD_HEADS // SSD_GROUPS) + hh
                hl = slice(h * P, (h + 1) * P)
                row = csT[h:h + 1, :]
                lm = jnp.exp(jnp.where(tmat > 0.0, cs_xx[:, h * CH:(h + 1) * CH] - row, -1e30))
                hprev = hst[d, h]
                y = jnp.dot((G * lm).astype(BF16), xdt_all[:, hl].astype(BF16), preferred_element_type=F32)
                y = y + jnp.dot(Cg, hprev.astype(BF16), preferred_element_type=F32) * ecs_all[:, hl]
                if d == 0:
                    y = y + drow_ref[:, hl] * xs_all[:, hl]
                y_ref[r0:r0 + CH, hl] = y
                hst[d, h] = etot_all[:, hl] * hprev + jnp.dot(BgT, xd_all[:, hl].astype(BF16),
                                                             preferred_element_type=F32)

    if want_final:
        @pl.when(j == nb - 1)
        def _():
            hfin_ref[...] = hst[...]


def _ssd_scan(xbc, dt, h0, conv_w, conv_b, dt_bias, a_log, d_skip, *, B, L, want_final):
    TB = min(512, L)
    nb = L // TB
    H = SSD_HEADS

    nrow8 = B * L // 8
    in_specs, args = [], []
    blks = (lambda b, j: b * nb + j, lambda b, j: b * nb + nb - 1 - j)
    for blk in blks:
        in_specs += [
            pl.BlockSpec((TB, SSD_CONV_CH), lambda b, j, blk=blk: (blk(b, j), 0)),
            pl.BlockSpec((8, SSD_CONV_CH), lambda b, j, blk=blk: (jnp.maximum(blk(b, j) * (TB // 8) - 1, 0), 0)),
            pl.BlockSpec((8, SSD_CONV_CH),
                         lambda b, j, blk=blk: (jnp.minimum((blk(b, j) + 1) * (TB // 8), nrow8 - 1), 0)),
            pl.BlockSpec((TB, DT_PAD), lambda b, j, blk=blk: (blk(b, j), 0)),
        ]
        args += [xbc, xbc, xbc, dt]
    if h0 is not None:
        in_specs.append(pl.BlockSpec((None, N_DIR, H, SSD_STATE, SSD_HEADDIM), lambda b, j: (b, 0, 0, 0, 0)))
        args.append(h0)
    dtb = jnp.pad(dt_bias.reshape(1, N_DIR * H), ((0, 0), (0, DT_PAD - N_DIR * H)))
    arow = jnp.pad(-jnp.exp(a_log), ((0, 0), (0, LANES - H))).reshape(N_DIR, 1, LANES)
    lane = jnp.arange(LANES)
    sel = jnp.stack([(lane[:, None] == (dd * H + lane[None, :])) & (lane[None, :] < H) for dd in range(N_DIR)]).astype(F32)
    drow = jnp.repeat(d_skip, SSD_HEADDIM).reshape(1, SSD_WIDTH)
    in_specs += [
        _full((3, SSD_CONV_CH)), _full((1, SSD_CONV_CH)), _full((1, DT_PAD)),
        _full((N_DIR, 1, LANES)), _full((N_DIR, LANES, LANES)), _full((1, SSD_WIDTH)),
    ]
    args += [conv_w, conv_b.reshape(1, SSD_CONV_CH), dtb, arow, sel, drow]
    out_shape = [jax.ShapeDtypeStruct((B * L, SSD_WIDTH), F32)] * N_DIR
    out_specs = [pl.BlockSpec((TB, SSD_WIDTH), lambda b, j, blk=blk: (blk(b, j), 0)) for blk in blks]
    if want_final:
        out_shape.append(jax.ShapeDtypeStruct((B, N_DIR, H, SSD_STATE, SSD_HEADDIM), F32))
        out_specs.append(pl.BlockSpec((None, N_DIR, H, SSD_STATE, SSD_HEADDIM), lambda b, j: (b, 0, 0, 0, 0)))
    return pl.pallas_call(
        functools.partial(_ssd_kernel, TB=TB, nb=nb, has_h0=h0 is not None, want_final=want_final),
        grid=(B, nb),
        in_specs=in_specs,
        out_specs=tuple(out_specs),
        out_shape=tuple(out_shape),
        scratch_shapes=[pltpu.VMEM((N_DIR, H, SSD_STATE, SSD_HEADDIM), F32),
                        pltpu.VMEM((N_DIR, TB, SSD_CONV_CH), F32), pltpu.VMEM((N_DIR, TB, DT_PAD), F32)],
        compiler_params=_cparams(("parallel", "arbitrary")),
        name="ssd_scan",
    )(*args)


def _rwkv_prep_kernel(rkv_ref, rp_ref, rn_ref, lo_ref, mu_ref, a0_ref, aup_ref, gup_ref, w0_ref, wup_ref,
                      kkw_ref, ka_ref, rk_ref, seg_ref,
                      r_ref, w0o_ref, w1o_ref, k_ref, v_ref, kk_ref, b_ref, g_ref, bonus_ref, *, nbs):
    tm = rkv_ref.shape[0]
    W = RWKV_WIDTH
    jj = pl.program_id(0) % nbs
    x = rkv_ref[...]
    prev_row = rp_ref[7:8, :] * (jj > 0).astype(F32)
    next_row = rn_ref[0:1, :] * (jj < nbs - 1).astype(F32)
    rows = lax.broadcasted_iota(jnp.int32, (tm, 1), 0)
    xp = jnp.where(rows == 0, prev_row, pltpu.roll(x, 1, 0))
    xn = jnp.where(rows == tm - 1, next_row, pltpu.roll(x, tm - 1, 0))
    x = x + mu_ref[0:1, :] * (xp - x) + mu_ref[1:2, :] * (xn - x)
    r = x[:, 0:W]
    k = x[:, W:2 * W]
    v = x[:, 2 * W:3 * W]
    lo = lo_ref[...]
    seg = seg_ref[...]

    def put(ref, val):
        ref[:, 0:W] = val
        if ref.shape[1] > W:
            ref[:, W:] = jnp.zeros((tm, ref.shape[1] - W), F32)

    a = _sigmoid(a0_ref[...] + _dot3(lo, aup_ref[...]))
    g_ref[...] = _dot3(_sigmoid(lo), gup_ref[...])
    tlo = jnp.tanh(lo)
    for d, o_ref in enumerate((w0o_ref, w1o_ref)):
        zw = w0_ref[d:d + 1, :] + _dot3(tlo, wup_ref[d])
        put(o_ref, jnp.exp(-DECAY_SCALE * _sigmoid(zw)))
    kk = k * kkw_ref[...]
    kk = kk * lax.rsqrt(jnp.maximum(_dot_exact_rhs(kk * kk, seg), 1e-24))
    k2 = k * (1.0 + (a - 1.0) * ka_ref[...])
    put(r_ref, r)
    put(k_ref, k2)
    put(v_ref, v)
    put(kk_ref, kk)
    put(b_ref, kk * a)
    bonus_ref[...] = _dot_exact_rhs(r * k2 * rk_ref[...], seg) * v


def _segment_ones(width, seg):
    i = jnp.arange(width)
    return (i[:, None] // seg == i[None, :] // seg).astype(F32)


def _rwkv_prep(rkv, lo, lp, seq_blocks, scan_width=RWKV_WIDTH):
    ntok = rkv.shape[0]
    tm = TOKEN_BLOCK
    W = RWKV_WIDTH
    nrow8 = ntok // 8
    nbs_of = seq_blocks
    widths = (scan_width,) * 7 + (W, W)
    pad_rows = lambda t, r0: jnp.zeros((LO_PAD, W), F32).at[r0:r0 + t.shape[0]].set(t)
    aup = pad_rows(lp['rwkv_a_up'], N_DIR * W_RANK)
    gup = pad_rows(lp['rwkv_g_up'], N_DIR * W_RANK + A_RANK)
    wup = jnp.stack([pad_rows(lp['rwkv_w_up'][d], d * W_RANK) for d in range(N_DIR)])
    row = lambda t: t.reshape(1, W)
    outs = tuple(jax.ShapeDtypeStruct((ntok, w), F32) for w in widths)
    return pl.pallas_call(
        functools.partial(_rwkv_prep_kernel, nbs=nbs_of),
        grid=(ntok // tm,),
        in_specs=[
            pl.BlockSpec((tm, 3 * W), lambda i: (i, 0)),
            pl.BlockSpec((8, 3 * W), lambda i: (jnp.maximum(i * (tm // 8) - 1, 0), 0)),
            pl.BlockSpec((8, 3 * W), lambda i: (jnp.minimum((i + 1) * (tm // 8), nrow8 - 1), 0)),
            pl.BlockSpec((tm, LO_PAD), lambda i: (i, 0)),
            _full((2, 3 * W)), _full((1, W)), _full((LO_PAD, W)), _full((LO_PAD, W)), _full((2, W)),
            _full((N_DIR, LO_PAD, W)), _full((1, W)), _full((1, W)), _full((1, W)), _full((W, W)),
        ],
        out_specs=tuple(pl.BlockSpec((tm, w), lambda i: (i, 0)) for w in widths),
        out_shape=outs,
        compiler_params=_cparams(("parallel",)),
        name="rwkv_prep",
    )(rkv, rkv, rkv, lo, lp['rwkv_mu'], row(lp['rwkv_a0']), aup, gup, lp['rwkv_w0'], wup,
      row(lp['rwkv_k_k']), row(lp['rwkv_k_a']), row(lp['rwkv_r_k']), lp['seg'])


def _rwkv_scan_kernel(rf, rm, wf, wm, kf, km, kkf, kkm, bf, bm, vf, vm, mask_ref, s0_ref, yf_ref, yb_ref, sfin_ref,
                      S, SA, OPS, VTS, *, TBLK, V, nblk):
    K = RWKV_HEADSIZE
    VH = V // 2
    tb = pl.program_id(1)

    @pl.when(tb == 0)
    def _():
        S[...] = s0_ref[...]

    is_fwd = mask_ref[...] > 0.5
    fwd_k = jnp.broadcast_to(is_fwd, (K, LANES))
    fwd_v = jnp.broadcast_to(is_fwd, (V, LANES))

    def pick(t, _):
        tm_ = TBLK - 1 - t
        for j, (f, m) in enumerate(((rf, rm), (wf, wm), (kf, km), (kkf, kkm), (bf, bm))):
            OPS[t, j] = jnp.where(fwd_k, f[t], m[tm_])
        VTS[t] = jnp.where(fwd_v, vf[t], vm[tm_])
        return 0
    lax.fori_loop(0, TBLK, pick, 0)
    R_, W_, K_, KK_, B_ = range(5)

    def bc(j, t, kx):
        return jnp.broadcast_to(OPS[t, j, pl.ds(kx, 1), :], (VH, LANES))

    def tree(parts):
        return (parts[0] + parts[1]) + (parts[2] + parts[3])

    for hv in range(2):
        rows = pl.ds(hv * VH, VH)
        accs = [None] * 4
        for kx in range(K):
            p = S[kx, rows, :] * bc(KK_, 0, kx)
            accs[kx % 4] = p if accs[kx % 4] is None else accs[kx % 4] + p
        SA[hv] = tree(accs)

    def step(t, _):
        tn = jnp.minimum(t + 1, TBLK - 1)
        tm_ = TBLK - 1 - t
        for hv in range(2):
            rows = pl.ds(hv * VH, VH)
            sa = SA[hv]
            vt = VTS[t, rows, :]
            yacc = [None] * 4
            sacc = [None] * 4
            for kx in range(K):
                s_new = S[kx, rows, :] * bc(W_, t, kx) - sa * bc(B_, t, kx) + vt * bc(K_, t, kx)
                S[kx, rows, :] = s_new
                py = s_new * bc(R_, t, kx)
                ps = s_new * bc(KK_, tn, kx)
                a = kx % 4
                yacc[a] = py if yacc[a] is None else yacc[a] + py
                sacc[a] = ps if sacc[a] is None else sacc[a] + ps
            y = tree(yacc)
            yf_ref[t, rows, :] = y
            yb_ref[tm_, rows, :] = y
            SA[hv] = tree(sacc)
        return 0
    lax.fori_loop(0, TBLK, step, 0)

    @pl.when(tb == nblk - 1)
    def _():
        sfin_ref[...] = S[...]


def _rwkv_scan(r, w, k, kk, b, v, fwd_mask, s0):
    L, K, NL = r.shape
    V = v.shape[1]
    TBLK = min(32, L)
    nblk = L // TBLK
    ngrp = NL // LANES
    kf = pl.BlockSpec((TBLK, K, LANES), lambda g, t: (t, 0, g))
    km = pl.BlockSpec((TBLK, K, LANES), lambda g, t: (nblk - 1 - t, 0, g))
    vf = pl.BlockSpec((TBLK, V, LANES), lambda g, t: (t, 0, g))
    vm = pl.BlockSpec((TBLK, V, LANES), lambda g, t: (nblk - 1 - t, 0, g))
    sspec = pl.BlockSpec((K, V, LANES), lambda g, t: (0, 0, g))
    return pl.pallas_call(
        functools.partial(_rwkv_scan_kernel, TBLK=TBLK, V=V, nblk=nblk),
        grid=(ngrp, nblk),
        in_specs=[kf, km] * 5 + [vf, vm, pl.BlockSpec((1, LANES), lambda g, t: (0, g)), sspec],
        out_specs=(vf, vm, sspec),
        out_shape=(jax.ShapeDtypeStruct((L, V, NL), F32), jax.ShapeDtypeStruct((L, V, NL), F32),
                   jax.ShapeDtypeStruct((K, V, NL), F32)),
        scratch_shapes=[pltpu.VMEM((K, V, LANES), F32), pltpu.VMEM((2, V // 2, LANES), F32),
                        pltpu.VMEM((TBLK, 5, K, LANES), F32), pltpu.VMEM((TBLK, V, LANES), F32)],
        compiler_params=_cparams(("parallel", "arbitrary")),
        name="rwkv_scan",
    )(r, r, w, w, k, k, kk, kk, b, b, v, v, fwd_mask, s0)


KQ = 4
CH = LANES // KQ


def _rwkv_scan2_kernel(rf, rm, wf, wm, kf, km, kkf, kkm, bf, bm, vf, vm, s0_ref, yf_ref, yb_ref, S, YQ, SA, *, TBLK):
    V = RWKV_HEADSIZE
    NQ = RWKV_HEADSIZE // KQ
    tb = pl.program_id(0)

    @pl.when(tb == 0)
    def _():
        S[...] = s0_ref[...]

    VH = V // 2

    def bc(ref, tt, q):
        return jnp.broadcast_to(ref[tt, pl.ds(q, 1), :], (VH, LANES))

    def all_quarters(x):
        return (x + pltpu.roll(x, CH, 1)) + (pltpu.roll(x, 2 * CH, 1) + pltpu.roll(x, 3 * CH, 1))

    dirs = ((rf, wf, kf, kkf, bf, vf), (rm, wm, km, kkm, bm, vm))
    time_of = (lambda t: t, lambda t: TBLK - 1 - t)
    chains = [(d, hv) for d in range(N_DIR) for hv in range(2)]

    def first_sa(d, hv):
        kk_ = dirs[d][3]
        tt = time_of[d](0)
        acc = [None, None]
        for q in range(NQ):
            p = S[d, q, pl.ds(hv * VH, VH), :] * bc(kk_, tt, q)
            acc[q % 2] = p if acc[q % 2] is None else acc[q % 2] + p
        return acc[0] + acc[1]

    TG = min(16, TBLK)
    lane_r = lax.broadcasted_iota(jnp.int32, (2 * LANES, CH), 0) % CH
    fold2 = jnp.where(lax.broadcasted_iota(jnp.int32, (2 * LANES, CH), 1) == lane_r, 1.0, 0.0).astype(BF16)

    last = len(chains) - 1
    for c, (d, hv) in enumerate(chains):
        part = first_sa(d, hv)
        SA[c] = part if c == last else all_quarters(part)

    def step(t, _):
        sa_last = all_quarters(SA[last])
        for c, (d, hv) in enumerate(chains):
            r_, w_, k_, kk_, b_, v_ = dirs[d]
            tt = time_of[d](t)
            tn = time_of[d](jnp.minimum(t + 1, TBLK - 1))
            rows = pl.ds(hv * VH, VH)
            sa = sa_last if c == last else SA[c]
            vt = v_[tt, rows, :]
            yacc = None
            sacc = None
            for q in range(NQ):
                s_new = S[d, q, rows, :] * bc(w_, tt, q) - sa * bc(b_, tt, q) + vt * bc(k_, tt, q)
                S[d, q, rows, :] = s_new
                py = s_new * bc(r_, tt, q)
                ps = s_new * bc(kk_, tn, q)
                yacc = py if yacc is None else yacc + py
                sacc = ps if sacc is None else sacc + ps
            YQ[d, tt, rows, :] = yacc
            SA[c] = sacc if c == last else all_quarters(sacc)
        return 0
    lax.fori_loop(0, TBLK, step, 0)

    def finish(i, _):
        for d, y_ in enumerate((yf_ref, yb_ref)):
            hi, lo = _split_bf16(YQ[d, pl.ds(i * TG, TG)].reshape(TG * V, LANES))
            y = jnp.dot(jnp.concatenate([hi, lo], axis=-1), fold2, preferred_element_type=F32)
            y_[pl.ds(i * TG, TG)] = y.reshape(TG, V, CH)
        return 0
    lax.fori_loop(0, TBLK // TG, finish, 0)


def _rwkv_scan2(r, w0, w1, k, kk, b, v, s0):
    L, NQ, _ = r.shape
    V = RWKV_HEADSIZE
    TBLK = min(64, L)
    nblk = L // TBLK
    fspec = pl.BlockSpec((TBLK, NQ, LANES), lambda t: (t, 0, 0))
    mspec = pl.BlockSpec((TBLK, NQ, LANES), lambda t: (nblk - 1 - t, 0, 0))
    vfspec = pl.BlockSpec((TBLK, V, LANES), lambda t: (t, 0, 0))
    vmspec = pl.BlockSpec((TBLK, V, LANES), lambda t: (nblk - 1 - t, 0, 0))
    yfspec = pl.BlockSpec((TBLK, V, CH), lambda t: (t, 0, 0))
    ymspec = pl.BlockSpec((TBLK, V, CH), lambda t: (nblk - 1 - t, 0, 0))
    yf, yb = pl.pallas_call(
        functools.partial(_rwkv_scan2_kernel, TBLK=TBLK),
        grid=(nblk,),
        in_specs=[fspec, mspec] * 5 + [vfspec, vmspec, _full((N_DIR, NQ, V, LANES))],
        out_specs=(yfspec, ymspec),
        out_shape=(jax.ShapeDtypeStruct((L, V, CH), F32), jax.ShapeDtypeStruct((L, V, CH), F32)),
        scratch_shapes=[pltpu.VMEM((N_DIR, NQ, V, LANES), F32), pltpu.VMEM((N_DIR, TBLK, V, LANES), F32),
                        pltpu.VMEM((2 * N_DIR, V // 2, LANES), F32)],
        compiler_params=_cparams(("arbitrary",)),
        name="rwkv_scan2",
    )(r, r, w0, w1, k, k, kk, kk, b, b, v, v, s0)
    return yf, yb


HEADS_PAD = 8
SCAN_WIDTH = HEADS_PAD * RWKV_HEADSIZE


def _pad_chains(x, B):
    return x if B * HEADS_PAD == CH else jnp.pad(x, ((0, 0),) * (x.ndim - 1) + ((0, CH - B * HEADS_PAD),))


def _to_scan2_k(t, B, L):
    x = t.reshape(B, L, HEADS_PAD, RWKV_HEADSIZE // KQ, KQ).transpose(1, 3, 4, 0, 2)
    x = _pad_chains(x.reshape(L, RWKV_HEADSIZE // KQ, KQ, B * HEADS_PAD), B)
    return x.reshape(L, RWKV_HEADSIZE // KQ, LANES)


def _to_scan2_v(t, B, L):
    x = t.reshape(B, L, HEADS_PAD, RWKV_HEADSIZE).transpose(1, 3, 0, 2).reshape(L, RWKV_HEADSIZE, B * HEADS_PAD)
    return jnp.tile(_pad_chains(x, B), (1, 1, KQ))


def _from_scan2(y, B, L):
    y = y[:, :, :B * HEADS_PAD]
    return y.reshape(L, RWKV_HEADSIZE, B, HEADS_PAD).transpose(2, 0, 3, 1).reshape(B * L, SCAN_WIDTH)


def _state_to_scan2(s):
    B = s.shape[0]
    s = jnp.pad(s, ((0, 0), (0, 0), (0, HEADS_PAD - RWKV_HEADS), (0, 0), (0, 0)))
    x = s.transpose(1, 4, 3, 0, 2).reshape(N_DIR, RWKV_HEADSIZE // KQ, KQ, RWKV_HEADSIZE, B * HEADS_PAD)
    x = _pad_chains(x, B)
    return x.transpose(0, 1, 3, 2, 4).reshape(N_DIR, RWKV_HEADSIZE // KQ, RWKV_HEADSIZE, LANES)


def _ctx_lanes(B):
    chains = B * N_DIR * RWKV_HEADS
    return chains, -(-chains // LANES) * LANES


def _to_ctx(t_fwd, t_bwd, B, L):
    chains, lanes = _ctx_lanes(B)
    if t_bwd is t_fwd:
        x = t_fwd.reshape(B, L, RWKV_HEADS, RWKV_HEADSIZE).transpose(1, 3, 0, 2)
        x = jnp.broadcast_to(x[:, :, :, None, :], (L, RWKV_HEADSIZE, B, N_DIR, RWKV_HEADS))
    else:
        x = jnp.stack([t_fwd, t_bwd], 0).reshape(N_DIR, B, L, RWKV_HEADS, RWKV_HEADSIZE).transpose(2, 4, 1, 0, 3)
    x = x.reshape(L, RWKV_HEADSIZE, chains)
    return x if lanes == chains else jnp.pad(x, ((0, 0), (0, 0), (0, lanes - chains)))


def _ctx_fwd_mask(B):
    chains, lanes = _ctx_lanes(B)
    lane = jnp.arange(lanes)
    return (((lane // RWKV_HEADS) % N_DIR == 0) & (lane < chains)).astype(F32).reshape(1, lanes)


def _from_ctx(y, B, L, d):
    chains, _ = _ctx_lanes(B)
    y = y[:, :, :chains].reshape(L, RWKV_HEADSIZE, B, N_DIR, RWKV_HEADS)[:, :, :, d]
    return y.transpose(2, 0, 3, 1).reshape(B * L, RWKV_WIDTH)


def _ctx_state_from(x, B):
    chains, _ = _ctx_lanes(B)
    x = x[:, :, :chains].reshape(RWKV_HEADSIZE, RWKV_HEADSIZE, B, N_DIR, RWKV_HEADS)
    return x.transpose(2, 3, 4, 1, 0)


def _post_kernel(x_ref, mod_ref, ys5_ref, yf_ref, yb_ref, z_ref, rf_ref, rb_ref, bonus_ref, g_ref,
                 gluw_ref, glub_ref, ssdg_ref, lng_ref, lnb_ref, seg_ref, wout_ref, n2g_ref, rw_ref,
                 x1_ref, hb_ref, aff_ref):
    m = mod_ref[...]
    D = D_MODEL
    zg = _gelu_tanh(ys5_ref[...])
    gate = jnp.dot(zg.astype(BF16), gluw_ref[...], preferred_element_type=F32) + glub_ref[...]
    y_a = zg * _sigmoid(gate)
    z = z_ref[...]
    yb = (yf_ref[...] + yb_ref[...]) * (z * _sigmoid(z))
    y_b = yb * lax.rsqrt(jnp.mean(yb * yb, axis=-1, keepdims=True) + EPS) * ssdg_ref[...]
    seg = seg_ref[...] * (1.0 / RWKV_HEADSIZE)
    yr = rf_ref[...] + rb_ref[...]
    mean = _dot_exact_rhs(yr, seg)
    cen = yr - mean
    var = jnp.dot((cen * cen).astype(BF16), seg.astype(BF16), preferred_element_type=F32)
    yn = cen * lax.rsqrt(var + GN_EPS) * lng_ref[...] + lnb_ref[...]
    y_c = (yn + bonus_ref[...]) * g_ref[...]
    o = jnp.dot(y_a.astype(BF16), wout_ref[0:S5_WIDTH, :], preferred_element_type=F32)
    o = o + jnp.dot(y_b.astype(BF16), wout_ref[S5_WIDTH:S5_WIDTH + SSD_WIDTH, :], preferred_element_type=F32)
    o = o + jnp.dot(y_c.astype(BF16), wout_ref[S5_WIDTH + SSD_WIDTH:, :], preferred_element_type=F32)
    x1 = x_ref[...] + m[:, 2 * D:3 * D] * o
    x1_ref[...] = x1
    h2 = x1 * lax.rsqrt(jnp.mean(x1 * x1, axis=-1, keepdims=True) + EPS) * n2g_ref[...]
    h2 = h2 * (1.0 + m[:, 4 * D:5 * D]) + m[:, 3 * D:4 * D]
    hb_ref[...] = h2.astype(BF16)
    logits = _dot3(rw_ref[...], h2, (((1,), (1,)), ((), ())))
    mx = jnp.max(logits, axis=0, keepdims=True)
    ex = jnp.exp(logits - mx)
    aff_ref[...] = ex / jnp.sum(ex, axis=0, keepdims=True)


def _post_mixer(x, mod3, row_of_block, ys5, yssd, z, rf, rb, bonus, g, lp):
    ntok = x.shape[0]
    tm = TOKEN_BLOCK
    tok = lambda w: pl.BlockSpec((tm, w), lambda i: (i, 0))
    row = lambda t: t.reshape(1, -1)
    W = RWKV_WIDTH
    return pl.pallas_call(
        _post_kernel,
        grid=(ntok // tm,),
        in_specs=[
            tok(D_MODEL),
            pl.BlockSpec((None, 1, 6 * D_MODEL), lambda i: (row_of_block(i), 0, 0)),
            tok(S5_WIDTH),
            tok(SSD_WIDTH), tok(SSD_WIDTH),
            tok(SSD_WIDTH), tok(W), tok(W), tok(W), tok(W),
            _full((S5_WIDTH, S5_WIDTH)), _full((1, S5_WIDTH)), _full((1, SSD_WIDTH)), _full((1, W)), _full((1, W)),
            _full((W, W)), _full((D_MODEL, D_MODEL)), _full((1, D_MODEL)), _full((N_EXPERTS, D_MODEL)),
        ],
        out_specs=(tok(D_MODEL), tok(D_MODEL), pl.BlockSpec((N_EXPERTS, tm), lambda i: (0, i))),
        out_shape=(jax.ShapeDtypeStruct((ntok, D_MODEL), F32), jax.ShapeDtypeStruct((ntok, D_MODEL), BF16),
                   jax.ShapeDtypeStruct((N_EXPERTS, ntok), F32)),
        compiler_params=_cparams(("parallel",)),
        name="post_mixer",
    )(x, mod3, ys5, yssd[0], yssd[1], z, rf, rb, bonus, g,
      lp['s5_glu_w'], row(lp['s5_glu_b']), row(lp['ssd_norm_g']), row(lp['rwkv_ln_g']),
      row(lp['rwkv_ln_b']), lp['seg'], lp['w_out'], row(lp['norm2_g']),
      lp['router_w'].T)


def _select_kernel(aff_ref, slot_ref, affo_ref, *, n, cap, bg):
    for g in range(bg):
        _select_one(aff_ref[:, g * n:(g + 1) * n], slot_ref.at[g], affo_ref.at[g], n, cap)


def _select_one(a, slot_ref, affo_ref, n, cap):
    E = N_EXPERTS
    bits = pltpu.bitcast(a, jnp.int32)
    thr = jnp.zeros((E, 1), jnp.int32)
    capf = float(cap)
    for bit in range(30, -1, -1):
        cand = thr | (1 << bit)
        cnt = jnp.sum(jnp.where(bits >= cand, 1.0, 0.0), axis=1, keepdims=True)
        thr = jnp.where(cnt >= capf, cand, thr)
    gt = bits > thr
    eq = bits == thr
    need = capf - jnp.sum(jnp.where(gt, 1.0, 0.0), axis=1, keepdims=True)
    CW = min(256, n)
    ui = lax.broadcasted_iota(jnp.int32, (CW, CW), 0)
    uj = lax.broadcasted_iota(jnp.int32, (CW, CW), 1)
    upper = jnp.where(ui < uj, 1.0, 0.0).astype(BF16)

    def excl_cumsum(mask_f):
        outs = []
        off = jnp.zeros((E, 1), F32)
        for c in range(n // CW):
            mc = mask_f[:, c * CW:(c + 1) * CW]
            outs.append(jnp.dot(mc.astype(BF16), upper, preferred_element_type=F32) + off)
            off = off + jnp.sum(mc, axis=1, keepdims=True)
        return jnp.concatenate(outs, axis=1)

    eq_rank = excl_cumsum(jnp.where(eq, 1.0, 0.0))
    sel = jnp.where(gt, 1.0, jnp.where(eq, jnp.where(eq_rank < need, 1.0, 0.0), 0.0))
    pos = excl_cumsum(sel)
    slot = jnp.where(sel > 0.0, pos, -1.0)
    for e in range(E):
        slot_ref[e] = slot[e:e + 1, :]
        affo_ref[e] = a[e:e + 1, :]


def _select(aff, B, n, cap):
    bg = 4 if (B % 4 == 0 and n <= 512) else (2 if B % 2 == 0 else 1)
    spec = pl.BlockSpec((bg, N_EXPERTS, 1, n), lambda b: (b, 0, 0, 0))
    return pl.pallas_call(
        functools.partial(_select_kernel, n=n, cap=cap, bg=bg),
        grid=(B // bg,),
        in_specs=[pl.BlockSpec((N_EXPERTS, bg * n), lambda b: (0, b))],
        out_specs=(spec, spec),
        out_shape=(jax.ShapeDtypeStruct((B, N_EXPERTS, 1, n), F32), jax.ShapeDtypeStruct((B, N_EXPERTS, 1, n), F32)),
        compiler_params=_cparams(("parallel",)),
        name="ec_select",
    )(aff)


def _slot_block_range(slots_f, cap, SB):
    lo = jnp.min(jnp.where(slots_f >= 0.0, slots_f, float(cap))).astype(jnp.int32)
    hi = jnp.max(slots_f).astype(jnp.int32)
    first = lo // SB
    count = jnp.where(hi >= 0, hi // SB - first + 1, 0)
    return first, count


EC_SLOT_BLOCK = 256


def _one_hot_all(slot_ref, aff_ref, n, cap):
    srow = lax.broadcasted_iota(jnp.int32, (cap, n), 0).astype(F32)
    ohs, gates = [], []
    for e in range(N_EXPERTS):
        hit = slot_ref[e] == srow
        ohs.append(jnp.where(hit, 1.0, 0.0).astype(BF16))
        if aff_ref is not None:
            gates.append(jnp.sum(jnp.where(hit, aff_ref[e], 0.0), axis=1, keepdims=True))
    return jnp.concatenate(ohs, axis=0), (jnp.concatenate(gates, axis=0) if gates else None)


def _gather_all_kernel(hb_ref, slot_ref, aff_ref, xs_ref, gate_ref, *, n, cap):
    oh, gate = _one_hot_all(slot_ref, aff_ref, n, cap)
    xs = jnp.dot(oh, hb_ref[...], preferred_element_type=F32)
    xs_ref[...] = xs.astype(BF16).reshape(N_EXPERTS, cap, D_MODEL)
    gate_ref[...] = gate.reshape(N_EXPERTS, cap, 1)


def _gather_kernel(hb_ref, slot_ref, aff_ref, xs_ref, gate_ref, acc, gacc, *, n, cap):
    NC = min(512, n)
    SB = min(EC_SLOT_BLOCK // 2, cap)
    acc[...] = jnp.zeros_like(acc)
    gacc[...] = jnp.zeros_like(gacc)
    srow = lax.broadcasted_iota(jnp.int32, (SB, NC), 0).astype(F32)
    seen = jnp.int32(0)
    for c in range(n // NC):
        sl = slot_ref[:, c * NC:(c + 1) * NC]
        cnt = jnp.sum(jnp.where(sl >= 0.0, 1.0, 0.0)).astype(jnp.int32)
        first = seen // SB
        count = jnp.where(cnt > 0, (seen + cnt - 1) // SB - first + 1, 0)
        seen = seen + cnt
        for j in range(min(cap // SB, NC // SB + 1)):
            @pl.when(j < count)
            def _(c=c, j=j, sl=sl, first=first):
                base = pl.multiple_of((first + j) * SB, SB)
                hit = (sl - base.astype(F32)) == srow
                oh = jnp.where(hit, 1.0, 0.0).astype(BF16)
                acc[pl.ds(base, SB), :] += jnp.dot(oh, hb_ref[c * NC:(c + 1) * NC, :], preferred_element_type=F32)
                gacc[pl.ds(base, SB), :] += jnp.sum(jnp.where(hit, aff_ref[:, c * NC:(c + 1) * NC], 0.0), axis=1,
                                                    keepdims=True)
    xs_ref[...] = acc[...].astype(BF16)
    gate_ref[...] = gacc[...]


def _gather(hb, slot4, aff4, B, n, cap):
    E = N_EXPERTS
    if E * cap <= 512:
        all_spec = pl.BlockSpec((None, E, 1, n), lambda b: (b, 0, 0, 0))
        return pl.pallas_call(
            functools.partial(_gather_all_kernel, n=n, cap=cap),
            grid=(B,),
            in_specs=[pl.BlockSpec((n, D_MODEL), lambda b: (b, 0)), all_spec, all_spec],
            out_specs=(pl.BlockSpec((None, E, cap, D_MODEL), lambda b: (b, 0, 0, 0)),
                       pl.BlockSpec((None, E, cap, 1), lambda b: (b, 0, 0, 0))),
            out_shape=(jax.ShapeDtypeStruct((B, E, cap, D_MODEL), BF16), jax.ShapeDtypeStruct((B, E, cap, 1), F32)),
            compiler_params=_cparams(("parallel",)),
            name="ec_gather_all",
        )(hb, slot4, aff4)
    return pl.pallas_call(
        functools.partial(_gather_kernel, n=n, cap=cap),
        grid=(B, E),
        in_specs=[
            pl.BlockSpec((n, D_MODEL), lambda b, e: (b, 0)),
            pl.BlockSpec((None, None, 1, n), lambda b, e: (b, e, 0, 0)),
            pl.BlockSpec((None, None, 1, n), lambda b, e: (b, e, 0, 0)),
        ],
        out_specs=(pl.BlockSpec((None, None, cap, D_MODEL), lambda b, e: (b, e, 0, 0)),
                   pl.BlockSpec((None, None, cap, 1), lambda b, e: (b, e, 0, 0))),
        out_shape=(jax.ShapeDtypeStruct((B, E, cap, D_MODEL), BF16), jax.ShapeDtypeStruct((B, E, cap, 1), F32)),
        scratch_shapes=[pltpu.VMEM((cap, D_MODEL), F32), pltpu.VMEM((cap, 1), F32)],
        compiler_params=_cparams(("parallel", "arbitrary")),
        name="ec_gather",
    )(hb, slot4, aff4)


def _ffn_kernel(xs_ref, gate_ref, w1_ref, w3_ref, w2_ref, o_ref):
    bg, cap, _ = xs_ref.shape
    x = xs_ref[...].reshape(bg * cap, D_MODEL)
    h1 = jnp.dot(x, w1_ref[...], preferred_element_type=F32)
    h3 = jnp.dot(x, w3_ref[...], preferred_element_type=F32)
    hid = (h1 * _sigmoid(h1) * h3).astype(BF16)
    o = jnp.dot(hid, w2_ref[...], preferred_element_type=F32) * gate_ref[...].reshape(bg * cap, 1)
    o_ref[...] = o.astype(BF16).reshape(bg, cap, D_MODEL)


def _expert_ffn(xs, gate, w1, w3, w2, bg):
    B, E, cap, _ = xs.shape
    return pl.pallas_call(
        _ffn_kernel,
        grid=(E, B // bg),
        in_specs=[
            pl.BlockSpec((bg, None, cap, D_MODEL), lambda e, b: (b, e, 0, 0)),
            pl.BlockSpec((bg, None, cap, 1), lambda e, b: (b, e, 0, 0)),
            pl.BlockSpec((None, D_MODEL, D_EXPERT), lambda e, b: (e, 0, 0)),
            pl.BlockSpec((None, D_MODEL, D_EXPERT), lambda e, b: (e, 0, 0)),
            pl.BlockSpec((None, D_EXPERT, D_MODEL), lambda e, b: (e, 0, 0)),
        ],
        out_specs=pl.BlockSpec((bg, None, cap, D_MODEL), lambda e, b: (b, e, 0, 0)),
        out_shape=jax.ShapeDtypeStruct((B, E, cap, D_MODEL), BF16),
        compiler_params=_cparams(("parallel", "parallel")),
        name="ec_ffn",
    )(xs, gate, w1, w3, w2)


_TN_DIMS = (((0,), (0,)), ((), ()))


def _ec_residual(x1_ref, mod_ref, fg_ref, out_ref, ffn, final):
    x2 = x1_ref[...] + mod_ref[:, 5 * D_MODEL:6 * D_MODEL] * ffn
    if final:
        x2 = x2 * lax.rsqrt(jnp.mean(x2 * x2, axis=-1, keepdims=True) + EPS) * fg_ref[...]
    out_ref[...] = x2


def _scatter_all_kernel(slot_ref, o_ref, x1_ref, mod_ref, fg_ref, out_ref, *, n, cap, final):
    oh, _ = _one_hot_all(slot_ref, None, n, cap)
    ffn = lax.dot_general(oh, o_ref[...].reshape(N_EXPERTS * cap, D_MODEL), _TN_DIMS, preferred_element_type=F32)
    _ec_residual(x1_ref, mod_ref, fg_ref, out_ref, ffn, final)


def _scatter_kernel(slot_ref, o_ref, x1_ref, mod_ref, fg_ref, out_ref, acc, *, cap, final):
    e = pl.program_id(2)
    tn = x1_ref.shape[0]

    @pl.when(e == 0)
    def _():
        acc[...] = jnp.zeros_like(acc)

    sl = slot_ref[...]
    SB = min(EC_SLOT_BLOCK, cap)
    srow = lax.broadcasted_iota(jnp.int32, (SB, tn), 0).astype(F32)
    first, count = _slot_block_range(sl, cap, SB)
    for j in range(min(cap // SB, tn // SB + 1)):
        @pl.when(j < count)
        def _(j=j):
            base = pl.multiple_of((first + j) * SB, SB)
            oh = jnp.where((sl - base.astype(F32)) == srow, 1.0, 0.0).astype(BF16)
            acc[...] += lax.dot_general(oh, o_ref[pl.ds(base, SB), :], _TN_DIMS, preferred_element_type=F32)

    @pl.when(e == N_EXPERTS - 1)
    def _():
        _ec_residual(x1_ref, mod_ref, fg_ref, out_ref, acc[...], final)


def _scatter(slot4, o, x1, mod3, mod_row, final_g, B, n, cap, final):
    E = N_EXPERTS
    if E * cap <= 512:
        return pl.pallas_call(
            functools.partial(_scatter_all_kernel, n=n, cap=cap, final=final),
            grid=(B,),
            in_specs=[
                pl.BlockSpec((None, E, 1, n), lambda b: (b, 0, 0, 0)),
                pl.BlockSpec((None, E, cap, D_MODEL), lambda b: (b, 0, 0, 0)),
                pl.BlockSpec((n, D_MODEL), lambda b: (b, 0)),
                pl.BlockSpec((None, 1, 6 * D_MODEL), lambda b: (mod_row(b), 0, 0)),
                _full((1, D_MODEL)),
            ],
            out_specs=pl.BlockSpec((n, D_MODEL), lambda b: (b, 0)),
            out_shape=jax.ShapeDtypeStruct((B * n, D_MODEL), F32),
            compiler_params=_cparams(("parallel",)),
            name="ec_scatter_all",
        )(slot4, o, x1, mod3, final_g.reshape(1, D_MODEL))
    tn = min(1024, n)
    nt = n // tn
    return pl.pallas_call(
        functools.partial(_scatter_kernel, cap=cap, final=final),
        grid=(B, nt, N_EXPERTS),
        in_specs=[
            pl.BlockSpec((None, None, 1, tn), lambda b, t, e: (b, e, 0, t)),
            pl.BlockSpec((None, None, cap, D_MODEL), lambda b, t, e: (b, e, 0, 0)),
            pl.BlockSpec((tn, D_MODEL), lambda b, t, e: (b * nt + t, 0)),
            pl.BlockSpec((None, 1, 6 * D_MODEL), lambda b, t, e: (mod_row(b), 0, 0)),
            _full((1, D_MODEL)),
        ],
        out_specs=pl.BlockSpec((tn, D_MODEL), lambda b, t, e: (b * nt + t, 0)),
        out_shape=jax.ShapeDtypeStruct((B * n, D_MODEL), F32),
        scratch_shapes=[pltpu.VMEM((tn, D_MODEL), F32)],
        compiler_params=_cparams(("parallel", "parallel", "arbitrary")),
        name="ec_scatter",
    )(slot4, o, x1, mod3, final_g.reshape(1, D_MODEL))


def _expert_choice(hb, aff, x1, mod3, mod_row, final_g, w1, w3, w2, B, n, final):
    cap = EC_FACTOR * n // N_EXPERTS
    slot4, aff4 = _select(aff, B, n, cap)
    xs, gate = _gather(hb, slot4, aff4, B, n, cap)
    bg = max(1, min(B, 256 // cap))
    o = _expert_ffn(xs, gate, w1, w3, w2, bg)
    return _scatter(slot4, o, x1, mod3, mod_row, final_g, B, n, cap, final)


def kernel(x_prompt, x_sample, c, state_s5_re, state_s5_im, state_ssd, state_rwkv, c_ctx, ada_w, ada_b, norm1_g, norm2_g, w_in, w_out, s5_a_re, s5_a_im, s5_log_dt, s5_b_re, s5_b_im, s5_c_re, s5_c_im, s5_d, s5_glu_w, s5_glu_b, ssd_conv_w, ssd_conv_b, ssd_a_log, ssd_dt_bias, ssd_d, ssd_norm_g, rwkv_mu, rwkv_w0, rwkv_w_up, rwkv_a0, rwkv_a_up, rwkv_g_up, rwkv_k_k, rwkv_k_a, rwkv_r_k, rwkv_ln_g, rwkv_ln_b, router_w, exp_w1, exp_w3, exp_w2, final_g):
    Bp, Lp, D = x_prompt.shape
    Bs, Ls, _ = x_sample.shape
    depth = ada_w.shape[0]
    Np, Ns = Bp * Lp, Bs * Ls
    tm = TOKEN_BLOCK
    grid_rows = Ls // GRID_W
    nfs = S5_GROUPS // 8
    assert Lp % tm == 0 and Ls % tm == 0 and Lp % SSD_CHUNK == 0

    n_rows = 1 + Bs
    rows_pad = -(-n_rows // 8) * 8
    cond = jnp.zeros((rows_pad, D), F32).at[0].set(c_ctx).at[1:n_rows].set(c)
    mod = _modulation(cond, ada_w, ada_b)
    s_blocks_per_req = Ls // tm
    row_p = lambda i: 0
    row_s = lambda i: 1 + i // s_blocks_per_req

    ab_re, ab_im, bb_re, bb_im = _s5_discretize(s5_a_re, s5_a_im, s5_log_dt, s5_b_re, s5_b_im)
    w_in_pad = _pad_in_weight(w_in)
    w_out_b = w_out.astype(BF16)
    glu_w_b = s5_glu_w.astype(BF16)
    exp_w1_b, exp_w3_b, exp_w2_b = exp_w1.astype(BF16), exp_w3.astype(BF16), exp_w2.astype(BF16)
    seg = _segment_ones(RWKV_WIDTH, RWKV_HEADSIZE)

    xp = x_prompt.reshape(Np, D)
    xs = x_sample.reshape(Ns, D)
    new_s5_re, new_s5_im, new_ssd, new_rwkv = [], [], [], []
    QP = 2 if Bp % 2 == 0 and Bp >= 2 else 1
    RP = Bp // QP
    assert Bs * HEADS_PAD <= CH

    for l in range(depth):
        lp = {
            'rwkv_mu': rwkv_mu[l], 'rwkv_w0': rwkv_w0[l], 'rwkv_w_up': rwkv_w_up[l], 'rwkv_a0': rwkv_a0[l],
            'rwkv_a_up': rwkv_a_up[l], 'rwkv_g_up': rwkv_g_up[l], 'rwkv_k_k': rwkv_k_k[l], 'rwkv_k_a': rwkv_k_a[l],
            'rwkv_r_k': rwkv_r_k[l].reshape(-1), 'rwkv_ln_g': rwkv_ln_g[l], 'rwkv_ln_b': rwkv_ln_b[l],
            's5_glu_w': glu_w_b[l], 's5_glu_b': s5_glu_b[l], 'ssd_norm_g': ssd_norm_g[l], 'w_out': w_out_b[l],
            'norm2_g': norm2_g[l], 'router_w': router_w[l], 'seg': seg,
        }
        mod3 = mod[l].reshape(rows_pad, 1, 6 * D)
        us5_p, z_p, xbc_p, dt_p, rkv_p, lo_p = _in_projection(xp, mod3, row_p, norm1_g[l], w_in_pad[l])
        us5_s, z_s, xbc_s, dt_s, rkv_s, lo_s = _in_projection(xs, mod3, row_s, norm1_g[l], w_in_pad[l])

        tables = _s5_layer_tables(ab_re[l], ab_im[l], bb_re[l], bb_im[l], s5_c_re[l], s5_c_im[l])
        d_row = s5_d[l].reshape(1, S5_WIDTH)
        up = us5_p.reshape(QP, RP, Lp, S5_WIDTH).transpose(0, 2, 1, 3).reshape(QP, Lp * RP, S5_WIDTH)
        yp, hfin = _s5_scan(up, None, tables, d_row, R=RP, n_slab=Lp, chained=False, want_final=True)
        ys5_p = yp.reshape(QP, Lp, RP, S5_WIDTH).transpose(0, 2, 1, 3).reshape(Np, S5_WIDTH)
        hf = hfin.transpose(0, 3, 1, 2, 4).reshape(Bp, N_DIR, nfs, 2, 8, S5_STATE)
        new_s5_re.append(hf[:, :, :, 0].reshape(Bp, N_DIR, S5_GROUPS, S5_STATE))
        new_s5_im.append(hf[:, :, :, 1].reshape(Bp, N_DIR, S5_GROUPS, S5_STATE))
        h0 = jnp.concatenate([state_s5_re[:, l].reshape(Bs, N_DIR, nfs, 1, S5_SLICE_ST),
                              state_s5_im[:, l].reshape(Bs, N_DIR, nfs, 1, S5_SLICE_ST)], axis=-1)
        (ysm,) = _s5_scan(us5_s.reshape(Bs, Ls, S5_WIDTH), h0, tables, d_row, R=GRID_W, n_slab=grid_rows,
                          chained=True, want_final=False)
        ys5_s = ysm.reshape(Ns, S5_WIDTH)

        ssd_args = (ssd_conv_w[l], ssd_conv_b[l], ssd_dt_bias[l], ssd_a_log[l], ssd_d[l])
        *yssd_p, hssd = _ssd_scan(xbc_p, dt_p, None, *ssd_args, B=Bp, L=Lp, want_final=True)
        new_ssd.append(hssd.transpose(0, 1, 2, 4, 3))
        yssd_s = _ssd_scan(xbc_s, dt_s, state_ssd[:, l].transpose(0, 1, 2, 4, 3), *ssd_args,
                           B=Bs, L=Ls, want_final=False)

        r_, w0_, w1_, k_, v_, kk_, b_, g_p, bonus_p = _rwkv_prep(rkv_p, lo_p, lp, Lp // tm)
        cx = lambda t: _to_ctx(t, t, Bp, Lp)
        zero_state = jnp.zeros((RWKV_HEADSIZE, RWKV_HEADSIZE, _ctx_lanes(Bp)[1]), F32)
        yf_, yb_, sfin = _rwkv_scan(cx(r_), _to_ctx(w0_, w1_, Bp, Lp), cx(k_), cx(kk_), cx(b_), cx(v_),
                                    _ctx_fwd_mask(Bp), zero_state)
        rf_p = _from_ctx(yf_, Bp, Lp, 0)
        rb_p = _from_ctx(yb_, Bp, Lp, 1)
        new_rwkv.append(_ctx_state_from(sfin, Bp))

        r_, w0_, w1_, k_, v_, kk_, b_, g_s, bonus_s = _rwkv_prep(rkv_s, lo_s, lp, Ls // tm, SCAN_WIDTH)
        sc = lambda t: _to_scan2_k(t, Bs, Ls)
        yf_, yb_ = _rwkv_scan2(sc(r_), sc(w0_), sc(w1_), sc(k_), sc(kk_), sc(b_), _to_scan2_v(v_, Bs, Ls),
                               _state_to_scan2(state_rwkv[:, l]))
        rf_s = _from_scan2(yf_, Bs, Ls)
        rb_s = _from_scan2(yb_, Bs, Ls)

        x1_p, hb_p, aff_p = _post_mixer(xp, mod3, row_p, ys5_p, yssd_p, z_p, rf_p, rb_p, bonus_p, g_p, lp)
        x1_s, hb_s, aff_s = _post_mixer(xs, mod3, row_s, ys5_s, yssd_s, z_s, rf_s, rb_s, bonus_s, g_s, lp)

        final = l == depth - 1
        ew = (exp_w1_b[l], exp_w3_b[l], exp_w2_b[l])
        xp = _expert_choice(hb_p, aff_p, x1_p, mod3, lambda b: 0, final_g, *ew, Bp, Lp, final)
        xs = _expert_choice(hb_s, aff_s, x1_s, mod3, lambda b: 1 + b, final_g, *ew, Bs, Ls, final)

    y_prompt = xp.reshape(Bp, Lp, D)
    y_sample = xs.reshape(Bs, Ls, D)
    return (y_prompt, y_sample, jnp.stack(new_s5_re, axis=1), jnp.stack(new_s5_im, axis=1),
            jnp.stack(new_ssd, axis=1), jnp.stack(new_rwkv, axis=1))
```

```python
import functools
import math

import jax
import jax.numpy as jnp
from jax import lax
from jax.experimental import pallas as pl
from jax.experimental.pallas import tpu as pltpu

F32 = jnp.float32
BF16 = jnp.bfloat16
HIGHEST = lax.Precision.HIGHEST

D_MODEL = 1024
GRID_W = 64
N_DIR = 2
EPS = 1e-6
S5_WIDTH = 256
S5_CH = 16
S5_GROUPS = 16
S5_STATE = 64
SSD_HEADDIM = 64
SSD_HEADS = 6
SSD_WIDTH = 384
SSD_GROUPS = 2
SSD_STATE = 64
SSD_BC = 128
SSD_CONV_CH = 640
SSD_CHUNK = 128
RWKV_HEADSIZE = 64
RWKV_HEADS = 6
RWKV_WIDTH = 384
W_RANK = 32
A_RANK = 32
G_RANK = 64
DECAY_SCALE = math.exp(-0.5)
GN_EPS = 64e-5
N_EXPERTS = 16
D_EXPERT = 512
EC_FACTOR = 2
IN_SIZES = (S5_WIDTH, SSD_WIDTH, SSD_CONV_CH, N_DIR * SSD_HEADS, 3 * RWKV_WIDTH, N_DIR * W_RANK, A_RANK, G_RANK)

LANES = 128
TOKEN_BLOCK = 256
DT_PAD = 128
LO_PAD = 256
S5_SLICE_CH = 128
S5_SLICE_ST = 512
VMEM_LIMIT = 56 * 1024 * 1024


def _cparams(sem):
    return pltpu.CompilerParams(dimension_semantics=sem, vmem_limit_bytes=VMEM_LIMIT)


def _full(shape):
    nd = len(shape)
    return pl.BlockSpec(shape, lambda *_: (0,) * nd)


def _sigmoid(x):
    return 1.0 / (1.0 + jnp.exp(-x))


def _softplus(x):
    return jnp.maximum(x, 0.0) + jnp.log1p(jnp.exp(-jnp.abs(x)))


def _split_bf16(x):
    hi = x.astype(BF16)
    lo = (x - hi.astype(F32)).astype(BF16)
    return hi, lo


def _dot_exact_rhs(x, m, dims=None):
    hi, lo = _split_bf16(x)
    mb = m.astype(BF16)
    if dims is None:
        return jnp.dot(hi, mb, preferred_element_type=F32) + jnp.dot(lo, mb, preferred_element_type=F32)
    return (lax.dot_general(hi, mb, dims, preferred_element_type=F32)
            + lax.dot_general(lo, mb, dims, preferred_element_type=F32))


def _dot_exact_lhs(m, x):
    hi, lo = _split_bf16(x)
    mb = m.astype(BF16)
    return jnp.dot(mb, hi, preferred_element_type=F32) + jnp.dot(mb, lo, preferred_element_type=F32)


def _dot3(x, w, dims=(((1,), (0,)), ((), ()))):
    xh, xl = _split_bf16(x)
    wh, wl = _split_bf16(w)
    dg = functools.partial(lax.dot_general, dimension_numbers=dims, preferred_element_type=F32)
    return dg(xh, wh) + (dg(xl, wh) + dg(xh, wl))


def _gelu_tanh(x):
    return 0.5 * x * (1.0 + jnp.tanh(math.sqrt(2.0 / math.pi) * (x + 0.044715 * (x * x * x))))


def _mod_kernel(c_ref, w_ref, b_ref, o_ref):
    c = c_ref[...]
    s = (c * _sigmoid(c)).astype(BF16)
    o_ref[...] = jnp.dot(s, w_ref[...].astype(BF16), preferred_element_type=F32) + b_ref[...]


def _modulation(cond, ada_w, ada_b):
    depth = ada_w.shape[0]
    rows = cond.shape[0]
    tn = 1536
    return pl.pallas_call(
        _mod_kernel,
        grid=(depth, 6 * D_MODEL // tn),
        in_specs=[
            pl.BlockSpec((rows, D_MODEL), lambda l, j: (0, 0)),
            pl.BlockSpec((None, D_MODEL, tn), lambda l, j: (l, 0, j)),
            pl.BlockSpec((None, 1, tn), lambda l, j: (l, 0, j)),
        ],
        out_specs=pl.BlockSpec((None, rows, tn), lambda l, j: (l, 0, j)),
        out_shape=jax.ShapeDtypeStruct((depth, rows, 6 * D_MODEL), F32),
        compiler_params=_cparams(("parallel", "parallel")),
        name="adaln_mod",
    )(cond, ada_w, ada_b.reshape(depth, 1, 6 * D_MODEL))


IN_PAD_SIZES = (S5_WIDTH, SSD_WIDTH, SSD_CONV_CH, DT_PAD, 3 * RWKV_WIDTH, LO_PAD)


def _inproj_kernel(x_ref, mod_ref, g_ref, w_ref, us5_ref, z_ref, xbc_ref, dt_ref, rkv_ref, lo_ref):
    x = x_ref[...]
    y = x * lax.rsqrt(jnp.mean(x * x, axis=-1, keepdims=True) + EPS) * g_ref[...]
    m = mod_ref[...]
    h = y * (1.0 + m[:, D_MODEL:2 * D_MODEL]) + m[:, 0:D_MODEL]
    p = jnp.dot(h.astype(BF16), w_ref[...], preferred_element_type=F32)
    start = 0
    for ref, size in zip((us5_ref, z_ref, xbc_ref, dt_ref, rkv_ref, lo_ref), IN_PAD_SIZES):
        ref[...] = p[:, start:start + size]
        start += size


def _pad_in_weight(w_in):
    parts, start = [], 0
    for s in IN_SIZES:
        parts.append(w_in[..., start:start + s])
        start += s
    us5, z, xbc, dt, rkv, wlo, alo, glo = parts
    zeros = lambda n: jnp.zeros(w_in.shape[:-1] + (n,), w_in.dtype)
    lo_used = wlo.shape[-1] + alo.shape[-1] + glo.shape[-1]
    return jnp.concatenate([us5, z, xbc, dt, zeros(DT_PAD - dt.shape[-1]), rkv, wlo, alo, glo,
                            zeros(LO_PAD - lo_used)], axis=-1).astype(BF16)


def _in_projection(x, mod3, row_of_block, norm_g, w_pad):
    ntok = x.shape[0]
    tm = TOKEN_BLOCK
    width = w_pad.shape[1]
    outs = tuple(jax.ShapeDtypeStruct((ntok, s), F32) for s in IN_PAD_SIZES)
    return pl.pallas_call(
        _inproj_kernel,
        grid=(ntok // tm,),
        in_specs=[
            pl.BlockSpec((tm, D_MODEL), lambda i: (i, 0)),
            pl.BlockSpec((None, 1, 6 * D_MODEL), lambda i: (row_of_block(i), 0, 0)),
            _full((1, D_MODEL)),
            _full((D_MODEL, width)),
        ],
        out_specs=tuple(pl.BlockSpec((tm, s), lambda i: (i, 0)) for s in IN_PAD_SIZES),
        out_shape=outs,
        compiler_params=_cparams(("parallel",)),
        name="in_proj",
    )(x, mod3, norm_g.reshape(1, D_MODEL), w_pad)


def _s5_disc_kernel(are_ref, aim_ref, ldt_ref, bre_ref, bim_ref, abre_ref, abim_ref, bbre_ref, bbim_ref):
    lam_re = jnp.minimum(are_ref[...], -1e-4)
    lam_im = aim_ref[...]
    dt = jnp.exp(ldt_ref[...])
    mag = jnp.exp(lam_re * dt)
    ab_re = mag * jnp.cos(lam_im * dt)
    ab_im = mag * jnp.sin(lam_im * dt)
    num_re, num_im = ab_re - 1.0, ab_im
    den = lam_re * lam_re + lam_im * lam_im
    q_re = (num_re * lam_re + num_im * lam_im) / den
    q_im = (num_im * lam_re - num_re * lam_im) / den
    abre_ref[...] = ab_re
    abim_ref[...] = ab_im
    b_re = bre_ref[...]
    b_im = bim_ref[...]
    qr = q_re[:, None, :]
    qi = q_im[:, None, :]
    bbre_ref[...] = qr * b_re - qi * b_im
    bbim_ref[...] = qr * b_im + qi * b_re


def _s5_discretize(a_re, a_im, log_dt, b_re, b_im):
    lead = a_re.shape[:3]
    n = lead[0] * lead[1] * lead[2]
    a2 = lambda t: t.reshape(n, S5_STATE)
    ldt = jnp.broadcast_to(log_dt.reshape(n, 1), (n, S5_STATE))
    b3 = lambda t: t.reshape(n, S5_STATE, S5_CH).transpose(0, 2, 1)
    ab_re, ab_im, bb_re, bb_im = pl.pallas_call(
        _s5_disc_kernel,
        out_shape=(jax.ShapeDtypeStruct((n, S5_STATE), F32), jax.ShapeDtypeStruct((n, S5_STATE), F32),
                   jax.ShapeDtypeStruct((n, S5_CH, S5_STATE), F32), jax.ShapeDtypeStruct((n, S5_CH, S5_STATE), F32)),
        name="s5_discretize",
    )(a2(a_re), a2(a_im), ldt, b3(b_re), b3(b_im))
    return (ab_re.reshape(lead + (S5_STATE,)), ab_im.reshape(lead + (S5_STATE,)),
            bb_re.reshape(lead + (S5_CH, S5_STATE)), bb_im.reshape(lead + (S5_CH, S5_STATE)))


def _s5_layer_tables(ab_re, ab_im, bb_re, bb_im, c_re, c_im):
    nfs = S5_GROUPS // 8
    eye = jnp.eye(8, dtype=F32)

    def rows(t):
        return t.reshape(N_DIR, nfs, 8 * S5_STATE)

    ab_row = jnp.concatenate([rows(ab_re), rows(ab_im)], axis=-1).reshape(N_DIR, nfs, 1, 2 * S5_SLICE_ST)

    def bmat(t):
        t = t.reshape(N_DIR, nfs, 8, S5_CH, S5_STATE)
        return jnp.einsum('dfghp,gk->dfghkp', t, eye).reshape(N_DIR, nfs, S5_SLICE_CH, S5_SLICE_ST)

    b_mat = jnp.concatenate([bmat(bb_re), bmat(bb_im)], axis=-1).astype(BF16)

    def cmat(t):
        t = t.reshape(N_DIR, nfs, 8, S5_CH, S5_STATE)
        return jnp.einsum('dfghp,gk->dfgpkh', t, eye).reshape(N_DIR, nfs, S5_SLICE_ST, S5_SLICE_CH)

    c_mat = jnp.concatenate([cmat(c_re), -cmat(c_im)], axis=-2).astype(BF16)
    return ab_row, b_mat, c_mat


def _s5_kernel(*refs, R, n_slab, chained, want_final):
    if chained:
        u_ref, h0_ref, ab_ref, bm_ref, cm_ref, d_ref = refs[:6]
        rest = refs[6:]
    else:
        u_ref, ab_ref, bm_ref, cm_ref, d_ref = refs[:5]
        h0_ref = None
        rest = refs[5:]
    y_ref = rest[0]
    rest = rest[1:]
    if want_final:
        hfin_ref = rest[0]
        rest = rest[1:]
    H = rest[0]
    if chained:
        PW, CIN = rest[1], rest[2]
    NR = R * n_slab
    RC = min(512, NR)
    ST = S5_SLICE_ST
    nchunk = ST // LANES

    y_ref[...] = u_ref[...] * d_ref[...]

    for d in range(N_DIR):
        def slab_of(i, d=d):
            return i if d == 0 else n_slab - 1 - i

        def bu_body(i, _, d=d):
            r0 = pl.multiple_of(i * RC, RC)
            H[pl.ds(r0, RC), :] = jnp.dot(u_ref[pl.ds(r0, RC), :].astype(BF16), bm_ref[d],
                                           preferred_element_type=F32)
            return 0
        lax.fori_loop(0, NR // RC, bu_body, 0)

        for c in range(nchunk):
            lre = slice(c * LANES, (c + 1) * LANES)
            lim = slice(ST + c * LANES, ST + (c + 1) * LANES)
            a_re = jnp.broadcast_to(ab_ref[d, :, lre], (R, LANES))
            a_im = jnp.broadcast_to(ab_ref[d, :, lim], (R, LANES))

            def step(i, carry, lre=lre, lim=lim, a_re=a_re, a_im=a_im, slab_of=slab_of):
                cr, ci = carry
                r0 = pl.multiple_of(slab_of(i) * R, R)
                nr = a_re * cr - a_im * ci + H[pl.ds(r0, R), lre]
                ni = a_re * ci + a_im * cr + H[pl.ds(r0, R), lim]
                H[pl.ds(r0, R), lre] = nr
                H[pl.ds(r0, R), lim] = ni
                return nr, ni
            zero = jnp.zeros((R, LANES), F32)
            lax.fori_loop(0, n_slab, step, (zero, zero))

        last0 = (n_slab - 1) * R if d == 0 else 0
        if chained:
            a_re_row = ab_ref[d, :, 0:ST]
            a_im_row = ab_ref[d, :, ST:2 * ST]

            def pw_step(j, carry, a_re_row=a_re_row, a_im_row=a_im_row):
                pr, pi = carry
                PW[j, :, 0:ST] = jnp.broadcast_to(pr, (8, ST))
                PW[j, :, ST:2 * ST] = jnp.broadcast_to(pi, (8, ST))
                return pr * a_re_row - pi * a_im_row, pr * a_im_row + pi * a_re_row
            lax.fori_loop(0, n_slab, pw_step, (a_re_row, a_im_row))
            t_re = PW[n_slab - 1, 0:1, 0:ST]
            t_im = PW[n_slab - 1, 0:1, ST:2 * ST]

            cr = h0_ref[d, :, 0:ST]
            ci = h0_ref[d, :, ST:2 * ST]
            for i in range(R):
                c = i if d == 0 else R - 1 - i
                CIN[c:c + 1, 0:ST] = cr
                CIN[c:c + 1, ST:2 * ST] = ci
                er = H[last0 + c:last0 + c + 1, 0:ST]
                ei = H[last0 + c:last0 + c + 1, ST:2 * ST]
                cr, ci = t_re * cr - t_im * ci + er, t_re * ci + t_im * cr + ei

            def fix_step(i, _, slab_of=slab_of):
                r0 = pl.multiple_of(slab_of(i) * R, R)
                p = PW[i]
                for c in range(nchunk):
                    lre = slice(c * LANES, (c + 1) * LANES)
                    lim = slice(ST + c * LANES, ST + (c + 1) * LANES)
                    pr = p[0:1, lre]
                    pi = p[0:1, lim]
                    cr = CIN[:, lre]
                    ci = CIN[:, lim]
                    H[pl.ds(r0, R), lre] = H[pl.ds(r0, R), lre] + (pr * cr - pi * ci)
                    H[pl.ds(r0, R), lim] = H[pl.ds(r0, R), lim] + (pr * ci + pi * cr)
                return 0
            lax.fori_loop(0, n_slab, fix_step, 0)

        if want_final:
            hfin_ref[d] = H[last0:last0 + R, :]

        def y_body(i, _, d=d):
            r0 = pl.multiple_of(i * RC, RC)
            y_ref[pl.ds(r0, RC), :] = y_ref[pl.ds(r0, RC), :] + jnp.dot(
                H[pl.ds(r0, RC), :].astype(BF16), cm_ref[d], preferred_element_type=F32)
            return 0
        lax.fori_loop(0, NR // RC, y_body, 0)


def _s5_scan(u, h0, tables, d_row, *, R, n_slab, chained, want_final):
    ab_row, b_mat, c_mat = tables
    Q, NR, _ = u.shape
    nfs = S5_GROUPS // 8
    W2 = 2 * S5_SLICE_ST
    in_specs = [pl.BlockSpec((None, NR, S5_SLICE_CH), lambda q, f: (q, 0, f))]
    args = [u]
    if chained:
        in_specs.append(pl.BlockSpec((None, N_DIR, None, 1, W2), lambda q, f: (q, 0, f, 0, 0)))
        args.append(h0)
    in_specs += [
        pl.BlockSpec((N_DIR, None, 1, W2), lambda q, f: (0, f, 0, 0)),
        pl.BlockSpec((N_DIR, None, S5_SLICE_CH, W2), lambda q, f: (0, f, 0, 0)),
        pl.BlockSpec((N_DIR, None, W2, S5_SLICE_CH), lambda q, f: (0, f, 0, 0)),
        pl.BlockSpec((1, S5_SLICE_CH), lambda q, f: (0, f)),
    ]
    args += [ab_row, b_mat, c_mat, d_row]
    out_shape = [jax.ShapeDtypeStruct((Q, NR, S5_WIDTH), F32)]
    out_specs = [pl.BlockSpec((None, NR, S5_SLICE_CH), lambda q, f: (q, 0, f))]
    if want_final:
        out_shape.append(jax.ShapeDtypeStruct((Q, N_DIR, nfs, R, W2), F32))
        out_specs.append(pl.BlockSpec((None, N_DIR, None, R, W2), lambda q, f: (q, 0, f, 0, 0)))
    scratch = [pltpu.VMEM((NR, W2), F32)]
    if chained:
        scratch += [pltpu.VMEM((n_slab, 8, W2), F32), pltpu.VMEM((R, W2), F32)]
    res = pl.pallas_call(
        functools.partial(_s5_kernel, R=R, n_slab=n_slab, chained=chained, want_final=want_final),
        grid=(Q, nfs),
        in_specs=in_specs,
        out_specs=tuple(out_specs),
        out_shape=tuple(out_shape),
        scratch_shapes=scratch,
        compiler_params=_cparams(("parallel", "parallel")),
        name="s5_scan_chained" if chained else "s5_scan",
    )(*args)
    return res


def _ssd_kernel(*refs, TB, nb, has_h0, want_final):
    io = [refs[4 * d:4 * d + 4] for d in range(N_DIR)]
    refs = refs[8:]
    if has_h0:
        h0_ref = refs[0]
        refs = refs[1:]
    cw_ref, cb_ref, dtb_ref, arow_ref, sel_ref, drow_ref = refs[:6]
    refs = refs[6:]
    y_refs = refs[:2]
    refs = refs[2:]
    if want_final:
        hfin_ref = refs[0]
        refs = refs[1:]
    hst, xc_s, dt_s = refs
    CH = SSD_CHUNK
    P = SSD_HEADDIM
    j = pl.program_id(1)

    @pl.when(j == 0)
    def _():
        if has_h0:
            hst[...] = h0_ref[...]
        else:
            hst[...] = jnp.zeros_like(hst)

    li = lax.broadcasted_iota(jnp.int32, (CH, CH), 0)
    si = lax.broadcasted_iota(jnp.int32, (CH, CH), 1)
    tmats = ((si <= li).astype(F32), (si >= li).astype(F32))
    rows = lax.broadcasted_iota(jnp.int32, (TB, 1), 0)
    for d in range(N_DIR):
        xbc_ref, xp_ref, xn_ref, dt_ref = io[d]
        jj = j if d == 0 else nb - 1 - j
        x = xbc_ref[...]
        prev_row = xp_ref[7:8, :] * (jj > 0).astype(F32)
        next_row = xn_ref[0:1, :] * (jj < nb - 1).astype(F32)
        x_prev = jnp.where(rows == 0, prev_row, pltpu.roll(x, 1, 0))
        x_next = jnp.where(rows == TB - 1, next_row, pltpu.roll(x, TB - 1, 0))
        conv = cw_ref[0:1, :] * x_prev + cw_ref[1:2, :] * x + cw_ref[2:3, :] * x_next + cb_ref[...]
        xc_s[d] = conv * _sigmoid(conv)
        dtf = _softplus(dt_ref[...] + dtb_ref[...])
        dt_s[d] = _dot_exact_rhs(dtf, sel_ref[d])

    hrow = lax.broadcasted_iota(jnp.int32, (LANES, SSD_WIDTH), 0)
    e_head = jnp.where(lax.broadcasted_iota(jnp.int32, (LANES, SSD_WIDTH), 1) // P == hrow, 1.0, 0.0)
    hrow2 = lax.broadcasted_iota(jnp.int32, (LANES, SSD_HEADS * CH), 0)
    e_chunk = jnp.where(lax.broadcasted_iota(jnp.int32, (LANES, SSD_HEADS * CH), 1) // CH == hrow2, 1.0, 0.0)

    n_ch = TB // CH
    for i, d in [(i, d) for i in range(n_ch) for d in range(N_DIR)]:
        tmat = tmats[d]
        y_ref = y_refs[d]
        r0 = (i if d == 0 else n_ch - 1 - i) * CH
        dtc = dt_s[d, r0:r0 + CH, :]
        dA = dtc * arow_ref[d]
        cs = _dot_exact_lhs(tmat, dA)
        csT = cs.T
        dt_x = _dot_exact_rhs(dtc, e_head)
        cs_x = _dot_exact_rhs(cs, e_head)
        tot_x = jnp.sum(_dot_exact_rhs(dA, e_head), axis=0, keepdims=True)
        cs_xx = _dot_exact_rhs(cs, e_chunk)
        xs_all = xc_s[d, r0:r0 + CH, 0:SSD_WIDTH]
        xdt_all = xs_all * dt_x
        xd_all = xdt_all * jnp.exp(tot_x - cs_x)
        ecs_all = jnp.exp(cs_x)
        etot_all = jnp.exp(tot_x)
        Bm = xc_s[d, r0:r0 + CH, SSD_WIDTH:SSD_WIDTH + SSD_BC]
        Cm = xc_s[d, r0:r0 + CH, SSD_WIDTH + SSD_BC:SSD_WIDTH + 2 * SSD_BC]
        BmT = Bm.T
        for g in range(SSD_GROUPS):
            Cg = Cm[:, g * SSD_STATE:(g + 1) * SSD_STATE].astype(BF16)
            Bg = Bm[:, g * SSD_STATE:(g + 1) * SSD_STATE].astype(BF16)
            BgT = BmT[g * SSD_STATE:(g + 1) * SSD_STATE, :].astype(BF16)
            G = lax.dot_general(Cg, Bg, (((1,), (1,)), ((), ())), preferred_element_type=F32)
            for hh in range(SSD_HEADS // SSD_GROUPS):
                h = g * (SSD_HEADS // SSD_GROUPS) + hh
                hl = slice(h * P, (h + 1) * P)
                row = csT[h:h + 1, :]
                lm = jnp.exp(jnp.where(tmat > 0.0, cs_xx[:, h * CH:(h + 1) * CH] - row, -1e30))
                hprev = hst[d, h]
                y = jnp.dot((G * lm).astype(BF16), xdt_all[:, hl].astype(BF16), preferred_element_type=F32)
                y = y + jnp.dot(Cg, hprev.astype(BF16), preferred_element_type=F32) * ecs_all[:, hl]
                if d == 0:
                    y = y + drow_ref[:, hl] * xs_all[:, hl]
                y_ref[r0:r0 + CH, hl] = y
                hst[d, h] = etot_all[:, hl] * hprev + jnp.dot(BgT, xd_all[:, hl].astype(BF16),
                                                             preferred_element_type=F32)

    if want_final:
        @pl.when(j == nb - 1)
        def _():
            hfin_ref[...] = hst[...]


def _ssd_scan(xbc, dt, h0, conv_w, conv_b, dt_bias, a_log, d_skip, *, B, L, want_final):
    TB = min(512, L)
    nb = L // TB
    H = SSD_HEADS

    nrow8 = B * L // 8
    in_specs, args = [], []
    blks = (lambda b, j: b * nb + j, lambda b, j: b * nb + nb - 1 - j)
    for blk in blks:
        in_specs += [
            pl.BlockSpec((TB, SSD_CONV_CH), lambda b, j, blk=blk: (blk(b, j), 0)),
            pl.BlockSpec((8, SSD_CONV_CH), lambda b, j, blk=blk: (jnp.maximum(blk(b, j) * (TB // 8) - 1, 0), 0)),
            pl.BlockSpec((8, SSD_CONV_CH),
                         lambda b, j, blk=blk: (jnp.minimum((blk(b, j) + 1) * (TB // 8), nrow8 - 1), 0)),
            pl.BlockSpec((TB, DT_PAD), lambda b, j, blk=blk: (blk(b, j), 0)),
        ]
        args += [xbc, xbc, xbc, dt]
    if h0 is not None:
        in_specs.append(pl.BlockSpec((None, N_DIR, H, SSD_STATE, SSD_HEADDIM), lambda b, j: (b, 0, 0, 0, 0)))
        args.append(h0)
    dtb = jnp.pad(dt_bias.reshape(1, N_DIR * H), ((0, 0), (0, DT_PAD - N_DIR * H)))
    arow = jnp.pad(-jnp.exp(a_log), ((0, 0), (0, LANES - H))).reshape(N_DIR, 1, LANES)
    lane = jnp.arange(LANES)
    sel = jnp.stack([(lane[:, None] == (dd * H + lane[None, :])) & (lane[None, :] < H) for dd in range(N_DIR)]).astype(F32)
    drow = jnp.repeat(d_skip, SSD_HEADDIM).reshape(1, SSD_WIDTH)
    in_specs += [
        _full((3, SSD_CONV_CH)), _full((1, SSD_CONV_CH)), _full((1, DT_PAD)),
        _full((N_DIR, 1, LANES)), _full((N_DIR, LANES, LANES)), _full((1, SSD_WIDTH)),
    ]
    args += [conv_w, conv_b.reshape(1, SSD_CONV_CH), dtb, arow, sel, drow]
    out_shape = [jax.ShapeDtypeStruct((B * L, SSD_WIDTH), F32)] * N_DIR
    out_specs = [pl.BlockSpec((TB, SSD_WIDTH), lambda b, j, blk=blk: (blk(b, j), 0)) for blk in blks]
    if want_final:
        out_shape.append(jax.ShapeDtypeStruct((B, N_DIR, H, SSD_STATE, SSD_HEADDIM), F32))
        out_specs.append(pl.BlockSpec((None, N_DIR, H, SSD_STATE, SSD_HEADDIM), lambda b, j: (b, 0, 0, 0, 0)))
    return pl.pallas_call(
        functools.partial(_ssd_kernel, TB=TB, nb=nb, has_h0=h0 is not None, want_final=want_final),
        grid=(B, nb),
        in_specs=in_specs,
        out_specs=tuple(out_specs),
        out_shape=tuple(out_shape),
        scratch_shapes=[pltpu.VMEM((N_DIR, H, SSD_STATE, SSD_HEADDIM), F32),
                        pltpu.VMEM((N_DIR, TB, SSD_CONV_CH), F32), pltpu.VMEM((N_DIR, TB, DT_PAD), F32)],
        compiler_params=_cparams(("parallel", "arbitrary")),
        name="ssd_scan",
    )(*args)


def _rwkv_prep_kernel(rkv_ref, rp_ref, rn_ref, lo_ref, mu_ref, a0_ref, aup_ref, gup_ref, w0_ref, wup_ref,
                      kkw_ref, ka_ref, rk_ref, seg_ref,
                      r_ref, w0o_ref, w1o_ref, k_ref, v_ref, kk_ref, b_ref, g_ref, bonus_ref, *, nbs):
    tm = rkv_ref.shape[0]
    W = RWKV_WIDTH
    jj = pl.program_id(0) % nbs
    x = rkv_ref[...]
    prev_row = rp_ref[7:8, :] * (jj > 0).astype(F32)
    next_row = rn_ref[0:1, :] * (jj < nbs - 1).astype(F32)
    rows = lax.broadcasted_iota(jnp.int32, (tm, 1), 0)
    xp = jnp.where(rows == 0, prev_row, pltpu.roll(x, 1, 0))
    xn = jnp.where(rows == tm - 1, next_row, pltpu.roll(x, tm - 1, 0))
    x = x + mu_ref[0:1, :] * (xp - x) + mu_ref[1:2, :] * (xn - x)
    r = x[:, 0:W]
    k = x[:, W:2 * W]
    v = x[:, 2 * W:3 * W]
    lo = lo_ref[...]
    seg = seg_ref[...]

    def put(ref, val):
        ref[:, 0:W] = val
        if ref.shape[1] > W:
            ref[:, W:] = jnp.zeros((tm, ref.shape[1] - W), F32)

    a = _sigmoid(a0_ref[...] + _dot3(lo, aup_ref[...]))
    g_ref[...] = _dot3(_sigmoid(lo), gup_ref[...])
    tlo = jnp.tanh(lo)
    for d, o_ref in enumerate((w0o_ref, w1o_ref)):
        zw = w0_ref[d:d + 1, :] + _dot3(tlo, wup_ref[d])
        put(o_ref, jnp.exp(-DECAY_SCALE * _sigmoid(zw)))
    kk = k * kkw_ref[...]
    kk = kk * lax.rsqrt(jnp.maximum(_dot_exact_rhs(kk * kk, seg), 1e-24))
    k2 = k * (1.0 + (a - 1.0) * ka_ref[...])
    put(r_ref, r)
    put(k_ref, k2)
    put(v_ref, v)
    put(kk_ref, kk)
    put(b_ref, kk * a)
    bonus_ref[...] = _dot_exact_rhs(r * k2 * rk_ref[...], seg) * v


def _segment_ones(width, seg):
    i = jnp.arange(width)
    return (i[:, None] // seg == i[None, :] // seg).astype(F32)


def _rwkv_prep(rkv, lo, lp, seq_blocks, scan_width=RWKV_WIDTH):
    ntok = rkv.shape[0]
    tm = TOKEN_BLOCK
    W = RWKV_WIDTH
    nrow8 = ntok // 8
    nbs_of = seq_blocks
    widths = (scan_width,) * 7 + (W, W)
    pad_rows = lambda t, r0: jnp.zeros((LO_PAD, W), F32).at[r0:r0 + t.shape[0]].set(t)
    aup = pad_rows(lp['rwkv_a_up'], N_DIR * W_RANK)
    gup = pad_rows(lp['rwkv_g_up'], N_DIR * W_RANK + A_RANK)
    wup = jnp.stack([pad_rows(lp['rwkv_w_up'][d], d * W_RANK) for d in range(N_DIR)])
    row = lambda t: t.reshape(1, W)
    outs = tuple(jax.ShapeDtypeStruct((ntok, w), F32) for w in widths)
    return pl.pallas_call(
        functools.partial(_rwkv_prep_kernel, nbs=nbs_of),
        grid=(ntok // tm,),
        in_specs=[
            pl.BlockSpec((tm, 3 * W), lambda i: (i, 0)),
            pl.BlockSpec((8, 3 * W), lambda i: (jnp.maximum(i * (tm // 8) - 1, 0), 0)),
            pl.BlockSpec((8, 3 * W), lambda i: (jnp.minimum((i + 1) * (tm // 8), nrow8 - 1), 0)),
            pl.BlockSpec((tm, LO_PAD), lambda i: (i, 0)),
            _full((2, 3 * W)), _full((1, W)), _full((LO_PAD, W)), _full((LO_PAD, W)), _full((2, W)),
            _full((N_DIR, LO_PAD, W)), _full((1, W)), _full((1, W)), _full((1, W)), _full((W, W)),
        ],
        out_specs=tuple(pl.BlockSpec((tm, w), lambda i: (i, 0)) for w in widths),
        out_shape=outs,
        compiler_params=_cparams(("parallel",)),
        name="rwkv_prep",
    )(rkv, rkv, rkv, lo, lp['rwkv_mu'], row(lp['rwkv_a0']), aup, gup, lp['rwkv_w0'], wup,
      row(lp['rwkv_k_k']), row(lp['rwkv_k_a']), row(lp['rwkv_r_k']), lp['seg'])


def _rwkv_scan_kernel(rf, rm, wf, wm, kf, km, kkf, kkm, bf, bm, vf, vm, mask_ref, s0_ref, yf_ref, yb_ref, sfin_ref,
                      S, SA, OPS, VTS, *, TBLK, V, nblk):
    K = RWKV_HEADSIZE
    VH = V // 2
    tb = pl.program_id(1)

    @pl.when(tb == 0)
    def _():
        S[...] = s0_ref[...]

    is_fwd = mask_ref[...] > 0.5
    fwd_k = jnp.broadcast_to(is_fwd, (K, LANES))
    fwd_v = jnp.broadcast_to(is_fwd, (V, LANES))

    def pick(t, _):
        tm_ = TBLK - 1 - t
        for j, (f, m) in enumerate(((rf, rm), (wf, wm), (kf, km), (kkf, kkm), (bf, bm))):
            OPS[t, j] = jnp.where(fwd_k, f[t], m[tm_])
        VTS[t] = jnp.where(fwd_v, vf[t], vm[tm_])
        return 0
    lax.fori_loop(0, TBLK, pick, 0)
    R_, W_, K_, KK_, B_ = range(5)

    def bc(j, t, kx):
        return jnp.broadcast_to(OPS[t, j, pl.ds(kx, 1), :], (VH, LANES))

    def tree(parts):
        return (parts[0] + parts[1]) + (parts[2] + parts[3])

    for hv in range(2):
        rows = pl.ds(hv * VH, VH)
        accs = [None] * 4
        for kx in range(K):
            p = S[kx, rows, :] * bc(KK_, 0, kx)
            accs[kx % 4] = p if accs[kx % 4] is None else accs[kx % 4] + p
        SA[hv] = tree(accs)

    def step(t, _):
        tn = jnp.minimum(t + 1, TBLK - 1)
        tm_ = TBLK - 1 - t
        for hv in range(2):
            rows = pl.ds(hv * VH, VH)
            sa = SA[hv]
            vt = VTS[t, rows, :]
            yacc = [None] * 4
            sacc = [None] * 4
            for kx in range(K):
                s_new = S[kx, rows, :] * bc(W_, t, kx) - sa * bc(B_, t, kx) + vt * bc(K_, t, kx)
                S[kx, rows, :] = s_new
                py = s_new * bc(R_, t, kx)
                ps = s_new * bc(KK_, tn, kx)
                a = kx % 4
                yacc[a] = py if yacc[a] is None else yacc[a] + py
                sacc[a] = ps if sacc[a] is None else sacc[a] + ps
            y = tree(yacc)
            yf_ref[t, rows, :] = y
            yb_ref[tm_, rows, :] = y
            SA[hv] = tree(sacc)
        return 0
    lax.fori_loop(0, TBLK, step, 0)

    @pl.when(tb == nblk - 1)
    def _():
        sfin_ref[...] = S[...]


def _rwkv_scan(r, w, k, kk, b, v, fwd_mask, s0):
    L, K, NL = r.shape
    V = v.shape[1]
    TBLK = min(32, L)
    nblk = L // TBLK
    ngrp = NL // LANES
    kf = pl.BlockSpec((TBLK, K, LANES), lambda g, t: (t, 0, g))
    km = pl.BlockSpec((TBLK, K, LANES), lambda g, t: (nblk - 1 - t, 0, g))
    vf = pl.BlockSpec((TBLK, V, LANES), lambda g, t: (t, 0, g))
    vm = pl.BlockSpec((TBLK, V, LANES), lambda g, t: (nblk - 1 - t, 0, g))
    sspec = pl.BlockSpec((K, V, LANES), lambda g, t: (0, 0, g))
    return pl.pallas_call(
        functools.partial(_rwkv_scan_kernel, TBLK=TBLK, V=V, nblk=nblk),
        grid=(ngrp, nblk),
        in_specs=[kf, km] * 5 + [vf, vm, pl.BlockSpec((1, LANES), lambda g, t: (0, g)), sspec],
        out_specs=(vf, vm, sspec),
        out_shape=(jax.ShapeDtypeStruct((L, V, NL), F32), jax.ShapeDtypeStruct((L, V, NL), F32),
                   jax.ShapeDtypeStruct((K, V, NL), F32)),
        scratch_shapes=[pltpu.VMEM((K, V, LANES), F32), pltpu.VMEM((2, V // 2, LANES), F32),
                        pltpu.VMEM((TBLK, 5, K, LANES), F32), pltpu.VMEM((TBLK, V, LANES), F32)],
        compiler_params=_cparams(("parallel", "arbitrary")),
        name="rwkv_scan",
    )(r, r, w, w, k, k, kk, kk, b, b, v, v, fwd_mask, s0)


KQ = 4
CH = LANES // KQ


def _rwkv_scan2_kernel(rf, rm, wf, wm, kf, km, kkf, kkm, bf, bm, vf, vm, s0_ref, yf_ref, yb_ref, S, YQ, SA, *, TBLK):
    V = RWKV_HEADSIZE
    NQ = RWKV_HEADSIZE // KQ
    tb = pl.program_id(0)

    @pl.when(tb == 0)
    def _():
        S[...] = s0_ref[...]

    VH = V // 2

    def bc(ref, tt, q):
        return jnp.broadcast_to(ref[tt, pl.ds(q, 1), :], (VH, LANES))

    def all_quarters(x):
        return (x + pltpu.roll(x, CH, 1)) + (pltpu.roll(x, 2 * CH, 1) + pltpu.roll(x, 3 * CH, 1))

    dirs = ((rf, wf, kf, kkf, bf, vf), (rm, wm, km, kkm, bm, vm))
    time_of = (lambda t: t, lambda t: TBLK - 1 - t)
    chains = [(d, hv) for d in range(N_DIR) for hv in range(2)]

    def first_sa(d, hv):
        kk_ = dirs[d][3]
        tt = time_of[d](0)
        acc = [None, None]
        for q in range(NQ):
            p = S[d, q, pl.ds(hv * VH, VH), :] * bc(kk_, tt, q)
            acc[q % 2] = p if acc[q % 2] is None else acc[q % 2] + p
        return acc[0] + acc[1]

    TG = min(16, TBLK)
    lane_r = lax.broadcasted_iota(jnp.int32, (2 * LANES, CH), 0) % CH
    fold2 = jnp.where(lax.broadcasted_iota(jnp.int32, (2 * LANES, CH), 1) == lane_r, 1.0, 0.0).astype(BF16)

    last = len(chains) - 1
    for c, (d, hv) in enumerate(chains):
        part = first_sa(d, hv)
        SA[c] = part if c == last else all_quarters(part)

    def step(t, _):
        sa_last = all_quarters(SA[last])
        for c, (d, hv) in enumerate(chains):
            r_, w_, k_, kk_, b_, v_ = dirs[d]
            tt = time_of[d](t)
            tn = time_of[d](jnp.minimum(t + 1, TBLK - 1))
            rows = pl.ds(hv * VH, VH)
            sa = sa_last if c == last else SA[c]
            vt = v_[tt, rows, :]
            yacc = None
            sacc = None
            for q in range(NQ):
                s_new = S[d, q, rows, :] * bc(w_, tt, q) - sa * bc(b_, tt, q) + vt * bc(k_, tt, q)
                S[d, q, rows, :] = s_new
                py = s_new * bc(r_, tt, q)
                ps = s_new * bc(kk_, tn, q)
                yacc = py if yacc is None else yacc + py
                sacc = ps if sacc is None else sacc + ps
            YQ[d, tt, rows, :] = yacc
            SA[c] = sacc if c == last else all_quarters(sacc)
        return 0
    lax.fori_loop(0, TBLK, step, 0)

    def finish(i, _):
        for d, y_ in enumerate((yf_ref, yb_ref)):
            hi, lo = _split_bf16(YQ[d, pl.ds(i * TG, TG)].reshape(TG * V, LANES))
            y = jnp.dot(jnp.concatenate([hi, lo], axis=-1), fold2, preferred_element_type=F32)
            y_[pl.ds(i * TG, TG)] = y.reshape(TG, V, CH)
        return 0
    lax.fori_loop(0, TBLK // TG, finish, 0)


def _rwkv_scan2(r, w0, w1, k, kk, b, v, s0):
    L, NQ, _ = r.shape
    V = RWKV_HEADSIZE
    TBLK = min(64, L)
    nblk = L // TBLK
    fspec = pl.BlockSpec((TBLK, NQ, LANES), lambda t: (t, 0, 0))
    mspec = pl.BlockSpec((TBLK, NQ, LANES), lambda t: (nblk - 1 - t, 0, 0))
    vfspec = pl.BlockSpec((TBLK, V, LANES), lambda t: (t, 0, 0))
    vmspec = pl.BlockSpec((TBLK, V, LANES), lambda t: (nblk - 1 - t, 0, 0))
    yfspec = pl.BlockSpec((TBLK, V, CH), lambda t: (t, 0, 0))
    ymspec = pl.BlockSpec((TBLK, V, CH), lambda t: (nblk - 1 - t, 0, 0))
    yf, yb = pl.pallas_call(
        functools.partial(_rwkv_scan2_kernel, TBLK=TBLK),
        grid=(nblk,),
        in_specs=[fspec, mspec] * 5 + [vfspec, vmspec, _full((N_DIR, NQ, V, LANES))],
        out_specs=(yfspec, ymspec),
        out_shape=(jax.ShapeDtypeStruct((L, V, CH), F32), jax.ShapeDtypeStruct((L, V, CH), F32)),
        scratch_shapes=[pltpu.VMEM((N_DIR, NQ, V, LANES), F32), pltpu.VMEM((N_DIR, TBLK, V, LANES), F32),
                        pltpu.VMEM((2 * N_DIR, V // 2, LANES), F32)],
        compiler_params=_cparams(("arbitrary",)),
        name="rwkv_scan2",
    )(r, r, w0, w1, k, k, kk, kk, b, b, v, v, s0)
    return yf, yb


HEADS_PAD = 8
SCAN_WIDTH = HEADS_PAD * RWKV_HEADSIZE


def _pad_chains(x, B):
    return x if B * HEADS_PAD == CH else jnp.pad(x, ((0, 0),) * (x.ndim - 1) + ((0, CH - B * HEADS_PAD),))


def _to_scan2_k(t, B, L):
    x = t.reshape(B, L, HEADS_PAD, RWKV_HEADSIZE).transpose(1, 3, 0, 2)
    x = _pad_chains(x.reshape(L, RWKV_HEADSIZE // KQ, KQ, B * HEADS_PAD), B)
    return x.reshape(L, RWKV_HEADSIZE // KQ, LANES)


def _to_scan2_v(t, B, L):
    x = t.reshape(B, L, HEADS_PAD, RWKV_HEADSIZE).transpose(1, 3, 0, 2).reshape(L, RWKV_HEADSIZE, B * HEADS_PAD)
    return jnp.tile(_pad_chains(x, B), (1, 1, KQ))


def _from_scan2(y, B, L):
    y = y[:, :, :B * HEADS_PAD]
    return y.reshape(L, RWKV_HEADSIZE, B, HEADS_PAD).transpose(2, 0, 3, 1).reshape(B * L, SCAN_WIDTH)


def _state_to_scan2(s):
    B = s.shape[0]
    s = jnp.pad(s, ((0, 0), (0, 0), (0, HEADS_PAD - RWKV_HEADS), (0, 0), (0, 0)))
    x = s.transpose(1, 4, 3, 0, 2).reshape(N_DIR, RWKV_HEADSIZE // KQ, KQ, RWKV_HEADSIZE, B * HEADS_PAD)
    x = _pad_chains(x, B)
    return x.transpose(0, 1, 3, 2, 4).reshape(N_DIR, RWKV_HEADSIZE // KQ, RWKV_HEADSIZE, LANES)


def _ctx_lanes(B):
    chains = B * N_DIR * RWKV_HEADS
    return chains, -(-chains // LANES) * LANES


def _to_ctx(t_fwd, t_bwd, B, L):
    chains, lanes = _ctx_lanes(B)
    if t_bwd is t_fwd:
        x = t_fwd.reshape(B, L, RWKV_HEADS, RWKV_HEADSIZE).transpose(1, 3, 0, 2)
        x = jnp.broadcast_to(x[:, :, :, None, :], (L, RWKV_HEADSIZE, B, N_DIR, RWKV_HEADS))
    else:
        x = jnp.stack([t_fwd, t_bwd], 0).reshape(N_DIR, B, L, RWKV_HEADS, RWKV_HEADSIZE).transpose(2, 4, 1, 0, 3)
    x = x.reshape(L, RWKV_HEADSIZE, chains)
    return x if lanes == chains else jnp.pad(x, ((0, 0), (0, 0), (0, lanes - chains)))


def _ctx_fwd_mask(B):
    chains, lanes = _ctx_lanes(B)
    lane = jnp.arange(lanes)
    return (((lane // RWKV_HEADS) % N_DIR == 0) & (lane < chains)).astype(F32).reshape(1, lanes)


def _from_ctx(y, B, L, d):
    chains, _ = _ctx_lanes(B)
    y = y[:, :, :chains].reshape(L, RWKV_HEADSIZE, B, N_DIR, RWKV_HEADS)[:, :, :, d]
    return y.transpose(2, 0, 3, 1).reshape(B * L, RWKV_WIDTH)


def _ctx_state_from(x, B):
    chains, _ = _ctx_lanes(B)
    x = x[:, :, :chains].reshape(RWKV_HEADSIZE, RWKV_HEADSIZE, B, N_DIR, RWKV_HEADS)
    return x.transpose(2, 3, 4, 1, 0)


def _post_kernel(x_ref, mod_ref, ys5_ref, yf_ref, yb_ref, z_ref, rf_ref, rb_ref, bonus_ref, g_ref,
                 gluw_ref, glub_ref, ssdg_ref, lng_ref, lnb_ref, seg_ref, wout_ref, n2g_ref, rw_ref,
                 x1_ref, hb_ref, aff_ref):
    m = mod_ref[...]
    D = D_MODEL
    zg = _gelu_tanh(ys5_ref[...])
    gate = jnp.dot(zg.astype(BF16), gluw_ref[...], preferred_element_type=F32) + glub_ref[...]
    y_a = zg * _sigmoid(gate)
    z = z_ref[...]
    yb = (yf_ref[...] + yb_ref[...]) * (z * _sigmoid(z))
    y_b = yb * lax.rsqrt(jnp.mean(yb * yb, axis=-1, keepdims=True) + EPS) * ssdg_ref[...]
    seg = seg_ref[...] * (1.0 / RWKV_HEADSIZE)
    yr = rf_ref[...] + rb_ref[...]
    mean = _dot_exact_rhs(yr, seg)
    cen = yr - mean
    var = jnp.dot((cen * cen).astype(BF16), seg.astype(BF16), preferred_element_type=F32)
    yn = cen * lax.rsqrt(var + GN_EPS) * lng_ref[...] + lnb_ref[...]
    y_c = (yn + bonus_ref[...]) * g_ref[...]
    o = jnp.dot(y_a.astype(BF16), wout_ref[0:S5_WIDTH, :], preferred_element_type=F32)
    o = o + jnp.dot(y_b.astype(BF16), wout_ref[S5_WIDTH:S5_WIDTH + SSD_WIDTH, :], preferred_element_type=F32)
    o = o + jnp.dot(y_c.astype(BF16), wout_ref[S5_WIDTH + SSD_WIDTH:, :], preferred_element_type=F32)
    x1 = x_ref[...] + m[:, 2 * D:3 * D] * o
    x1_ref[...] = x1
    h2 = x1 * lax.rsqrt(jnp.mean(x1 * x1, axis=-1, keepdims=True) + EPS) * n2g_ref[...]
    h2 = h2 * (1.0 + m[:, 4 * D:5 * D]) + m[:, 3 * D:4 * D]
    hb_ref[...] = h2.astype(BF16)
    logits = _dot3(rw_ref[...], h2, (((1,), (1,)), ((), ())))
    mx = jnp.max(logits, axis=0, keepdims=True)
    ex = jnp.exp(logits - mx)
    aff_ref[...] = ex / jnp.sum(ex, axis=0, keepdims=True)


def _post_mixer(x, mod3, row_of_block, ys5, yssd, z, rf, rb, bonus, g, lp):
    ntok = x.shape[0]
    tm = TOKEN_BLOCK
    tok = lambda w: pl.BlockSpec((tm, w), lambda i: (i, 0))
    row = lambda t: t.reshape(1, -1)
    W = RWKV_WIDTH
    return pl.pallas_call(
        _post_kernel,
        grid=(ntok // tm,),
        in_specs=[
            tok(D_MODEL),
            pl.BlockSpec((None, 1, 6 * D_MODEL), lambda i: (row_of_block(i), 0, 0)),
            tok(S5_WIDTH),
            tok(SSD_WIDTH), tok(SSD_WIDTH),
            tok(SSD_WIDTH), tok(W), tok(W), tok(W), tok(W),
            _full((S5_WIDTH, S5_WIDTH)), _full((1, S5_WIDTH)), _full((1, SSD_WIDTH)), _full((1, W)), _full((1, W)),
            _full((W, W)), _full((D_MODEL, D_MODEL)), _full((1, D_MODEL)), _full((N_EXPERTS, D_MODEL)),
        ],
        out_specs=(tok(D_MODEL), tok(D_MODEL), pl.BlockSpec((N_EXPERTS, tm), lambda i: (0, i))),
        out_shape=(jax.ShapeDtypeStruct((ntok, D_MODEL), F32), jax.ShapeDtypeStruct((ntok, D_MODEL), BF16),
                   jax.ShapeDtypeStruct((N_EXPERTS, ntok), F32)),
        compiler_params=_cparams(("parallel",)),
        name="post_mixer",
    )(x, mod3, ys5, yssd[0], yssd[1], z, rf, rb, bonus, g,
      lp['s5_glu_w'], row(lp['s5_glu_b']), row(lp['ssd_norm_g']), row(lp['rwkv_ln_g']),
      row(lp['rwkv_ln_b']), lp['seg'], lp['w_out'], row(lp['norm2_g']),
      lp['router_w'].T)


def _select_kernel(aff_ref, slot_ref, affo_ref, *, n, cap, bg):
    for g in range(bg):
        _select_one(aff_ref[:, g * n:(g + 1) * n], slot_ref.at[g], affo_ref.at[g], n, cap)


def _select_one(a, slot_ref, affo_ref, n, cap):
    E = N_EXPERTS
    bits = pltpu.bitcast(a, jnp.int32)
    thr = jnp.zeros((E, 1), jnp.int32)
    capf = float(cap)
    for bit in range(30, -1, -1):
        cand = thr | (1 << bit)
        cnt = jnp.sum(jnp.where(bits >= cand, 1.0, 0.0), axis=1, keepdims=True)
        thr = jnp.where(cnt >= capf, cand, thr)
    gt = bits > thr
    eq = bits == thr
    need = capf - jnp.sum(jnp.where(gt, 1.0, 0.0), axis=1, keepdims=True)
    CW = min(256, n)
    ui = lax.broadcasted_iota(jnp.int32, (CW, CW), 0)
    uj = lax.broadcasted_iota(jnp.int32, (CW, CW), 1)
    upper = jnp.where(ui < uj, 1.0, 0.0).astype(BF16)

    def excl_cumsum(mask_f):
        outs = []
        off = jnp.zeros((E, 1), F32)
        for c in range(n // CW):
            mc = mask_f[:, c * CW:(c + 1) * CW]
            outs.append(jnp.dot(mc.astype(BF16), upper, preferred_element_type=F32) + off)
            off = off + jnp.sum(mc, axis=1, keepdims=True)
        return jnp.concatenate(outs, axis=1)

    eq_rank = excl_cumsum(jnp.where(eq, 1.0, 0.0))
    sel = jnp.where(gt, 1.0, jnp.where(eq, jnp.where(eq_rank < need, 1.0, 0.0), 0.0))
    pos = excl_cumsum(sel)
    slot = jnp.where(sel > 0.0, pos, -1.0)
    for e in range(E):
        slot_ref[e] = slot[e:e + 1, :]
        affo_ref[e] = a[e:e + 1, :]


def _select(aff, B, n, cap):
    bg = 4 if (B % 4 == 0 and n <= 512) else (2 if B % 2 == 0 else 1)
    spec = pl.BlockSpec((bg, N_EXPERTS, 1, n), lambda b: (b, 0, 0, 0))
    return pl.pallas_call(
        functools.partial(_select_kernel, n=n, cap=cap, bg=bg),
        grid=(B // bg,),
        in_specs=[pl.BlockSpec((N_EXPERTS, bg * n), lambda b: (0, b))],
        out_specs=(spec, spec),
        out_shape=(jax.ShapeDtypeStruct((B, N_EXPERTS, 1, n), F32), jax.ShapeDtypeStruct((B, N_EXPERTS, 1, n), F32)),
        compiler_params=_cparams(("parallel",)),
        name="ec_select",
    )(aff)


def _slot_block_range(slots_f, cap, SB):
    lo = jnp.min(jnp.where(slots_f >= 0.0, slots_f, float(cap))).astype(jnp.int32)
    hi = jnp.max(slots_f).astype(jnp.int32)
    first = lo // SB
    count = jnp.where(hi >= 0, hi // SB - first + 1, 0)
    return first, count


EC_SLOT_BLOCK = 256


def _one_hot_all(slot_ref, aff_ref, n, cap):
    srow = lax.broadcasted_iota(jnp.int32, (cap, n), 0).astype(F32)
    ohs, gates = [], []
    for e in range(N_EXPERTS):
        hit = slot_ref[e] == srow
        ohs.append(jnp.where(hit, 1.0, 0.0).astype(BF16))
        if aff_ref is not None:
            gates.append(jnp.sum(jnp.where(hit, aff_ref[e], 0.0), axis=1, keepdims=True))
    return jnp.concatenate(ohs, axis=0), (jnp.concatenate(gates, axis=0) if gates else None)


def _gather_all_kernel(hb_ref, slot_ref, aff_ref, xs_ref, gate_ref, *, n, cap):
    oh, gate = _one_hot_all(slot_ref, aff_ref, n, cap)
    xs = jnp.dot(oh, hb_ref[...], preferred_element_type=F32)
    xs_ref[...] = xs.astype(BF16).reshape(N_EXPERTS, cap, D_MODEL)
    gate_ref[...] = gate.reshape(N_EXPERTS, cap, 1)


def _gather_kernel(hb_ref, slot_ref, aff_ref, xs_ref, gate_ref, acc, gacc, *, n, cap):
    NC = min(512, n)
    SB = min(EC_SLOT_BLOCK // 2, cap)
    acc[...] = jnp.zeros_like(acc)
    gacc[...] = jnp.zeros_like(gacc)
    srow = lax.broadcasted_iota(jnp.int32, (SB, NC), 0).astype(F32)
    seen = jnp.int32(0)
    for c in range(n // NC):
        sl = slot_ref[:, c * NC:(c + 1) * NC]
        cnt = jnp.sum(jnp.where(sl >= 0.0, 1.0, 0.0)).astype(jnp.int32)
        first = seen // SB
        count = jnp.where(cnt > 0, (seen + cnt - 1) // SB - first + 1, 0)
        seen = seen + cnt
        for j in range(min(cap // SB, NC // SB + 1)):
            @pl.when(j < count)
            def _(c=c, j=j, sl=sl, first=first):
                base = pl.multiple_of((first + j) * SB, SB)
                hit = (sl - base.astype(F32)) == srow
                oh = jnp.where(hit, 1.0, 0.0).astype(BF16)
                acc[pl.ds(base, SB), :] += jnp.dot(oh, hb_ref[c * NC:(c + 1) * NC, :], preferred_element_type=F32)
                gacc[pl.ds(base, SB), :] += jnp.sum(jnp.where(hit, aff_ref[:, c * NC:(c + 1) * NC], 0.0), axis=1,
                                                    keepdims=True)
    xs_ref[...] = acc[...].astype(BF16)
    gate_ref[...] = gacc[...]


def _gather(hb, slot4, aff4, B, n, cap):
    E = N_EXPERTS
    if E * cap <= 512:
        all_spec = pl.BlockSpec((None, E, 1, n), lambda b: (b, 0, 0, 0))
        return pl.pallas_call(
            functools.partial(_gather_all_kernel, n=n, cap=cap),
            grid=(B,),
            in_specs=[pl.BlockSpec((n, D_MODEL), lambda b: (b, 0)), all_spec, all_spec],
            out_specs=(pl.BlockSpec((None, E, cap, D_MODEL), lambda b: (b, 0, 0, 0)),
                       pl.BlockSpec((None, E, cap, 1), lambda b: (b, 0, 0, 0))),
            out_shape=(jax.ShapeDtypeStruct((B, E, cap, D_MODEL), BF16), jax.ShapeDtypeStruct((B, E, cap, 1), F32)),
            compiler_params=_cparams(("parallel",)),
            name="ec_gather_all",
        )(hb, slot4, aff4)
    return pl.pallas_call(
        functools.partial(_gather_kernel, n=n, cap=cap),
        grid=(B, E),
        in_specs=[
            pl.BlockSpec((n, D_MODEL), lambda b, e: (b, 0)),
            pl.BlockSpec((None, None, 1, n), lambda b, e: (b, e, 0, 0)),
            pl.BlockSpec((None, None, 1, n), lambda b, e: (b, e, 0, 0)),
        ],
        out_specs=(pl.BlockSpec((None, None, cap, D_MODEL), lambda b, e: (b, e, 0, 0)),
                   pl.BlockSpec((None, None, cap, 1), lambda b, e: (b, e, 0, 0))),
        out_shape=(jax.ShapeDtypeStruct((B, E, cap, D_MODEL), BF16), jax.ShapeDtypeStruct((B, E, cap, 1), F32)),
        scratch_shapes=[pltpu.VMEM((cap, D_MODEL), F32), pltpu.VMEM((cap, 1), F32)],
        compiler_params=_cparams(("parallel", "arbitrary")),
        name="ec_gather",
    )(hb, slot4, aff4)


def _ffn_kernel(xs_ref, gate_ref, w1_ref, w3_ref, w2_ref, o_ref):
    bg, cap, _ = xs_ref.shape
    x = xs_ref[...].reshape(bg * cap, D_MODEL)
    h1 = jnp.dot(x, w1_ref[...], preferred_element_type=F32)
    h3 = jnp.dot(x, w3_ref[...], preferred_element_type=F32)
    hid = (h1 * _sigmoid(h1) * h3).astype(BF16)
    o = jnp.dot(hid, w2_ref[...], preferred_element_type=F32) * gate_ref[...].reshape(bg * cap, 1)
    o_ref[...] = o.astype(BF16).reshape(bg, cap, D_MODEL)


def _expert_ffn(xs, gate, w1, w3, w2, bg):
    B, E, cap, _ = xs.shape
    return pl.pallas_call(
        _ffn_kernel,
        grid=(E, B // bg),
        in_specs=[
            pl.BlockSpec((bg, None, cap, D_MODEL), lambda e, b: (b, e, 0, 0)),
            pl.BlockSpec((bg, None, cap, 1), lambda e, b: (b, e, 0, 0)),
            pl.BlockSpec((None, D_MODEL, D_EXPERT), lambda e, b: (e, 0, 0)),
            pl.BlockSpec((None, D_MODEL, D_EXPERT), lambda e, b: (e, 0, 0)),
            pl.BlockSpec((None, D_EXPERT, D_MODEL), lambda e, b: (e, 0, 0)),
        ],
        out_specs=pl.BlockSpec((bg, None, cap, D_MODEL), lambda e, b: (b, e, 0, 0)),
        out_shape=jax.ShapeDtypeStruct((B, E, cap, D_MODEL), BF16),
        compiler_params=_cparams(("parallel", "parallel")),
        name="ec_ffn",
    )(xs, gate, w1, w3, w2)


_TN_DIMS = (((0,), (0,)), ((), ()))


def _ec_residual(x1_ref, mod_ref, fg_ref, out_ref, ffn, final):
    x2 = x1_ref[...] + mod_ref[:, 5 * D_MODEL:6 * D_MODEL] * ffn
    if final:
        x2 = x2 * lax.rsqrt(jnp.mean(x2 * x2, axis=-1, keepdims=True) + EPS) * fg_ref[...]
    out_ref[...] = x2


def _scatter_all_kernel(slot_ref, o_ref, x1_ref, mod_ref, fg_ref, out_ref, *, n, cap, final):
    oh, _ = _one_hot_all(slot_ref, None, n, cap)
    ffn = lax.dot_general(oh, o_ref[...].reshape(N_EXPERTS * cap, D_MODEL), _TN_DIMS, preferred_element_type=F32)
    _ec_residual(x1_ref, mod_ref, fg_ref, out_ref, ffn, final)


def _scatter_kernel(slot_ref, o_ref, x1_ref, mod_ref, fg_ref, out_ref, acc, *, cap, final):
    e = pl.program_id(2)
    tn = x1_ref.shape[0]

    @pl.when(e == 0)
    def _():
        acc[...] = jnp.zeros_like(acc)

    sl = slot_ref[...]
    SB = min(EC_SLOT_BLOCK, cap)
    srow = lax.broadcasted_iota(jnp.int32, (SB, tn), 0).astype(F32)
    first, count = _slot_block_range(sl, cap, SB)
    for j in range(min(cap // SB, tn // SB + 1)):
        @pl.when(j < count)
        def _(j=j):
            base = pl.multiple_of((first + j) * SB, SB)
            oh = jnp.where((sl - base.astype(F32)) == srow, 1.0, 0.0).astype(BF16)
            acc[...] += lax.dot_general(oh, o_ref[pl.ds(base, SB), :], _TN_DIMS, preferred_element_type=F32)

    @pl.when(e == N_EXPERTS - 1)
    def _():
        _ec_residual(x1_ref, mod_ref, fg_ref, out_ref, acc[...], final)


def _scatter(slot4, o, x1, mod3, mod_row, final_g, B, n, cap, final):
    E = N_EXPERTS
    if E * cap <= 512:
        return pl.pallas_call(
            functools.partial(_scatter_all_kernel, n=n, cap=cap, final=final),
            grid=(B,),
            in_specs=[
                pl.BlockSpec((None, E, 1, n), lambda b: (b, 0, 0, 0)),
                pl.BlockSpec((None, E, cap, D_MODEL), lambda b: (b, 0, 0, 0)),
                pl.BlockSpec((n, D_MODEL), lambda b: (b, 0)),
                pl.BlockSpec((None, 1, 6 * D_MODEL), lambda b: (mod_row(b), 0, 0)),
                _full((1, D_MODEL)),
            ],
            out_specs=pl.BlockSpec((n, D_MODEL), lambda b: (b, 0)),
            out_shape=jax.ShapeDtypeStruct((B * n, D_MODEL), F32),
            compiler_params=_cparams(("parallel",)),
            name="ec_scatter_all",
        )(slot4, o, x1, mod3, final_g.reshape(1, D_MODEL))
    tn = min(1024, n)
    nt = n // tn
    return pl.pallas_call(
        functools.partial(_scatter_kernel, cap=cap, final=final),
        grid=(B, nt, N_EXPERTS),
        in_specs=[
            pl.BlockSpec((None, None, 1, tn), lambda b, t, e: (b, e, 0, t)),
            pl.BlockSpec((None, None, cap, D_MODEL), lambda b, t, e: (b, e, 0, 0)),
            pl.BlockSpec((tn, D_MODEL), lambda b, t, e: (b * nt + t, 0)),
            pl.BlockSpec((None, 1, 6 * D_MODEL), lambda b, t, e: (mod_row(b), 0, 0)),
            _full((1, D_MODEL)),
        ],
        out_specs=pl.BlockSpec((tn, D_MODEL), lambda b, t, e: (b * nt + t, 0)),
        out_shape=jax.ShapeDtypeStruct((B * n, D_MODEL), F32),
        scratch_shapes=[pltpu.VMEM((tn, D_MODEL), F32)],
        compiler_params=_cparams(("parallel", "parallel", "arbitrary")),
        name="ec_scatter",
    )(slot4, o, x1, mod3, final_g.reshape(1, D_MODEL))


def _expert_choice(hb, aff, x1, mod3, mod_row, final_g, w1, w3, w2, B, n, final):
    cap = EC_FACTOR * n // N_EXPERTS
    slot4, aff4 = _select(aff, B, n, cap)
    xs, gate = _gather(hb, slot4, aff4, B, n, cap)
    bg = max(1, min(B, 256 // cap))
    o = _expert_ffn(xs, gate, w1, w3, w2, bg)
    return _scatter(slot4, o, x1, mod3, mod_row, final_g, B, n, cap, final)


def kernel(x_prompt, x_sample, c, state_s5_re, state_s5_im, state_ssd, state_rwkv, c_ctx, ada_w, ada_b, norm1_g, norm2_g, w_in, w_out, s5_a_re, s5_a_im, s5_log_dt, s5_b_re, s5_b_im, s5_c_re, s5_c_im, s5_d, s5_glu_w, s5_glu_b, ssd_conv_w, ssd_conv_b, ssd_a_log, ssd_dt_bias, ssd_d, ssd_norm_g, rwkv_mu, rwkv_w0, rwkv_w_up, rwkv_a0, rwkv_a_up, rwkv_g_up, rwkv_k_k, rwkv_k_a, rwkv_r_k, rwkv_ln_g, rwkv_ln_b, router_w, exp_w1, exp_w3, exp_w2, final_g):
    Bp, Lp, D = x_prompt.shape
    Bs, Ls, _ = x_sample.shape
    depth = ada_w.shape[0]
    Np, Ns = Bp * Lp, Bs * Ls
    tm = TOKEN_BLOCK
    grid_rows = Ls // GRID_W
    nfs = S5_GROUPS // 8
    assert Lp % tm == 0 and Ls % tm == 0 and Lp % SSD_CHUNK == 0

    n_rows = 1 + Bs
    rows_pad = -(-n_rows // 8) * 8
    cond = jnp.zeros((rows_pad, D), F32).at[0].set(c_ctx).at[1:n_rows].set(c)
    mod = _modulation(cond, ada_w, ada_b)
    s_blocks_per_req = Ls // tm
    row_p = lambda i: 0
    row_s = lambda i: 1 + i // s_blocks_per_req

    ab_re, ab_im, bb_re, bb_im = _s5_discretize(s5_a_re, s5_a_im, s5_log_dt, s5_b_re, s5_b_im)
    w_in_pad = _pad_in_weight(w_in)
    w_out_b = w_out.astype(BF16)
    glu_w_b = s5_glu_w.astype(BF16)
    exp_w1_b, exp_w3_b, exp_w2_b = exp_w1.astype(BF16), exp_w3.astype(BF16), exp_w2.astype(BF16)
    seg = _segment_ones(RWKV_WIDTH, RWKV_HEADSIZE)

    xp = x_prompt.reshape(Np, D)
    xs = x_sample.reshape(Ns, D)
    new_s5_re, new_s5_im, new_ssd, new_rwkv = [], [], [], []
    QP = 2 if Bp % 2 == 0 and Bp >= 2 else 1
    RP = Bp // QP
    assert Bs * HEADS_PAD <= CH

    for l in range(depth):
        lp = {
            'rwkv_mu': rwkv_mu[l], 'rwkv_w0': rwkv_w0[l], 'rwkv_w_up': rwkv_w_up[l], 'rwkv_a0': rwkv_a0[l],
            'rwkv_a_up': rwkv_a_up[l], 'rwkv_g_up': rwkv_g_up[l], 'rwkv_k_k': rwkv_k_k[l], 'rwkv_k_a': rwkv_k_a[l],
            'rwkv_r_k': rwkv_r_k[l].reshape(-1), 'rwkv_ln_g': rwkv_ln_g[l], 'rwkv_ln_b': rwkv_ln_b[l],
            's5_glu_w': glu_w_b[l], 's5_glu_b': s5_glu_b[l], 'ssd_norm_g': ssd_norm_g[l], 'w_out': w_out_b[l],
            'norm2_g': norm2_g[l], 'router_w': router_w[l], 'seg': seg,
        }
        mod3 = mod[l].reshape(rows_pad, 1, 6 * D)
        us5_p, z_p, xbc_p, dt_p, rkv_p, lo_p = _in_projection(xp, mod3, row_p, norm1_g[l], w_in_pad[l])
        us5_s, z_s, xbc_s, dt_s, rkv_s, lo_s = _in_projection(xs, mod3, row_s, norm1_g[l], w_in_pad[l])

        tables = _s5_layer_tables(ab_re[l], ab_im[l], bb_re[l], bb_im[l], s5_c_re[l], s5_c_im[l])
        d_row = s5_d[l].reshape(1, S5_WIDTH)
        up = us5_p.reshape(QP, RP, Lp, S5_WIDTH).transpose(0, 2, 1, 3).reshape(QP, Lp * RP, S5_WIDTH)
        yp, hfin = _s5_scan(up, None, tables, d_row, R=RP, n_slab=Lp, chained=False, want_final=True)
        ys5_p = yp.reshape(QP, Lp, RP, S5_WIDTH).transpose(0, 2, 1, 3).reshape(Np, S5_WIDTH)
        hf = hfin.transpose(0, 3, 1, 2, 4).reshape(Bp, N_DIR, nfs, 2, 8, S5_STATE)
        new_s5_re.append(hf[:, :, :, 0].reshape(Bp, N_DIR, S5_GROUPS, S5_STATE))
        new_s5_im.append(hf[:, :, :, 1].reshape(Bp, N_DIR, S5_GROUPS, S5_STATE))
        h0 = jnp.concatenate([state_s5_re[:, l].reshape(Bs, N_DIR, nfs, 1, S5_SLICE_ST),
                              state_s5_im[:, l].reshape(Bs, N_DIR, nfs, 1, S5_SLICE_ST)], axis=-1)
        (ysm,) = _s5_scan(us5_s.reshape(Bs, Ls, S5_WIDTH), h0, tables, d_row, R=GRID_W, n_slab=grid_rows,
                          chained=True, want_final=False)
        ys5_s = ysm.reshape(Ns, S5_WIDTH)

        ssd_args = (ssd_conv_w[l], ssd_conv_b[l], ssd_dt_bias[l], ssd_a_log[l], ssd_d[l])
        *yssd_p, hssd = _ssd_scan(xbc_p, dt_p, None, *ssd_args, B=Bp, L=Lp, want_final=True)
        new_ssd.append(hssd.transpose(0, 1, 2, 4, 3))
        yssd_s = _ssd_scan(xbc_s, dt_s, state_ssd[:, l].transpose(0, 1, 2, 4, 3), *ssd_args,
                           B=Bs, L=Ls, want_final=False)

        r_, w0_, w1_, k_, v_, kk_, b_, g_p, bonus_p = _rwkv_prep(rkv_p, lo_p, lp, Lp // tm)
        cx = lambda t: _to_ctx(t, t, Bp, Lp)
        zero_state = jnp.zeros((RWKV_HEADSIZE, RWKV_HEADSIZE, _ctx_lanes(Bp)[1]), F32)
        yf_, yb_, sfin = _rwkv_scan(cx(r_), _to_ctx(w0_, w1_, Bp, Lp), cx(k_), cx(kk_), cx(b_), cx(v_),
                                    _ctx_fwd_mask(Bp), zero_state)
        rf_p = _from_ctx(yf_, Bp, Lp, 0)
        rb_p = _from_ctx(yb_, Bp, Lp, 1)
        new_rwkv.append(_ctx_state_from(sfin, Bp))

        r_, w0_, w1_, k_, v_, kk_, b_, g_s, bonus_s = _rwkv_prep(rkv_s, lo_s, lp, Ls // tm, SCAN_WIDTH)
        sc = lambda t: _to_scan2_k(t, Bs, Ls)
        yf_, yb_ = _rwkv_scan2(sc(r_), sc(w0_), sc(w1_), sc(k_), sc(kk_), sc(b_), _to_scan2_v(v_, Bs, Ls),
                               _state_to_scan2(state_rwkv[:, l]))
        rf_s = _from_scan2(yf_, Bs, Ls)
        rb_s = _from_scan2(yb_, Bs, Ls)

        x1_p, hb_p, aff_p = _post_mixer(xp, mod3, row_p, ys5_p, yssd_p, z_p, rf_p, rb_p, bonus_p, g_p, lp)
        x1_s, hb_s, aff_s = _post_mixer(xs, mod3, row_s, ys5_s, yssd_s, z_s, rf_s, rb_s, bonus_s, g_s, lp)

        final = l == depth - 1
        ew = (exp_w1_b[l], exp_w3_b[l], exp_w2_b[l])
        xp = _expert_choice(hb_p, aff_p, x1_p, mod3, lambda b: 0, final_g, *ew, Bp, Lp, final)
        xs = _expert_choice(hb_s, aff_s, x1_s, mod3, lambda b: 1 + b, final_g, *ew, Bs, Ls, final)

    y_prompt = xp.reshape(Bp, Lp, D)
    y_sample = xs.reshape(Bs, Ls, D)
    return (y_prompt, y_sample, jnp.stack(new_s5_re, axis=1), jnp.stack(new_s5_im, axis=1),
            jnp.stack(new_ssd, axis=1), jnp.stack(new_rwkv, axis=1))
```

```python
import functools
import math

import jax
import jax.numpy as jnp
from jax import lax
from jax.experimental import pallas as pl
from jax.experimental.pallas import tpu as pltpu

F32 = jnp.float32
BF16 = jnp.bfloat16

D_MODEL = 1024
GRID_W = 64
N_DIR = 2
EPS = 1e-6
S5_WIDTH = 256
S5_CH = 16
S5_GROUPS = 16
S5_STATE = 64
SSD_HEADDIM = 64
SSD_HEADS = 6
SSD_WIDTH = 384
SSD_GROUPS = 2
SSD_STATE = 64
SSD_BC = 128
SSD_CONV_CH = 640
SSD_CHUNK = 128
RWKV_HEADSIZE = 64
RWKV_HEADS = 6
RWKV_WIDTH = 384
W_RANK = 32
A_RANK = 32
G_RANK = 64
DECAY_SCALE = math.exp(-0.5)
GN_EPS = 64e-5
N_EXPERTS = 16
D_EXPERT = 512
EC_FACTOR = 2
IN_SIZES = (S5_WIDTH, SSD_WIDTH, SSD_CONV_CH, N_DIR * SSD_HEADS, 3 * RWKV_WIDTH, N_DIR * W_RANK, A_RANK, G_RANK)

LANES = 128
TOKEN_BLOCK = 256
DT_PAD = 128
LO_PAD = 256
S5_SLICE_CH = 128
S5_SLICE_ST = 512
VMEM_LIMIT = 56 * 1024 * 1024


def _cparams(sem):
    return pltpu.CompilerParams(dimension_semantics=sem, vmem_limit_bytes=VMEM_LIMIT)


def _full(shape):
    nd = len(shape)
    return pl.BlockSpec(shape, lambda *_: (0,) * nd)


def _sigmoid(x):
    return 1.0 / (1.0 + jnp.exp(-x))


def _softplus(x):
    return jnp.maximum(x, 0.0) + jnp.log1p(jnp.exp(-jnp.abs(x)))


def _split_bf16(x):
    hi = x.astype(BF16)
    lo = (x - hi.astype(F32)).astype(BF16)
    return hi, lo


def _dot_exact_rhs(x, m):
    hi, lo = _split_bf16(x)
    mb = m.astype(BF16)
    return jnp.dot(hi, mb, preferred_element_type=F32) + jnp.dot(lo, mb, preferred_element_type=F32)


def _dot_exact_lhs(m, x):
    hi, lo = _split_bf16(x)
    mb = m.astype(BF16)
    return jnp.dot(mb, hi, preferred_element_type=F32) + jnp.dot(mb, lo, preferred_element_type=F32)


def _dot3(x, w, dims=(((1,), (0,)), ((), ()))):
    xh, xl = _split_bf16(x)
    wh, wl = _split_bf16(w)
    dg = functools.partial(lax.dot_general, dimension_numbers=dims, preferred_element_type=F32)
    return dg(xh, wh) + (dg(xl, wh) + dg(xh, wl))


def _gelu_tanh(x):
    return 0.5 * x * (1.0 + jnp.tanh(math.sqrt(2.0 / math.pi) * (x + 0.044715 * (x * x * x))))


def _mod_kernel(c_ref, w_ref, b_ref, o_ref):
    c = c_ref[...]
    s = (c * _sigmoid(c)).astype(BF16)
    o_ref[...] = jnp.dot(s, w_ref[...].astype(BF16), preferred_element_type=F32) + b_ref[...]


def _modulation(cond, ada_w, ada_b):
    depth = ada_w.shape[0]
    rows = cond.shape[0]
    tn = 1536
    return pl.pallas_call(
        _mod_kernel,
        grid=(depth, 6 * D_MODEL // tn),
        in_specs=[
            pl.BlockSpec((rows, D_MODEL), lambda l, j: (0, 0)),
            pl.BlockSpec((None, D_MODEL, tn), lambda l, j: (l, 0, j)),
            pl.BlockSpec((None, 1, tn), lambda l, j: (l, 0, j)),
        ],
        out_specs=pl.BlockSpec((None, rows, tn), lambda l, j: (l, 0, j)),
        out_shape=jax.ShapeDtypeStruct((depth, rows, 6 * D_MODEL), F32),
        compiler_params=_cparams(("parallel", "parallel")),
        name="adaln_mod",
    )(cond, ada_w, ada_b.reshape(depth, 1, 6 * D_MODEL))


IN_PAD_SIZES = (S5_WIDTH, SSD_WIDTH, SSD_CONV_CH, DT_PAD, 3 * RWKV_WIDTH, LO_PAD)


def _inproj_kernel(x_ref, mod_ref, g_ref, w_ref, us5_ref, z_ref, xbc_ref, dt_ref, rkv_ref, lo_ref):
    x = x_ref[...]
    y = x * lax.rsqrt(jnp.mean(x * x, axis=-1, keepdims=True) + EPS) * g_ref[...]
    m = mod_ref[...]
    h = y * (1.0 + m[:, D_MODEL:2 * D_MODEL]) + m[:, 0:D_MODEL]
    p = jnp.dot(h.astype(BF16), w_ref[...], preferred_element_type=F32)
    start = 0
    for ref, size in zip((us5_ref, z_ref, xbc_ref, dt_ref, rkv_ref, lo_ref), IN_PAD_SIZES):
        ref[...] = p[:, start:start + size]
        start += size


def _pad_in_weight(w_in):
    parts, start = [], 0
    for s in IN_SIZES:
        parts.append(w_in[..., start:start + s])
        start += s
    us5, z, xbc, dt, rkv, wlo, alo, glo = parts
    zeros = lambda n: jnp.zeros(w_in.shape[:-1] + (n,), w_in.dtype)
    lo_used = wlo.shape[-1] + alo.shape[-1] + glo.shape[-1]
    return jnp.concatenate([us5, z, xbc, dt, zeros(DT_PAD - dt.shape[-1]), rkv, wlo, alo, glo,
                            zeros(LO_PAD - lo_used)], axis=-1).astype(BF16)


def _in_projection(x, mod3, row_of_block, norm_g, w_pad):
    ntok = x.shape[0]
    tm = TOKEN_BLOCK
    width = w_pad.shape[1]
    outs = tuple(jax.ShapeDtypeStruct((ntok, s), F32) for s in IN_PAD_SIZES)
    return pl.pallas_call(
        _inproj_kernel,
        grid=(ntok // tm,),
        in_specs=[
            pl.BlockSpec((tm, D_MODEL), lambda i: (i, 0)),
            pl.BlockSpec((None, 1, 6 * D_MODEL), lambda i: (row_of_block(i), 0, 0)),
            _full((1, D_MODEL)),
            _full((D_MODEL, width)),
        ],
        out_specs=tuple(pl.BlockSpec((tm, s), lambda i: (i, 0)) for s in IN_PAD_SIZES),
        out_shape=outs,
        compiler_params=_cparams(("parallel",)),
        name="in_proj",
    )(x, mod3, norm_g.reshape(1, D_MODEL), w_pad)


def _s5_disc_kernel(are_ref, aim_ref, ldt_ref, bre_ref, bim_ref, abre_ref, abim_ref, bbre_ref, bbim_ref):
    lam_re = jnp.minimum(are_ref[...], -1e-4)
    lam_im = aim_ref[...]
    dt = jnp.exp(ldt_ref[...])
    mag = jnp.exp(lam_re * dt)
    ab_re = mag * jnp.cos(lam_im * dt)
    ab_im = mag * jnp.sin(lam_im * dt)
    num_re, num_im = ab_re - 1.0, ab_im
    den = lam_re * lam_re + lam_im * lam_im
    q_re = (num_re * lam_re + num_im * lam_im) / den
    q_im = (num_im * lam_re - num_re * lam_im) / den
    abre_ref[...] = ab_re
    abim_ref[...] = ab_im
    b_re = bre_ref[...]
    b_im = bim_ref[...]
    qr = q_re[:, None, :]
    qi = q_im[:, None, :]
    bbre_ref[...] = qr * b_re - qi * b_im
    bbim_ref[...] = qr * b_im + qi * b_re


def _s5_discretize(a_re, a_im, log_dt, b_re, b_im):
    lead = a_re.shape[:3]
    n = lead[0] * lead[1] * lead[2]
    a2 = lambda t: t.reshape(n, S5_STATE)
    ldt = jnp.broadcast_to(log_dt.reshape(n, 1), (n, S5_STATE))
    b3 = lambda t: t.reshape(n, S5_STATE, S5_CH).transpose(0, 2, 1)
    ab_re, ab_im, bb_re, bb_im = pl.pallas_call(
        _s5_disc_kernel,
        out_shape=(jax.ShapeDtypeStruct((n, S5_STATE), F32), jax.ShapeDtypeStruct((n, S5_STATE), F32),
                   jax.ShapeDtypeStruct((n, S5_CH, S5_STATE), F32), jax.ShapeDtypeStruct((n, S5_CH, S5_STATE), F32)),
        name="s5_discretize",
    )(a2(a_re), a2(a_im), ldt, b3(b_re), b3(b_im))
    return (ab_re.reshape(lead + (S5_STATE,)), ab_im.reshape(lead + (S5_STATE,)),
            bb_re.reshape(lead + (S5_CH, S5_STATE)), bb_im.reshape(lead + (S5_CH, S5_STATE)))


def _s5_layer_tables(ab_re, ab_im, bb_re, bb_im, c_re, c_im):
    nfs = S5_GROUPS // 8
    eye = jnp.eye(8, dtype=F32)

    def rows(t):
        return t.reshape(N_DIR, nfs, 8 * S5_STATE)

    ab_row = jnp.concatenate([rows(ab_re), rows(ab_im)], axis=-1).reshape(N_DIR, nfs, 1, 2 * S5_SLICE_ST)

    def bmat(t):
        t = t.reshape(N_DIR, nfs, 8, S5_CH, S5_STATE)
        return jnp.einsum('dfghp,gk->dfghkp', t, eye).reshape(N_DIR, nfs, S5_SLICE_CH, S5_SLICE_ST)

    b_mat = jnp.concatenate([bmat(bb_re), bmat(bb_im)], axis=-1).astype(BF16)

    def cmat(t):
        t = t.reshape(N_DIR, nfs, 8, S5_CH, S5_STATE)
        return jnp.einsum('dfghp,gk->dfgpkh', t, eye).reshape(N_DIR, nfs, S5_SLICE_ST, S5_SLICE_CH)

    c_mat = jnp.concatenate([cmat(c_re), -cmat(c_im)], axis=-2).astype(BF16)
    return ab_row, b_mat, c_mat


def _s5_kernel(*refs, R, n_slab, chained, want_final):
    if chained:
        u_ref, h0_ref, ab_ref, bm_ref, cm_ref, d_ref = refs[:6]
        rest = refs[6:]
    else:
        u_ref, ab_ref, bm_ref, cm_ref, d_ref = refs[:5]
        h0_ref = None
        rest = refs[5:]
    y_ref = rest[0]
    rest = rest[1:]
    if want_final:
        hfin_ref = rest[0]
        rest = rest[1:]
    H = rest[0]
    if chained:
        PW, CIN = rest[1], rest[2]
    NR = R * n_slab
    RC = min(512, NR)
    ST = S5_SLICE_ST
    nchunk = ST // LANES

    y_ref[...] = u_ref[...] * d_ref[...]

    for d in range(N_DIR):
        def slab_of(i, d=d):
            return i if d == 0 else n_slab - 1 - i

        def bu_body(i, _, d=d):
            r0 = pl.multiple_of(i * RC, RC)
            H[pl.ds(r0, RC), :] = jnp.dot(u_ref[pl.ds(r0, RC), :].astype(BF16), bm_ref[d],
                                           preferred_element_type=F32)
            return 0
        lax.fori_loop(0, NR // RC, bu_body, 0)

        for c in range(nchunk):
            lre = slice(c * LANES, (c + 1) * LANES)
            lim = slice(ST + c * LANES, ST + (c + 1) * LANES)
            a_re = jnp.broadcast_to(ab_ref[d, :, lre], (R, LANES))
            a_im = jnp.broadcast_to(ab_ref[d, :, lim], (R, LANES))

            def step(i, carry, lre=lre, lim=lim, a_re=a_re, a_im=a_im, slab_of=slab_of):
                cr, ci = carry
                r0 = pl.multiple_of(slab_of(i) * R, R)
                nr = a_re * cr - a_im * ci + H[pl.ds(r0, R), lre]
                ni = a_re * ci + a_im * cr + H[pl.ds(r0, R), lim]
                H[pl.ds(r0, R), lre] = nr
                H[pl.ds(r0, R), lim] = ni
                return nr, ni
            zero = jnp.zeros((R, LANES), F32)
            lax.fori_loop(0, n_slab, step, (zero, zero))

        last0 = (n_slab - 1) * R if d == 0 else 0
        if chained:
            a_re_row = ab_ref[d, :, 0:ST]
            a_im_row = ab_ref[d, :, ST:2 * ST]

            def pw_step(j, carry, a_re_row=a_re_row, a_im_row=a_im_row):
                pr, pi = carry
                PW[j, :, 0:ST] = jnp.broadcast_to(pr, (8, ST))
                PW[j, :, ST:2 * ST] = jnp.broadcast_to(pi, (8, ST))
                return pr * a_re_row - pi * a_im_row, pr * a_im_row + pi * a_re_row
            lax.fori_loop(0, n_slab, pw_step, (a_re_row, a_im_row))
            t_re = PW[n_slab - 1, 0:1, 0:ST]
            t_im = PW[n_slab - 1, 0:1, ST:2 * ST]

            cr = h0_ref[d, :, 0:ST]
            ci = h0_ref[d, :, ST:2 * ST]
            for i in range(R):
                c = i if d == 0 else R - 1 - i
                CIN[c:c + 1, 0:ST] = cr
                CIN[c:c + 1, ST:2 * ST] = ci
                er = H[last0 + c:last0 + c + 1, 0:ST]
                ei = H[last0 + c:last0 + c + 1, ST:2 * ST]
                cr, ci = t_re * cr - t_im * ci + er, t_re * ci + t_im * cr + ei

            def fix_step(i, _, slab_of=slab_of):
                r0 = pl.multiple_of(slab_of(i) * R, R)
                p = PW[i]
                for c in range(nchunk):
                    lre = slice(c * LANES, (c + 1) * LANES)
                    lim = slice(ST + c * LANES, ST + (c + 1) * LANES)
                    pr = p[0:1, lre]
                    pi = p[0:1, lim]
                    cr = CIN[:, lre]
                    ci = CIN[:, lim]
                    H[pl.ds(r0, R), lre] = H[pl.ds(r0, R), lre] + (pr * cr - pi * ci)
                    H[pl.ds(r0, R), lim] = H[pl.ds(r0, R), lim] + (pr * ci + pi * cr)
                return 0
            lax.fori_loop(0, n_slab, fix_step, 0)

        if want_final:
            hfin_ref[d] = H[last0:last0 + R, :]

        def y_body(i, _, d=d):
            r0 = pl.multiple_of(i * RC, RC)
            y_ref[pl.ds(r0, RC), :] = y_ref[pl.ds(r0, RC), :] + jnp.dot(
                H[pl.ds(r0, RC), :].astype(BF16), cm_ref[d], preferred_element_type=F32)
            return 0
        lax.fori_loop(0, NR // RC, y_body, 0)


def _s5_scan(u, h0, tables, d_row, *, R, n_slab, chained, want_final):
    ab_row, b_mat, c_mat = tables
    Q, NR, _ = u.shape
    nfs = S5_GROUPS // 8
    W2 = 2 * S5_SLICE_ST
    in_specs = [pl.BlockSpec((None, NR, S5_SLICE_CH), lambda q, f: (q, 0, f))]
    args = [u]
    if chained:
        in_specs.append(pl.BlockSpec((None, N_DIR, None, 1, W2), lambda q, f: (q, 0, f, 0, 0)))
        args.append(h0)
    in_specs += [
        pl.BlockSpec((N_DIR, None, 1, W2), lambda q, f: (0, f, 0, 0)),
        pl.BlockSpec((N_DIR, None, S5_SLICE_CH, W2), lambda q, f: (0, f, 0, 0)),
        pl.BlockSpec((N_DIR, None, W2, S5_SLICE_CH), lambda q, f: (0, f, 0, 0)),
        pl.BlockSpec((1, S5_SLICE_CH), lambda q, f: (0, f)),
    ]
    args += [ab_row, b_mat, c_mat, d_row]
    out_shape = [jax.ShapeDtypeStruct((Q, NR, S5_WIDTH), F32)]
    out_specs = [pl.BlockSpec((None, NR, S5_SLICE_CH), lambda q, f: (q, 0, f))]
    if want_final:
        out_shape.append(jax.ShapeDtypeStruct((Q, N_DIR, nfs, R, W2), F32))
        out_specs.append(pl.BlockSpec((None, N_DIR, None, R, W2), lambda q, f: (q, 0, f, 0, 0)))
    scratch = [pltpu.VMEM((NR, W2), F32)]
    if chained:
        scratch += [pltpu.VMEM((n_slab, 8, W2), F32), pltpu.VMEM((R, W2), F32)]
    res = pl.pallas_call(
        functools.partial(_s5_kernel, R=R, n_slab=n_slab, chained=chained, want_final=want_final),
        grid=(Q, nfs),
        in_specs=in_specs,
        out_specs=tuple(out_specs),
        out_shape=tuple(out_shape),
        scratch_shapes=scratch,
        compiler_params=_cparams(("parallel", "parallel")),
        name="s5_scan_chained" if chained else "s5_scan",
    )(*args)
    return res


def _ssd_kernel(*refs, TB, nb, has_h0, want_final):
    io = [refs[4 * d:4 * d + 4] for d in range(N_DIR)]
    refs = refs[8:]
    if has_h0:
        h0_ref = refs[0]
        refs = refs[1:]
    cw_ref, cb_ref, dtb_ref, arow_ref, sel_ref, drow_ref = refs[:6]
    refs = refs[6:]
    y_refs = refs[:2]
    refs = refs[2:]
    if want_final:
        hfin_ref = refs[0]
        refs = refs[1:]
    hst, xc_s, dt_s = refs
    CH = SSD_CHUNK
    P = SSD_HEADDIM
    j = pl.program_id(1)

    @pl.when(j == 0)
    def _():
        if has_h0:
            hst[...] = h0_ref[...]
        else:
            hst[...] = jnp.zeros_like(hst)

    li = lax.broadcasted_iota(jnp.int32, (CH, CH), 0)
    si = lax.broadcasted_iota(jnp.int32, (CH, CH), 1)
    tmats = ((si <= li).astype(F32), (si >= li).astype(F32))
    rows = lax.broadcasted_iota(jnp.int32, (TB, 1), 0)
    for d in range(N_DIR):
        xbc_ref, xp_ref, xn_ref, dt_ref = io[d]
        jj = j if d == 0 else nb - 1 - j
        x = xbc_ref[...]
        prev_row = xp_ref[7:8, :] * (jj > 0).astype(F32)
        next_row = xn_ref[0:1, :] * (jj < nb - 1).astype(F32)
        x_prev = jnp.where(rows == 0, prev_row, pltpu.roll(x, 1, 0))
        x_next = jnp.where(rows == TB - 1, next_row, pltpu.roll(x, TB - 1, 0))
        conv = cw_ref[0:1, :] * x_prev + cw_ref[1:2, :] * x + cw_ref[2:3, :] * x_next + cb_ref[...]
        xc_s[d] = conv * _sigmoid(conv)
        dtf = _softplus(dt_ref[...] + dtb_ref[...])
        dt_s[d] = _dot_exact_rhs(dtf, sel_ref[d])

    hrow = lax.broadcasted_iota(jnp.int32, (LANES, SSD_WIDTH), 0)
    e_head = jnp.where(lax.broadcasted_iota(jnp.int32, (LANES, SSD_WIDTH), 1) // P == hrow, 1.0, 0.0)
    hrow2 = lax.broadcasted_iota(jnp.int32, (LANES, SSD_HEADS * CH), 0)
    e_chunk = jnp.where(lax.broadcasted_iota(jnp.int32, (LANES, SSD_HEADS * CH), 1) // CH == hrow2, 1.0, 0.0)

    n_ch = TB // CH
    for i, d in [(i, d) for i in range(n_ch) for d in range(N_DIR)]:
        tmat = tmats[d]
        y_ref = y_refs[d]
        r0 = (i if d == 0 else n_ch - 1 - i) * CH
        dtc = dt_s[d, r0:r0 + CH, :]
        dA = dtc * arow_ref[d]
        cs = _dot_exact_lhs(tmat, dA)
        csT = cs.T
        dt_x = _dot_exact_rhs(dtc, e_head)
        cs_x = _dot_exact_rhs(cs, e_head)
        tot_x = jnp.sum(_dot_exact_rhs(dA, e_head), axis=0, keepdims=True)
        cs_xx = _dot_exact_rhs(cs, e_chunk)
        xs_all = xc_s[d, r0:r0 + CH, 0:SSD_WIDTH]
        xdt_all = xs_all * dt_x
        xd_all = xdt_all * jnp.exp(tot_x - cs_x)
        ecs_all = jnp.exp(cs_x)
        etot_all = jnp.exp(tot_x)
        Bm = xc_s[d, r0:r0 + CH, SSD_WIDTH:SSD_WIDTH + SSD_BC]
        Cm = xc_s[d, r0:r0 + CH, SSD_WIDTH + SSD_BC:SSD_WIDTH + 2 * SSD_BC]
        BmT = Bm.T
        for g in range(SSD_GROUPS):
            Cg = Cm[:, g * SSD_STATE:(g + 1) * SSD_STATE].astype(BF16)
            Bg = Bm[:, g * SSD_STATE:(g + 1) * SSD_STATE].astype(BF16)
            BgT = BmT[g * SSD_STATE:(g + 1) * SSD_STATE, :].astype(BF16)
            G = lax.dot_general(Cg, Bg, (((1,), (1,)), ((), ())), preferred_element_type=F32)
            for hh in range(SSD_HEADS // SSD_GROUPS):
                h = g * (SSD_HEADS // SSD_GROUPS) + hh
                hl = slice(h * P, (h + 1) * P)
                row = csT[h:h + 1, :]
                lm = jnp.exp(jnp.where(tmat > 0.0, cs_xx[:, h * CH:(h + 1) * CH] - row, -1e30))
                hprev = hst[d, h]
                y = jnp.dot((G * lm).astype(BF16), xdt_all[:, hl].astype(BF16), preferred_element_type=F32)
                y = y + jnp.dot(Cg, hprev.astype(BF16), preferred_element_type=F32) * ecs_all[:, hl]
                if d == 0:
                    y = y + drow_ref[:, hl] * xs_all[:, hl]
                y_ref[r0:r0 + CH, hl] = y
                hst[d, h] = etot_all[:, hl] * hprev + jnp.dot(BgT, xd_all[:, hl].astype(BF16),
                                                             preferred_element_type=F32)

    if want_final:
        @pl.when(j == nb - 1)
        def _():
            hfin_ref[...] = hst[...]


def _ssd_scan(xbc, dt, h0, conv_w, conv_b, dt_bias, a_log, d_skip, *, B, L, want_final):
    TB = min(512, L)
    nb = L // TB
    H = SSD_HEADS

    nrow8 = B * L // 8
    in_specs, args = [], []
    blks = (lambda b, j: b * nb + j, lambda b, j: b * nb + nb - 1 - j)
    for blk in blks:
        in_specs += [
            pl.BlockSpec((TB, SSD_CONV_CH), lambda b, j, blk=blk: (blk(b, j), 0)),
            pl.BlockSpec((8, SSD_CONV_CH), lambda b, j, blk=blk: (jnp.maximum(blk(b, j) * (TB // 8) - 1, 0), 0)),
            pl.BlockSpec((8, SSD_CONV_CH),
                         lambda b, j, blk=blk: (jnp.minimum((blk(b, j) + 1) * (TB // 8), nrow8 - 1), 0)),
            pl.BlockSpec((TB, DT_PAD), lambda b, j, blk=blk: (blk(b, j), 0)),
        ]
        args += [xbc, xbc, xbc, dt]
    if h0 is not None:
        in_specs.append(pl.BlockSpec((None, N_DIR, H, SSD_STATE, SSD_HEADDIM), lambda b, j: (b, 0, 0, 0, 0)))
        args.append(h0)
    dtb = jnp.pad(dt_bias.reshape(1, N_DIR * H), ((0, 0), (0, DT_PAD - N_DIR * H)))
    arow = jnp.pad(-jnp.exp(a_log), ((0, 0), (0, LANES - H))).reshape(N_DIR, 1, LANES)
    lane = jnp.arange(LANES)
    sel = jnp.stack([(lane[:, None] == (dd * H + lane[None, :])) & (lane[None, :] < H) for dd in range(N_DIR)]).astype(F32)
    drow = jnp.repeat(d_skip, SSD_HEADDIM).reshape(1, SSD_WIDTH)
    in_specs += [
        _full((3, SSD_CONV_CH)), _full((1, SSD_CONV_CH)), _full((1, DT_PAD)),
        _full((N_DIR, 1, LANES)), _full((N_DIR, LANES, LANES)), _full((1, SSD_WIDTH)),
    ]
    args += [conv_w, conv_b.reshape(1, SSD_CONV_CH), dtb, arow, sel, drow]
    out_shape = [jax.ShapeDtypeStruct((B * L, SSD_WIDTH), F32)] * N_DIR
    out_specs = [pl.BlockSpec((TB, SSD_WIDTH), lambda b, j, blk=blk: (blk(b, j), 0)) for blk in blks]
    if want_final:
        out_shape.append(jax.ShapeDtypeStruct((B, N_DIR, H, SSD_STATE, SSD_HEADDIM), F32))
        out_specs.append(pl.BlockSpec((None, N_DIR, H, SSD_STATE, SSD_HEADDIM), lambda b, j: (b, 0, 0, 0, 0)))
    return pl.pallas_call(
        functools.partial(_ssd_kernel, TB=TB, nb=nb, has_h0=h0 is not None, want_final=want_final),
        grid=(B, nb),
        in_specs=in_specs,
        out_specs=tuple(out_specs),
        out_shape=tuple(out_shape),
        scratch_shapes=[pltpu.VMEM((N_DIR, H, SSD_STATE, SSD_HEADDIM), F32),
                        pltpu.VMEM((N_DIR, TB, SSD_CONV_CH), F32), pltpu.VMEM((N_DIR, TB, DT_PAD), F32)],
        compiler_params=_cparams(("parallel", "arbitrary")),
        name="ssd_scan",
    )(*args)


def _rwkv_prep_kernel(rkv_ref, rp_ref, rn_ref, lo_ref, mu_ref, a0_ref, aup_ref, gup_ref, w0_ref, wup_ref,
                      kkw_ref, ka_ref, rk_ref, seg_ref,
                      r_ref, w0o_ref, w1o_ref, k_ref, v_ref, kk_ref, b_ref, g_ref, bonus_ref, *, nbs):
    tm = rkv_ref.shape[0]
    W = RWKV_WIDTH
    jj = pl.program_id(0) % nbs
    x = rkv_ref[...]
    prev_row = rp_ref[7:8, :] * (jj > 0).astype(F32)
    next_row = rn_ref[0:1, :] * (jj < nbs - 1).astype(F32)
    rows = lax.broadcasted_iota(jnp.int32, (tm, 1), 0)
    xp = jnp.where(rows == 0, prev_row, pltpu.roll(x, 1, 0))
    xn = jnp.where(rows == tm - 1, next_row, pltpu.roll(x, tm - 1, 0))
    x = x + mu_ref[0:1, :] * (xp - x) + mu_ref[1:2, :] * (xn - x)
    r = x[:, 0:W]
    k = x[:, W:2 * W]
    v = x[:, 2 * W:3 * W]
    lo = lo_ref[...]
    seg = seg_ref[...]

    def put(ref, val):
        ref[:, 0:W] = val
        if ref.shape[1] > W:
            ref[:, W:] = jnp.zeros((tm, ref.shape[1] - W), F32)

    a = _sigmoid(a0_ref[...] + _dot3(lo, aup_ref[...]))
    g_ref[...] = _dot3(_sigmoid(lo), gup_ref[...])
    tlo = jnp.tanh(lo)
    for d, o_ref in enumerate((w0o_ref, w1o_ref)):
        zw = w0_ref[d:d + 1, :] + _dot3(tlo, wup_ref[d])
        put(o_ref, jnp.exp(-DECAY_SCALE * _sigmoid(zw)))
    kk = k * kkw_ref[...]
    kk = kk * lax.rsqrt(jnp.maximum(_dot_exact_rhs(kk * kk, seg), 1e-24))
    k2 = k * (1.0 + (a - 1.0) * ka_ref[...])
    put(r_ref, r)
    put(k_ref, k2)
    put(v_ref, v)
    put(kk_ref, kk)
    put(b_ref, kk * a)
    bonus_ref[...] = _dot_exact_rhs(r * k2 * rk_ref[...], seg) * v


def _segment_ones(width, seg):
    i = jnp.arange(width)
    return (i[:, None] // seg == i[None, :] // seg).astype(F32)


def _rwkv_prep(rkv, lo, lp, seq_blocks, scan_width=RWKV_WIDTH):
    ntok = rkv.shape[0]
    tm = TOKEN_BLOCK
    W = RWKV_WIDTH
    nrow8 = ntok // 8
    nbs_of = seq_blocks
    widths = (scan_width,) * 7 + (W, W)
    pad_rows = lambda t, r0: jnp.zeros((LO_PAD, W), F32).at[r0:r0 + t.shape[0]].set(t)
    aup = pad_rows(lp['rwkv_a_up'], N_DIR * W_RANK)
    gup = pad_rows(lp['rwkv_g_up'], N_DIR * W_RANK + A_RANK)
    wup = jnp.stack([pad_rows(lp['rwkv_w_up'][d], d * W_RANK) for d in range(N_DIR)])
    row = lambda t: t.reshape(1, W)
    outs = tuple(jax.ShapeDtypeStruct((ntok, w), F32) for w in widths)
    return pl.pallas_call(
        functools.partial(_rwkv_prep_kernel, nbs=nbs_of),
        grid=(ntok // tm,),
        in_specs=[
            pl.BlockSpec((tm, 3 * W), lambda i: (i, 0)),
            pl.BlockSpec((8, 3 * W), lambda i: (jnp.maximum(i * (tm // 8) - 1, 0), 0)),
            pl.BlockSpec((8, 3 * W), lambda i: (jnp.minimum((i + 1) * (tm // 8), nrow8 - 1), 0)),
            pl.BlockSpec((tm, LO_PAD), lambda i: (i, 0)),
            _full((2, 3 * W)), _full((1, W)), _full((LO_PAD, W)), _full((LO_PAD, W)), _full((2, W)),
            _full((N_DIR, LO_PAD, W)), _full((1, W)), _full((1, W)), _full((1, W)), _full((W, W)),
        ],
        out_specs=tuple(pl.BlockSpec((tm, w), lambda i: (i, 0)) for w in widths),
        out_shape=outs,
        compiler_params=_cparams(("parallel",)),
        name="rwkv_prep",
    )(rkv, rkv, rkv, lo, lp['rwkv_mu'], row(lp['rwkv_a0']), aup, gup, lp['rwkv_w0'], wup,
      row(lp['rwkv_k_k']), row(lp['rwkv_k_a']), row(lp['rwkv_r_k']), lp['seg'])


def _rwkv_scan_kernel(rf, rm, wf, wm, kf, km, kkf, kkm, bf, bm, vf, vm, mask_ref, s0_ref, yf_ref, yb_ref, sfin_ref,
                      S, SA, OPS, VTS, *, TBLK, V, nblk):
    K = RWKV_HEADSIZE
    VH = V // 2
    tb = pl.program_id(1)

    @pl.when(tb == 0)
    def _():
        S[...] = s0_ref[...]

    is_fwd = mask_ref[...] > 0.5
    fwd_k = jnp.broadcast_to(is_fwd, (K, LANES))
    fwd_v = jnp.broadcast_to(is_fwd, (V, LANES))

    def pick(t, _):
        tm_ = TBLK - 1 - t
        for j, (f, m) in enumerate(((rf, rm), (wf, wm), (kf, km), (kkf, kkm), (bf, bm))):
            OPS[t, j] = jnp.where(fwd_k, f[t], m[tm_])
        VTS[t] = jnp.where(fwd_v, vf[t], vm[tm_])
        return 0
    lax.fori_loop(0, TBLK, pick, 0)
    R_, W_, K_, KK_, B_ = range(5)

    def bc(j, t, kx):
        return jnp.broadcast_to(OPS[t, j, pl.ds(kx, 1), :], (VH, LANES))

    def tree(parts):
        return (parts[0] + parts[1]) + (parts[2] + parts[3])

    for hv in range(2):
        rows = pl.ds(hv * VH, VH)
        accs = [None] * 4
        for kx in range(K):
            p = S[kx, rows, :] * bc(KK_, 0, kx)
            accs[kx % 4] = p if accs[kx % 4] is None else accs[kx % 4] + p
        SA[hv] = tree(accs)

    def step(t, _):
        tn = jnp.minimum(t + 1, TBLK - 1)
        tm_ = TBLK - 1 - t
        for hv in range(2):
            rows = pl.ds(hv * VH, VH)
            sa = SA[hv]
            vt = VTS[t, rows, :]
            yacc = [None] * 4
            sacc = [None] * 4
            for kx in range(K):
                s_new = S[kx, rows, :] * bc(W_, t, kx) - sa * bc(B_, t, kx) + vt * bc(K_, t, kx)
                S[kx, rows, :] = s_new
                py = s_new * bc(R_, t, kx)
                ps = s_new * bc(KK_, tn, kx)
                a = kx % 4
                yacc[a] = py if yacc[a] is None else yacc[a] + py
                sacc[a] = ps if sacc[a] is None else sacc[a] + ps
            y = tree(yacc)
            yf_ref[t, rows, :] = y
            yb_ref[tm_, rows, :] = y
            SA[hv] = tree(sacc)
        return 0
    lax.fori_loop(0, TBLK, step, 0)

    @pl.when(tb == nblk - 1)
    def _():
        sfin_ref[...] = S[...]


def _rwkv_scan(r, w, k, kk, b, v, fwd_mask, s0):
    L, K, NL = r.shape
    V = v.shape[1]
    TBLK = min(32, L)
    nblk = L // TBLK
    ngrp = NL // LANES
    kf = pl.BlockSpec((TBLK, K, LANES), lambda g, t: (t, 0, g))
    km = pl.BlockSpec((TBLK, K, LANES), lambda g, t: (nblk - 1 - t, 0, g))
    vf = pl.BlockSpec((TBLK, V, LANES), lambda g, t: (t, 0, g))
    vm = pl.BlockSpec((TBLK, V, LANES), lambda g, t: (nblk - 1 - t, 0, g))
    sspec = pl.BlockSpec((K, V, LANES), lambda g, t: (0, 0, g))
    return pl.pallas_call(
        functools.partial(_rwkv_scan_kernel, TBLK=TBLK, V=V, nblk=nblk),
        grid=(ngrp, nblk),
        in_specs=[kf, km] * 5 + [vf, vm, pl.BlockSpec((1, LANES), lambda g, t: (0, g)), sspec],
        out_specs=(vf, vm, sspec),
        out_shape=(jax.ShapeDtypeStruct((L, V, NL), F32), jax.ShapeDtypeStruct((L, V, NL), F32),
                   jax.ShapeDtypeStruct((K, V, NL), F32)),
        scratch_shapes=[pltpu.VMEM((K, V, LANES), F32), pltpu.VMEM((2, V // 2, LANES), F32),
                        pltpu.VMEM((TBLK, 5, K, LANES), F32), pltpu.VMEM((TBLK, V, LANES), F32)],
        compiler_params=_cparams(("parallel", "arbitrary")),
        name="rwkv_scan",
    )(r, r, w, w, k, k, kk, kk, b, b, v, v, fwd_mask, s0)


KQ = 4
CH = LANES // KQ


def _rwkv_scan2_kernel(rf, rm, wf, wm, kf, km, kkf, kkm, bf, bm, vf, vm, s0_ref, yf_ref, yb_ref, S, YQ, SA, *, TBLK):
    V = RWKV_HEADSIZE
    NQ = RWKV_HEADSIZE // KQ
    tb = pl.program_id(0)

    @pl.when(tb == 0)
    def _():
        S[...] = s0_ref[...]

    VH = V // 2

    def bc(ref, tt, q):
        return jnp.broadcast_to(ref[tt, pl.ds(q, 1), :], (VH, LANES))

    def all_quarters(x):
        return (x + pltpu.roll(x, CH, 1)) + (pltpu.roll(x, 2 * CH, 1) + pltpu.roll(x, 3 * CH, 1))

    dirs = ((rf, wf, kf, kkf, bf, vf), (rm, wm, km, kkm, bm, vm))
    time_of = (lambda t: t, lambda t: TBLK - 1 - t)
    chains = [(d, hv) for d in range(N_DIR) for hv in range(2)]

    def first_sa(d, hv):
        kk_ = dirs[d][3]
        tt = time_of[d](0)
        acc = [None, None]
        for q in range(NQ):
            p = S[d, q, pl.ds(hv * VH, VH), :] * bc(kk_, tt, q)
            acc[q % 2] = p if acc[q % 2] is None else acc[q % 2] + p
        return acc[0] + acc[1]

    TG = min(16, TBLK)
    lane_r = lax.broadcasted_iota(jnp.int32, (2 * LANES, CH), 0) % CH
    fold2 = jnp.where(lax.broadcasted_iota(jnp.int32, (2 * LANES, CH), 1) == lane_r, 1.0, 0.0).astype(BF16)

    last = len(chains) - 1
    for c, (d, hv) in enumerate(chains):
        part = first_sa(d, hv)
        SA[c] = part if c == last else all_quarters(part)

    def step(t, _):
        sa_last = all_quarters(SA[last])
        for c, (d, hv) in enumerate(chains):
            r_, w_, k_, kk_, b_, v_ = dirs[d]
            tt = time_of[d](t)
            tn = time_of[d](jnp.minimum(t + 1, TBLK - 1))
            rows = pl.ds(hv * VH, VH)
            sa = sa_last if c == last else SA[c]
            vt = v_[tt, rows, :]
            yacc = None
            sacc = None
            for q in range(NQ):
                s_new = S[d, q, rows, :] * bc(w_, tt, q) - sa * bc(b_, tt, q) + vt * bc(k_, tt, q)
                S[d, q, rows, :] = s_new
                py = s_new * bc(r_, tt, q)
                ps = s_new * bc(kk_, tn, q)
                yacc = py if yacc is None else yacc + py
                sacc = ps if sacc is None else sacc + ps
            YQ[d, tt, rows, :] = yacc
            SA[c] = sacc if c == last else all_quarters(sacc)
        return 0
    lax.fori_loop(0, TBLK, step, 0)

    def finish(i, _):
        for d, y_ in enumerate((yf_ref, yb_ref)):
            hi, lo = _split_bf16(YQ[d, pl.ds(i * TG, TG)].reshape(TG * V, LANES))
            y = jnp.dot(jnp.concatenate([hi, lo], axis=-1), fold2, preferred_element_type=F32)
            y_[pl.ds(i * TG, TG)] = y.reshape(TG, V, CH)
        return 0
    lax.fori_loop(0, TBLK // TG, finish, 0)


def _rwkv_scan2(r, w0, w1, k, kk, b, v, s0):
    L, NQ, _ = r.shape
    V = RWKV_HEADSIZE
    TBLK = min(64, L)
    nblk = L // TBLK
    fspec = pl.BlockSpec((TBLK, NQ, LANES), lambda t: (t, 0, 0))
    mspec = pl.BlockSpec((TBLK, NQ, LANES), lambda t: (nblk - 1 - t, 0, 0))
    vfspec = pl.BlockSpec((TBLK, V, LANES), lambda t: (t, 0, 0))
    vmspec = pl.BlockSpec((TBLK, V, LANES), lambda t: (nblk - 1 - t, 0, 0))
    yfspec = pl.BlockSpec((TBLK, V, CH), lambda t: (t, 0, 0))
    ymspec = pl.BlockSpec((TBLK, V, CH), lambda t: (nblk - 1 - t, 0, 0))
    yf, yb = pl.pallas_call(
        functools.partial(_rwkv_scan2_kernel, TBLK=TBLK),
        grid=(nblk,),
        in_specs=[fspec, mspec] * 5 + [vfspec, vmspec, _full((N_DIR, NQ, V, LANES))],
        out_specs=(yfspec, ymspec),
        out_shape=(jax.ShapeDtypeStruct((L, V, CH), F32), jax.ShapeDtypeStruct((L, V, CH), F32)),
        scratch_shapes=[pltpu.VMEM((N_DIR, NQ, V, LANES), F32), pltpu.VMEM((N_DIR, TBLK, V, LANES), F32),
                        pltpu.VMEM((2 * N_DIR, V // 2, LANES), F32)],
        compiler_params=_cparams(("arbitrary",)),
        name="rwkv_scan2",
    )(r, r, w0, w1, k, k, kk, kk, b, b, v, v, s0)
    return yf, yb


HEADS_PAD = 8
SCAN_WIDTH = HEADS_PAD * RWKV_HEADSIZE


def _pad_chains(x, B):
    return x if B * HEADS_PAD == CH else jnp.pad(x, ((0, 0),) * (x.ndim - 1) + ((0, CH - B * HEADS_PAD),))


def _to_scan2_k(t, B, L):
    x = t.reshape(B, L, HEADS_PAD, RWKV_HEADSIZE).transpose(1, 3, 0, 2)
    x = _pad_chains(x.reshape(L, RWKV_HEADSIZE // KQ, KQ, B * HEADS_PAD), B)
    return x.reshape(L, RWKV_HEADSIZE // KQ, LANES)


def _to_scan2_v(t, B, L):
    x = t.reshape(B, L, HEADS_PAD, RWKV_HEADSIZE).transpose(1, 3, 0, 2).reshape(L, RWKV_HEADSIZE, B * HEADS_PAD)
    return jnp.tile(_pad_chains(x, B), (1, 1, KQ))


def _from_scan2(y, B, L):
    y = y[:, :, :B * HEADS_PAD]
    return y.reshape(L, RWKV_HEADSIZE, B, HEADS_PAD).transpose(2, 0, 3, 1).reshape(B * L, SCAN_WIDTH)


def _state_to_scan2(s):
    B = s.shape[0]
    s = jnp.pad(s, ((0, 0), (0, 0), (0, HEADS_PAD - RWKV_HEADS), (0, 0), (0, 0)))
    x = s.transpose(1, 4, 3, 0, 2).reshape(N_DIR, RWKV_HEADSIZE // KQ, KQ, RWKV_HEADSIZE, B * HEADS_PAD)
    x = _pad_chains(x, B)
    return x.transpose(0, 1, 3, 2, 4).reshape(N_DIR, RWKV_HEADSIZE // KQ, RWKV_HEADSIZE, LANES)


def _ctx_lanes(B):
    chains = B * N_DIR * RWKV_HEADS
    return chains, -(-chains // LANES) * LANES


def _to_ctx(t_fwd, t_bwd, B, L):
    chains, lanes = _ctx_lanes(B)
    if t_bwd is t_fwd:
        x = t_fwd.reshape(B, L, RWKV_HEADS, RWKV_HEADSIZE).transpose(1, 3, 0, 2)
        x = jnp.broadcast_to(x[:, :, :, None, :], (L, RWKV_HEADSIZE, B, N_DIR, RWKV_HEADS))
    else:
        x = jnp.stack([t_fwd, t_bwd], 0).reshape(N_DIR, B, L, RWKV_HEADS, RWKV_HEADSIZE).transpose(2, 4, 1, 0, 3)
    x = x.reshape(L, RWKV_HEADSIZE, chains)
    return x if lanes == chains else jnp.pad(x, ((0, 0), (0, 0), (0, lanes - chains)))


def _ctx_fwd_mask(B):
    chains, lanes = _ctx_lanes(B)
    lane = jnp.arange(lanes)
    return (((lane // RWKV_HEADS) % N_DIR == 0) & (lane < chains)).astype(F32).reshape(1, lanes)


def _from_ctx(y, B, L, d):
    chains, _ = _ctx_lanes(B)
    y = y[:, :, :chains].reshape(L, RWKV_HEADSIZE, B, N_DIR, RWKV_HEADS)[:, :, :, d]
    return y.transpose(2, 0, 3, 1).reshape(B * L, RWKV_WIDTH)


def _ctx_state_from(x, B):
    chains, _ = _ctx_lanes(B)
    x = x[:, :, :chains].reshape(RWKV_HEADSIZE, RWKV_HEADSIZE, B, N_DIR, RWKV_HEADS)
    return x.transpose(2, 3, 4, 1, 0)


def _post_kernel(x_ref, mod_ref, ys5_ref, yf_ref, yb_ref, z_ref, rf_ref, rb_ref, bonus_ref, g_ref,
                 gluw_ref, glub_ref, ssdg_ref, lng_ref, lnb_ref, seg_ref, wout_ref, n2g_ref, rw_ref,
                 x1_ref, hb_ref, aff_ref):
    m = mod_ref[...]
    D = D_MODEL
    zg = _gelu_tanh(ys5_ref[...])
    gate = jnp.dot(zg.astype(BF16), gluw_ref[...], preferred_element_type=F32) + glub_ref[...]
    y_a = zg * _sigmoid(gate)
    z = z_ref[...]
    yb = (yf_ref[...] + yb_ref[...]) * (z * _sigmoid(z))
    y_b = yb * lax.rsqrt(jnp.mean(yb * yb, axis=-1, keepdims=True) + EPS) * ssdg_ref[...]
    seg = seg_ref[...] * (1.0 / RWKV_HEADSIZE)
    yr = rf_ref[...] + rb_ref[...]
    mean = _dot_exact_rhs(yr, seg)
    cen = yr - mean
    var = jnp.dot((cen * cen).astype(BF16), seg.astype(BF16), preferred_element_type=F32)
    yn = cen * lax.rsqrt(var + GN_EPS) * lng_ref[...] + lnb_ref[...]
    y_c = (yn + bonus_ref[...]) * g_ref[...]
    o = jnp.dot(y_a.astype(BF16), wout_ref[0:S5_WIDTH, :], preferred_element_type=F32)
    o = o + jnp.dot(y_b.astype(BF16), wout_ref[S5_WIDTH:S5_WIDTH + SSD_WIDTH, :], preferred_element_type=F32)
    o = o + jnp.dot(y_c.astype(BF16), wout_ref[S5_WIDTH + SSD_WIDTH:, :], preferred_element_type=F32)
    x1 = x_ref[...] + m[:, 2 * D:3 * D] * o
    x1_ref[...] = x1
    h2 = x1 * lax.rsqrt(jnp.mean(x1 * x1, axis=-1, keepdims=True) + EPS) * n2g_ref[...]
    h2 = h2 * (1.0 + m[:, 4 * D:5 * D]) + m[:, 3 * D:4 * D]
    hb_ref[...] = h2.astype(BF16)
    logits = _dot3(rw_ref[...], h2, (((1,), (1,)), ((), ())))
    mx = jnp.max(logits, axis=0, keepdims=True)
    ex = jnp.exp(logits - mx)
    aff_ref[...] = ex / jnp.sum(ex, axis=0, keepdims=True)


def _post_mixer(x, mod3, row_of_block, ys5, yssd, z, rf, rb, bonus, g, lp):
    ntok = x.shape[0]
    tm = TOKEN_BLOCK
    tok = lambda w: pl.BlockSpec((tm, w), lambda i: (i, 0))
    row = lambda t: t.reshape(1, -1)
    W = RWKV_WIDTH
    return pl.pallas_call(
        _post_kernel,
        grid=(ntok // tm,),
        in_specs=[
            tok(D_MODEL),
            pl.BlockSpec((None, 1, 6 * D_MODEL), lambda i: (row_of_block(i), 0, 0)),
            tok(S5_WIDTH),
            tok(SSD_WIDTH), tok(SSD_WIDTH),
            tok(SSD_WIDTH), tok(W), tok(W), tok(W), tok(W),
            _full((S5_WIDTH, S5_WIDTH)), _full((1, S5_WIDTH)), _full((1, SSD_WIDTH)), _full((1, W)), _full((1, W)),
            _full((W, W)), _full((D_MODEL, D_MODEL)), _full((1, D_MODEL)), _full((N_EXPERTS, D_MODEL)),
        ],
        out_specs=(tok(D_MODEL), tok(D_MODEL), pl.BlockSpec((N_EXPERTS, tm), lambda i: (0, i))),
        out_shape=(jax.ShapeDtypeStruct((ntok, D_MODEL), F32), jax.ShapeDtypeStruct((ntok, D_MODEL), BF16),
                   jax.ShapeDtypeStruct((N_EXPERTS, ntok), F32)),
        compiler_params=_cparams(("parallel",)),
        name="post_mixer",
    )(x, mod3, ys5, yssd[0], yssd[1], z, rf, rb, bonus, g,
      lp['s5_glu_w'], row(lp['s5_glu_b']), row(lp['ssd_norm_g']), row(lp['rwkv_ln_g']),
      row(lp['rwkv_ln_b']), lp['seg'], lp['w_out'], row(lp['norm2_g']),
      lp['router_w'].T)


def _select_kernel(aff_ref, slot_ref, affo_ref, *, n, cap, bg):
    for g in range(bg):
        _select_one(aff_ref[:, g * n:(g + 1) * n], slot_ref.at[g], affo_ref.at[g], n, cap)


def _select_one(a, slot_ref, affo_ref, n, cap):
    E = N_EXPERTS
    bits = pltpu.bitcast(a, jnp.int32)
    thr = jnp.zeros((E, 1), jnp.int32)
    capf = float(cap)
    for bit in range(30, -1, -1):
        cand = thr | (1 << bit)
        cnt = jnp.sum(jnp.where(bits >= cand, 1.0, 0.0), axis=1, keepdims=True)
        thr = jnp.where(cnt >= capf, cand, thr)
    gt = bits > thr
    eq = bits == thr
    need = capf - jnp.sum(jnp.where(gt, 1.0, 0.0), axis=1, keepdims=True)
    CW = min(256, n)
    ui = lax.broadcasted_iota(jnp.int32, (CW, CW), 0)
    uj = lax.broadcasted_iota(jnp.int32, (CW, CW), 1)
    upper = jnp.where(ui < uj, 1.0, 0.0).astype(BF16)

    def excl_cumsum(mask_f):
        outs = []
        off = jnp.zeros((E, 1), F32)
        for c in range(n // CW):
            mc = mask_f[:, c * CW:(c + 1) * CW]
            outs.append(jnp.dot(mc.astype(BF16), upper, preferred_element_type=F32) + off)
            off = off + jnp.sum(mc, axis=1, keepdims=True)
        return jnp.concatenate(outs, axis=1)

    eq_rank = excl_cumsum(jnp.where(eq, 1.0, 0.0))
    sel = jnp.where(gt, 1.0, jnp.where(eq, jnp.where(eq_rank < need, 1.0, 0.0), 0.0))
    pos = excl_cumsum(sel)
    slot = jnp.where(sel > 0.0, pos, -1.0)
    for e in range(E):
        slot_ref[e] = slot[e:e + 1, :]
        affo_ref[e] = a[e:e + 1, :]


def _select(aff, B, n, cap):
    bg = 4 if (B % 4 == 0 and n <= 512) else (2 if B % 2 == 0 else 1)
    spec = pl.BlockSpec((bg, N_EXPERTS, 1, n), lambda b: (b, 0, 0, 0))
    return pl.pallas_call(
        functools.partial(_select_kernel, n=n, cap=cap, bg=bg),
        grid=(B // bg,),
        in_specs=[pl.BlockSpec((N_EXPERTS, bg * n), lambda b: (0, b))],
        out_specs=(spec, spec),
        out_shape=(jax.ShapeDtypeStruct((B, N_EXPERTS, 1, n), F32), jax.ShapeDtypeStruct((B, N_EXPERTS, 1, n), F32)),
        compiler_params=_cparams(("parallel",)),
        name="ec_select",
    )(aff)


def _slot_block_range(slots_f, cap, SB):
    lo = jnp.min(jnp.where(slots_f >= 0.0, slots_f, float(cap))).astype(jnp.int32)
    hi = jnp.max(slots_f).astype(jnp.int32)
    first = lo // SB
    count = jnp.where(hi >= 0, hi // SB - first + 1, 0)
    return first, count


EC_SLOT_BLOCK = 256


def _one_hot_all(slot_ref, aff_ref, n, cap):
    srow = lax.broadcasted_iota(jnp.int32, (cap, n), 0).astype(F32)
    ohs, gates = [], []
    for e in range(N_EXPERTS):
        hit = slot_ref[e] == srow
        ohs.append(jnp.where(hit, 1.0, 0.0).astype(BF16))
        if aff_ref is not None:
            gates.append(jnp.sum(jnp.where(hit, aff_ref[e], 0.0), axis=1, keepdims=True))
    return jnp.concatenate(ohs, axis=0), (jnp.concatenate(gates, axis=0) if gates else None)


def _gather_all_kernel(hb_ref, slot_ref, aff_ref, xs_ref, gate_ref, *, n, cap):
    oh, gate = _one_hot_all(slot_ref, aff_ref, n, cap)
    xs = jnp.dot(oh, hb_ref[...], preferred_element_type=F32)
    xs_ref[...] = xs.astype(BF16).reshape(N_EXPERTS, cap, D_MODEL)
    gate_ref[...] = gate.reshape(N_EXPERTS, cap, 1)


def _gather_kernel(hb_ref, slot_ref, aff_ref, xs_ref, gate_ref, acc, gacc, *, n, cap):
    NC = min(512, n)
    SB = min(EC_SLOT_BLOCK // 2, cap)
    acc[...] = jnp.zeros_like(acc)
    gacc[...] = jnp.zeros_like(gacc)
    srow = lax.broadcasted_iota(jnp.int32, (SB, NC), 0).astype(F32)
    seen = jnp.int32(0)
    for c in range(n // NC):
        sl = slot_ref[:, c * NC:(c + 1) * NC]
        cnt = jnp.sum(jnp.where(sl >= 0.0, 1.0, 0.0)).astype(jnp.int32)
        first = seen // SB
        count = jnp.where(cnt > 0, (seen + cnt - 1) // SB - first + 1, 0)
        seen = seen + cnt
        for j in range(min(cap // SB, NC // SB + 1)):
            @pl.when(j < count)
            def _(c=c, j=j, sl=sl, first=first):
                base = pl.multiple_of((first + j) * SB, SB)
                hit = (sl - base.astype(F32)) == srow
                oh = jnp.where(hit, 1.0, 0.0).astype(BF16)
                acc[pl.ds(base, SB), :] += jnp.dot(oh, hb_ref[c * NC:(c + 1) * NC, :], preferred_element_type=F32)
                gacc[pl.ds(base, SB), :] += jnp.sum(jnp.where(hit, aff_ref[:, c * NC:(c + 1) * NC], 0.0), axis=1,
                                                    keepdims=True)
    xs_ref[...] = acc[...].astype(BF16)
    gate_ref[...] = gacc[...]


def _gather(hb, slot4, aff4, B, n, cap):
    E = N_EXPERTS
    if E * cap <= 512:
        all_spec = pl.BlockSpec((None, E, 1, n), lambda b: (b, 0, 0, 0))
        return pl.pallas_call(
            functools.partial(_gather_all_kernel, n=n, cap=cap),
            grid=(B,),
            in_specs=[pl.BlockSpec((n, D_MODEL), lambda b: (b, 0)), all_spec, all_spec],
            out_specs=(pl.BlockSpec((None, E, cap, D_MODEL), lambda b: (b, 0, 0, 0)),
                       pl.BlockSpec((None, E, cap, 1), lambda b: (b, 0, 0, 0))),
            out_shape=(jax.ShapeDtypeStruct((B, E, cap, D_MODEL), BF16), jax.ShapeDtypeStruct((B, E, cap, 1), F32)),
            compiler_params=_cparams(("parallel",)),
            name="ec_gather_all",
        )(hb, slot4, aff4)
    return pl.pallas_call(
        functools.partial(_gather_kernel, n=n, cap=cap),
        grid=(B, E),
        in_specs=[
            pl.BlockSpec((n, D_MODEL), lambda b, e: (b, 0)),
            pl.BlockSpec((None, None, 1, n), lambda b, e: (b, e, 0, 0)),
            pl.BlockSpec((None, None, 1, n), lambda b, e: (b, e, 0, 0)),
        ],
        out_specs=(pl.BlockSpec((None, None, cap, D_MODEL), lambda b, e: (b, e, 0, 0)),
                   pl.BlockSpec((None, None, cap, 1), lambda b, e: (b, e, 0, 0))),
        out_shape=(jax.ShapeDtypeStruct((B, E, cap, D_MODEL), BF16), jax.ShapeDtypeStruct((B, E, cap, 1), F32)),
        scratch_shapes=[pltpu.VMEM((cap, D_MODEL), F32), pltpu.VMEM((cap, 1), F32)],
        compiler_params=_cparams(("parallel", "arbitrary")),
        name="ec_gather",
    )(hb, slot4, aff4)


def _ffn_kernel(xs_ref, gate_ref, w1_ref, w3_ref, w2_ref, o_ref):
    bg, cap, _ = xs_ref.shape
    x = xs_ref[...].reshape(bg * cap, D_MODEL)
    h1 = jnp.dot(x, w1_ref[...], preferred_element_type=F32)
    h3 = jnp.dot(x, w3_ref[...], preferred_element_type=F32)
    hid = (h1 * _sigmoid(h1) * h3).astype(BF16)
    o = jnp.dot(hid, w2_ref[...], preferred_element_type=F32) * gate_ref[...].reshape(bg * cap, 1)
    o_ref[...] = o.astype(BF16).reshape(bg, cap, D_MODEL)


def _expert_ffn(xs, gate, w1, w3, w2, bg):
    B, E, cap, _ = xs.shape
    return pl.pallas_call(
        _ffn_kernel,
        grid=(E, B // bg),
        in_specs=[
            pl.BlockSpec((bg, None, cap, D_MODEL), lambda e, b: (b, e, 0, 0)),
            pl.BlockSpec((bg, None, cap, 1), lambda e, b: (b, e, 0, 0)),
            pl.BlockSpec((None, D_MODEL, D_EXPERT), lambda e, b: (e, 0, 0)),
            pl.BlockSpec((None, D_MODEL, D_EXPERT), lambda e, b: (e, 0, 0)),
            pl.BlockSpec((None, D_EXPERT, D_MODEL), lambda e, b: (e, 0, 0)),
        ],
        out_specs=pl.BlockSpec((bg, None, cap, D_MODEL), lambda e, b: (b, e, 0, 0)),
        out_shape=jax.ShapeDtypeStruct((B, E, cap, D_MODEL), BF16),
        compiler_params=_cparams(("parallel", "parallel")),
        name="ec_ffn",
    )(xs, gate, w1, w3, w2)


_TN_DIMS = (((0,), (0,)), ((), ()))


def _ec_residual(x1_ref, mod_ref, fg_ref, out_ref, ffn, final):
    x2 = x1_ref[...] + mod_ref[:, 5 * D_MODEL:6 * D_MODEL] * ffn
    if final:
        x2 = x2 * lax.rsqrt(jnp.mean(x2 * x2, axis=-1, keepdims=True) + EPS) * fg_ref[...]
    out_ref[...] = x2


def _scatter_all_kernel(slot_ref, o_ref, x1_ref, mod_ref, fg_ref, out_ref, *, n, cap, final):
    oh, _ = _one_hot_all(slot_ref, None, n, cap)
    ffn = lax.dot_general(oh, o_ref[...].reshape(N_EXPERTS * cap, D_MODEL), _TN_DIMS, preferred_element_type=F32)
    _ec_residual(x1_ref, mod_ref, fg_ref, out_ref, ffn, final)


def _scatter_kernel(slot_ref, o_ref, x1_ref, mod_ref, fg_ref, out_ref, acc, *, cap, final):
    e = pl.program_id(2)
    tn = x1_ref.shape[0]

    @pl.when(e == 0)
    def _():
        acc[...] = jnp.zeros_like(acc)

    sl = slot_ref[...]
    SB = min(EC_SLOT_BLOCK, cap)
    srow = lax.broadcasted_iota(jnp.int32, (SB, tn), 0).astype(F32)
    first, count = _slot_block_range(sl, cap, SB)
    for j in range(min(cap // SB, tn // SB + 1)):
        @pl.when(j < count)
        def _(j=j):
            base = pl.multiple_of((first + j) * SB, SB)
            oh = jnp.where((sl - base.astype(F32)) == srow, 1.0, 0.0).astype(BF16)
            acc[...] += lax.dot_general(oh, o_ref[pl.ds(base, SB), :], _TN_DIMS, preferred_element_type=F32)

    @pl.when(e == N_EXPERTS - 1)
    def _():
        _ec_residual(x1_ref, mod_ref, fg_ref, out_ref, acc[...], final)


def _scatter(slot4, o, x1, mod3, mod_row, final_g, B, n, cap, final):
    E = N_EXPERTS
    if E * cap <= 512:
        return pl.pallas_call(
            functools.partial(_scatter_all_kernel, n=n, cap=cap, final=final),
            grid=(B,),
            in_specs=[
                pl.BlockSpec((None, E, 1, n), lambda b: (b, 0, 0, 0)),
                pl.BlockSpec((None, E, cap, D_MODEL), lambda b: (b, 0, 0, 0)),
                pl.BlockSpec((n, D_MODEL), lambda b: (b, 0)),
                pl.BlockSpec((None, 1, 6 * D_MODEL), lambda b: (mod_row(b), 0, 0)),
                _full((1, D_MODEL)),
            ],
            out_specs=pl.BlockSpec((n, D_MODEL), lambda b: (b, 0)),
            out_shape=jax.ShapeDtypeStruct((B * n, D_MODEL), F32),
            compiler_params=_cparams(("parallel",)),
            name="ec_scatter_all",
        )(slot4, o, x1, mod3, final_g.reshape(1, D_MODEL))
    tn = min(1024, n)
    nt = n // tn
    return pl.pallas_call(
        functools.partial(_scatter_kernel, cap=cap, final=final),
        grid=(B, nt, N_EXPERTS),
        in_specs=[
            pl.BlockSpec((None, None, 1, tn), lambda b, t, e: (b, e, 0, t)),
            pl.BlockSpec((None, None, cap, D_MODEL), lambda b, t, e: (b, e, 0, 0)),
            pl.BlockSpec((tn, D_MODEL), lambda b, t, e: (b * nt + t, 0)),
            pl.BlockSpec((None, 1, 6 * D_MODEL), lambda b, t, e: (mod_row(b), 0, 0)),
            _full((1, D_MODEL)),
        ],
        out_specs=pl.BlockSpec((tn, D_MODEL), lambda b, t, e: (b * nt + t, 0)),
        out_shape=jax.ShapeDtypeStruct((B * n, D_MODEL), F32),
        scratch_shapes=[pltpu.VMEM((tn, D_MODEL), F32)],
        compiler_params=_cparams(("parallel", "parallel", "arbitrary")),
        name="ec_scatter",
    )(slot4, o, x1, mod3, final_g.reshape(1, D_MODEL))


def _expert_choice(hb, aff, x1, mod3, mod_row, final_g, w1, w3, w2, B, n, final):
    cap = EC_FACTOR * n // N_EXPERTS
    slot4, aff4 = _select(aff, B, n, cap)
    xs, gate = _gather(hb, slot4, aff4, B, n, cap)
    bg = max(1, min(B, 256 // cap))
    o = _expert_ffn(xs, gate, w1, w3, w2, bg)
    return _scatter(slot4, o, x1, mod3, mod_row, final_g, B, n, cap, final)


def kernel(x_prompt, x_sample, c, state_s5_re, state_s5_im, state_ssd, state_rwkv, c_ctx, ada_w, ada_b, norm1_g, norm2_g, w_in, w_out, s5_a_re, s5_a_im, s5_log_dt, s5_b_re, s5_b_im, s5_c_re, s5_c_im, s5_d, s5_glu_w, s5_glu_b, ssd_conv_w, ssd_conv_b, ssd_a_log, ssd_dt_bias, ssd_d, ssd_norm_g, rwkv_mu, rwkv_w0, rwkv_w_up, rwkv_a0, rwkv_a_up, rwkv_g_up, rwkv_k_k, rwkv_k_a, rwkv_r_k, rwkv_ln_g, rwkv_ln_b, router_w, exp_w1, exp_w3, exp_w2, final_g):
    Bp, Lp, D = x_prompt.shape
    Bs, Ls, _ = x_sample.shape
    depth = ada_w.shape[0]
    Np, Ns = Bp * Lp, Bs * Ls
    tm = TOKEN_BLOCK
    grid_rows = Ls // GRID_W
    nfs = S5_GROUPS // 8
    assert Lp % tm == 0 and Ls % tm == 0 and Lp % SSD_CHUNK == 0

    n_rows = 1 + Bs
    rows_pad = -(-n_rows // 8) * 8
    cond = jnp.zeros((rows_pad, D), F32).at[0].set(c_ctx).at[1:n_rows].set(c)
    mod = _modulation(cond, ada_w, ada_b)
    s_blocks_per_req = Ls // tm
    row_p = lambda i: 0
    row_s = lambda i: 1 + i // s_blocks_per_req

    ab_re, ab_im, bb_re, bb_im = _s5_discretize(s5_a_re, s5_a_im, s5_log_dt, s5_b_re, s5_b_im)
    w_in_pad = _pad_in_weight(w_in)
    w_out_b = w_out.astype(BF16)
    glu_w_b = s5_glu_w.astype(BF16)
    exp_w1_b, exp_w3_b, exp_w2_b = exp_w1.astype(BF16), exp_w3.astype(BF16), exp_w2.astype(BF16)
    seg = _segment_ones(RWKV_WIDTH, RWKV_HEADSIZE)

    xp = x_prompt.reshape(Np, D)
    xs = x_sample.reshape(Ns, D)
    new_s5_re, new_s5_im, new_ssd, new_rwkv = [], [], [], []
    QP = 2 if Bp % 2 == 0 and Bp >= 2 else 1
    RP = Bp // QP
    assert Bs * HEADS_PAD <= CH

    for l in range(depth):
        lp = {
            'rwkv_mu': rwkv_mu[l], 'rwkv_w0': rwkv_w0[l], 'rwkv_w_up': rwkv_w_up[l], 'rwkv_a0': rwkv_a0[l],
            'rwkv_a_up': rwkv_a_up[l], 'rwkv_g_up': rwkv_g_up[l], 'rwkv_k_k': rwkv_k_k[l], 'rwkv_k_a': rwkv_k_a[l],
            'rwkv_r_k': rwkv_r_k[l].reshape(-1), 'rwkv_ln_g': rwkv_ln_g[l], 'rwkv_ln_b': rwkv_ln_b[l],
            's5_glu_w': glu_w_b[l], 's5_glu_b': s5_glu_b[l], 'ssd_norm_g': ssd_norm_g[l], 'w_out': w_out_b[l],
            'norm2_g': norm2_g[l], 'router_w': router_w[l], 'seg': seg,
        }
        mod3 = mod[l].reshape(rows_pad, 1, 6 * D)
        us5_p, z_p, xbc_p, dt_p, rkv_p, lo_p = _in_projection(xp, mod3, row_p, norm1_g[l], w_in_pad[l])
        us5_s, z_s, xbc_s, dt_s, rkv_s, lo_s = _in_projection(xs, mod3, row_s, norm1_g[l], w_in_pad[l])

        tables = _s5_layer_tables(ab_re[l], ab_im[l], bb_re[l], bb_im[l], s5_c_re[l], s5_c_im[l])
        d_row = s5_d[l].reshape(1, S5_WIDTH)
        up = us5_p.reshape(QP, RP, Lp, S5_WIDTH).transpose(0, 2, 1, 3).reshape(QP, Lp * RP, S5_WIDTH)
        yp, hfin = _s5_scan(up, None, tables, d_row, R=RP, n_slab=Lp, chained=False, want_final=True)
        ys5_p = yp.reshape(QP, Lp, RP, S5_WIDTH).transpose(0, 2, 1, 3).reshape(Np, S5_WIDTH)
        hf = hfin.transpose(0, 3, 1, 2, 4).reshape(Bp, N_DIR, nfs, 2, 8, S5_STATE)
        new_s5_re.append(hf[:, :, :, 0].reshape(Bp, N_DIR, S5_GROUPS, S5_STATE))
        new_s5_im.append(hf[:, :, :, 1].reshape(Bp, N_DIR, S5_GROUPS, S5_STATE))
        h0 = jnp.concatenate([state_s5_re[:, l].reshape(Bs, N_DIR, nfs, 1, S5_SLICE_ST),
                              state_s5_im[:, l].reshape(Bs, N_DIR, nfs, 1, S5_SLICE_ST)], axis=-1)
        (ysm,) = _s5_scan(us5_s.reshape(Bs, Ls, S5_WIDTH), h0, tables, d_row, R=GRID_W, n_slab=grid_rows,
                          chained=True, want_final=False)
        ys5_s = ysm.reshape(Ns, S5_WIDTH)

        ssd_args = (ssd_conv_w[l], ssd_conv_b[l], ssd_dt_bias[l], ssd_a_log[l], ssd_d[l])
        *yssd_p, hssd = _ssd_scan(xbc_p, dt_p, None, *ssd_args, B=Bp, L=Lp, want_final=True)
        new_ssd.append(hssd.transpose(0, 1, 2, 4, 3))
        yssd_s = _ssd_scan(xbc_s, dt_s, state_ssd[:, l].transpose(0, 1, 2, 4, 3), *ssd_args,
                           B=Bs, L=Ls, want_final=False)

        r_, w0_, w1_, k_, v_, kk_, b_, g_p, bonus_p = _rwkv_prep(rkv_p, lo_p, lp, Lp // tm)
        cx = lambda t: _to_ctx(t, t, Bp, Lp)
        zero_state = jnp.zeros((RWKV_HEADSIZE, RWKV_HEADSIZE, _ctx_lanes(Bp)[1]), F32)
        yf_, yb_, sfin = _rwkv_scan(cx(r_), _to_ctx(w0_, w1_, Bp, Lp), cx(k_), cx(kk_), cx(b_), cx(v_),
                                    _ctx_fwd_mask(Bp), zero_state)
        rf_p = _from_ctx(yf_, Bp, Lp, 0)
        rb_p = _from_ctx(yb_, Bp, Lp, 1)
        new_rwkv.append(_ctx_state_from(sfin, Bp))

        r_, w0_, w1_, k_, v_, kk_, b_, g_s, bonus_s = _rwkv_prep(rkv_s, lo_s, lp, Ls // tm, SCAN_WIDTH)
        sc = lambda t: _to_scan2_k(t, Bs, Ls)
        yf_, yb_ = _rwkv_scan2(sc(r_), sc(w0_), sc(w1_), sc(k_), sc(kk_), sc(b_), _to_scan2_v(v_, Bs, Ls),
                               _state_to_scan2(state_rwkv[:, l]))
        rf_s = _from_scan2(yf_, Bs, Ls)
        rb_s = _from_scan2(yb_, Bs, Ls)

        x1_p, hb_p, aff_p = _post_mixer(xp, mod3, row_p, ys5_p, yssd_p, z_p, rf_p, rb_p, bonus_p, g_p, lp)
        x1_s, hb_s, aff_s = _post_mixer(xs, mod3, row_s, ys5_s, yssd_s, z_s, rf_s, rb_s, bonus_s, g_s, lp)

        final = l == depth - 1
        ew = (exp_w1_b[l], exp_w3_b[l], exp_w2_b[l])
        xp = _expert_choice(hb_p, aff_p, x1_p, mod3, lambda b: 0, final_g, *ew, Bp, Lp, final)
        xs = _expert_choice(hb_s, aff_s, x1_s, mod3, lambda b: 1 + b, final_g, *ew, Bs, Ls, final)

    y_prompt = xp.reshape(Bp, Lp, D)
    y_sample = xs.reshape(Bs, Ls, D)
    return (y_prompt, y_sample, jnp.stack(new_s5_re, axis=1), jnp.stack(new_s5_im, axis=1),
            jnp.stack(new_ssd, axis=1), jnp.stack(new_rwkv, axis=1))
```

```python
import functools
import math

import jax
import jax.numpy as jnp
from jax import lax
from jax.experimental import pallas as pl
from jax.experimental.pallas import tpu as pltpu

F32 = jnp.float32
BF16 = jnp.bfloat16

D_MODEL = 1024
GRID_W = 64
N_DIR = 2
EPS = 1e-6
S5_WIDTH = 256
S5_CH = 16
S5_GROUPS = 16
S5_STATE = 64
SSD_HEADDIM = 64
SSD_HEADS = 6
SSD_WIDTH = 384
SSD_GROUPS = 2
SSD_STATE = 64
SSD_BC = 128
SSD_CONV_CH = 640
SSD_CHUNK = 128
RWKV_HEADSIZE = 64
RWKV_HEADS = 6
RWKV_WIDTH = 384
W_RANK = 32
A_RANK = 32
G_RANK = 64
DECAY_SCALE = math.exp(-0.5)
GN_EPS = 64e-5
N_EXPERTS = 16
D_EXPERT = 512
EC_FACTOR = 2
IN_SIZES = (S5_WIDTH, SSD_WIDTH, SSD_CONV_CH, N_DIR * SSD_HEADS, 3 * RWKV_WIDTH, N_DIR * W_RANK, A_RANK, G_RANK)

LANES = 128
TOKEN_BLOCK = 256
MATMUL_TOKEN_BLOCK = 512
DT_PAD = 128
LO_PAD = 256
S5_SLICE_CH = 128
S5_SLICE_ST = 512
VMEM_LIMIT = 56 * 1024 * 1024


def _cparams(sem):
    return pltpu.CompilerParams(dimension_semantics=sem, vmem_limit_bytes=VMEM_LIMIT)


def _full(shape):
    nd = len(shape)
    return pl.BlockSpec(shape, lambda *_: (0,) * nd)


def _sigmoid(x):
    return 1.0 / (1.0 + jnp.exp(-x))


def _softplus(x):
    return jnp.maximum(x, 0.0) + jnp.log1p(jnp.exp(-jnp.abs(x)))


def _split_bf16(x):
    hi = x.astype(BF16)
    lo = (x - hi.astype(F32)).astype(BF16)
    return hi, lo


def _dot_exact_rhs(x, m):
    hi, lo = _split_bf16(x)
    mb = m.astype(BF16)
    return jnp.dot(hi, mb, preferred_element_type=F32) + jnp.dot(lo, mb, preferred_element_type=F32)


def _dot_exact_lhs(m, x):
    hi, lo = _split_bf16(x)
    mb = m.astype(BF16)
    return jnp.dot(mb, hi, preferred_element_type=F32) + jnp.dot(mb, lo, preferred_element_type=F32)


def _dot3(x, w, dims=(((1,), (0,)), ((), ()))):
    xh, xl = _split_bf16(x)
    wh, wl = _split_bf16(w)
    dg = functools.partial(lax.dot_general, dimension_numbers=dims, preferred_element_type=F32)
    return dg(xh, wh) + (dg(xl, wh) + dg(xh, wl))


def _gelu_tanh(x):
    return 0.5 * x * (1.0 + jnp.tanh(math.sqrt(2.0 / math.pi) * (x + 0.044715 * (x * x * x))))


def _mod_kernel(c_ref, w_ref, b_ref, o_ref):
    c = c_ref[...]
    s = (c * _sigmoid(c)).astype(BF16)
    o_ref[...] = jnp.dot(s, w_ref[...].astype(BF16), preferred_element_type=F32) + b_ref[...]


def _modulation(cond, ada_w, ada_b):
    depth = ada_w.shape[0]
    rows = cond.shape[0]
    tn = 1536
    return pl.pallas_call(
        _mod_kernel,
        grid=(depth, 6 * D_MODEL // tn),
        in_specs=[
            pl.BlockSpec((rows, D_MODEL), lambda l, j: (0, 0)),
            pl.BlockSpec((None, D_MODEL, tn), lambda l, j: (l, 0, j)),
            pl.BlockSpec((None, 1, tn), lambda l, j: (l, 0, j)),
        ],
        out_specs=pl.BlockSpec((None, rows, tn), lambda l, j: (l, 0, j)),
        out_shape=jax.ShapeDtypeStruct((depth, rows, 6 * D_MODEL), F32),
        compiler_params=_cparams(("parallel", "parallel")),
        name="adaln_mod",
    )(cond, ada_w, ada_b.reshape(depth, 1, 6 * D_MODEL))


IN_PAD_SIZES = (S5_WIDTH, SSD_WIDTH, SSD_CONV_CH, DT_PAD, 3 * RWKV_WIDTH, LO_PAD)


def _inproj_kernel(x_ref, mod_ref, g_ref, w_ref, us5_ref, z_ref, xbc_ref, dt_ref, rkv_ref, lo_ref):
    x = x_ref[...]
    y = x * lax.rsqrt(jnp.mean(x * x, axis=-1, keepdims=True) + EPS) * g_ref[...]
    m = mod_ref[...]
    h = y * (1.0 + m[:, D_MODEL:2 * D_MODEL]) + m[:, 0:D_MODEL]
    p = jnp.dot(h.astype(BF16), w_ref[...], preferred_element_type=F32)
    start = 0
    for ref, size in zip((us5_ref, z_ref, xbc_ref, dt_ref, rkv_ref, lo_ref), IN_PAD_SIZES):
        ref[...] = p[:, start:start + size]
        start += size


def _pad_in_weight(w_in):
    parts, start = [], 0
    for s in IN_SIZES:
        parts.append(w_in[..., start:start + s])
        start += s
    us5, z, xbc, dt, rkv, wlo, alo, glo = parts
    zeros = lambda n: jnp.zeros(w_in.shape[:-1] + (n,), w_in.dtype)
    lo_used = wlo.shape[-1] + alo.shape[-1] + glo.shape[-1]
    return jnp.concatenate([us5, z, xbc, dt, zeros(DT_PAD - dt.shape[-1]), rkv, wlo, alo, glo,
                            zeros(LO_PAD - lo_used)], axis=-1).astype(BF16)


def _in_projection(x, mod3, row_of_block, norm_g, w_pad, tm):
    ntok = x.shape[0]
    width = w_pad.shape[1]
    outs = tuple(jax.ShapeDtypeStruct((ntok, s), F32) for s in IN_PAD_SIZES)
    return pl.pallas_call(
        _inproj_kernel,
        grid=(ntok // tm,),
        in_specs=[
            pl.BlockSpec((tm, D_MODEL), lambda i: (i, 0)),
            pl.BlockSpec((None, 1, 6 * D_MODEL), lambda i: (row_of_block(i), 0, 0)),
            _full((1, D_MODEL)),
            _full((D_MODEL, width)),
        ],
        out_specs=tuple(pl.BlockSpec((tm, s), lambda i: (i, 0)) for s in IN_PAD_SIZES),
        out_shape=outs,
        compiler_params=_cparams(("parallel",)),
        name="in_proj",
    )(x, mod3, norm_g.reshape(1, D_MODEL), w_pad)


def _s5_disc_kernel(are_ref, aim_ref, ldt_ref, bre_ref, bim_ref, abre_ref, abim_ref, bbre_ref, bbim_ref):
    lam_re = jnp.minimum(are_ref[...], -1e-4)
    lam_im = aim_ref[...]
    dt = jnp.exp(ldt_ref[...])
    mag = jnp.exp(lam_re * dt)
    ab_re = mag * jnp.cos(lam_im * dt)
    ab_im = mag * jnp.sin(lam_im * dt)
    num_re, num_im = ab_re - 1.0, ab_im
    den = lam_re * lam_re + lam_im * lam_im
    q_re = (num_re * lam_re + num_im * lam_im) / den
    q_im = (num_im * lam_re - num_re * lam_im) / den
    abre_ref[...] = ab_re
    abim_ref[...] = ab_im
    b_re = bre_ref[...]
    b_im = bim_ref[...]
    qr = q_re[:, None, :]
    qi = q_im[:, None, :]
    bbre_ref[...] = qr * b_re - qi * b_im
    bbim_ref[...] = qr * b_im + qi * b_re


def _s5_discretize(a_re, a_im, log_dt, b_re, b_im):
    lead = a_re.shape[:3]
    n = lead[0] * lead[1] * lead[2]
    a2 = lambda t: t.reshape(n, S5_STATE)
    ldt = jnp.broadcast_to(log_dt.reshape(n, 1), (n, S5_STATE))
    b3 = lambda t: t.reshape(n, S5_STATE, S5_CH).transpose(0, 2, 1)
    ab_re, ab_im, bb_re, bb_im = pl.pallas_call(
        _s5_disc_kernel,
        out_shape=(jax.ShapeDtypeStruct((n, S5_STATE), F32), jax.ShapeDtypeStruct((n, S5_STATE), F32),
                   jax.ShapeDtypeStruct((n, S5_CH, S5_STATE), F32), jax.ShapeDtypeStruct((n, S5_CH, S5_STATE), F32)),
        name="s5_discretize",
    )(a2(a_re), a2(a_im), ldt, b3(b_re), b3(b_im))
    return (ab_re.reshape(lead + (S5_STATE,)), ab_im.reshape(lead + (S5_STATE,)),
            bb_re.reshape(lead + (S5_CH, S5_STATE)), bb_im.reshape(lead + (S5_CH, S5_STATE)))


def _s5_layer_tables(ab_re, ab_im, bb_re, bb_im, c_re, c_im):
    nfs = S5_GROUPS // 8
    eye = jnp.eye(8, dtype=F32)

    def rows(t):
        return t.reshape(N_DIR, nfs, 8 * S5_STATE)

    ab_row = jnp.concatenate([rows(ab_re), rows(ab_im)], axis=-1).reshape(N_DIR, nfs, 1, 2 * S5_SLICE_ST)

    def bmat(t):
        t = t.reshape(N_DIR, nfs, 8, S5_CH, S5_STATE)
        return jnp.einsum('dfghp,gk->dfghkp', t, eye).reshape(N_DIR, nfs, S5_SLICE_CH, S5_SLICE_ST)

    b_mat = jnp.concatenate([bmat(bb_re), bmat(bb_im)], axis=-1).astype(BF16)

    def cmat(t):
        t = t.reshape(N_DIR, nfs, 8, S5_CH, S5_STATE)
        return jnp.einsum('dfghp,gk->dfgpkh', t, eye).reshape(N_DIR, nfs, S5_SLICE_ST, S5_SLICE_CH)

    c_mat = jnp.concatenate([cmat(c_re), -cmat(c_im)], axis=-2).astype(BF16)
    return ab_row, b_mat, c_mat


def _s5_kernel(*refs, R, n_slab, chained, want_final):
    if chained:
        u_ref, h0_ref, ab_ref, bm_ref, cm_ref, d_ref = refs[:6]
        rest = refs[6:]
    else:
        u_ref, ab_ref, bm_ref, cm_ref, d_ref = refs[:5]
        h0_ref = None
        rest = refs[5:]
    y_ref = rest[0]
    rest = rest[1:]
    if want_final:
        hfin_ref = rest[0]
        rest = rest[1:]
    H = rest[0]
    if chained:
        PW, CIN = rest[1], rest[2]
    NR = R * n_slab
    RC = min(512, NR)
    ST = S5_SLICE_ST
    nchunk = ST // LANES

    y_ref[...] = u_ref[...] * d_ref[...]

    for d in range(N_DIR):
        def slab_of(i, d=d):
            return i if d == 0 else n_slab - 1 - i

        def bu_body(i, _, d=d):
            r0 = pl.multiple_of(i * RC, RC)
            H[pl.ds(r0, RC), :] = jnp.dot(u_ref[pl.ds(r0, RC), :].astype(BF16), bm_ref[d],
                                           preferred_element_type=F32)
            return 0
        lax.fori_loop(0, NR // RC, bu_body, 0)

        for c in range(nchunk):
            lre = slice(c * LANES, (c + 1) * LANES)
            lim = slice(ST + c * LANES, ST + (c + 1) * LANES)
            a_re = jnp.broadcast_to(ab_ref[d, :, lre], (R, LANES))
            a_im = jnp.broadcast_to(ab_ref[d, :, lim], (R, LANES))

            def step(i, carry, lre=lre, lim=lim, a_re=a_re, a_im=a_im, slab_of=slab_of):
                cr, ci = carry
                r0 = pl.multiple_of(slab_of(i) * R, R)
                nr = a_re * cr - a_im * ci + H[pl.ds(r0, R), lre]
                ni = a_re * ci + a_im * cr + H[pl.ds(r0, R), lim]
                H[pl.ds(r0, R), lre] = nr
                H[pl.ds(r0, R), lim] = ni
                return nr, ni
            zero = jnp.zeros((R, LANES), F32)
            lax.fori_loop(0, n_slab, step, (zero, zero))

        last0 = (n_slab - 1) * R if d == 0 else 0
        if chained:
            a_re_row = ab_ref[d, :, 0:ST]
            a_im_row = ab_ref[d, :, ST:2 * ST]

            def pw_step(j, carry, a_re_row=a_re_row, a_im_row=a_im_row):
                pr, pi = carry
                PW[j, :, 0:ST] = jnp.broadcast_to(pr, (8, ST))
                PW[j, :, ST:2 * ST] = jnp.broadcast_to(pi, (8, ST))
                return pr * a_re_row - pi * a_im_row, pr * a_im_row + pi * a_re_row
            lax.fori_loop(0, n_slab, pw_step, (a_re_row, a_im_row))
            t_re = PW[n_slab - 1, 0:1, 0:ST]
            t_im = PW[n_slab - 1, 0:1, ST:2 * ST]

            cr = h0_ref[d, :, 0:ST]
            ci = h0_ref[d, :, ST:2 * ST]
            for i in range(R):
                c = i if d == 0 else R - 1 - i
                CIN[c:c + 1, 0:ST] = cr
                CIN[c:c + 1, ST:2 * ST] = ci
                er = H[last0 + c:last0 + c + 1, 0:ST]
                ei = H[last0 + c:last0 + c + 1, ST:2 * ST]
                cr, ci = t_re * cr - t_im * ci + er, t_re * ci + t_im * cr + ei

            def fix_step(i, _, slab_of=slab_of):
                r0 = pl.multiple_of(slab_of(i) * R, R)
                p = PW[i]
                for c in range(nchunk):
                    lre = slice(c * LANES, (c + 1) * LANES)
                    lim = slice(ST + c * LANES, ST + (c + 1) * LANES)
                    pr = p[0:1, lre]
                    pi = p[0:1, lim]
                    cr = CIN[:, lre]
                    ci = CIN[:, lim]
                    H[pl.ds(r0, R), lre] = H[pl.ds(r0, R), lre] + (pr * cr - pi * ci)
                    H[pl.ds(r0, R), lim] = H[pl.ds(r0, R), lim] + (pr * ci + pi * cr)
                return 0
            lax.fori_loop(0, n_slab, fix_step, 0)

        if want_final:
            hfin_ref[d] = H[last0:last0 + R, :]

        def y_body(i, _, d=d):
            r0 = pl.multiple_of(i * RC, RC)
            y_ref[pl.ds(r0, RC), :] = y_ref[pl.ds(r0, RC), :] + jnp.dot(
                H[pl.ds(r0, RC), :].astype(BF16), cm_ref[d], preferred_element_type=F32)
            return 0
        lax.fori_loop(0, NR // RC, y_body, 0)


def _s5_scan(u, h0, tables, d_row, *, R, n_slab, chained, want_final):
    ab_row, b_mat, c_mat = tables
    Q, NR, _ = u.shape
    nfs = S5_GROUPS // 8
    W2 = 2 * S5_SLICE_ST
    in_specs = [pl.BlockSpec((None, NR, S5_SLICE_CH), lambda q, f: (q, 0, f))]
    args = [u]
    if chained:
        in_specs.append(pl.BlockSpec((None, N_DIR, None, 1, W2), lambda q, f: (q, 0, f, 0, 0)))
        args.append(h0)
    in_specs += [
        pl.BlockSpec((N_DIR, None, 1, W2), lambda q, f: (0, f, 0, 0)),
        pl.BlockSpec((N_DIR, None, S5_SLICE_CH, W2), lambda q, f: (0, f, 0, 0)),
        pl.BlockSpec((N_DIR, None, W2, S5_SLICE_CH), lambda q, f: (0, f, 0, 0)),
        pl.BlockSpec((1, S5_SLICE_CH), lambda q, f: (0, f)),
    ]
    args += [ab_row, b_mat, c_mat, d_row]
    out_shape = [jax.ShapeDtypeStruct((Q, NR, S5_WIDTH), F32)]
    out_specs = [pl.BlockSpec((None, NR, S5_SLICE_CH), lambda q, f: (q, 0, f))]
    if want_final:
        out_shape.append(jax.ShapeDtypeStruct((Q, N_DIR, nfs, R, W2), F32))
        out_specs.append(pl.BlockSpec((None, N_DIR, None, R, W2), lambda q, f: (q, 0, f, 0, 0)))
    scratch = [pltpu.VMEM((NR, W2), F32)]
    if chained:
        scratch += [pltpu.VMEM((n_slab, 8, W2), F32), pltpu.VMEM((R, W2), F32)]
    res = pl.pallas_call(
        functools.partial(_s5_kernel, R=R, n_slab=n_slab, chained=chained, want_final=want_final),
        grid=(Q, nfs),
        in_specs=in_specs,
        out_specs=tuple(out_specs),
        out_shape=tuple(out_shape),
        scratch_shapes=scratch,
        compiler_params=_cparams(("parallel", "parallel")),
        name="s5_scan_chained" if chained else "s5_scan",
    )(*args)
    return res


def _ssd_kernel(*refs, TB, nb, has_h0, want_final):
    io = [refs[4 * d:4 * d + 4] for d in range(N_DIR)]
    refs = refs[8:]
    if has_h0:
        h0_ref = refs[0]
        refs = refs[1:]
    cw_ref, cb_ref, dtb_ref, arow_ref, sel_ref, drow_ref = refs[:6]
    refs = refs[6:]
    y_refs = refs[:2]
    refs = refs[2:]
    if want_final:
        hfin_ref = refs[0]
        refs = refs[1:]
    hst, xc_s, dt_s = refs
    CH = SSD_CHUNK
    P = SSD_HEADDIM
    j = pl.program_id(1)

    @pl.when(j == 0)
    def _():
        if has_h0:
            hst[...] = h0_ref[...]
        else:
            hst[...] = jnp.zeros_like(hst)

    li = lax.broadcasted_iota(jnp.int32, (CH, CH), 0)
    si = lax.broadcasted_iota(jnp.int32, (CH, CH), 1)
    tmats = ((si <= li).astype(F32), (si >= li).astype(F32))
    rows = lax.broadcasted_iota(jnp.int32, (TB, 1), 0)
    for d in range(N_DIR):
        xbc_ref, xp_ref, xn_ref, dt_ref = io[d]
        jj = j if d == 0 else nb - 1 - j
        x = xbc_ref[...]
        prev_row = xp_ref[7:8, :] * (jj > 0).astype(F32)
        next_row = xn_ref[0:1, :] * (jj < nb - 1).astype(F32)
        x_prev = jnp.where(rows == 0, prev_row, pltpu.roll(x, 1, 0))
        x_next = jnp.where(rows == TB - 1, next_row, pltpu.roll(x, TB - 1, 0))
        conv = cw_ref[0:1, :] * x_prev + cw_ref[1:2, :] * x + cw_ref[2:3, :] * x_next + cb_ref[...]
        xc_s[d] = conv * _sigmoid(conv)
        dtf = _softplus(dt_ref[...] + dtb_ref[...])
        dt_s[d] = _dot_exact_rhs(dtf, sel_ref[d])

    hrow = lax.broadcasted_iota(jnp.int32, (LANES, SSD_WIDTH), 0)
    e_head = jnp.where(lax.broadcasted_iota(jnp.int32, (LANES, SSD_WIDTH), 1) // P == hrow, 1.0, 0.0)
    hrow2 = lax.broadcasted_iota(jnp.int32, (LANES, SSD_HEADS * CH), 0)
    e_chunk = jnp.where(lax.broadcasted_iota(jnp.int32, (LANES, SSD_HEADS * CH), 1) // CH == hrow2, 1.0, 0.0)

    n_ch = TB // CH
    for i, d in [(i, d) for i in range(n_ch) for d in range(N_DIR)]:
        tmat = tmats[d]
        y_ref = y_refs[d]
        r0 = (i if d == 0 else n_ch - 1 - i) * CH
        dtc = dt_s[d, r0:r0 + CH, :]
        dA = dtc * arow_ref[d]
        cs = _dot_exact_lhs(tmat, dA)
        csT = cs.T
        dt_x = _dot_exact_rhs(dtc, e_head)
        cs_x = _dot_exact_rhs(cs, e_head)
        tot_x = jnp.sum(_dot_exact_rhs(dA, e_head), axis=0, keepdims=True)
        cs_xx = _dot_exact_rhs(cs, e_chunk)
        xs_all = xc_s[d, r0:r0 + CH, 0:SSD_WIDTH]
        xdt_all = xs_all * dt_x
        xd_all = xdt_all * jnp.exp(tot_x - cs_x)
        ecs_all = jnp.exp(cs_x)
        etot_all = jnp.exp(tot_x)
        Bm = xc_s[d, r0:r0 + CH, SSD_WIDTH:SSD_WIDTH + SSD_BC]
        Cm = xc_s[d, r0:r0 + CH, SSD_WIDTH + SSD_BC:SSD_WIDTH + 2 * SSD_BC]
        BmT = Bm.T
        for g in range(SSD_GROUPS):
            Cg = Cm[:, g * SSD_STATE:(g + 1) * SSD_STATE].astype(BF16)
            Bg = Bm[:, g * SSD_STATE:(g + 1) * SSD_STATE].astype(BF16)
            BgT = BmT[g * SSD_STATE:(g + 1) * SSD_STATE, :].astype(BF16)
            G = lax.dot_general(Cg, Bg, (((1,), (1,)), ((), ())), preferred_element_type=F32)
            for hh in range(SSD_HEADS // SSD_GROUPS):
                h = g * (SSD_HEADS // SSD_GROUPS) + hh
                hl = slice(h * P, (h + 1) * P)
                row = csT[h:h + 1, :]
                lm = jnp.exp(jnp.where(tmat > 0.0, cs_xx[:, h * CH:(h + 1) * CH] - row, -1e30))
                hprev = hst[d, h]
                y = jnp.dot((G * lm).astype(BF16), xdt_all[:, hl].astype(BF16), preferred_element_type=F32)
                y = y + jnp.dot(Cg, hprev.astype(BF16), preferred_element_type=F32) * ecs_all[:, hl]
                if d == 0:
                    y = y + drow_ref[:, hl] * xs_all[:, hl]
                y_ref[r0:r0 + CH, hl] = y
                hst[d, h] = etot_all[:, hl] * hprev + jnp.dot(BgT, xd_all[:, hl].astype(BF16),
                                                             preferred_element_type=F32)

    if want_final:
        @pl.when(j == nb - 1)
        def _():
            hfin_ref[...] = hst[...]


def _ssd_scan(xbc, dt, h0, conv_w, conv_b, dt_bias, a_log, d_skip, *, B, L, want_final):
    TB = min(512, L)
    nb = L // TB
    H = SSD_HEADS

    nrow8 = B * L // 8
    in_specs, args = [], []
    blks = (lambda b, j: b * nb + j, lambda b, j: b * nb + nb - 1 - j)
    for blk in blks:
        in_specs += [
            pl.BlockSpec((TB, SSD_CONV_CH), lambda b, j, blk=blk: (blk(b, j), 0)),
            pl.BlockSpec((8, SSD_CONV_CH), lambda b, j, blk=blk: (jnp.maximum(blk(b, j) * (TB // 8) - 1, 0), 0)),
            pl.BlockSpec((8, SSD_CONV_CH),
                         lambda b, j, blk=blk: (jnp.minimum((blk(b, j) + 1) * (TB // 8), nrow8 - 1), 0)),
            pl.BlockSpec((TB, DT_PAD), lambda b, j, blk=blk: (blk(b, j), 0)),
        ]
        args += [xbc, xbc, xbc, dt]
    if h0 is not None:
        in_specs.append(pl.BlockSpec((None, N_DIR, H, SSD_STATE, SSD_HEADDIM), lambda b, j: (b, 0, 0, 0, 0)))
        args.append(h0)
    dtb = jnp.pad(dt_bias.reshape(1, N_DIR * H), ((0, 0), (0, DT_PAD - N_DIR * H)))
    arow = jnp.pad(-jnp.exp(a_log), ((0, 0), (0, LANES - H))).reshape(N_DIR, 1, LANES)
    lane = jnp.arange(LANES)
    sel = jnp.stack([(lane[:, None] == (dd * H + lane[None, :])) & (lane[None, :] < H) for dd in range(N_DIR)]).astype(F32)
    drow = jnp.repeat(d_skip, SSD_HEADDIM).reshape(1, SSD_WIDTH)
    in_specs += [
        _full((3, SSD_CONV_CH)), _full((1, SSD_CONV_CH)), _full((1, DT_PAD)),
        _full((N_DIR, 1, LANES)), _full((N_DIR, LANES, LANES)), _full((1, SSD_WIDTH)),
    ]
    args += [conv_w, conv_b.reshape(1, SSD_CONV_CH), dtb, arow, sel, drow]
    out_shape = [jax.ShapeDtypeStruct((B * L, SSD_WIDTH), F32)] * N_DIR
    out_specs = [pl.BlockSpec((TB, SSD_WIDTH), lambda b, j, blk=blk: (blk(b, j), 0)) for blk in blks]
    if want_final:
        out_shape.append(jax.ShapeDtypeStruct((B, N_DIR, H, SSD_STATE, SSD_HEADDIM), F32))
        out_specs.append(pl.BlockSpec((None, N_DIR, H, SSD_STATE, SSD_HEADDIM), lambda b, j: (b, 0, 0, 0, 0)))
    return pl.pallas_call(
        functools.partial(_ssd_kernel, TB=TB, nb=nb, has_h0=h0 is not None, want_final=want_final),
        grid=(B, nb),
        in_specs=in_specs,
        out_specs=tuple(out_specs),
        out_shape=tuple(out_shape),
        scratch_shapes=[pltpu.VMEM((N_DIR, H, SSD_STATE, SSD_HEADDIM), F32),
                        pltpu.VMEM((N_DIR, TB, SSD_CONV_CH), F32), pltpu.VMEM((N_DIR, TB, DT_PAD), F32)],
        compiler_params=_cparams(("parallel", "arbitrary")),
        name="ssd_scan",
    )(*args)


def _rwkv_prep_kernel(rkv_ref, rp_ref, rn_ref, lo_ref, mu_ref, a0_ref, aup_ref, gup_ref, w0_ref, wup_ref,
                      kkw_ref, ka_ref, rk_ref, seg_ref,
                      r_ref, w0o_ref, w1o_ref, k_ref, v_ref, kk_ref, b_ref, g_ref, bonus_ref, *, nbs):
    tm = rkv_ref.shape[0]
    W = RWKV_WIDTH
    jj = pl.program_id(0) % nbs
    x = rkv_ref[...]
    prev_row = rp_ref[7:8, :] * (jj > 0).astype(F32)
    next_row = rn_ref[0:1, :] * (jj < nbs - 1).astype(F32)
    rows = lax.broadcasted_iota(jnp.int32, (tm, 1), 0)
    xp = jnp.where(rows == 0, prev_row, pltpu.roll(x, 1, 0))
    xn = jnp.where(rows == tm - 1, next_row, pltpu.roll(x, tm - 1, 0))
    x = x + mu_ref[0:1, :] * (xp - x) + mu_ref[1:2, :] * (xn - x)
    r = x[:, 0:W]
    k = x[:, W:2 * W]
    v = x[:, 2 * W:3 * W]
    lo = lo_ref[...]
    seg = seg_ref[...]

    def put(ref, val):
        ref[:, 0:W] = val
        if ref.shape[1] > W:
            ref[:, W:] = jnp.zeros((tm, ref.shape[1] - W), F32)

    a = _sigmoid(a0_ref[...] + _dot3(lo, aup_ref[...]))
    g_ref[...] = _dot3(_sigmoid(lo), gup_ref[...])
    tlo = jnp.tanh(lo)
    for d, o_ref in enumerate((w0o_ref, w1o_ref)):
        zw = w0_ref[d:d + 1, :] + _dot3(tlo, wup_ref[d])
        put(o_ref, jnp.exp(-DECAY_SCALE * _sigmoid(zw)))
    kk = k * kkw_ref[...]
    kk = kk * lax.rsqrt(jnp.maximum(_dot_exact_rhs(kk * kk, seg), 1e-24))
    k2 = k * (1.0 + (a - 1.0) * ka_ref[...])
    put(r_ref, r)
    put(k_ref, k2)
    put(v_ref, v)
    put(kk_ref, kk)
    put(b_ref, kk * a)
    bonus_ref[...] = _dot_exact_rhs(r * k2 * rk_ref[...], seg) * v


def _segment_ones(width, seg):
    i = jnp.arange(width)
    return (i[:, None] // seg == i[None, :] // seg).astype(F32)


def _rwkv_prep(rkv, lo, lp, seq_blocks, scan_width=RWKV_WIDTH):
    ntok = rkv.shape[0]
    tm = TOKEN_BLOCK
    W = RWKV_WIDTH
    nrow8 = ntok // 8
    nbs_of = seq_blocks
    widths = (scan_width,) * 7 + (W, W)
    pad_rows = lambda t, r0: jnp.zeros((LO_PAD, W), F32).at[r0:r0 + t.shape[0]].set(t)
    aup = pad_rows(lp['rwkv_a_up'], N_DIR * W_RANK)
    gup = pad_rows(lp['rwkv_g_up'], N_DIR * W_RANK + A_RANK)
    wup = jnp.stack([pad_rows(lp['rwkv_w_up'][d], d * W_RANK) for d in range(N_DIR)])
    row = lambda t: t.reshape(1, W)
    outs = tuple(jax.ShapeDtypeStruct((ntok, w), F32) for w in widths)
    return pl.pallas_call(
        functools.partial(_rwkv_prep_kernel, nbs=nbs_of),
        grid=(ntok // tm,),
        in_specs=[
            pl.BlockSpec((tm, 3 * W), lambda i: (i, 0)),
            pl.BlockSpec((8, 3 * W), lambda i: (jnp.maximum(i * (tm // 8) - 1, 0), 0)),
            pl.BlockSpec((8, 3 * W), lambda i: (jnp.minimum((i + 1) * (tm // 8), nrow8 - 1), 0)),
            pl.BlockSpec((tm, LO_PAD), lambda i: (i, 0)),
            _full((2, 3 * W)), _full((1, W)), _full((LO_PAD, W)), _full((LO_PAD, W)), _full((2, W)),
            _full((N_DIR, LO_PAD, W)), _full((1, W)), _full((1, W)), _full((1, W)), _full((W, W)),
        ],
        out_specs=tuple(pl.BlockSpec((tm, w), lambda i: (i, 0)) for w in widths),
        out_shape=outs,
        compiler_params=_cparams(("parallel",)),
        name="rwkv_prep",
    )(rkv, rkv, rkv, lo, lp['rwkv_mu'], row(lp['rwkv_a0']), aup, gup, lp['rwkv_w0'], wup,
      row(lp['rwkv_k_k']), row(lp['rwkv_k_a']), row(lp['rwkv_r_k']), lp['seg'])


def _rwkv_scan_kernel(rf, rm, wf, wm, kf, km, kkf, kkm, bf, bm, vf, vm, mask_ref, s0_ref, yf_ref, yb_ref, sfin_ref,
                      S, SA, OPS, VTS, *, TBLK, V, nblk):
    K = RWKV_HEADSIZE
    VH = V // 2
    tb = pl.program_id(1)

    @pl.when(tb == 0)
    def _():
        S[...] = s0_ref[...]

    is_fwd = mask_ref[...] > 0.5
    fwd_k = jnp.broadcast_to(is_fwd, (K, LANES))
    fwd_v = jnp.broadcast_to(is_fwd, (V, LANES))

    def pick(t, _):
        tm_ = TBLK - 1 - t
        for j, (f, m) in enumerate(((rf, rm), (wf, wm), (kf, km), (kkf, kkm), (bf, bm))):
            OPS[t, j] = jnp.where(fwd_k, f[t], m[tm_])
        VTS[t] = jnp.where(fwd_v, vf[t], vm[tm_])
        return 0
    lax.fori_loop(0, TBLK, pick, 0)
    R_, W_, K_, KK_, B_ = range(5)

    def bc(j, t, kx):
        return jnp.broadcast_to(OPS[t, j, pl.ds(kx, 1), :], (VH, LANES))

    def tree(parts):
        return (parts[0] + parts[1]) + (parts[2] + parts[3])

    for hv in range(2):
        rows = pl.ds(hv * VH, VH)
        accs = [None] * 4
        for kx in range(K):
            p = S[kx, rows, :] * bc(KK_, 0, kx)
            accs[kx % 4] = p if accs[kx % 4] is None else accs[kx % 4] + p
        SA[hv] = tree(accs)

    def step(t, _):
        tn = jnp.minimum(t + 1, TBLK - 1)
        tm_ = TBLK - 1 - t
        for hv in range(2):
            rows = pl.ds(hv * VH, VH)
            sa = SA[hv]
            vt = VTS[t, rows, :]
            yacc = [None] * 4
            sacc = [None] * 4
            for kx in range(K):
                s_new = S[kx, rows, :] * bc(W_, t, kx) - sa * bc(B_, t, kx) + vt * bc(K_, t, kx)
                S[kx, rows, :] = s_new
                py = s_new * bc(R_, t, kx)
                ps = s_new * bc(KK_, tn, kx)
                a = kx % 4
                yacc[a] = py if yacc[a] is None else yacc[a] + py
                sacc[a] = ps if sacc[a] is None else sacc[a] + ps
            y = tree(yacc)
            yf_ref[t, rows, :] = y
            yb_ref[tm_, rows, :] = y
            SA[hv] = tree(sacc)
        return 0
    lax.fori_loop(0, TBLK, step, 0)

    @pl.when(tb == nblk - 1)
    def _():
        sfin_ref[...] = S[...]


def _rwkv_scan(r, w, k, kk, b, v, fwd_mask, s0):
    L, K, NL = r.shape
    V = v.shape[1]
    TBLK = min(32, L)
    nblk = L // TBLK
    ngrp = NL // LANES
    kf = pl.BlockSpec((TBLK, K, LANES), lambda g, t: (t, 0, g))
    km = pl.BlockSpec((TBLK, K, LANES), lambda g, t: (nblk - 1 - t, 0, g))
    vf = pl.BlockSpec((TBLK, V, LANES), lambda g, t: (t, 0, g))
    vm = pl.BlockSpec((TBLK, V, LANES), lambda g, t: (nblk - 1 - t, 0, g))
    sspec = pl.BlockSpec((K, V, LANES), lambda g, t: (0, 0, g))
    return pl.pallas_call(
        functools.partial(_rwkv_scan_kernel, TBLK=TBLK, V=V, nblk=nblk),
        grid=(ngrp, nblk),
        in_specs=[kf, km] * 5 + [vf, vm, pl.BlockSpec((1, LANES), lambda g, t: (0, g)), sspec],
        out_specs=(vf, vm, sspec),
        out_shape=(jax.ShapeDtypeStruct((L, V, NL), F32), jax.ShapeDtypeStruct((L, V, NL), F32),
                   jax.ShapeDtypeStruct((K, V, NL), F32)),
        scratch_shapes=[pltpu.VMEM((K, V, LANES), F32), pltpu.VMEM((2, V // 2, LANES), F32),
                        pltpu.VMEM((TBLK, 5, K, LANES), F32), pltpu.VMEM((TBLK, V, LANES), F32)],
        compiler_params=_cparams(("parallel", "arbitrary")),
        name="rwkv_scan",
    )(r, r, w, w, k, k, kk, kk, b, b, v, v, fwd_mask, s0)


KQ = 4
CH = LANES // KQ


def _rwkv_scan2_kernel(rf, rm, wf, wm, kf, km, kkf, kkm, bf, bm, vf, vm, s0_ref, yf_ref, yb_ref, S, YQ, SA, *, TBLK):
    V = RWKV_HEADSIZE
    NQ = RWKV_HEADSIZE // KQ
    tb = pl.program_id(0)

    @pl.when(tb == 0)
    def _():
        S[...] = s0_ref[...]

    VH = V // 2

    def bc(ref, tt, q):
        return jnp.broadcast_to(ref[tt, pl.ds(q, 1), :], (VH, LANES))

    def all_quarters(x):
        return (x + pltpu.roll(x, CH, 1)) + (pltpu.roll(x, 2 * CH, 1) + pltpu.roll(x, 3 * CH, 1))

    dirs = ((rf, wf, kf, kkf, bf, vf), (rm, wm, km, kkm, bm, vm))
    time_of = (lambda t: t, lambda t: TBLK - 1 - t)
    chains = [(d, hv) for d in range(N_DIR) for hv in range(2)]

    def first_sa(d, hv):
        kk_ = dirs[d][3]
        tt = time_of[d](0)
        acc = [None, None]
        for q in range(NQ):
            p = S[d, q, pl.ds(hv * VH, VH), :] * bc(kk_, tt, q)
            acc[q % 2] = p if acc[q % 2] is None else acc[q % 2] + p
        return acc[0] + acc[1]

    TG = min(16, TBLK)
    lane_r = lax.broadcasted_iota(jnp.int32, (2 * LANES, CH), 0) % CH
    fold2 = jnp.where(lax.broadcasted_iota(jnp.int32, (2 * LANES, CH), 1) == lane_r, 1.0, 0.0).astype(BF16)

    last = len(chains) - 1
    for c, (d, hv) in enumerate(chains):
        part = first_sa(d, hv)
        SA[c] = part if c == last else all_quarters(part)

    def step(t, _):
        sa_last = all_quarters(SA[last])
        for c, (d, hv) in enumerate(chains):
            r_, w_, k_, kk_, b_, v_ = dirs[d]
            tt = time_of[d](t)
            tn = time_of[d](jnp.minimum(t + 1, TBLK - 1))
            rows = pl.ds(hv * VH, VH)
            sa = sa_last if c == last else SA[c]
            vt = v_[tt, rows, :]
            yacc = None
            sacc = None
            for q in range(NQ):
                s_new = S[d, q, rows, :] * bc(w_, tt, q) - sa * bc(b_, tt, q) + vt * bc(k_, tt, q)
                S[d, q, rows, :] = s_new
                py = s_new * bc(r_, tt, q)
                ps = s_new * bc(kk_, tn, q)
                yacc = py if yacc is None else yacc + py
                sacc = ps if sacc is None else sacc + ps
            YQ[d, tt, rows, :] = yacc
            SA[c] = sacc if c == last else all_quarters(sacc)
        return 0
    lax.fori_loop(0, TBLK, step, 0)

    def finish(i, _):
        for d, y_ in enumerate((yf_ref, yb_ref)):
            hi, lo = _split_bf16(YQ[d, pl.ds(i * TG, TG)].reshape(TG * V, LANES))
            y = jnp.dot(jnp.concatenate([hi, lo], axis=-1), fold2, preferred_element_type=F32)
            y_[pl.ds(i * TG, TG)] = y.reshape(TG, V, CH)
        return 0
    lax.fori_loop(0, TBLK // TG, finish, 0)


def _rwkv_scan2(r, w0, w1, k, kk, b, v, s0):
    L, NQ, _ = r.shape
    V = RWKV_HEADSIZE
    TBLK = min(64, L)
    nblk = L // TBLK
    fspec = pl.BlockSpec((TBLK, NQ, LANES), lambda t: (t, 0, 0))
    mspec = pl.BlockSpec((TBLK, NQ, LANES), lambda t: (nblk - 1 - t, 0, 0))
    vfspec = pl.BlockSpec((TBLK, V, LANES), lambda t: (t, 0, 0))
    vmspec = pl.BlockSpec((TBLK, V, LANES), lambda t: (nblk - 1 - t, 0, 0))
    yfspec = pl.BlockSpec((TBLK, V, CH), lambda t: (t, 0, 0))
    ymspec = pl.BlockSpec((TBLK, V, CH), lambda t: (nblk - 1 - t, 0, 0))
    yf, yb = pl.pallas_call(
        functools.partial(_rwkv_scan2_kernel, TBLK=TBLK),
        grid=(nblk,),
        in_specs=[fspec, mspec] * 5 + [vfspec, vmspec, _full((N_DIR, NQ, V, LANES))],
        out_specs=(yfspec, ymspec),
        out_shape=(jax.ShapeDtypeStruct((L, V, CH), F32), jax.ShapeDtypeStruct((L, V, CH), F32)),
        scratch_shapes=[pltpu.VMEM((N_DIR, NQ, V, LANES), F32), pltpu.VMEM((N_DIR, TBLK, V, LANES), F32),
                        pltpu.VMEM((2 * N_DIR, V // 2, LANES), F32)],
        compiler_params=_cparams(("arbitrary",)),
        name="rwkv_scan2",
    )(r, r, w0, w1, k, k, kk, kk, b, b, v, v, s0)
    return yf, yb


HEADS_PAD = 8
SCAN_WIDTH = HEADS_PAD * RWKV_HEADSIZE


def _pad_chains(x, B):
    return x if B * HEADS_PAD == CH else jnp.pad(x, ((0, 0),) * (x.ndim - 1) + ((0, CH - B * HEADS_PAD),))


def _to_scan2_k(t, B, L):
    x = t.reshape(B, L, HEADS_PAD, RWKV_HEADSIZE).transpose(1, 3, 0, 2)
    x = _pad_chains(x.reshape(L, RWKV_HEADSIZE // KQ, KQ, B * HEADS_PAD), B)
    return x.reshape(L, RWKV_HEADSIZE // KQ, LANES)


def _to_scan2_v(t, B, L):
    x = t.reshape(B, L, HEADS_PAD, RWKV_HEADSIZE).transpose(1, 3, 0, 2).reshape(L, RWKV_HEADSIZE, B * HEADS_PAD)
    return jnp.tile(_pad_chains(x, B), (1, 1, KQ))


def _from_scan2(y, B, L):
    y = y[:, :, :B * HEADS_PAD]
    return y.reshape(L, RWKV_HEADSIZE, B, HEADS_PAD).transpose(2, 0, 3, 1).reshape(B * L, SCAN_WIDTH)


def _state_to_scan2(s):
    B = s.shape[0]
    s = jnp.pad(s, ((0, 0), (0, 0), (0, HEADS_PAD - RWKV_HEADS), (0, 0), (0, 0)))
    x = s.transpose(1, 4, 3, 0, 2).reshape(N_DIR, RWKV_HEADSIZE // KQ, KQ, RWKV_HEADSIZE, B * HEADS_PAD)
    x = _pad_chains(x, B)
    return x.transpose(0, 1, 3, 2, 4).reshape(N_DIR, RWKV_HEADSIZE // KQ, RWKV_HEADSIZE, LANES)


def _ctx_lanes(B):
    chains = B * N_DIR * RWKV_HEADS
    return chains, -(-chains // LANES) * LANES


def _to_ctx(t_fwd, t_bwd, B, L):
    chains, lanes = _ctx_lanes(B)
    if t_bwd is t_fwd:
        x = t_fwd.reshape(B, L, RWKV_HEADS, RWKV_HEADSIZE).transpose(1, 3, 0, 2)
        x = jnp.broadcast_to(x[:, :, :, None, :], (L, RWKV_HEADSIZE, B, N_DIR, RWKV_HEADS))
    else:
        x = jnp.stack([t_fwd, t_bwd], 0).reshape(N_DIR, B, L, RWKV_HEADS, RWKV_HEADSIZE).transpose(2, 4, 1, 0, 3)
    x = x.reshape(L, RWKV_HEADSIZE, chains)
    return x if lanes == chains else jnp.pad(x, ((0, 0), (0, 0), (0, lanes - chains)))


def _ctx_fwd_mask(B):
    chains, lanes = _ctx_lanes(B)
    lane = jnp.arange(lanes)
    return (((lane // RWKV_HEADS) % N_DIR == 0) & (lane < chains)).astype(F32).reshape(1, lanes)


def _from_ctx(y, B, L, d):
    chains, _ = _ctx_lanes(B)
    y = y[:, :, :chains].reshape(L, RWKV_HEADSIZE, B, N_DIR, RWKV_HEADS)[:, :, :, d]
    return y.transpose(2, 0, 3, 1).reshape(B * L, RWKV_WIDTH)


def _ctx_state_from(x, B):
    chains, _ = _ctx_lanes(B)
    x = x[:, :, :chains].reshape(RWKV_HEADSIZE, RWKV_HEADSIZE, B, N_DIR, RWKV_HEADS)
    return x.transpose(2, 3, 4, 1, 0)


def _post_kernel(x_ref, mod_ref, ys5_ref, yf_ref, yb_ref, z_ref, rf_ref, rb_ref, bonus_ref, g_ref,
                 gluw_ref, glub_ref, ssdg_ref, lng_ref, lnb_ref, seg_ref, wout_ref, n2g_ref, rw_ref,
                 x1_ref, hb_ref, aff_ref):
    m = mod_ref[...]
    D = D_MODEL
    zg = _gelu_tanh(ys5_ref[...])
    gate = jnp.dot(zg.astype(BF16), gluw_ref[...], preferred_element_type=F32) + glub_ref[...]
    y_a = zg * _sigmoid(gate)
    z = z_ref[...]
    yb = (yf_ref[...] + yb_ref[...]) * (z * _sigmoid(z))
    y_b = yb * lax.rsqrt(jnp.mean(yb * yb, axis=-1, keepdims=True) + EPS) * ssdg_ref[...]
    seg = seg_ref[...] * (1.0 / RWKV_HEADSIZE)
    yr = rf_ref[...] + rb_ref[...]
    mean = _dot_exact_rhs(yr, seg)
    cen = yr - mean
    var = jnp.dot((cen * cen).astype(BF16), seg.astype(BF16), preferred_element_type=F32)
    yn = cen * lax.rsqrt(var + GN_EPS) * lng_ref[...] + lnb_ref[...]
    y_c = (yn + bonus_ref[...]) * g_ref[...]
    o = jnp.dot(y_a.astype(BF16), wout_ref[0:S5_WIDTH, :], preferred_element_type=F32)
    o = o + jnp.dot(y_b.astype(BF16), wout_ref[S5_WIDTH:S5_WIDTH + SSD_WIDTH, :], preferred_element_type=F32)
    o = o + jnp.dot(y_c.astype(BF16), wout_ref[S5_WIDTH + SSD_WIDTH:, :], preferred_element_type=F32)
    x1 = x_ref[...] + m[:, 2 * D:3 * D] * o
    x1_ref[...] = x1
    h2 = x1 * lax.rsqrt(jnp.mean(x1 * x1, axis=-1, keepdims=True) + EPS) * n2g_ref[...]
    h2 = h2 * (1.0 + m[:, 4 * D:5 * D]) + m[:, 3 * D:4 * D]
    hb_ref[...] = h2.astype(BF16)
    logits = _dot3(rw_ref[...], h2, (((1,), (1,)), ((), ())))
    mx = jnp.max(logits, axis=0, keepdims=True)
    ex = jnp.exp(logits - mx)
    aff_ref[...] = ex / jnp.sum(ex, axis=0, keepdims=True)


def _post_mixer(x, mod3, row_of_block, ys5, yssd, z, rf, rb, bonus, g, lp, tm):
    ntok = x.shape[0]
    tok = lambda w: pl.BlockSpec((tm, w), lambda i: (i, 0))
    row = lambda t: t.reshape(1, -1)
    W = RWKV_WIDTH
    return pl.pallas_call(
        _post_kernel,
        grid=(ntok // tm,),
        in_specs=[
            tok(D_MODEL),
            pl.BlockSpec((None, 1, 6 * D_MODEL), lambda i: (row_of_block(i), 0, 0)),
            tok(S5_WIDTH),
            tok(SSD_WIDTH), tok(SSD_WIDTH),
            tok(SSD_WIDTH), tok(W), tok(W), tok(W), tok(W),
            _full((S5_WIDTH, S5_WIDTH)), _full((1, S5_WIDTH)), _full((1, SSD_WIDTH)), _full((1, W)), _full((1, W)),
            _full((W, W)), _full((D_MODEL, D_MODEL)), _full((1, D_MODEL)), _full((N_EXPERTS, D_MODEL)),
        ],
        out_specs=(tok(D_MODEL), tok(D_MODEL), pl.BlockSpec((N_EXPERTS, tm), lambda i: (0, i))),
        out_shape=(jax.ShapeDtypeStruct((ntok, D_MODEL), F32), jax.ShapeDtypeStruct((ntok, D_MODEL), BF16),
                   jax.ShapeDtypeStruct((N_EXPERTS, ntok), F32)),
        compiler_params=_cparams(("parallel",)),
        name="post_mixer",
    )(x, mod3, ys5, yssd[0], yssd[1], z, rf, rb, bonus, g,
      lp['s5_glu_w'], row(lp['s5_glu_b']), row(lp['ssd_norm_g']), row(lp['rwkv_ln_g']),
      row(lp['rwkv_ln_b']), lp['seg'], lp['w_out'], row(lp['norm2_g']),
      lp['router_w'].T)


def _select_kernel(aff_ref, slot_ref, affo_ref, *, n, cap, bg):
    for g in range(bg):
        _select_one(aff_ref[:, g * n:(g + 1) * n], slot_ref.at[g], affo_ref.at[g], n, cap)


def _select_one(a, slot_ref, affo_ref, n, cap):
    E = N_EXPERTS
    bits = pltpu.bitcast(a, jnp.int32)
    thr = jnp.zeros((E, 1), jnp.int32)
    capf = float(cap)
    for bit in range(30, -1, -1):
        cand = thr | (1 << bit)
        cnt = jnp.sum(jnp.where(bits >= cand, 1.0, 0.0), axis=1, keepdims=True)
        thr = jnp.where(cnt >= capf, cand, thr)
    gt = bits > thr
    eq = bits == thr
    need = capf - jnp.sum(jnp.where(gt, 1.0, 0.0), axis=1, keepdims=True)
    CW = min(256, n)
    ui = lax.broadcasted_iota(jnp.int32, (CW, CW), 0)
    uj = lax.broadcasted_iota(jnp.int32, (CW, CW), 1)
    upper = jnp.where(ui < uj, 1.0, 0.0).astype(BF16)

    def excl_cumsum(mask_f):
        outs = []
        off = jnp.zeros((E, 1), F32)
        for c in range(n // CW):
            mc = mask_f[:, c * CW:(c + 1) * CW]
            outs.append(jnp.dot(mc.astype(BF16), upper, preferred_element_type=F32) + off)
            off = off + jnp.sum(mc, axis=1, keepdims=True)
        return jnp.concatenate(outs, axis=1)

    eq_rank = excl_cumsum(jnp.where(eq, 1.0, 0.0))
    sel = jnp.where(gt, 1.0, jnp.where(eq, jnp.where(eq_rank < need, 1.0, 0.0), 0.0))
    pos = excl_cumsum(sel)
    slot = jnp.where(sel > 0.0, pos, -1.0)
    for e in range(E):
        slot_ref[e] = slot[e:e + 1, :]
        affo_ref[e] = a[e:e + 1, :]


def _select(aff, B, n, cap):
    bg = 4 if (B % 4 == 0 and n <= 512) else (2 if B % 2 == 0 else 1)
    spec = pl.BlockSpec((bg, N_EXPERTS, 1, n), lambda b: (b, 0, 0, 0))
    return pl.pallas_call(
        functools.partial(_select_kernel, n=n, cap=cap, bg=bg),
        grid=(B // bg,),
        in_specs=[pl.BlockSpec((N_EXPERTS, bg * n), lambda b: (0, b))],
        out_specs=(spec, spec),
        out_shape=(jax.ShapeDtypeStruct((B, N_EXPERTS, 1, n), F32), jax.ShapeDtypeStruct((B, N_EXPERTS, 1, n), F32)),
        compiler_params=_cparams(("parallel",)),
        name="ec_select",
    )(aff)


def _slot_block_range(slots_f, cap, SB):
    lo = jnp.min(jnp.where(slots_f >= 0.0, slots_f, float(cap))).astype(jnp.int32)
    hi = jnp.max(slots_f).astype(jnp.int32)
    first = lo // SB
    count = jnp.where(hi >= 0, hi // SB - first + 1, 0)
    return first, count


EC_SLOT_BLOCK = 256


def _one_hot_all(slot_ref, aff_ref, n, cap):
    srow = lax.broadcasted_iota(jnp.int32, (cap, n), 0).astype(F32)
    ohs, gates = [], []
    for e in range(N_EXPERTS):
        hit = slot_ref[e] == srow
        ohs.append(jnp.where(hit, 1.0, 0.0).astype(BF16))
        if aff_ref is not None:
            gates.append(jnp.sum(jnp.where(hit, aff_ref[e], 0.0), axis=1, keepdims=True))
    return jnp.concatenate(ohs, axis=0), (jnp.concatenate(gates, axis=0) if gates else None)


def _gather_all_kernel(hb_ref, slot_ref, aff_ref, xs_ref, gate_ref, *, n, cap):
    oh, gate = _one_hot_all(slot_ref, aff_ref, n, cap)
    xs = jnp.dot(oh, hb_ref[...], preferred_element_type=F32)
    xs_ref[...] = xs.astype(BF16).reshape(N_EXPERTS, cap, D_MODEL)
    gate_ref[...] = gate.reshape(N_EXPERTS, cap, 1)


def _gather_kernel(hb_ref, slot_ref, aff_ref, xs_ref, gate_ref, acc, gacc, *, n, cap):
    NC = min(512, n)
    SB = min(EC_SLOT_BLOCK // 2, cap)
    acc[...] = jnp.zeros_like(acc)
    gacc[...] = jnp.zeros_like(gacc)
    srow = lax.broadcasted_iota(jnp.int32, (SB, NC), 0).astype(F32)
    seen = jnp.int32(0)
    for c in range(n // NC):
        sl = slot_ref[:, c * NC:(c + 1) * NC]
        cnt = jnp.sum(jnp.where(sl >= 0.0, 1.0, 0.0)).astype(jnp.int32)
        first = seen // SB
        count = jnp.where(cnt > 0, (seen + cnt - 1) // SB - first + 1, 0)
        seen = seen + cnt
        for j in range(min(cap // SB, NC // SB + 1)):
            @pl.when(j < count)
            def _(c=c, j=j, sl=sl, first=first):
                base = pl.multiple_of((first + j) * SB, SB)
                hit = (sl - base.astype(F32)) == srow
                oh = jnp.where(hit, 1.0, 0.0).astype(BF16)
                acc[pl.ds(base, SB), :] += jnp.dot(oh, hb_ref[c * NC:(c + 1) * NC, :], preferred_element_type=F32)
                gacc[pl.ds(base, SB), :] += jnp.sum(jnp.where(hit, aff_ref[:, c * NC:(c + 1) * NC], 0.0), axis=1,
                                                    keepdims=True)
    xs_ref[...] = acc[...].astype(BF16)
    gate_ref[...] = gacc[...]


def _gather(hb, slot4, aff4, B, n, cap):
    E = N_EXPERTS
    if E * cap <= 512:
        all_spec = pl.BlockSpec((None, E, 1, n), lambda b: (b, 0, 0, 0))
        return pl.pallas_call(
            functools.partial(_gather_all_kernel, n=n, cap=cap),
            grid=(B,),
            in_specs=[pl.BlockSpec((n, D_MODEL), lambda b: (b, 0)), all_spec, all_spec],
            out_specs=(pl.BlockSpec((None, E, cap, D_MODEL), lambda b: (b, 0, 0, 0)),
                       pl.BlockSpec((None, E, cap, 1), lambda b: (b, 0, 0, 0))),
            out_shape=(jax.ShapeDtypeStruct((B, E, cap, D_MODEL), BF16), jax.ShapeDtypeStruct((B, E, cap, 1), F32)),
            compiler_params=_cparams(("parallel",)),
            name="ec_gather_all",
        )(hb, slot4, aff4)
    return pl.pallas_call(
        functools.partial(_gather_kernel, n=n, cap=cap),
        grid=(B, E),
        in_specs=[
            pl.BlockSpec((n, D_MODEL), lambda b, e: (b, 0)),
            pl.BlockSpec((None, None, 1, n), lambda b, e: (b, e, 0, 0)),
            pl.BlockSpec((None, None, 1, n), lambda b, e: (b, e, 0, 0)),
        ],
        out_specs=(pl.BlockSpec((None, None, cap, D_MODEL), lambda b, e: (b, e, 0, 0)),
                   pl.BlockSpec((None, None, cap, 1), lambda b, e: (b, e, 0, 0))),
        out_shape=(jax.ShapeDtypeStruct((B, E, cap, D_MODEL), BF16), jax.ShapeDtypeStruct((B, E, cap, 1), F32)),
        scratch_shapes=[pltpu.VMEM((cap, D_MODEL), F32), pltpu.VMEM((cap, 1), F32)],
        compiler_params=_cparams(("parallel", "arbitrary")),
        name="ec_gather",
    )(hb, slot4, aff4)


def _ffn_kernel(xs_ref, gate_ref, w1_ref, w3_ref, w2_ref, o_ref):
    bg, cap, _ = xs_ref.shape
    x = xs_ref[...].reshape(bg * cap, D_MODEL)
    h1 = jnp.dot(x, w1_ref[...], preferred_element_type=F32)
    h3 = jnp.dot(x, w3_ref[...], preferred_element_type=F32)
    hid = (h1 * _sigmoid(h1) * h3).astype(BF16)
    o = jnp.dot(hid, w2_ref[...], preferred_element_type=F32) * gate_ref[...].reshape(bg * cap, 1)
    o_ref[...] = o.astype(BF16).reshape(bg, cap, D_MODEL)


def _expert_ffn(xs, gate, w1, w3, w2, bg):
    B, E, cap, _ = xs.shape
    return pl.pallas_call(
        _ffn_kernel,
        grid=(E, B // bg),
        in_specs=[
            pl.BlockSpec((bg, None, cap, D_MODEL), lambda e, b: (b, e, 0, 0)),
            pl.BlockSpec((bg, None, cap, 1), lambda e, b: (b, e, 0, 0)),
            pl.BlockSpec((None, D_MODEL, D_EXPERT), lambda e, b: (e, 0, 0)),
            pl.BlockSpec((None, D_MODEL, D_EXPERT), lambda e, b: (e, 0, 0)),
            pl.BlockSpec((None, D_EXPERT, D_MODEL), lambda e, b: (e, 0, 0)),
        ],
        out_specs=pl.BlockSpec((bg, None, cap, D_MODEL), lambda e, b: (b, e, 0, 0)),
        out_shape=jax.ShapeDtypeStruct((B, E, cap, D_MODEL), BF16),
        compiler_params=_cparams(("parallel", "parallel")),
        name="ec_ffn",
    )(xs, gate, w1, w3, w2)


_TN_DIMS = (((0,), (0,)), ((), ()))


def _ec_residual(x1_ref, mod_ref, fg_ref, out_ref, ffn, final):
    x2 = x1_ref[...] + mod_ref[:, 5 * D_MODEL:6 * D_MODEL] * ffn
    if final:
        x2 = x2 * lax.rsqrt(jnp.mean(x2 * x2, axis=-1, keepdims=True) + EPS) * fg_ref[...]
    out_ref[...] = x2


def _scatter_all_kernel(slot_ref, o_ref, x1_ref, mod_ref, fg_ref, out_ref, *, n, cap, final):
    oh, _ = _one_hot_all(slot_ref, None, n, cap)
    ffn = lax.dot_general(oh, o_ref[...].reshape(N_EXPERTS * cap, D_MODEL), _TN_DIMS, preferred_element_type=F32)
    _ec_residual(x1_ref, mod_ref, fg_ref, out_ref, ffn, final)


def _scatter_kernel(slot_ref, o_ref, x1_ref, mod_ref, fg_ref, out_ref, acc, *, cap, final):
    e = pl.program_id(2)
    tn = x1_ref.shape[0]

    @pl.when(e == 0)
    def _():
        acc[...] = jnp.zeros_like(acc)

    sl = slot_ref[...]
    SB = min(EC_SLOT_BLOCK, cap)
    srow = lax.broadcasted_iota(jnp.int32, (SB, tn), 0).astype(F32)
    first, count = _slot_block_range(sl, cap, SB)
    for j in range(min(cap // SB, tn // SB + 1)):
        @pl.when(j < count)
        def _(j=j):
            base = pl.multiple_of((first + j) * SB, SB)
            oh = jnp.where((sl - base.astype(F32)) == srow, 1.0, 0.0).astype(BF16)
            acc[...] += lax.dot_general(oh, o_ref[pl.ds(base, SB), :], _TN_DIMS, preferred_element_type=F32)

    @pl.when(e == N_EXPERTS - 1)
    def _():
        _ec_residual(x1_ref, mod_ref, fg_ref, out_ref, acc[...], final)


def _scatter(slot4, o, x1, mod3, mod_row, final_g, B, n, cap, final):
    E = N_EXPERTS
    if E * cap <= 512:
        return pl.pallas_call(
            functools.partial(_scatter_all_kernel, n=n, cap=cap, final=final),
            grid=(B,),
            in_specs=[
                pl.BlockSpec((None, E, 1, n), lambda b: (b, 0, 0, 0)),
                pl.BlockSpec((None, E, cap, D_MODEL), lambda b: (b, 0, 0, 0)),
                pl.BlockSpec((n, D_MODEL), lambda b: (b, 0)),
                pl.BlockSpec((None, 1, 6 * D_MODEL), lambda b: (mod_row(b), 0, 0)),
                _full((1, D_MODEL)),
            ],
            out_specs=pl.BlockSpec((n, D_MODEL), lambda b: (b, 0)),
            out_shape=jax.ShapeDtypeStruct((B * n, D_MODEL), F32),
            compiler_params=_cparams(("parallel",)),
            name="ec_scatter_all",
        )(slot4, o, x1, mod3, final_g.reshape(1, D_MODEL))
    tn = min(1024, n)
    nt = n // tn
    return pl.pallas_call(
        functools.partial(_scatter_kernel, cap=cap, final=final),
        grid=(B, nt, N_EXPERTS),
        in_specs=[
            pl.BlockSpec((None, None, 1, tn), lambda b, t, e: (b, e, 0, t)),
            pl.BlockSpec((None, None, cap, D_MODEL), lambda b, t, e: (b, e, 0, 0)),
            pl.BlockSpec((tn, D_MODEL), lambda b, t, e: (b * nt + t, 0)),
            pl.BlockSpec((None, 1, 6 * D_MODEL), lambda b, t, e: (mod_row(b), 0, 0)),
            _full((1, D_MODEL)),
        ],
        out_specs=pl.BlockSpec((tn, D_MODEL), lambda b, t, e: (b * nt + t, 0)),
        out_shape=jax.ShapeDtypeStruct((B * n, D_MODEL), F32),
        scratch_shapes=[pltpu.VMEM((tn, D_MODEL), F32)],
        compiler_params=_cparams(("parallel", "parallel", "arbitrary")),
        name="ec_scatter",
    )(slot4, o, x1, mod3, final_g.reshape(1, D_MODEL))


def _expert_choice(hb, aff, x1, mod3, mod_row, final_g, w1, w3, w2, B, n, final):
    cap = EC_FACTOR * n // N_EXPERTS
    slot4, aff4 = _select(aff, B, n, cap)
    xs, gate = _gather(hb, slot4, aff4, B, n, cap)
    bg = max(1, min(B, 256 // cap))
    o = _expert_ffn(xs, gate, w1, w3, w2, bg)
    return _scatter(slot4, o, x1, mod3, mod_row, final_g, B, n, cap, final)


def kernel(x_prompt, x_sample, c, state_s5_re, state_s5_im, state_ssd, state_rwkv, c_ctx, ada_w, ada_b, norm1_g, norm2_g, w_in, w_out, s5_a_re, s5_a_im, s5_log_dt, s5_b_re, s5_b_im, s5_c_re, s5_c_im, s5_d, s5_glu_w, s5_glu_b, ssd_conv_w, ssd_conv_b, ssd_a_log, ssd_dt_bias, ssd_d, ssd_norm_g, rwkv_mu, rwkv_w0, rwkv_w_up, rwkv_a0, rwkv_a_up, rwkv_g_up, rwkv_k_k, rwkv_k_a, rwkv_r_k, rwkv_ln_g, rwkv_ln_b, router_w, exp_w1, exp_w3, exp_w2, final_g):
    Bp, Lp, D = x_prompt.shape
    Bs, Ls, _ = x_sample.shape
    depth = ada_w.shape[0]
    Np, Ns = Bp * Lp, Bs * Ls
    tm = TOKEN_BLOCK
    grid_rows = Ls // GRID_W
    nfs = S5_GROUPS // 8
    assert Lp % tm == 0 and Ls % tm == 0 and Lp % SSD_CHUNK == 0

    n_rows = 1 + Bs
    rows_pad = -(-n_rows // 8) * 8
    cond = jnp.zeros((rows_pad, D), F32).at[0].set(c_ctx).at[1:n_rows].set(c)
    mod = _modulation(cond, ada_w, ada_b)
    tmm = MATMUL_TOKEN_BLOCK if (Np % MATMUL_TOKEN_BLOCK == 0 and Ls % MATMUL_TOKEN_BLOCK == 0) else tm
    s_blocks_per_req = Ls // tmm
    row_p = lambda i: 0
    row_s = lambda i: 1 + i // s_blocks_per_req

    ab_re, ab_im, bb_re, bb_im = _s5_discretize(s5_a_re, s5_a_im, s5_log_dt, s5_b_re, s5_b_im)
    w_in_pad = _pad_in_weight(w_in)
    w_out_b = w_out.astype(BF16)
    glu_w_b = s5_glu_w.astype(BF16)
    exp_w1_b, exp_w3_b, exp_w2_b = exp_w1.astype(BF16), exp_w3.astype(BF16), exp_w2.astype(BF16)
    seg = _segment_ones(RWKV_WIDTH, RWKV_HEADSIZE)

    xp = x_prompt.reshape(Np, D)
    xs = x_sample.reshape(Ns, D)
    new_s5_re, new_s5_im, new_ssd, new_rwkv = [], [], [], []
    QP = 2 if Bp % 2 == 0 and Bp >= 2 else 1
    RP = Bp // QP
    assert Bs * HEADS_PAD <= CH

    for l in range(depth):
        lp = {
            'rwkv_mu': rwkv_mu[l], 'rwkv_w0': rwkv_w0[l], 'rwkv_w_up': rwkv_w_up[l], 'rwkv_a0': rwkv_a0[l],
            'rwkv_a_up': rwkv_a_up[l], 'rwkv_g_up': rwkv_g_up[l], 'rwkv_k_k': rwkv_k_k[l], 'rwkv_k_a': rwkv_k_a[l],
            'rwkv_r_k': rwkv_r_k[l].reshape(-1), 'rwkv_ln_g': rwkv_ln_g[l], 'rwkv_ln_b': rwkv_ln_b[l],
            's5_glu_w': glu_w_b[l], 's5_glu_b': s5_glu_b[l], 'ssd_norm_g': ssd_norm_g[l], 'w_out': w_out_b[l],
            'norm2_g': norm2_g[l], 'router_w': router_w[l], 'seg': seg,
        }
        mod3 = mod[l].reshape(rows_pad, 1, 6 * D)
        us5_p, z_p, xbc_p, dt_p, rkv_p, lo_p = _in_projection(xp, mod3, row_p, norm1_g[l], w_in_pad[l], tmm)
        us5_s, z_s, xbc_s, dt_s, rkv_s, lo_s = _in_projection(xs, mod3, row_s, norm1_g[l], w_in_pad[l], tmm)

        tables = _s5_layer_tables(ab_re[l], ab_im[l], bb_re[l], bb_im[l], s5_c_re[l], s5_c_im[l])
        d_row = s5_d[l].reshape(1, S5_WIDTH)
        up = us5_p.reshape(QP, RP, Lp, S5_WIDTH).transpose(0, 2, 1, 3).reshape(QP, Lp * RP, S5_WIDTH)
        yp, hfin = _s5_scan(up, None, tables, d_row, R=RP, n_slab=Lp, chained=False, want_final=True)
        ys5_p = yp.reshape(QP, Lp, RP, S5_WIDTH).transpose(0, 2, 1, 3).reshape(Np, S5_WIDTH)
        hf = hfin.transpose(0, 3, 1, 2, 4).reshape(Bp, N_DIR, nfs, 2, 8, S5_STATE)
        new_s5_re.append(hf[:, :, :, 0].reshape(Bp, N_DIR, S5_GROUPS, S5_STATE))
        new_s5_im.append(hf[:, :, :, 1].reshape(Bp, N_DIR, S5_GROUPS, S5_STATE))
        h0 = jnp.concatenate([state_s5_re[:, l].reshape(Bs, N_DIR, nfs, 1, S5_SLICE_ST),
                              state_s5_im[:, l].reshape(Bs, N_DIR, nfs, 1, S5_SLICE_ST)], axis=-1)
        (ysm,) = _s5_scan(us5_s.reshape(Bs, Ls, S5_WIDTH), h0, tables, d_row, R=GRID_W, n_slab=grid_rows,
                          chained=True, want_final=False)
        ys5_s = ysm.reshape(Ns, S5_WIDTH)

        ssd_args = (ssd_conv_w[l], ssd_conv_b[l], ssd_dt_bias[l], ssd_a_log[l], ssd_d[l])
        *yssd_p, hssd = _ssd_scan(xbc_p, dt_p, None, *ssd_args, B=Bp, L=Lp, want_final=True)
        new_ssd.append(hssd.transpose(0, 1, 2, 4, 3))
        yssd_s = _ssd_scan(xbc_s, dt_s, state_ssd[:, l].transpose(0, 1, 2, 4, 3), *ssd_args,
                           B=Bs, L=Ls, want_final=False)

        r_, w0_, w1_, k_, v_, kk_, b_, g_p, bonus_p = _rwkv_prep(rkv_p, lo_p, lp, Lp // tm)
        cx = lambda t: _to_ctx(t, t, Bp, Lp)
        zero_state = jnp.zeros((RWKV_HEADSIZE, RWKV_HEADSIZE, _ctx_lanes(Bp)[1]), F32)
        yf_, yb_, sfin = _rwkv_scan(cx(r_), _to_ctx(w0_, w1_, Bp, Lp), cx(k_), cx(kk_), cx(b_), cx(v_),
                                    _ctx_fwd_mask(Bp), zero_state)
        rf_p = _from_ctx(yf_, Bp, Lp, 0)
        rb_p = _from_ctx(yb_, Bp, Lp, 1)
        new_rwkv.append(_ctx_state_from(sfin, Bp))

        r_, w0_, w1_, k_, v_, kk_, b_, g_s, bonus_s = _rwkv_prep(rkv_s, lo_s, lp, Ls // tm, SCAN_WIDTH)
        sc = lambda t: _to_scan2_k(t, Bs, Ls)
        yf_, yb_ = _rwkv_scan2(sc(r_), sc(w0_), sc(w1_), sc(k_), sc(kk_), sc(b_), _to_scan2_v(v_, Bs, Ls),
                               _state_to_scan2(state_rwkv[:, l]))
        rf_s = _from_scan2(yf_, Bs, Ls)
        rb_s = _from_scan2(yb_, Bs, Ls)

        x1_p, hb_p, aff_p = _post_mixer(xp, mod3, row_p, ys5_p, yssd_p, z_p, rf_p, rb_p, bonus_p, g_p, lp, tmm)
        x1_s, hb_s, aff_s = _post_mixer(xs, mod3, row_s, ys5_s, yssd_s, z_s, rf_s, rb_s, bonus_s, g_s, lp, tmm)

        final = l == depth - 1
        ew = (exp_w1_b[l], exp_w3_b[l], exp_w2_b[l])
        xp = _expert_choice(hb_p, aff_p, x1_p, mod3, lambda b: 0, final_g, *ew, Bp, Lp, final)
        xs = _expert_choice(hb_s, aff_s, x1_s, mod3, lambda b: 1 + b, final_g, *ew, Bs, Ls, final)

    y_prompt = xp.reshape(Bp, Lp, D)
    y_sample = xs.reshape(Bs, Ls, D)
    return (y_prompt, y_sample, jnp.stack(new_s5_re, axis=1), jnp.stack(new_s5_im, axis=1),
            jnp.stack(new_ssd, axis=1), jnp.stack(new_rwkv, axis=1))
```

```python
import functools
import math

import jax
import jax.numpy as jnp
from jax import lax
from jax.experimental import pallas as pl
from jax.experimental.pallas import tpu as pltpu

F32 = jnp.float32
BF16 = jnp.bfloat16

D_MODEL = 1024
GRID_W = 64
N_DIR = 2
EPS = 1e-6
S5_WIDTH = 256
S5_CH = 16
S5_GROUPS = 16
S5_STATE = 64
SSD_HEADDIM = 64
SSD_HEADS = 6
SSD_WIDTH = 384
SSD_GROUPS = 2
SSD_STATE = 64
SSD_BC = 128
SSD_CONV_CH = 640
SSD_CHUNK = 128
RWKV_HEADSIZE = 64
RWKV_HEADS = 6
RWKV_WIDTH = 384
W_RANK = 32
A_RANK = 32
G_RANK = 64
DECAY_SCALE = math.exp(-0.5)
GN_EPS = 64e-5
N_EXPERTS = 16
D_EXPERT = 512
EC_FACTOR = 2
IN_SIZES = (S5_WIDTH, SSD_WIDTH, SSD_CONV_CH, N_DIR * SSD_HEADS, 3 * RWKV_WIDTH, N_DIR * W_RANK, A_RANK, G_RANK)

LANES = 128
TOKEN_BLOCK = 256
MATMUL_TOKEN_BLOCK = 512
DT_PAD = 128
LO_PAD = 256
S5_SLICE_CH = 128
S5_SLICE_ST = 512
VMEM_LIMIT = 56 * 1024 * 1024


def _cparams(sem):
    return pltpu.CompilerParams(dimension_semantics=sem, vmem_limit_bytes=VMEM_LIMIT)


def _full(shape):
    nd = len(shape)
    return pl.BlockSpec(shape, lambda *_: (0,) * nd)


def _sigmoid(x):
    return 1.0 / (1.0 + jnp.exp(-x))


def _softplus(x):
    return jnp.maximum(x, 0.0) + jnp.log1p(jnp.exp(-jnp.abs(x)))


def _split_bf16(x):
    hi = x.astype(BF16)
    lo = (x - hi.astype(F32)).astype(BF16)
    return hi, lo


def _dot_exact_rhs(x, m):
    hi, lo = _split_bf16(x)
    mb = m.astype(BF16)
    return jnp.dot(hi, mb, preferred_element_type=F32) + jnp.dot(lo, mb, preferred_element_type=F32)


def _dot_exact_lhs(m, x):
    hi, lo = _split_bf16(x)
    mb = m.astype(BF16)
    return jnp.dot(mb, hi, preferred_element_type=F32) + jnp.dot(mb, lo, preferred_element_type=F32)


def _dot3(x, w, dims=(((1,), (0,)), ((), ()))):
    xh, xl = _split_bf16(x)
    wh, wl = _split_bf16(w)
    dg = functools.partial(lax.dot_general, dimension_numbers=dims, preferred_element_type=F32)
    return dg(xh, wh) + (dg(xl, wh) + dg(xh, wl))


def _gelu_tanh(x):
    return 0.5 * x * (1.0 + jnp.tanh(math.sqrt(2.0 / math.pi) * (x + 0.044715 * (x * x * x))))


def _mod_kernel(c_ref, w_ref, b_ref, o_ref):
    c = c_ref[...]
    s = (c * _sigmoid(c)).astype(BF16)
    o_ref[...] = jnp.dot(s, w_ref[...].astype(BF16), preferred_element_type=F32) + b_ref[...]


def _modulation(cond, ada_w, ada_b):
    depth = ada_w.shape[0]
    rows = cond.shape[0]
    tn = 1536
    return pl.pallas_call(
        _mod_kernel,
        grid=(depth, 6 * D_MODEL // tn),
        in_specs=[
            pl.BlockSpec((rows, D_MODEL), lambda l, j: (0, 0)),
            pl.BlockSpec((None, D_MODEL, tn), lambda l, j: (l, 0, j)),
            pl.BlockSpec((None, 1, tn), lambda l, j: (l, 0, j)),
        ],
        out_specs=pl.BlockSpec((None, rows, tn), lambda l, j: (l, 0, j)),
        out_shape=jax.ShapeDtypeStruct((depth, rows, 6 * D_MODEL), F32),
        compiler_params=_cparams(("parallel", "parallel")),
        name="adaln_mod",
    )(cond, ada_w, ada_b.reshape(depth, 1, 6 * D_MODEL))


IN_PAD_SIZES = (S5_WIDTH, SSD_WIDTH, SSD_CONV_CH, DT_PAD, 3 * RWKV_WIDTH, LO_PAD)


def _inproj_kernel(x_ref, mod_ref, g_ref, w_ref, us5_ref, z_ref, xbc_ref, dt_ref, rkv_ref, lo_ref):
    x = x_ref[...]
    y = x * lax.rsqrt(jnp.mean(x * x, axis=-1, keepdims=True) + EPS) * g_ref[...]
    m = mod_ref[...]
    h = y * (1.0 + m[:, D_MODEL:2 * D_MODEL]) + m[:, 0:D_MODEL]
    p = jnp.dot(h.astype(BF16), w_ref[...], preferred_element_type=F32)
    start = 0
    for ref, size in zip((us5_ref, z_ref, xbc_ref, dt_ref, rkv_ref, lo_ref), IN_PAD_SIZES):
        ref[...] = p[:, start:start + size]
        start += size


def _pad_in_weight(w_in):
    parts, start = [], 0
    for s in IN_SIZES:
        parts.append(w_in[..., start:start + s])
        start += s
    us5, z, xbc, dt, rkv, wlo, alo, glo = parts
    zeros = lambda n: jnp.zeros(w_in.shape[:-1] + (n,), w_in.dtype)
    lo_used = wlo.shape[-1] + alo.shape[-1] + glo.shape[-1]
    return jnp.concatenate([us5, z, xbc, dt, zeros(DT_PAD - dt.shape[-1]), rkv, wlo, alo, glo,
                            zeros(LO_PAD - lo_used)], axis=-1).astype(BF16)


def _in_projection(x, mod3, row_of_block, norm_g, w_pad, tm):
    ntok = x.shape[0]
    width = w_pad.shape[1]
    outs = tuple(jax.ShapeDtypeStruct((ntok, s), F32) for s in IN_PAD_SIZES)
    return pl.pallas_call(
        _inproj_kernel,
        grid=(ntok // tm,),
        in_specs=[
            pl.BlockSpec((tm, D_MODEL), lambda i: (i, 0)),
            pl.BlockSpec((None, 1, 6 * D_MODEL), lambda i: (row_of_block(i), 0, 0)),
            _full((1, D_MODEL)),
            _full((D_MODEL, width)),
        ],
        out_specs=tuple(pl.BlockSpec((tm, s), lambda i: (i, 0)) for s in IN_PAD_SIZES),
        out_shape=outs,
        compiler_params=_cparams(("parallel",)),
        name="in_proj",
    )(x, mod3, norm_g.reshape(1, D_MODEL), w_pad)


def _s5_disc_kernel(are_ref, aim_ref, ldt_ref, bre_ref, bim_ref, abre_ref, abim_ref, bbre_ref, bbim_ref):
    lam_re = jnp.minimum(are_ref[...], -1e-4)
    lam_im = aim_ref[...]
    dt = jnp.exp(ldt_ref[...])
    mag = jnp.exp(lam_re * dt)
    ab_re = mag * jnp.cos(lam_im * dt)
    ab_im = mag * jnp.sin(lam_im * dt)
    num_re, num_im = ab_re - 1.0, ab_im
    den = lam_re * lam_re + lam_im * lam_im
    q_re = (num_re * lam_re + num_im * lam_im) / den
    q_im = (num_im * lam_re - num_re * lam_im) / den
    abre_ref[...] = ab_re
    abim_ref[...] = ab_im
    b_re = bre_ref[...]
    b_im = bim_ref[...]
    qr = q_re[:, None, :]
    qi = q_im[:, None, :]
    bbre_ref[...] = qr * b_re - qi * b_im
    bbim_ref[...] = qr * b_im + qi * b_re


def _s5_discretize(a_re, a_im, log_dt, b_re, b_im):
    lead = a_re.shape[:3]
    n = lead[0] * lead[1] * lead[2]
    a2 = lambda t: t.reshape(n, S5_STATE)
    ldt = jnp.broadcast_to(log_dt.reshape(n, 1), (n, S5_STATE))
    b3 = lambda t: t.reshape(n, S5_STATE, S5_CH).transpose(0, 2, 1)
    ab_re, ab_im, bb_re, bb_im = pl.pallas_call(
        _s5_disc_kernel,
        out_shape=(jax.ShapeDtypeStruct((n, S5_STATE), F32), jax.ShapeDtypeStruct((n, S5_STATE), F32),
                   jax.ShapeDtypeStruct((n, S5_CH, S5_STATE), F32), jax.ShapeDtypeStruct((n, S5_CH, S5_STATE), F32)),
        name="s5_discretize",
    )(a2(a_re), a2(a_im), ldt, b3(b_re), b3(b_im))
    return (ab_re.reshape(lead + (S5_STATE,)), ab_im.reshape(lead + (S5_STATE,)),
            bb_re.reshape(lead + (S5_CH, S5_STATE)), bb_im.reshape(lead + (S5_CH, S5_STATE)))


def _s5_layer_tables(ab_re, ab_im, bb_re, bb_im, c_re, c_im):
    nfs = S5_GROUPS // 8
    eye = jnp.eye(8, dtype=F32)

    def rows(t):
        return t.reshape(N_DIR, nfs, 8 * S5_STATE)

    ab_row = jnp.concatenate([rows(ab_re), rows(ab_im)], axis=-1).reshape(N_DIR, nfs, 1, 2 * S5_SLICE_ST)

    def bmat(t):
        t = t.reshape(N_DIR, nfs, 8, S5_CH, S5_STATE)
        return jnp.einsum('dfghp,gk->dfghkp', t, eye).reshape(N_DIR, nfs, S5_SLICE_CH, S5_SLICE_ST)

    b_mat = jnp.concatenate([bmat(bb_re), bmat(bb_im)], axis=-1).astype(BF16)

    def cmat(t):
        t = t.reshape(N_DIR, nfs, 8, S5_CH, S5_STATE)
        return jnp.einsum('dfghp,gk->dfgpkh', t, eye).reshape(N_DIR, nfs, S5_SLICE_ST, S5_SLICE_CH)

    c_mat = jnp.concatenate([cmat(c_re), -cmat(c_im)], axis=-2).astype(BF16)
    return ab_row, b_mat, c_mat


def _s5_kernel(*refs, R, n_slab, chained, want_final):
    if chained:
        u_ref, h0_ref, ab_ref, bm_ref, cm_ref, d_ref = refs[:6]
        rest = refs[6:]
    else:
        u_ref, ab_ref, bm_ref, cm_ref, d_ref = refs[:5]
        h0_ref = None
        rest = refs[5:]
    y_ref = rest[0]
    rest = rest[1:]
    if want_final:
        hfin_ref = rest[0]
        rest = rest[1:]
    H = rest[0]
    if chained:
        PW, CIN = rest[1], rest[2]
    NR = R * n_slab
    RC = min(512, NR)
    ST = S5_SLICE_ST
    nchunk = ST // LANES

    y_ref[...] = u_ref[...] * d_ref[...]

    for d in range(N_DIR):
        def slab_of(i, d=d):
            return i if d == 0 else n_slab - 1 - i

        def bu_body(i, _, d=d):
            r0 = pl.multiple_of(i * RC, RC)
            H[pl.ds(r0, RC), :] = jnp.dot(u_ref[pl.ds(r0, RC), :].astype(BF16), bm_ref[d],
                                           preferred_element_type=F32)
            return 0
        lax.fori_loop(0, NR // RC, bu_body, 0)

        for c in range(nchunk):
            lre = slice(c * LANES, (c + 1) * LANES)
            lim = slice(ST + c * LANES, ST + (c + 1) * LANES)
            a_re = jnp.broadcast_to(ab_ref[d, :, lre], (R, LANES))
            a_im = jnp.broadcast_to(ab_ref[d, :, lim], (R, LANES))

            def step(i, carry, lre=lre, lim=lim, a_re=a_re, a_im=a_im, slab_of=slab_of):
                cr, ci = carry
                r0 = pl.multiple_of(slab_of(i) * R, R)
                nr = a_re * cr - a_im * ci + H[pl.ds(r0, R), lre]
                ni = a_re * ci + a_im * cr + H[pl.ds(r0, R), lim]
                H[pl.ds(r0, R), lre] = nr
                H[pl.ds(r0, R), lim] = ni
                return nr, ni
            zero = jnp.zeros((R, LANES), F32)
            lax.fori_loop(0, n_slab, step, (zero, zero))

        last0 = (n_slab - 1) * R if d == 0 else 0
        if chained:
            a_re_row = ab_ref[d, :, 0:ST]
            a_im_row = ab_ref[d, :, ST:2 * ST]

            def pw_step(j, carry, a_re_row=a_re_row, a_im_row=a_im_row):
                pr, pi = carry
                PW[j, :, 0:ST] = jnp.broadcast_to(pr, (8, ST))
                PW[j, :, ST:2 * ST] = jnp.broadcast_to(pi, (8, ST))
                return pr * a_re_row - pi * a_im_row, pr * a_im_row + pi * a_re_row
            lax.fori_loop(0, n_slab, pw_step, (a_re_row, a_im_row))
            t_re = PW[n_slab - 1, 0:1, 0:ST]
            t_im = PW[n_slab - 1, 0:1, ST:2 * ST]

            cr = h0_ref[d, :, 0:ST]
            ci = h0_ref[d, :, ST:2 * ST]
            for i in range(R):
                c = i if d == 0 else R - 1 - i
                CIN[c:c + 1, 0:ST] = cr
                CIN[c:c + 1, ST:2 * ST] = ci
                er = H[last0 + c:last0 + c + 1, 0:ST]
                ei = H[last0 + c:last0 + c + 1, ST:2 * ST]
                cr, ci = t_re * cr - t_im * ci + er, t_re * ci + t_im * cr + ei

            def fix_step(i, _, slab_of=slab_of):
                r0 = pl.multiple_of(slab_of(i) * R, R)
                p = PW[i]
                for c in range(nchunk):
                    lre = slice(c * LANES, (c + 1) * LANES)
                    lim = slice(ST + c * LANES, ST + (c + 1) * LANES)
                    pr = p[0:1, lre]
                    pi = p[0:1, lim]
                    cr = CIN[:, lre]
                    ci = CIN[:, lim]
                    H[pl.ds(r0, R), lre] = H[pl.ds(r0, R), lre] + (pr * cr - pi * ci)
                    H[pl.ds(r0, R), lim] = H[pl.ds(r0, R), lim] + (pr * ci + pi * cr)
                return 0
            lax.fori_loop(0, n_slab, fix_step, 0)

        if want_final:
            hfin_ref[d] = H[last0:last0 + R, :]

        def y_body(i, _, d=d):
            r0 = pl.multiple_of(i * RC, RC)
            y_ref[pl.ds(r0, RC), :] = y_ref[pl.ds(r0, RC), :] + jnp.dot(
                H[pl.ds(r0, RC), :].astype(BF16), cm_ref[d], preferred_element_type=F32)
            return 0
        lax.fori_loop(0, NR // RC, y_body, 0)


def _s5_scan(u, h0, tables, d_row, *, R, n_slab, chained, want_final):
    ab_row, b_mat, c_mat = tables
    Q, NR, _ = u.shape
    nfs = S5_GROUPS // 8
    W2 = 2 * S5_SLICE_ST
    in_specs = [pl.BlockSpec((None, NR, S5_SLICE_CH), lambda q, f: (q, 0, f))]
    args = [u]
    if chained:
        in_specs.append(pl.BlockSpec((None, N_DIR, None, 1, W2), lambda q, f: (q, 0, f, 0, 0)))
        args.append(h0)
    in_specs += [
        pl.BlockSpec((N_DIR, None, 1, W2), lambda q, f: (0, f, 0, 0)),
        pl.BlockSpec((N_DIR, None, S5_SLICE_CH, W2), lambda q, f: (0, f, 0, 0)),
        pl.BlockSpec((N_DIR, None, W2, S5_SLICE_CH), lambda q, f: (0, f, 0, 0)),
        pl.BlockSpec((1, S5_SLICE_CH), lambda q, f: (0, f)),
    ]
    args += [ab_row, b_mat, c_mat, d_row]
    out_shape = [jax.ShapeDtypeStruct((Q, NR, S5_WIDTH), F32)]
    out_specs = [pl.BlockSpec((None, NR, S5_SLICE_CH), lambda q, f: (q, 0, f))]
    if want_final:
        out_shape.append(jax.ShapeDtypeStruct((Q, N_DIR, nfs, R, W2), F32))
        out_specs.append(pl.BlockSpec((None, N_DIR, None, R, W2), lambda q, f: (q, 0, f, 0, 0)))
    scratch = [pltpu.VMEM((NR, W2), F32)]
    if chained:
        scratch += [pltpu.VMEM((n_slab, 8, W2), F32), pltpu.VMEM((R, W2), F32)]
    res = pl.pallas_call(
        functools.partial(_s5_kernel, R=R, n_slab=n_slab, chained=chained, want_final=want_final),
        grid=(Q, nfs),
        in_specs=in_specs,
        out_specs=tuple(out_specs),
        out_shape=tuple(out_shape),
        scratch_shapes=scratch,
        compiler_params=_cparams(("parallel", "parallel")),
        name="s5_scan_chained" if chained else "s5_scan",
    )(*args)
    return res


def _ssd_kernel(*refs, TB, nb, has_h0, want_final):
    io = [refs[4 * d:4 * d + 4] for d in range(N_DIR)]
    refs = refs[8:]
    if has_h0:
        h0_ref = refs[0]
        refs = refs[1:]
    cw_ref, cb_ref, dtb_ref, arow_ref, sel_ref, drow_ref = refs[:6]
    refs = refs[6:]
    y_refs = refs[:2]
    refs = refs[2:]
    if want_final:
        hfin_ref = refs[0]
        refs = refs[1:]
    hst, xc_s, dt_s = refs
    CH = SSD_CHUNK
    P = SSD_HEADDIM
    j = pl.program_id(1)

    @pl.when(j == 0)
    def _():
        if has_h0:
            hst[...] = h0_ref[...]
        else:
            hst[...] = jnp.zeros_like(hst)

    li = lax.broadcasted_iota(jnp.int32, (CH, CH), 0)
    si = lax.broadcasted_iota(jnp.int32, (CH, CH), 1)
    tmats = ((si <= li).astype(F32), (si >= li).astype(F32))
    rows = lax.broadcasted_iota(jnp.int32, (TB, 1), 0)
    for d in range(N_DIR):
        xbc_ref, xp_ref, xn_ref, dt_ref = io[d]
        jj = j if d == 0 else nb - 1 - j
        x = xbc_ref[...]
        prev_row = xp_ref[7:8, :] * (jj > 0).astype(F32)
        next_row = xn_ref[0:1, :] * (jj < nb - 1).astype(F32)
        x_prev = jnp.where(rows == 0, prev_row, pltpu.roll(x, 1, 0))
        x_next = jnp.where(rows == TB - 1, next_row, pltpu.roll(x, TB - 1, 0))
        conv = cw_ref[0:1, :] * x_prev + cw_ref[1:2, :] * x + cw_ref[2:3, :] * x_next + cb_ref[...]
        xc_s[d] = conv * _sigmoid(conv)
        dtf = _softplus(dt_ref[...] + dtb_ref[...])
        dt_s[d] = _dot_exact_rhs(dtf, sel_ref[d])

    hrow = lax.broadcasted_iota(jnp.int32, (LANES, SSD_WIDTH), 0)
    e_head = jnp.where(lax.broadcasted_iota(jnp.int32, (LANES, SSD_WIDTH), 1) // P == hrow, 1.0, 0.0)
    hrow2 = lax.broadcasted_iota(jnp.int32, (LANES, SSD_HEADS * CH), 0)
    e_chunk = jnp.where(lax.broadcasted_iota(jnp.int32, (LANES, SSD_HEADS * CH), 1) // CH == hrow2, 1.0, 0.0)

    n_ch = TB // CH
    for i, d in [(i, d) for i in range(n_ch) for d in range(N_DIR)]:
        tmat = tmats[d]
        y_ref = y_refs[d]
        r0 = (i if d == 0 else n_ch - 1 - i) * CH
        dtc = dt_s[d, r0:r0 + CH, :]
        dA = dtc * arow_ref[d]
        cs = _dot_exact_lhs(tmat, dA)
        csT = cs.T
        dt_x = _dot_exact_rhs(dtc, e_head)
        cs_x = _dot_exact_rhs(cs, e_head)
        tot_x = jnp.sum(_dot_exact_rhs(dA, e_head), axis=0, keepdims=True)
        cs_xx = _dot_exact_rhs(cs, e_chunk)
        xs_all = xc_s[d, r0:r0 + CH, 0:SSD_WIDTH]
        xdt_all = xs_all * dt_x
        xd_all = xdt_all * jnp.exp(tot_x - cs_x)
        ecs_all = jnp.exp(cs_x)
        etot_all = jnp.exp(tot_x)
        Bm = xc_s[d, r0:r0 + CH, SSD_WIDTH:SSD_WIDTH + SSD_BC]
        Cm = xc_s[d, r0:r0 + CH, SSD_WIDTH + SSD_BC:SSD_WIDTH + 2 * SSD_BC]
        BmT = Bm.T
        for g in range(SSD_GROUPS):
            Cg = Cm[:, g * SSD_STATE:(g + 1) * SSD_STATE].astype(BF16)
            Bg = Bm[:, g * SSD_STATE:(g + 1) * SSD_STATE].astype(BF16)
            BgT = BmT[g * SSD_STATE:(g + 1) * SSD_STATE, :].astype(BF16)
            G = lax.dot_general(Cg, Bg, (((1,), (1,)), ((), ())), preferred_element_type=F32)
            for hh in range(SSD_HEADS // SSD_GROUPS):
                h = g * (SSD_HEADS // SSD_GROUPS) + hh
                hl = slice(h * P, (h + 1) * P)
                row = csT[h:h + 1, :]
                lm = jnp.exp(jnp.where(tmat > 0.0, cs_xx[:, h * CH:(h + 1) * CH] - row, -1e30))
                hprev = hst[d, h]
                y = jnp.dot((G * lm).astype(BF16), xdt_all[:, hl].astype(BF16), preferred_element_type=F32)
                y = y + jnp.dot(Cg, hprev.astype(BF16), preferred_element_type=F32) * ecs_all[:, hl]
                if d == 0:
                    y = y + drow_ref[:, hl] * xs_all[:, hl]
                y_ref[r0:r0 + CH, hl] = y
                hst[d, h] = etot_all[:, hl] * hprev + jnp.dot(BgT, xd_all[:, hl].astype(BF16),
                                                             preferred_element_type=F32)

    if want_final:
        @pl.when(j == nb - 1)
        def _():
            hfin_ref[...] = hst[...]


def _ssd_scan(xbc, dt, h0, conv_w, conv_b, dt_bias, a_log, d_skip, *, B, L, want_final):
    TB = min(512, L)
    nb = L // TB
    H = SSD_HEADS

    nrow8 = B * L // 8
    in_specs, args = [], []
    blks = (lambda b, j: b * nb + j, lambda b, j: b * nb + nb - 1 - j)
    for blk in blks:
        in_specs += [
            pl.BlockSpec((TB, SSD_CONV_CH), lambda b, j, blk=blk: (blk(b, j), 0)),
            pl.BlockSpec((8, SSD_CONV_CH), lambda b, j, blk=blk: (jnp.maximum(blk(b, j) * (TB // 8) - 1, 0), 0)),
            pl.BlockSpec((8, SSD_CONV_CH),
                         lambda b, j, blk=blk: (jnp.minimum((blk(b, j) + 1) * (TB // 8), nrow8 - 1), 0)),
            pl.BlockSpec((TB, DT_PAD), lambda b, j, blk=blk: (blk(b, j), 0)),
        ]
        args += [xbc, xbc, xbc, dt]
    if h0 is not None:
        in_specs.append(pl.BlockSpec((None, N_DIR, H, SSD_STATE, SSD_HEADDIM), lambda b, j: (b, 0, 0, 0, 0)))
        args.append(h0)
    dtb = jnp.pad(dt_bias.reshape(1, N_DIR * H), ((0, 0), (0, DT_PAD - N_DIR * H)))
    arow = jnp.pad(-jnp.exp(a_log), ((0, 0), (0, LANES - H))).reshape(N_DIR, 1, LANES)
    lane = jnp.arange(LANES)
    sel = jnp.stack([(lane[:, None] == (dd * H + lane[None, :])) & (lane[None, :] < H) for dd in range(N_DIR)]).astype(F32)
    drow = jnp.repeat(d_skip, SSD_HEADDIM).reshape(1, SSD_WIDTH)
    in_specs += [
        _full((3, SSD_CONV_CH)), _full((1, SSD_CONV_CH)), _full((1, DT_PAD)),
        _full((N_DIR, 1, LANES)), _full((N_DIR, LANES, LANES)), _full((1, SSD_WIDTH)),
    ]
    args += [conv_w, conv_b.reshape(1, SSD_CONV_CH), dtb, arow, sel, drow]
    out_shape = [jax.ShapeDtypeStruct((B * L, SSD_WIDTH), F32)] * N_DIR
    out_specs = [pl.BlockSpec((TB, SSD_WIDTH), lambda b, j, blk=blk: (blk(b, j), 0)) for blk in blks]
    if want_final:
        out_shape.append(jax.ShapeDtypeStruct((B, N_DIR, H, SSD_STATE, SSD_HEADDIM), F32))
        out_specs.append(pl.BlockSpec((None, N_DIR, H, SSD_STATE, SSD_HEADDIM), lambda b, j: (b, 0, 0, 0, 0)))
    return pl.pallas_call(
        functools.partial(_ssd_kernel, TB=TB, nb=nb, has_h0=h0 is not None, want_final=want_final),
        grid=(B, nb),
        in_specs=in_specs,
        out_specs=tuple(out_specs),
        out_shape=tuple(out_shape),
        scratch_shapes=[pltpu.VMEM((N_DIR, H, SSD_STATE, SSD_HEADDIM), F32),
                        pltpu.VMEM((N_DIR, TB, SSD_CONV_CH), F32), pltpu.VMEM((N_DIR, TB, DT_PAD), F32)],
        compiler_params=_cparams(("parallel", "arbitrary")),
        name="ssd_scan",
    )(*args)


def _rwkv_prep_kernel(rkv_ref, rp_ref, rn_ref, lo_ref, mu_ref, a0_ref, aup_ref, gup_ref, w0_ref, wup_ref,
                      kkw_ref, ka_ref, rk_ref, seg_ref,
                      r_ref, w0o_ref, w1o_ref, k_ref, v_ref, kk_ref, b_ref, g_ref, bonus_ref, *, nbs):
    tm = rkv_ref.shape[0]
    W = RWKV_WIDTH
    jj = pl.program_id(0) % nbs
    x = rkv_ref[...]
    prev_row = rp_ref[7:8, :] * (jj > 0).astype(F32)
    next_row = rn_ref[0:1, :] * (jj < nbs - 1).astype(F32)
    rows = lax.broadcasted_iota(jnp.int32, (tm, 1), 0)
    xp = jnp.where(rows == 0, prev_row, pltpu.roll(x, 1, 0))
    xn = jnp.where(rows == tm - 1, next_row, pltpu.roll(x, tm - 1, 0))
    x = x + mu_ref[0:1, :] * (xp - x) + mu_ref[1:2, :] * (xn - x)
    r = x[:, 0:W]
    k = x[:, W:2 * W]
    v = x[:, 2 * W:3 * W]
    lo = lo_ref[...]
    seg = seg_ref[...]

    def put(ref, val):
        ref[:, 0:W] = val
        if ref.shape[1] > W:
            ref[:, W:] = jnp.zeros((tm, ref.shape[1] - W), F32)

    a = _sigmoid(a0_ref[...] + _dot3(lo, aup_ref[...]))
    g_ref[...] = _dot3(_sigmoid(lo), gup_ref[...])
    tlo = jnp.tanh(lo)
    for d, o_ref in enumerate((w0o_ref, w1o_ref)):
        zw = w0_ref[d:d + 1, :] + _dot3(tlo, wup_ref[d])
        put(o_ref, jnp.exp(-DECAY_SCALE * _sigmoid(zw)))
    kk = k * kkw_ref[...]
    kk = kk * lax.rsqrt(jnp.maximum(_dot_exact_rhs(kk * kk, seg), 1e-24))
    k2 = k * (1.0 + (a - 1.0) * ka_ref[...])
    put(r_ref, r)
    put(k_ref, k2)
    put(v_ref, v)
    put(kk_ref, kk)
    put(b_ref, kk * a)
    bonus_ref[...] = _dot_exact_rhs(r * k2 * rk_ref[...], seg) * v


def _segment_ones(width, seg):
    i = jnp.arange(width)
    return (i[:, None] // seg == i[None, :] // seg).astype(F32)


def _rwkv_prep(rkv, lo, lp, seq_blocks, scan_width=RWKV_WIDTH, tm=TOKEN_BLOCK):
    ntok = rkv.shape[0]
    W = RWKV_WIDTH
    nrow8 = ntok // 8
    nbs_of = seq_blocks
    widths = (scan_width,) * 7 + (W, W)
    pad_rows = lambda t, r0: jnp.zeros((LO_PAD, W), F32).at[r0:r0 + t.shape[0]].set(t)
    aup = pad_rows(lp['rwkv_a_up'], N_DIR * W_RANK)
    gup = pad_rows(lp['rwkv_g_up'], N_DIR * W_RANK + A_RANK)
    wup = jnp.stack([pad_rows(lp['rwkv_w_up'][d], d * W_RANK) for d in range(N_DIR)])
    row = lambda t: t.reshape(1, W)
    outs = tuple(jax.ShapeDtypeStruct((ntok, w), F32) for w in widths)
    return pl.pallas_call(
        functools.partial(_rwkv_prep_kernel, nbs=nbs_of),
        grid=(ntok // tm,),
        in_specs=[
            pl.BlockSpec((tm, 3 * W), lambda i: (i, 0)),
            pl.BlockSpec((8, 3 * W), lambda i: (jnp.maximum(i * (tm // 8) - 1, 0), 0)),
            pl.BlockSpec((8, 3 * W), lambda i: (jnp.minimum((i + 1) * (tm // 8), nrow8 - 1), 0)),
            pl.BlockSpec((tm, LO_PAD), lambda i: (i, 0)),
            _full((2, 3 * W)), _full((1, W)), _full((LO_PAD, W)), _full((LO_PAD, W)), _full((2, W)),
            _full((N_DIR, LO_PAD, W)), _full((1, W)), _full((1, W)), _full((1, W)), _full((W, W)),
        ],
        out_specs=tuple(pl.BlockSpec((tm, w), lambda i: (i, 0)) for w in widths),
        out_shape=outs,
        compiler_params=_cparams(("parallel",)),
        name="rwkv_prep",
    )(rkv, rkv, rkv, lo, lp['rwkv_mu'], row(lp['rwkv_a0']), aup, gup, lp['rwkv_w0'], wup,
      row(lp['rwkv_k_k']), row(lp['rwkv_k_a']), row(lp['rwkv_r_k']), lp['seg'])


def _rwkv_scan_kernel(rf, rm, wf, wm, kf, km, kkf, kkm, bf, bm, vf, vm, mask_ref, s0_ref, yf_ref, yb_ref, sfin_ref,
                      S, SA, OPS, VTS, *, TBLK, V, nblk):
    K = RWKV_HEADSIZE
    VH = V // 2
    tb = pl.program_id(1)

    @pl.when(tb == 0)
    def _():
        S[...] = s0_ref[...]

    is_fwd = mask_ref[...] > 0.5
    fwd_k = jnp.broadcast_to(is_fwd, (K, LANES))
    fwd_v = jnp.broadcast_to(is_fwd, (V, LANES))

    def pick(t, _):
        tm_ = TBLK - 1 - t
        for j, (f, m) in enumerate(((rf, rm), (wf, wm), (kf, km), (kkf, kkm), (bf, bm))):
            OPS[t, j] = jnp.where(fwd_k, f[t], m[tm_])
        VTS[t] = jnp.where(fwd_v, vf[t], vm[tm_])
        return 0
    lax.fori_loop(0, TBLK, pick, 0)
    R_, W_, K_, KK_, B_ = range(5)

    def bc(j, t, kx):
        return jnp.broadcast_to(OPS[t, j, pl.ds(kx, 1), :], (VH, LANES))

    def tree(parts):
        return (parts[0] + parts[1]) + (parts[2] + parts[3])

    for hv in range(2):
        rows = pl.ds(hv * VH, VH)
        accs = [None] * 4
        for kx in range(K):
            p = S[kx, rows, :] * bc(KK_, 0, kx)
            accs[kx % 4] = p if accs[kx % 4] is None else accs[kx % 4] + p
        SA[hv] = tree(accs)

    def step(t, _):
        tn = jnp.minimum(t + 1, TBLK - 1)
        tm_ = TBLK - 1 - t
        for hv in range(2):
            rows = pl.ds(hv * VH, VH)
            sa = SA[hv]
            vt = VTS[t, rows, :]
            yacc = [None] * 4
            sacc = [None] * 4
            for kx in range(K):
                s_new = S[kx, rows, :] * bc(W_, t, kx) - sa * bc(B_, t, kx) + vt * bc(K_, t, kx)
                S[kx, rows, :] = s_new
                py = s_new * bc(R_, t, kx)
                ps = s_new * bc(KK_, tn, kx)
                a = kx % 4
                yacc[a] = py if yacc[a] is None else yacc[a] + py
                sacc[a] = ps if sacc[a] is None else sacc[a] + ps
            y = tree(yacc)
            yf_ref[t, rows, :] = y
            yb_ref[tm_, rows, :] = y
            SA[hv] = tree(sacc)
        return 0
    lax.fori_loop(0, TBLK, step, 0)

    @pl.when(tb == nblk - 1)
    def _():
        sfin_ref[...] = S[...]


def _rwkv_scan(r, w, k, kk, b, v, fwd_mask, s0):
    L, K, NL = r.shape
    V = v.shape[1]
    TBLK = min(32, L)
    nblk = L // TBLK
    ngrp = NL // LANES
    kf = pl.BlockSpec((TBLK, K, LANES), lambda g, t: (t, 0, g))
    km = pl.BlockSpec((TBLK, K, LANES), lambda g, t: (nblk - 1 - t, 0, g))
    vf = pl.BlockSpec((TBLK, V, LANES), lambda g, t: (t, 0, g))
    vm = pl.BlockSpec((TBLK, V, LANES), lambda g, t: (nblk - 1 - t, 0, g))
    sspec = pl.BlockSpec((K, V, LANES), lambda g, t: (0, 0, g))
    return pl.pallas_call(
        functools.partial(_rwkv_scan_kernel, TBLK=TBLK, V=V, nblk=nblk),
        grid=(ngrp, nblk),
        in_specs=[kf, km] * 5 + [vf, vm, pl.BlockSpec((1, LANES), lambda g, t: (0, g)), sspec],
        out_specs=(vf, vm, sspec),
        out_shape=(jax.ShapeDtypeStruct((L, V, NL), F32), jax.ShapeDtypeStruct((L, V, NL), F32),
                   jax.ShapeDtypeStruct((K, V, NL), F32)),
        scratch_shapes=[pltpu.VMEM((K, V, LANES), F32), pltpu.VMEM((2, V // 2, LANES), F32),
                        pltpu.VMEM((TBLK, 5, K, LANES), F32), pltpu.VMEM((TBLK, V, LANES), F32)],
        compiler_params=_cparams(("parallel", "arbitrary")),
        name="rwkv_scan",
    )(r, r, w, w, k, k, kk, kk, b, b, v, v, fwd_mask, s0)


KQ = 4
CH = LANES // KQ


def _rwkv_scan2_kernel(rf, rm, wf, wm, kf, km, kkf, kkm, bf, bm, vf, vm, s0_ref, yf_ref, yb_ref, S, YQ, SA, *, TBLK):
    V = RWKV_HEADSIZE
    NQ = RWKV_HEADSIZE // KQ
    tb = pl.program_id(0)

    @pl.when(tb == 0)
    def _():
        S[...] = s0_ref[...]

    VH = V // 2

    def bc(ref, tt, q):
        return jnp.broadcast_to(ref[tt, pl.ds(q, 1), :], (VH, LANES))

    def all_quarters(x):
        return (x + pltpu.roll(x, CH, 1)) + (pltpu.roll(x, 2 * CH, 1) + pltpu.roll(x, 3 * CH, 1))

    dirs = ((rf, wf, kf, kkf, bf, vf), (rm, wm, km, kkm, bm, vm))
    time_of = (lambda t: t, lambda t: TBLK - 1 - t)
    chains = [(d, hv) for d in range(N_DIR) for hv in range(2)]

    def first_sa(d, hv):
        kk_ = dirs[d][3]
        tt = time_of[d](0)
        acc = [None, None]
        for q in range(NQ):
            p = S[d, q, pl.ds(hv * VH, VH), :] * bc(kk_, tt, q)
            acc[q % 2] = p if acc[q % 2] is None else acc[q % 2] + p
        return acc[0] + acc[1]

    TG = min(16, TBLK)
    lane_r = lax.broadcasted_iota(jnp.int32, (2 * LANES, CH), 0) % CH
    fold2 = jnp.where(lax.broadcasted_iota(jnp.int32, (2 * LANES, CH), 1) == lane_r, 1.0, 0.0).astype(BF16)

    last = len(chains) - 1
    for c, (d, hv) in enumerate(chains):
        part = first_sa(d, hv)
        SA[c] = part if c == last else all_quarters(part)

    def step(t, _):
        sa_last = all_quarters(SA[last])
        for c, (d, hv) in enumerate(chains):
            r_, w_, k_, kk_, b_, v_ = dirs[d]
            tt = time_of[d](t)
            tn = time_of[d](jnp.minimum(t + 1, TBLK - 1))
            rows = pl.ds(hv * VH, VH)
            sa = sa_last if c == last else SA[c]
            vt = v_[tt, rows, :]
            yacc = None
            sacc = None
            for q in range(NQ):
                s_new = S[d, q, rows, :] * bc(w_, tt, q) - sa * bc(b_, tt, q) + vt * bc(k_, tt, q)
                S[d, q, rows, :] = s_new
                py = s_new * bc(r_, tt, q)
                ps = s_new * bc(kk_, tn, q)
                yacc = py if yacc is None else yacc + py
                sacc = ps if sacc is None else sacc + ps
            YQ[d, tt, rows, :] = yacc
            SA[c] = sacc if c == last else all_quarters(sacc)
        return 0
    lax.fori_loop(0, TBLK, step, 0)

    def finish(i, _):
        for d, y_ in enumerate((yf_ref, yb_ref)):
            hi, lo = _split_bf16(YQ[d, pl.ds(i * TG, TG)].reshape(TG * V, LANES))
            y = jnp.dot(jnp.concatenate([hi, lo], axis=-1), fold2, preferred_element_type=F32)
            y_[pl.ds(i * TG, TG)] = y.reshape(TG, V, CH)
        return 0
    lax.fori_loop(0, TBLK // TG, finish, 0)


def _rwkv_scan2(r, w0, w1, k, kk, b, v, s0):
    L, NQ, _ = r.shape
    V = RWKV_HEADSIZE
    TBLK = min(64, L)
    nblk = L // TBLK
    fspec = pl.BlockSpec((TBLK, NQ, LANES), lambda t: (t, 0, 0))
    mspec = pl.BlockSpec((TBLK, NQ, LANES), lambda t: (nblk - 1 - t, 0, 0))
    vfspec = pl.BlockSpec((TBLK, V, LANES), lambda t: (t, 0, 0))
    vmspec = pl.BlockSpec((TBLK, V, LANES), lambda t: (nblk - 1 - t, 0, 0))
    yfspec = pl.BlockSpec((TBLK, V, CH), lambda t: (t, 0, 0))
    ymspec = pl.BlockSpec((TBLK, V, CH), lambda t: (nblk - 1 - t, 0, 0))
    yf, yb = pl.pallas_call(
        functools.partial(_rwkv_scan2_kernel, TBLK=TBLK),
        grid=(nblk,),
        in_specs=[fspec, mspec] * 5 + [vfspec, vmspec, _full((N_DIR, NQ, V, LANES))],
        out_specs=(yfspec, ymspec),
        out_shape=(jax.ShapeDtypeStruct((L, V, CH), F32), jax.ShapeDtypeStruct((L, V, CH), F32)),
        scratch_shapes=[pltpu.VMEM((N_DIR, NQ, V, LANES), F32), pltpu.VMEM((N_DIR, TBLK, V, LANES), F32),
                        pltpu.VMEM((2 * N_DIR, V // 2, LANES), F32)],
        compiler_params=_cparams(("arbitrary",)),
        name="rwkv_scan2",
    )(r, r, w0, w1, k, k, kk, kk, b, b, v, v, s0)
    return yf, yb


HEADS_PAD = 8
SCAN_WIDTH = HEADS_PAD * RWKV_HEADSIZE


def _pad_chains(x, B):
    return x if B * HEADS_PAD == CH else jnp.pad(x, ((0, 0),) * (x.ndim - 1) + ((0, CH - B * HEADS_PAD),))


def _to_scan2_k(t, B, L):
    x = t.reshape(B, L, HEADS_PAD, RWKV_HEADSIZE).transpose(1, 3, 0, 2)
    x = _pad_chains(x.reshape(L, RWKV_HEADSIZE // KQ, KQ, B * HEADS_PAD), B)
    return x.reshape(L, RWKV_HEADSIZE // KQ, LANES)


def _to_scan2_v(t, B, L):
    x = t.reshape(B, L, HEADS_PAD, RWKV_HEADSIZE).transpose(1, 3, 0, 2).reshape(L, RWKV_HEADSIZE, B * HEADS_PAD)
    return jnp.tile(_pad_chains(x, B), (1, 1, KQ))


def _from_scan2(y, B, L):
    y = y[:, :, :B * HEADS_PAD]
    return y.reshape(L, RWKV_HEADSIZE, B, HEADS_PAD).transpose(2, 0, 3, 1).reshape(B * L, SCAN_WIDTH)


def _state_to_scan2(s):
    B = s.shape[0]
    s = jnp.pad(s, ((0, 0), (0, 0), (0, HEADS_PAD - RWKV_HEADS), (0, 0), (0, 0)))
    x = s.transpose(1, 4, 3, 0, 2).reshape(N_DIR, RWKV_HEADSIZE // KQ, KQ, RWKV_HEADSIZE, B * HEADS_PAD)
    x = _pad_chains(x, B)
    return x.transpose(0, 1, 3, 2, 4).reshape(N_DIR, RWKV_HEADSIZE // KQ, RWKV_HEADSIZE, LANES)


def _ctx_lanes(B):
    chains = B * N_DIR * RWKV_HEADS
    return chains, -(-chains // LANES) * LANES


def _to_ctx(t_fwd, t_bwd, B, L):
    chains, lanes = _ctx_lanes(B)
    if t_bwd is t_fwd:
        x = t_fwd.reshape(B, L, RWKV_HEADS, RWKV_HEADSIZE).transpose(1, 3, 0, 2)
        x = jnp.broadcast_to(x[:, :, :, None, :], (L, RWKV_HEADSIZE, B, N_DIR, RWKV_HEADS))
    else:
        x = jnp.stack([t_fwd, t_bwd], 0).reshape(N_DIR, B, L, RWKV_HEADS, RWKV_HEADSIZE).transpose(2, 4, 1, 0, 3)
    x = x.reshape(L, RWKV_HEADSIZE, chains)
    return x if lanes == chains else jnp.pad(x, ((0, 0), (0, 0), (0, lanes - chains)))


def _ctx_fwd_mask(B):
    chains, lanes = _ctx_lanes(B)
    lane = jnp.arange(lanes)
    return (((lane // RWKV_HEADS) % N_DIR == 0) & (lane < chains)).astype(F32).reshape(1, lanes)


def _from_ctx(y, B, L, d):
    chains, _ = _ctx_lanes(B)
    y = y[:, :, :chains].reshape(L, RWKV_HEADSIZE, B, N_DIR, RWKV_HEADS)[:, :, :, d]
    return y.transpose(2, 0, 3, 1).reshape(B * L, RWKV_WIDTH)


def _ctx_state_from(x, B):
    chains, _ = _ctx_lanes(B)
    x = x[:, :, :chains].reshape(RWKV_HEADSIZE, RWKV_HEADSIZE, B, N_DIR, RWKV_HEADS)
    return x.transpose(2, 3, 4, 1, 0)


def _post_kernel(x_ref, mod_ref, ys5_ref, yf_ref, yb_ref, z_ref, rf_ref, rb_ref, bonus_ref, g_ref,
                 gluw_ref, glub_ref, ssdg_ref, lng_ref, lnb_ref, seg_ref, wout_ref, n2g_ref, rw_ref,
                 x1_ref, hb_ref, aff_ref):
    m = mod_ref[...]
    D = D_MODEL
    zg = _gelu_tanh(ys5_ref[...])
    gate = jnp.dot(zg.astype(BF16), gluw_ref[...], preferred_element_type=F32) + glub_ref[...]
    y_a = zg * _sigmoid(gate)
    z = z_ref[...]
    yb = (yf_ref[...] + yb_ref[...]) * (z * _sigmoid(z))
    y_b = yb * lax.rsqrt(jnp.mean(yb * yb, axis=-1, keepdims=True) + EPS) * ssdg_ref[...]
    seg = seg_ref[...] * (1.0 / RWKV_HEADSIZE)
    yr = rf_ref[...] + rb_ref[...]
    mean = _dot_exact_rhs(yr, seg)
    cen = yr - mean
    var = jnp.dot((cen * cen).astype(BF16), seg.astype(BF16), preferred_element_type=F32)
    yn = cen * lax.rsqrt(var + GN_EPS) * lng_ref[...] + lnb_ref[...]
    y_c = (yn + bonus_ref[...]) * g_ref[...]
    o = jnp.dot(y_a.astype(BF16), wout_ref[0:S5_WIDTH, :], preferred_element_type=F32)
    o = o + jnp.dot(y_b.astype(BF16), wout_ref[S5_WIDTH:S5_WIDTH + SSD_WIDTH, :], preferred_element_type=F32)
    o = o + jnp.dot(y_c.astype(BF16), wout_ref[S5_WIDTH + SSD_WIDTH:, :], preferred_element_type=F32)
    x1 = x_ref[...] + m[:, 2 * D:3 * D] * o
    x1_ref[...] = x1
    h2 = x1 * lax.rsqrt(jnp.mean(x1 * x1, axis=-1, keepdims=True) + EPS) * n2g_ref[...]
    h2 = h2 * (1.0 + m[:, 4 * D:5 * D]) + m[:, 3 * D:4 * D]
    hb_ref[...] = h2.astype(BF16)
    logits = _dot3(rw_ref[...], h2, (((1,), (1,)), ((), ())))
    mx = jnp.max(logits, axis=0, keepdims=True)
    ex = jnp.exp(logits - mx)
    aff_ref[...] = ex / jnp.sum(ex, axis=0, keepdims=True)


def _post_mixer(x, mod3, row_of_block, ys5, yssd, z, rf, rb, bonus, g, lp, tm):
    ntok = x.shape[0]
    tok = lambda w: pl.BlockSpec((tm, w), lambda i: (i, 0))
    row = lambda t: t.reshape(1, -1)
    W = RWKV_WIDTH
    return pl.pallas_call(
        _post_kernel,
        grid=(ntok // tm,),
        in_specs=[
            tok(D_MODEL),
            pl.BlockSpec((None, 1, 6 * D_MODEL), lambda i: (row_of_block(i), 0, 0)),
            tok(S5_WIDTH),
            tok(SSD_WIDTH), tok(SSD_WIDTH),
            tok(SSD_WIDTH), tok(W), tok(W), tok(W), tok(W),
            _full((S5_WIDTH, S5_WIDTH)), _full((1, S5_WIDTH)), _full((1, SSD_WIDTH)), _full((1, W)), _full((1, W)),
            _full((W, W)), _full((D_MODEL, D_MODEL)), _full((1, D_MODEL)), _full((N_EXPERTS, D_MODEL)),
        ],
        out_specs=(tok(D_MODEL), tok(D_MODEL), pl.BlockSpec((N_EXPERTS, tm), lambda i: (0, i))),
        out_shape=(jax.ShapeDtypeStruct((ntok, D_MODEL), F32), jax.ShapeDtypeStruct((ntok, D_MODEL), BF16),
                   jax.ShapeDtypeStruct((N_EXPERTS, ntok), F32)),
        compiler_params=_cparams(("parallel",)),
        name="post_mixer",
    )(x, mod3, ys5, yssd[0], yssd[1], z, rf, rb, bonus, g,
      lp['s5_glu_w'], row(lp['s5_glu_b']), row(lp['ssd_norm_g']), row(lp['rwkv_ln_g']),
      row(lp['rwkv_ln_b']), lp['seg'], lp['w_out'], row(lp['norm2_g']),
      lp['router_w'].T)


def _select_kernel(aff_ref, slot_ref, affo_ref, *, n, cap, bg):
    for g in range(bg):
        _select_one(aff_ref[:, g * n:(g + 1) * n], slot_ref.at[g], affo_ref.at[g], n, cap)


def _select_one(a, slot_ref, affo_ref, n, cap):
    E = N_EXPERTS
    bits = pltpu.bitcast(a, jnp.int32)
    thr = jnp.zeros((E, 1), jnp.int32)
    capf = float(cap)
    for bit in range(30, -1, -1):
        cand = thr | (1 << bit)
        cnt = jnp.sum(jnp.where(bits >= cand, 1.0, 0.0), axis=1, keepdims=True)
        thr = jnp.where(cnt >= capf, cand, thr)
    gt = bits > thr
    eq = bits == thr
    need = capf - jnp.sum(jnp.where(gt, 1.0, 0.0), axis=1, keepdims=True)
    CW = min(256, n)
    ui = lax.broadcasted_iota(jnp.int32, (CW, CW), 0)
    uj = lax.broadcasted_iota(jnp.int32, (CW, CW), 1)
    upper = jnp.where(ui < uj, 1.0, 0.0).astype(BF16)

    def excl_cumsum(mask_f):
        outs = []
        off = jnp.zeros((E, 1), F32)
        for c in range(n // CW):
            mc = mask_f[:, c * CW:(c + 1) * CW]
            outs.append(jnp.dot(mc.astype(BF16), upper, preferred_element_type=F32) + off)
            off = off + jnp.sum(mc, axis=1, keepdims=True)
        return jnp.concatenate(outs, axis=1)

    eq_rank = excl_cumsum(jnp.where(eq, 1.0, 0.0))
    sel = jnp.where(gt, 1.0, jnp.where(eq, jnp.where(eq_rank < need, 1.0, 0.0), 0.0))
    pos = excl_cumsum(sel)
    slot = jnp.where(sel > 0.0, pos, -1.0)
    for e in range(E):
        slot_ref[e] = slot[e:e + 1, :]
        affo_ref[e] = a[e:e + 1, :]


def _select(aff, B, n, cap):
    bg = 4 if (B % 4 == 0 and n <= 512) else (2 if B % 2 == 0 else 1)
    spec = pl.BlockSpec((bg, N_EXPERTS, 1, n), lambda b: (b, 0, 0, 0))
    return pl.pallas_call(
        functools.partial(_select_kernel, n=n, cap=cap, bg=bg),
        grid=(B // bg,),
        in_specs=[pl.BlockSpec((N_EXPERTS, bg * n), lambda b: (0, b))],
        out_specs=(spec, spec),
        out_shape=(jax.ShapeDtypeStruct((B, N_EXPERTS, 1, n), F32), jax.ShapeDtypeStruct((B, N_EXPERTS, 1, n), F32)),
        compiler_params=_cparams(("parallel",)),
        name="ec_select",
    )(aff)


def _slot_block_range(slots_f, cap, SB):
    lo = jnp.min(jnp.where(slots_f >= 0.0, slots_f, float(cap))).astype(jnp.int32)
    hi = jnp.max(slots_f).astype(jnp.int32)
    first = lo // SB
    count = jnp.where(hi >= 0, hi // SB - first + 1, 0)
    return first, count


EC_SLOT_BLOCK = 256


def _one_hot_all(slot_ref, aff_ref, n, cap):
    srow = lax.broadcasted_iota(jnp.int32, (cap, n), 0).astype(F32)
    ohs, gates = [], []
    for e in range(N_EXPERTS):
        hit = slot_ref[e] == srow
        ohs.append(jnp.where(hit, 1.0, 0.0).astype(BF16))
        if aff_ref is not None:
            gates.append(jnp.sum(jnp.where(hit, aff_ref[e], 0.0), axis=1, keepdims=True))
    return jnp.concatenate(ohs, axis=0), (jnp.concatenate(gates, axis=0) if gates else None)


def _gather_all_kernel(hb_ref, slot_ref, aff_ref, xs_ref, gate_ref, *, n, cap):
    oh, gate = _one_hot_all(slot_ref, aff_ref, n, cap)
    xs = jnp.dot(oh, hb_ref[...], preferred_element_type=F32)
    xs_ref[...] = xs.astype(BF16).reshape(N_EXPERTS, cap, D_MODEL)
    gate_ref[...] = gate.reshape(N_EXPERTS, cap, 1)


def _gather_kernel(hb_ref, slot_ref, aff_ref, xs_ref, gate_ref, acc, gacc, *, n, cap):
    NC = min(512, n)
    SB = min(EC_SLOT_BLOCK // 2, cap)
    acc[...] = jnp.zeros_like(acc)
    gacc[...] = jnp.zeros_like(gacc)
    srow = lax.broadcasted_iota(jnp.int32, (SB, NC), 0).astype(F32)
    seen = jnp.int32(0)
    for c in range(n // NC):
        sl = slot_ref[:, c * NC:(c + 1) * NC]
        cnt = jnp.sum(jnp.where(sl >= 0.0, 1.0, 0.0)).astype(jnp.int32)
        first = seen // SB
        count = jnp.where(cnt > 0, (seen + cnt - 1) // SB - first + 1, 0)
        seen = seen + cnt
        for j in range(min(cap // SB, NC // SB + 1)):
            @pl.when(j < count)
            def _(c=c, j=j, sl=sl, first=first):
                base = pl.multiple_of((first + j) * SB, SB)
                hit = (sl - base.astype(F32)) == srow
                oh = jnp.where(hit, 1.0, 0.0).astype(BF16)
                acc[pl.ds(base, SB), :] += jnp.dot(oh, hb_ref[c * NC:(c + 1) * NC, :], preferred_element_type=F32)
                gacc[pl.ds(base, SB), :] += jnp.sum(jnp.where(hit, aff_ref[:, c * NC:(c + 1) * NC], 0.0), axis=1,
                                                    keepdims=True)
    xs_ref[...] = acc[...].astype(BF16)
    gate_ref[...] = gacc[...]


def _gather(hb, slot4, aff4, B, n, cap):
    E = N_EXPERTS
    if E * cap <= 512:
        all_spec = pl.BlockSpec((None, E, 1, n), lambda b: (b, 0, 0, 0))
        return pl.pallas_call(
            functools.partial(_gather_all_kernel, n=n, cap=cap),
            grid=(B,),
            in_specs=[pl.BlockSpec((n, D_MODEL), lambda b: (b, 0)), all_spec, all_spec],
            out_specs=(pl.BlockSpec((None, E, cap, D_MODEL), lambda b: (b, 0, 0, 0)),
                       pl.BlockSpec((None, E, cap, 1), lambda b: (b, 0, 0, 0))),
            out_shape=(jax.ShapeDtypeStruct((B, E, cap, D_MODEL), BF16), jax.ShapeDtypeStruct((B, E, cap, 1), F32)),
            compiler_params=_cparams(("parallel",)),
            name="ec_gather_all",
        )(hb, slot4, aff4)
    return pl.pallas_call(
        functools.partial(_gather_kernel, n=n, cap=cap),
        grid=(B, E),
        in_specs=[
            pl.BlockSpec((n, D_MODEL), lambda b, e: (b, 0)),
            pl.BlockSpec((None, None, 1, n), lambda b, e: (b, e, 0, 0)),
            pl.BlockSpec((None, None, 1, n), lambda b, e: (b, e, 0, 0)),
        ],
        out_specs=(pl.BlockSpec((None, None, cap, D_MODEL), lambda b, e: (b, e, 0, 0)),
                   pl.BlockSpec((None, None, cap, 1), lambda b, e: (b, e, 0, 0))),
        out_shape=(jax.ShapeDtypeStruct((B, E, cap, D_MODEL), BF16), jax.ShapeDtypeStruct((B, E, cap, 1), F32)),
        scratch_shapes=[pltpu.VMEM((cap, D_MODEL), F32), pltpu.VMEM((cap, 1), F32)],
        compiler_params=_cparams(("parallel", "arbitrary")),
        name="ec_gather",
    )(hb, slot4, aff4)


def _ffn_kernel(xs_ref, gate_ref, w1_ref, w3_ref, w2_ref, o_ref):
    bg, cap, _ = xs_ref.shape
    x = xs_ref[...].reshape(bg * cap, D_MODEL)
    h1 = jnp.dot(x, w1_ref[...], preferred_element_type=F32)
    h3 = jnp.dot(x, w3_ref[...], preferred_element_type=F32)
    hid = (h1 * _sigmoid(h1) * h3).astype(BF16)
    o = jnp.dot(hid, w2_ref[...], preferred_element_type=F32) * gate_ref[...].reshape(bg * cap, 1)
    o_ref[...] = o.astype(BF16).reshape(bg, cap, D_MODEL)


def _expert_ffn(xs, gate, w1, w3, w2, bg):
    B, E, cap, _ = xs.shape
    return pl.pallas_call(
        _ffn_kernel,
        grid=(E, B // bg),
        in_specs=[
            pl.BlockSpec((bg, None, cap, D_MODEL), lambda e, b: (b, e, 0, 0)),
            pl.BlockSpec((bg, None, cap, 1), lambda e, b: (b, e, 0, 0)),
            pl.BlockSpec((None, D_MODEL, D_EXPERT), lambda e, b: (e, 0, 0)),
            pl.BlockSpec((None, D_MODEL, D_EXPERT), lambda e, b: (e, 0, 0)),
            pl.BlockSpec((None, D_EXPERT, D_MODEL), lambda e, b: (e, 0, 0)),
        ],
        out_specs=pl.BlockSpec((bg, None, cap, D_MODEL), lambda e, b: (b, e, 0, 0)),
        out_shape=jax.ShapeDtypeStruct((B, E, cap, D_MODEL), BF16),
        compiler_params=_cparams(("parallel", "parallel")),
        name="ec_ffn",
    )(xs, gate, w1, w3, w2)


_TN_DIMS = (((0,), (0,)), ((), ()))


def _ec_residual(x1_ref, mod_ref, fg_ref, out_ref, ffn, final):
    x2 = x1_ref[...] + mod_ref[:, 5 * D_MODEL:6 * D_MODEL] * ffn
    if final:
        x2 = x2 * lax.rsqrt(jnp.mean(x2 * x2, axis=-1, keepdims=True) + EPS) * fg_ref[...]
    out_ref[...] = x2


def _scatter_all_kernel(slot_ref, o_ref, x1_ref, mod_ref, fg_ref, out_ref, *, n, cap, final):
    oh, _ = _one_hot_all(slot_ref, None, n, cap)
    ffn = lax.dot_general(oh, o_ref[...].reshape(N_EXPERTS * cap, D_MODEL), _TN_DIMS, preferred_element_type=F32)
    _ec_residual(x1_ref, mod_ref, fg_ref, out_ref, ffn, final)


def _scatter_kernel(slot_ref, o_ref, x1_ref, mod_ref, fg_ref, out_ref, acc, *, cap, final):
    e = pl.program_id(2)
    tn = x1_ref.shape[0]

    @pl.when(e == 0)
    def _():
        acc[...] = jnp.zeros_like(acc)

    sl = slot_ref[...]
    SB = min(EC_SLOT_BLOCK, cap)
    srow = lax.broadcasted_iota(jnp.int32, (SB, tn), 0).astype(F32)
    first, count = _slot_block_range(sl, cap, SB)
    for j in range(min(cap // SB, tn // SB + 1)):
        @pl.when(j < count)
        def _(j=j):
            base = pl.multiple_of((first + j) * SB, SB)
            oh = jnp.where((sl - base.astype(F32)) == srow, 1.0, 0.0).astype(BF16)
            acc[...] += lax.dot_general(oh, o_ref[pl.ds(base, SB), :], _TN_DIMS, preferred_element_type=F32)

    @pl.when(e == N_EXPERTS - 1)
    def _():
        _ec_residual(x1_ref, mod_ref, fg_ref, out_ref, acc[...], final)


def _scatter(slot4, o, x1, mod3, mod_row, final_g, B, n, cap, final):
    E = N_EXPERTS
    if E * cap <= 512:
        return pl.pallas_call(
            functools.partial(_scatter_all_kernel, n=n, cap=cap, final=final),
            grid=(B,),
            in_specs=[
                pl.BlockSpec((None, E, 1, n), lambda b: (b, 0, 0, 0)),
                pl.BlockSpec((None, E, cap, D_MODEL), lambda b: (b, 0, 0, 0)),
                pl.BlockSpec((n, D_MODEL), lambda b: (b, 0)),
                pl.BlockSpec((None, 1, 6 * D_MODEL), lambda b: (mod_row(b), 0, 0)),
                _full((1, D_MODEL)),
            ],
            out_specs=pl.BlockSpec((n, D_MODEL), lambda b: (b, 0)),
            out_shape=jax.ShapeDtypeStruct((B * n, D_MODEL), F32),
            compiler_params=_cparams(("parallel",)),
            name="ec_scatter_all",
        )(slot4, o, x1, mod3, final_g.reshape(1, D_MODEL))
    tn = min(1024, n)
    nt = n // tn
    return pl.pallas_call(
        functools.partial(_scatter_kernel, cap=cap, final=final),
        grid=(B, nt, N_EXPERTS),
        in_specs=[
            pl.BlockSpec((None, None, 1, tn), lambda b, t, e: (b, e, 0, t)),
            pl.BlockSpec((None, None, cap, D_MODEL), lambda b, t, e: (b, e, 0, 0)),
            pl.BlockSpec((tn, D_MODEL), lambda b, t, e: (b * nt + t, 0)),
            pl.BlockSpec((None, 1, 6 * D_MODEL), lambda b, t, e: (mod_row(b), 0, 0)),
            _full((1, D_MODEL)),
        ],
        out_specs=pl.BlockSpec((tn, D_MODEL), lambda b, t, e: (b * nt + t, 0)),
        out_shape=jax.ShapeDtypeStruct((B * n, D_MODEL), F32),
        scratch_shapes=[pltpu.VMEM((tn, D_MODEL), F32)],
        compiler_params=_cparams(("parallel", "parallel", "arbitrary")),
        name="ec_scatter",
    )(slot4, o, x1, mod3, final_g.reshape(1, D_MODEL))


def _expert_choice(hb, aff, x1, mod3, mod_row, final_g, w1, w3, w2, B, n, final):
    cap = EC_FACTOR * n // N_EXPERTS
    slot4, aff4 = _select(aff, B, n, cap)
    xs, gate = _gather(hb, slot4, aff4, B, n, cap)
    bg = max(1, min(B, MATMUL_TOKEN_BLOCK // cap))
    o = _expert_ffn(xs, gate, w1, w3, w2, bg)
    return _scatter(slot4, o, x1, mod3, mod_row, final_g, B, n, cap, final)


def kernel(x_prompt, x_sample, c, state_s5_re, state_s5_im, state_ssd, state_rwkv, c_ctx, ada_w, ada_b, norm1_g, norm2_g, w_in, w_out, s5_a_re, s5_a_im, s5_log_dt, s5_b_re, s5_b_im, s5_c_re, s5_c_im, s5_d, s5_glu_w, s5_glu_b, ssd_conv_w, ssd_conv_b, ssd_a_log, ssd_dt_bias, ssd_d, ssd_norm_g, rwkv_mu, rwkv_w0, rwkv_w_up, rwkv_a0, rwkv_a_up, rwkv_g_up, rwkv_k_k, rwkv_k_a, rwkv_r_k, rwkv_ln_g, rwkv_ln_b, router_w, exp_w1, exp_w3, exp_w2, final_g):
    Bp, Lp, D = x_prompt.shape
    Bs, Ls, _ = x_sample.shape
    depth = ada_w.shape[0]
    Np, Ns = Bp * Lp, Bs * Ls
    tm = TOKEN_BLOCK
    grid_rows = Ls // GRID_W
    nfs = S5_GROUPS // 8
    assert Lp % tm == 0 and Ls % tm == 0 and Lp % SSD_CHUNK == 0

    n_rows = 1 + Bs
    rows_pad = -(-n_rows // 8) * 8
    cond = jnp.zeros((rows_pad, D), F32).at[0].set(c_ctx).at[1:n_rows].set(c)
    mod = _modulation(cond, ada_w, ada_b)
    tmm = MATMUL_TOKEN_BLOCK if (Np % MATMUL_TOKEN_BLOCK == 0 and Ls % MATMUL_TOKEN_BLOCK == 0) else tm
    s_blocks_per_req = Ls // tmm
    row_p = lambda i: 0
    row_s = lambda i: 1 + i // s_blocks_per_req

    ab_re, ab_im, bb_re, bb_im = _s5_discretize(s5_a_re, s5_a_im, s5_log_dt, s5_b_re, s5_b_im)
    w_in_pad = _pad_in_weight(w_in)
    w_out_b = w_out.astype(BF16)
    glu_w_b = s5_glu_w.astype(BF16)
    exp_w1_b, exp_w3_b, exp_w2_b = exp_w1.astype(BF16), exp_w3.astype(BF16), exp_w2.astype(BF16)
    seg = _segment_ones(RWKV_WIDTH, RWKV_HEADSIZE)

    xp = x_prompt.reshape(Np, D)
    xs = x_sample.reshape(Ns, D)
    new_s5_re, new_s5_im, new_ssd, new_rwkv = [], [], [], []
    QP = 2 if Bp % 2 == 0 and Bp >= 2 else 1
    RP = Bp // QP
    assert Bs * HEADS_PAD <= CH

    for l in range(depth):
        lp = {
            'rwkv_mu': rwkv_mu[l], 'rwkv_w0': rwkv_w0[l], 'rwkv_w_up': rwkv_w_up[l], 'rwkv_a0': rwkv_a0[l],
            'rwkv_a_up': rwkv_a_up[l], 'rwkv_g_up': rwkv_g_up[l], 'rwkv_k_k': rwkv_k_k[l], 'rwkv_k_a': rwkv_k_a[l],
            'rwkv_r_k': rwkv_r_k[l].reshape(-1), 'rwkv_ln_g': rwkv_ln_g[l], 'rwkv_ln_b': rwkv_ln_b[l],
            's5_glu_w': glu_w_b[l], 's5_glu_b': s5_glu_b[l], 'ssd_norm_g': ssd_norm_g[l], 'w_out': w_out_b[l],
            'norm2_g': norm2_g[l], 'router_w': router_w[l], 'seg': seg,
        }
        mod3 = mod[l].reshape(rows_pad, 1, 6 * D)
        us5_p, z_p, xbc_p, dt_p, rkv_p, lo_p = _in_projection(xp, mod3, row_p, norm1_g[l], w_in_pad[l], tmm)
        us5_s, z_s, xbc_s, dt_s, rkv_s, lo_s = _in_projection(xs, mod3, row_s, norm1_g[l], w_in_pad[l], tmm)

        tables = _s5_layer_tables(ab_re[l], ab_im[l], bb_re[l], bb_im[l], s5_c_re[l], s5_c_im[l])
        d_row = s5_d[l].reshape(1, S5_WIDTH)
        up = us5_p.reshape(QP, RP, Lp, S5_WIDTH).transpose(0, 2, 1, 3).reshape(QP, Lp * RP, S5_WIDTH)
        yp, hfin = _s5_scan(up, None, tables, d_row, R=RP, n_slab=Lp, chained=False, want_final=True)
        ys5_p = yp.reshape(QP, Lp, RP, S5_WIDTH).transpose(0, 2, 1, 3).reshape(Np, S5_WIDTH)
        hf = hfin.transpose(0, 3, 1, 2, 4).reshape(Bp, N_DIR, nfs, 2, 8, S5_STATE)
        new_s5_re.append(hf[:, :, :, 0].reshape(Bp, N_DIR, S5_GROUPS, S5_STATE))
        new_s5_im.append(hf[:, :, :, 1].reshape(Bp, N_DIR, S5_GROUPS, S5_STATE))
        h0 = jnp.concatenate([state_s5_re[:, l].reshape(Bs, N_DIR, nfs, 1, S5_SLICE_ST),
                              state_s5_im[:, l].reshape(Bs, N_DIR, nfs, 1, S5_SLICE_ST)], axis=-1)
        (ysm,) = _s5_scan(us5_s.reshape(Bs, Ls, S5_WIDTH), h0, tables, d_row, R=GRID_W, n_slab=grid_rows,
                          chained=True, want_final=False)
        ys5_s = ysm.reshape(Ns, S5_WIDTH)

        ssd_args = (ssd_conv_w[l], ssd_conv_b[l], ssd_dt_bias[l], ssd_a_log[l], ssd_d[l])
        *yssd_p, hssd = _ssd_scan(xbc_p, dt_p, None, *ssd_args, B=Bp, L=Lp, want_final=True)
        new_ssd.append(hssd.transpose(0, 1, 2, 4, 3))
        yssd_s = _ssd_scan(xbc_s, dt_s, state_ssd[:, l].transpose(0, 1, 2, 4, 3), *ssd_args,
                           B=Bs, L=Ls, want_final=False)

        r_, w0_, w1_, k_, v_, kk_, b_, g_p, bonus_p = _rwkv_prep(rkv_p, lo_p, lp, Lp // tm)
        cx = lambda t: _to_ctx(t, t, Bp, Lp)
        zero_state = jnp.zeros((RWKV_HEADSIZE, RWKV_HEADSIZE, _ctx_lanes(Bp)[1]), F32)
        yf_, yb_, sfin = _rwkv_scan(cx(r_), _to_ctx(w0_, w1_, Bp, Lp), cx(k_), cx(kk_), cx(b_), cx(v_),
                                    _ctx_fwd_mask(Bp), zero_state)
        rf_p = _from_ctx(yf_, Bp, Lp, 0)
        rb_p = _from_ctx(yb_, Bp, Lp, 1)
        new_rwkv.append(_ctx_state_from(sfin, Bp))

        r_, w0_, w1_, k_, v_, kk_, b_, g_s, bonus_s = _rwkv_prep(rkv_s, lo_s, lp, Ls // tmm, SCAN_WIDTH, tmm)
        sc = lambda t: _to_scan2_k(t, Bs, Ls)
        yf_, yb_ = _rwkv_scan2(sc(r_), sc(w0_), sc(w1_), sc(k_), sc(kk_), sc(b_), _to_scan2_v(v_, Bs, Ls),
                               _state_to_scan2(state_rwkv[:, l]))
        rf_s = _from_scan2(yf_, Bs, Ls)
        rb_s = _from_scan2(yb_, Bs, Ls)

        x1_p, hb_p, aff_p = _post_mixer(xp, mod3, row_p, ys5_p, yssd_p, z_p, rf_p, rb_p, bonus_p, g_p, lp, tmm)
        x1_s, hb_s, aff_s = _post_mixer(xs, mod3, row_s, ys5_s, yssd_s, z_s, rf_s, rb_s, bonus_s, g_s, lp, tmm)

        final = l == depth - 1
        ew = (exp_w1_b[l], exp_w3_b[l], exp_w2_b[l])
        xp = _expert_choice(hb_p, aff_p, x1_p, mod3, lambda b: 0, final_g, *ew, Bp, Lp, final)
        xs = _expert_choice(hb_s, aff_s, x1_s, mod3, lambda b: 1 + b, final_g, *ew, Bs, Ls, final)

    y_prompt = xp.reshape(Bp, Lp, D)
    y_sample = xs.reshape(Bs, Ls, D)
    return (y_prompt, y_sample, jnp.stack(new_s5_re, axis=1), jnp.stack(new_s5_im, axis=1),
            jnp.stack(new_ssd, axis=1), jnp.stack(new_rwkv, axis=1))
```
